```python
import math
import jax, jax.numpy as jnp
from jax import lax
import numpy as np

D_MODEL = 2048
BATCH = 2
SEQ = 4096
DEPTH = 1

D_MIX = D_MODEL
DA_WIDTH = D_MIX // 2
DA_HEADS = 8
DA_V_DIM = DA_WIDTH // DA_HEADS
DA_HEAD_DIM = DA_V_DIM // 2
HG_WIDTH = D_MIX - DA_WIDTH
HG_HEADS = 8
HG_V_DIM = HG_WIDTH // HG_HEADS
HG_K_DIM = 128
HG_KEY_WIDTH = HG_HEADS * HG_K_DIM
HG_CHUNK = 64
SPLIT_SIZES = (DA_WIDTH, DA_WIDTH, DA_WIDTH, HG_KEY_WIDTH, HG_KEY_WIDTH, HG_WIDTH, HG_WIDTH)
D_IN = sum(SPLIT_SIZES)
Q_BLOCK = 128
NUM_BUCKETS = 32
MAX_DISTANCE = 128
N_MEM = 256
MEM_HEADS = 4
MEM_HEAD_DIM = D_MODEL // MEM_HEADS
D_FF = 5632
CONV_WIDTH = 3
EPS = 1e-6

kernel_name = "hymba_diffattn_hgrn2_convffn_block"


def rmsnorm(x, w):
    xf = x.astype(jnp.float32)
    y = xf * lax.rsqrt(jnp.mean(xf * xf, axis=-1, keepdims=True) + EPS)
    return (y * w.astype(jnp.float32)).astype(x.dtype)


def t5_bucket(rel):
    n = jnp.maximum(rel, 0)
    max_exact = NUM_BUCKETS // 2
    nf = jnp.maximum(n, 1).astype(jnp.float32)
    large = max_exact + (jnp.log(nf / max_exact) / math.log(MAX_DISTANCE / max_exact)
                         * (NUM_BUCKETS - max_exact)).astype(jnp.int32)
    large = jnp.minimum(large, NUM_BUCKETS - 1)
    return jnp.where(n < max_exact, n, large)


def diff_attention(q, k, v, lam, bias_table):
    B, S = q.shape[0], q.shape[1]
    nb = S // Q_BLOCK
    scale = DA_HEAD_DIM ** -0.5
    qb = q.reshape(B, nb, Q_BLOCK, DA_HEADS, 2, DA_HEAD_DIM).transpose(1, 0, 3, 4, 2, 5)
    kt = k.transpose(0, 2, 3, 1, 4)
    vt = v.transpose(0, 2, 1, 3)
    kpos = jnp.arange(S)

    def block(args):
        i, qi = args
        qpos = i * Q_BLOCK + jnp.arange(Q_BLOCK)
        rel = qpos[:, None] - kpos[None, :]
        bias = bias_table[t5_bucket(rel)].transpose(2, 0, 1).astype(jnp.float32)
        logits = jnp.einsum('bhmqd,bhmkd->bhmqk', qi, kt).astype(jnp.float32) * scale
        logits = logits + bias[None, :, None]
        logits = jnp.where(rel >= 0, logits, -1e30)
        p = jax.nn.softmax(logits, axis=-1)
        a = p[:, :, 0] - lam * p[:, :, 1]
        return jnp.einsum('bhqk,bhkv->bhqv', a.astype(v.dtype), vt)

    out = lax.map(block, (jnp.arange(nb), qb))
    return out.transpose(1, 0, 3, 2, 4).reshape(B, S, DA_HEADS, DA_V_DIM)


def hgrn2(q, fz, i, lb):
    B, S = q.shape[0], q.shape[1]
    nc = S // HG_CHUNK
    f32 = jnp.float32
    f = lb + (1.0 - lb) * jax.nn.sigmoid(fz.astype(f32))
    log_f = jnp.log(f)
    kk = 1.0 - f

    def to_chunks(t):
        return t.astype(f32).reshape(B, nc, HG_CHUNK, HG_HEADS, t.shape[-1]).transpose(1, 0, 3, 2, 4)

    qc, kc, vc, gc = to_chunks(q), to_chunks(kk), to_chunks(i), to_chunks(log_f)
    causal = jnp.tril(jnp.ones((HG_CHUNK, HG_CHUNK), dtype=bool))[:, :, None]

    def step(state, inp):
        q_c, k_c, v_c, g_c = inp
        G = jnp.cumsum(g_c, axis=2)
        o_inter = jnp.einsum('bhck,bhkv->bhcv', q_c * jnp.exp(G), state)
        diff = G[:, :, :, None, :] - G[:, :, None, :, :]
        decay = jnp.where(causal, jnp.exp(jnp.where(causal, diff, 0.0)), 0.0)
        a = jnp.einsum('bhtk,bhsk,bhtsk->bhts', q_c, k_c, decay)
        o_intra = jnp.einsum('bhts,bhsv->bhtv', a, v_c)
        g_last = G[:, :, -1]
        k_dec = k_c * jnp.exp(g_last[:, :, None, :] - G)
        new_state = jnp.exp(g_last)[..., None] * state + jnp.einsum('bhsk,bhsv->bhkv', k_dec, v_c)
        return new_state, o_inter + o_intra

    s0 = jnp.zeros((B, HG_HEADS, HG_K_DIM, HG_V_DIM), f32)
    _, o = lax.scan(step, s0, (qc, kc, vc, gc))
    return o.transpose(1, 0, 3, 2, 4).reshape(B, S, HG_HEADS, HG_V_DIM).astype(i.dtype)


def causal_dwconv(u, w, b):
    S = u.shape[1]
    up = jnp.pad(u, ((0, 0), (CONV_WIDTH - 1, 0), (0, 0)))
    return sum(w[j] * up[:, j:j + S] for j in range(CONV_WIDTH)) + b


def split_points():
    pts, acc = [], 0
    for s in SPLIT_SIZES[:-1]:
        acc += s
        pts.append(acc)
    return pts


def setup_inputs(seed: int = 0) -> dict:
    key = jax.random.key(seed)
    ks = jax.random.split(key, 24)
    f32 = jnp.float32

    def nrm(k, shape, scale):
        return jax.random.normal(k, shape, f32) * scale

    def gain(k, shape):
        return 1.0 + 0.1 * jax.random.normal(k, shape, f32)

    L = DEPTH
    return {
        "x": nrm(ks[0], (BATCH, SEQ, D_MODEL), 1.0),
        "mem": nrm(ks[1], (BATCH, N_MEM, D_MODEL), 1.0),
        "w_in": nrm(ks[2], (L, D_MODEL, D_IN), D_MODEL ** -0.5),
        "w_out": nrm(ks[3], (L, D_MIX, D_MODEL), D_MIX ** -0.5),
        "norm_mix_w": gain(ks[4], (L, D_MODEL)),
        "lam_q1": nrm(ks[5], (L, DA_HEAD_DIM), 0.1),
        "lam_k1": nrm(ks[6], (L, DA_HEAD_DIM), 0.1),
        "lam_q2": nrm(ks[7], (L, DA_HEAD_DIM), 0.1),
        "lam_k2": nrm(ks[8], (L, DA_HEAD_DIM), 0.1),
        "da_subln_w": gain(ks[9], (L, DA_V_DIM)),
        "hg_lb_raw": nrm(ks[10], (L + 1, HG_KEY_WIDTH), 0.5),
        "hg_norm_w": gain(ks[11], (L, HG_V_DIM)),
        "rel_bias": nrm(ks[12], (NUM_BUCKETS, DA_HEADS), 0.5),
        "norm_mem_w": gain(ks[13], (L, D_MODEL)),
        "mem_kv_norm_w": gain(ks[14], (L, D_MODEL)),
        "w_mq": nrm(ks[15], (L, D_MODEL, D_MODEL), D_MODEL ** -0.5),
        "w_mkv": nrm(ks[16], (L, D_MODEL, 2 * D_MODEL), D_MODEL ** -0.5),
        "w_mo": nrm(ks[17], (L, D_MODEL, D_MODEL), D_MODEL ** -0.5),
        "norm_ffn_w": gain(ks[18], (L, D_MODEL)),
        "w_up": nrm(ks[19], (L, D_MODEL, 2 * D_FF), D_MODEL ** -0.5),
        "conv_w": nrm(ks[20], (L, CONV_WIDTH, 2 * D_FF), CONV_WIDTH ** -0.5),
        "conv_b": nrm(ks[21], (L, 2 * D_FF), 0.02),
        "w_down": nrm(ks[22], (L, D_FF, D_MODEL), D_FF ** -0.5),
        "final_norm_w": gain(ks[23], (D_MODEL,)),
    }


def reference(x, mem, w_in, w_out, norm_mix_w, lam_q1, lam_k1, lam_q2, lam_k2, da_subln_w,
              hg_lb_raw, hg_norm_w, rel_bias, norm_mem_w, mem_kv_norm_w, w_mq, w_mkv, w_mo,
              norm_ffn_w, w_up, conv_w, conv_b, w_down, final_norm_w):
    B, S, _ = x.shape
    f32 = jnp.float32
    lb_all = jnp.cumsum(jax.nn.softmax(hg_lb_raw.astype(f32), axis=0), axis=0)
    pts = split_points()
    for layer in range(DEPTH):
        h = rmsnorm(x, norm_mix_w[layer])
        proj = h @ w_in[layer]
        da_q, da_k, da_v, hg_q, hg_f, hg_i, hg_g = jnp.split(proj, pts, axis=-1)

        lam_init = 0.8 - 0.6 * math.exp(-0.3 * layer)
        lam = (jnp.exp(jnp.sum(lam_q1[layer].astype(f32) * lam_k1[layer].astype(f32)))
               - jnp.exp(jnp.sum(lam_q2[layer].astype(f32) * lam_k2[layer].astype(f32)))
               + lam_init)
        da_o = diff_attention(da_q.reshape(B, S, DA_HEADS, 2, DA_HEAD_DIM),
                              da_k.reshape(B, S, DA_HEADS, 2, DA_HEAD_DIM),
                              da_v.reshape(B, S, DA_HEADS, DA_V_DIM), lam, rel_bias)
        da_o = rmsnorm(da_o, da_subln_w[layer]) * (1.0 - lam_init)

        lb = lb_all[layer].reshape(HG_HEADS, HG_K_DIM)
        hg_o = hgrn2(hg_q.reshape(B, S, HG_HEADS, HG_K_DIM),
                     hg_f.reshape(B, S, HG_HEADS, HG_K_DIM),
                     hg_i.reshape(B, S, HG_HEADS, HG_V_DIM), lb)
        hg_o = rmsnorm(hg_o, hg_norm_w[layer]) * jax.nn.silu(hg_g.reshape(B, S, HG_HEADS, HG_V_DIM))

        mix = jnp.concatenate([da_o.reshape(B, S, DA_WIDTH), hg_o.reshape(B, S, HG_WIDTH)], axis=-1)
        x = x + mix @ w_out[layer]

        hq = rmsnorm(x, norm_mem_w[layer])
        mk = rmsnorm(mem, mem_kv_norm_w[layer])
        mq = (hq @ w_mq[layer]).reshape(B, S, MEM_HEADS, MEM_HEAD_DIM)
        m_k, m_v = jnp.split(mk @ w_mkv[layer], 2, axis=-1)
        m_k = m_k.reshape(B, N_MEM, MEM_HEADS, MEM_HEAD_DIM)
        m_v = m_v.reshape(B, N_MEM, MEM_HEADS, MEM_HEAD_DIM)
        m_logits = jnp.einsum('bqhd,bkhd->bhqk', mq, m_k).astype(f32) * (MEM_HEAD_DIM ** -0.5)
        m_p = jax.nn.softmax(m_logits, axis=-1).astype(x.dtype)
        m_o = jnp.einsum('bhqk,bkhd->bqhd', m_p, m_v).reshape(B, S, D_MODEL)
        x = x + m_o @ w_mo[layer]

        hf = rmsnorm(x, norm_ffn_w[layer])
        u = causal_dwconv(hf @ w_up[layer], conv_w[layer], conv_b[layer])
        a, b = jnp.split(u, 2, axis=-1)
        x = x + (jax.nn.silu(a) * b) @ w_down[layer]
    return rmsnorm(x, final_norm_w)
```

```python
import functools
import math

import numpy as np
import jax
import jax.numpy as jnp
from jax import lax
from jax.experimental import pallas as pl
from jax.experimental.pallas import tpu as pltpu

DA_HEADS = 8
HG_HEADS = 8
MEM_HEADS = 4
NUM_BUCKETS = 32
MAX_DISTANCE = 128
CONV_WIDTH = 3
EPS = 1e-6

LANES = 128
SUBLANES = 8
VMEM_LIMIT_BYTES = 56 * 1024 * 1024

HG_BLOCK = 16
NEG_BIG = -1e30

F32 = jnp.float32
BF16 = jnp.bfloat16


def _params(*sem):
    return pltpu.CompilerParams(dimension_semantics=sem, vmem_limit_bytes=VMEM_LIMIT_BYTES)


def _rms_rows(x, w):
    ms = jnp.mean(x * x, axis=-1, keepdims=True)
    return (x * lax.rsqrt(ms + EPS)) * w


def _pick(n, pref):
    t = min(pref, n)
    while n % t:
        t //= 2
    return t


def _norm_mm_kernel(x_ref, nw_ref, w_ref, o_ref, h_ref):
    @pl.when(pl.program_id(1) == 0)
    def _():
        h_ref[...] = _rms_rows(x_ref[...], nw_ref[...]).astype(h_ref.dtype)

    o_ref[...] = jnp.dot(h_ref[...], w_ref[...], preferred_element_type=F32).astype(o_ref.dtype)


def norm_matmul(x, nw, w, out_dtype, tm_pref=512, tn_pref=512, name="norm_matmul"):
    t, k = x.shape
    n = w.shape[1]
    tm, tn = _pick(t, tm_pref), _pick(n, tn_pref)
    return pl.pallas_call(
        _norm_mm_kernel,
        grid=(t // tm, n // tn),
        in_specs=[
            pl.BlockSpec((tm, k), lambda i, j: (i, 0)),
            pl.BlockSpec((1, k), lambda i, j: (0, 0)),
            pl.BlockSpec((k, tn), lambda i, j: (0, j)),
        ],
        out_specs=pl.BlockSpec((tm, tn), lambda i, j: (i, j)),
        out_shape=jax.ShapeDtypeStruct((t, n), out_dtype),
        scratch_shapes=[pltpu.VMEM((tm, k), BF16)],
        compiler_params=_params("parallel", "arbitrary"),
        name=name,
    )(x, nw.reshape(1, k), w)


def _mix_out_kernel(a0_ref, a1_ref, w0_ref, w1_ref, res_ref, o_ref):
    acc = res_ref[...]
    acc = acc + jnp.dot(a0_ref[...], w0_ref[...], preferred_element_type=F32)
    acc = acc + jnp.dot(a1_ref[...], w1_ref[...], preferred_element_type=F32)
    o_ref[...] = acc


def mix_out_proj(a0, a1, w, res, tm_pref=512, tn_pref=512):
    t, k0 = a0.shape
    k1 = a1.shape[1]
    assert k0 == k1 and w.shape[0] == k0 + k1
    n = w.shape[1]
    tm, tn = _pick(t, tm_pref), _pick(n, tn_pref)
    return pl.pallas_call(
        _mix_out_kernel,
        grid=(t // tm, n // tn),
        in_specs=[
            pl.BlockSpec((tm, k0), lambda i, j: (i, 0)),
            pl.BlockSpec((tm, k1), lambda i, j: (i, 0)),
            pl.BlockSpec((k0, tn), lambda i, j: (0, j)),
            pl.BlockSpec((k1, tn), lambda i, j: (1, j)),
            pl.BlockSpec((tm, tn), lambda i, j: (i, j)),
        ],
        out_specs=pl.BlockSpec((tm, tn), lambda i, j: (i, j)),
        out_shape=jax.ShapeDtypeStruct((t, n), F32),
        compiler_params=_params("parallel", "arbitrary"),
        name="mix_out_proj",
    )(a0, a1, w, w, res)


def _t5_bucket_np(n):
    max_exact = NUM_BUCKETS // 2
    nf = np.maximum(n, 1).astype(np.float32)
    large = max_exact + (np.log(nf / max_exact) / math.log(MAX_DISTANCE / max_exact)
                         * (NUM_BUCKETS - max_exact)).astype(np.int32)
    large = np.minimum(large, NUM_BUCKETS - 1)
    return np.where(n < max_exact, n, large).astype(np.int32)


def _tables_kernel(lamv_ref, lbraw_ref, lam_ref, lb_ref, *, lam_init, layer):
    v = lamv_ref[...]
    s1 = jnp.sum(v[0:1] * v[1:2], axis=-1, keepdims=True)
    s2 = jnp.sum(v[2:3] * v[3:4], axis=-1, keepdims=True)
    lam = jnp.exp(s1) - jnp.exp(s2) + lam_init
    lam_ref[...] = jnp.broadcast_to(lam, lam_ref.shape)
    raw = lbraw_ref[...]
    e = jnp.exp(raw - jnp.max(raw, axis=0, keepdims=True))
    sm = e / jnp.sum(e, axis=0, keepdims=True)
    lb_ref[...] = jnp.sum(sm[0:layer + 1], axis=0, keepdims=True)


def param_tables(lam_vecs, lb_raw, lam_init, layer):
    kw = lb_raw.shape[1]
    return pl.pallas_call(
        functools.partial(_tables_kernel, lam_init=lam_init, layer=layer),
        out_shape=(jax.ShapeDtypeStruct((1, LANES), F32), jax.ShapeDtypeStruct((1, kw), F32)),
        name="param_tables",
    )(lam_vecs, lb_raw)


def _bias_tiles_kernel(rb_ref, bk_ref, o_ref):
    h = pl.program_id(0)
    bk = bk_ref[...]
    acc = jnp.zeros(bk.shape, F32)
    for n in range(NUM_BUCKETS):
        acc = jnp.where(bk == n, rb_ref[n, h], acc)
    o_ref[0] = acc


def bias_tiles(rel_bias, buckets):
    nb, h = rel_bias.shape
    return pl.pallas_call(
        _bias_tiles_kernel,
        grid=(h,),
        in_specs=[
            pl.BlockSpec(memory_space=pltpu.SMEM),
            pl.BlockSpec(buckets.shape, lambda i: (0, 0, 0)),
        ],
        out_specs=pl.BlockSpec((1,) + buckets.shape, lambda i: (i, 0, 0, 0)),
        out_shape=jax.ShapeDtypeStruct((h,) + buckets.shape, F32),
        compiler_params=_params("arbitrary"),
        name="bias_tiles",
    )(rel_bias, buckets)


def _diff_attn_kernel(rb_ref, q_ref, k_ref, v_ref, bias_ref, lam_ref, nw_ref, o_ref,
                      m_ref, l_ref, acc_ref, *, tq, far_bucket, out_scale):
    h = pl.program_id(1)
    i = pl.program_id(2)
    d2 = q_ref.shape[1]
    dh = d2 // 2
    q = q_ref[...]
    lane = lax.broadcasted_iota(jnp.int32, (1, d2), 1)
    qm = (jnp.where(lane < dh, q, jnp.zeros_like(q)), jnp.where(lane >= dh, q, jnp.zeros_like(q)))
    far_bias = rb_ref[far_bucket, h]

    m_ref[...] = jnp.full(m_ref.shape, NEG_BIG, F32)
    l_ref[...] = jnp.zeros(l_ref.shape, F32)
    acc_ref[...] = jnp.zeros(acc_ref.shape, F32)

    def step(j, tile):
        start = pl.multiple_of(j * tq, tq)
        kb = k_ref[pl.ds(start, tq), :]
        vb = v_ref[pl.ds(start, tq), :]
        for m in range(2):
            s = lax.dot_general(qm[m], kb, (((1,), (1,)), ((), ())), preferred_element_type=F32)
            if tile is None:
                s = s + far_bias
            else:
                s = s + bias_ref[0, tile]
            if tile == 0:
                row = lax.broadcasted_iota(jnp.int32, s.shape, 0)
                col = lax.broadcasted_iota(jnp.int32, s.shape, 1)
                s = jnp.where(row >= col, s, NEG_BIG)
            m_old = m_ref[m]
            m_new = jnp.maximum(m_old, jnp.max(s, axis=-1, keepdims=True))
            p = jnp.exp(s - m_new)
            alpha = jnp.exp(m_old - m_new)
            l_ref[m] = alpha * l_ref[m] + jnp.sum(p, axis=-1, keepdims=True)
            acc_ref[m] = alpha * acc_ref[m] + jnp.dot(p.astype(BF16), vb, preferred_element_type=F32)
            m_ref[m] = m_new

    def far_body(j, carry):
        step(j, None)
        return carry

    lax.fori_loop(0, jnp.maximum(i - 1, 0), far_body, 0)

    @pl.when(i >= 1)
    def _():
        step(i - 1, 1)

    step(i, 0)

    o = acc_ref[0] / l_ref[0] - lam_ref[...] * (acc_ref[1] / l_ref[1])
    o_ref[...] = (_rms_rows(o, nw_ref[...]) * out_scale).astype(o_ref.dtype)


def diff_attention(proj, rel_bias, bias_t, lam_row, nw, batch, seq, tq, far_bucket, out_scale):
    nq = seq // tq
    dv = LANES
    kern = functools.partial(_diff_attn_kernel, tq=tq, far_bucket=far_bucket, out_scale=out_scale)
    return pl.pallas_call(
        kern,
        grid=(batch, DA_HEADS, nq),
        in_specs=[
            pl.BlockSpec(memory_space=pltpu.SMEM),
            pl.BlockSpec((tq, dv), lambda b, h, i: (b * nq + i, h)),
            pl.BlockSpec((seq, dv), lambda b, h, i: (b, DA_HEADS + h)),
            pl.BlockSpec((seq, dv), lambda b, h, i: (b, 2 * DA_HEADS + h)),
            pl.BlockSpec((1, 2, tq, tq), lambda b, h, i: (h, 0, 0, 0)),
            pl.BlockSpec((1, LANES), lambda b, h, i: (0, 0)),
            pl.BlockSpec((1, dv), lambda b, h, i: (0, 0)),
        ],
        out_specs=pl.BlockSpec((tq, dv), lambda b, h, i: (b * nq + i, h)),
        out_shape=jax.ShapeDtypeStruct((batch * seq, DA_HEADS * dv), BF16),
        scratch_shapes=[
            pltpu.VMEM((2, tq, 1), F32),
            pltpu.VMEM((2, tq, 1), F32),
            pltpu.VMEM((2, tq, dv), F32),
        ],
        compiler_params=_params("parallel", "parallel", "arbitrary"),
        name="diff_attention",
    )(rel_bias, proj, proj, proj, bias_t, lam_row, nw.reshape(1, dv))


def _exact_bf16_dot(tri, x):
    hi = x.astype(BF16)
    r1 = x - hi.astype(F32)
    mid = r1.astype(BF16)
    lo = (r1 - mid.astype(F32)).astype(BF16)
    return (jnp.dot(tri, hi, preferred_element_type=F32)
            + jnp.dot(tri, mid, preferred_element_type=F32)
            + jnp.dot(tri, lo, preferred_element_type=F32))


def _hgrn_kernel(q_ref, f_ref, i_ref, g_ref, lb_ref, nw_ref, o_ref, st_ref, o_acc, *, ts):
    bs = HG_BLOCK

    @pl.when(pl.program_id(2) == 0)
    def _():
        st_ref[...] = jnp.zeros(st_ref.shape, F32)

    lb = lb_ref[...]
    f = lb + (1.0 - lb) * jax.nn.sigmoid(f_ref[...])
    logf = jnp.log(f)
    kk = 1.0 - f

    row = lax.broadcasted_iota(jnp.int32, (ts, ts), 0)
    col = lax.broadcasted_iota(jnp.int32, (ts, ts), 1)
    tri = ((row >= col) & ((row // bs) == (col // bs))).astype(BF16)
    gcum = _exact_bf16_dot(tri, logf)

    t3 = lax.broadcasted_iota(jnp.int32, (bs, bs, LANES), 0)
    s3 = lax.broadcasted_iota(jnp.int32, (bs, bs, LANES), 1)
    causal = t3 >= s3
    ones = jnp.ones((LANES, LANES), BF16)

    st = st_ref[...]
    for blk in range(ts // bs):
        sl = slice(blk * bs, (blk + 1) * bs)
        gb = gcum[sl]
        qb = q_ref[sl, :].astype(F32)
        kb = kk[sl]
        vb = i_ref[sl, :]
        g_last = gb[bs - 1:bs]
        qg = (qb * jnp.exp(gb)).astype(BF16)
        o = lax.dot_general(qg, st.astype(BF16), (((1,), (1,)), ((), ())), preferred_element_type=F32)
        diff = gb[:, None, :] - gb[None, :, :]
        dec = jnp.where(causal, jnp.exp(diff), 0.0)
        w = (qb[:, None, :] * kb[None, :, :]) * dec
        a_rep = jnp.dot(w.reshape(bs * bs, LANES).astype(BF16), ones, preferred_element_type=F32)
        o = o + jnp.sum(a_rep.reshape(bs, bs, LANES) * vb[None, :, :], axis=1)
        o_acc[sl, :] = o
        kd = (kb * jnp.exp(g_last - gb)).astype(BF16)
        upd = lax.dot_general(vb.astype(BF16), kd, (((0,), (0,)), ((), ())), preferred_element_type=F32)
        st = st * jnp.exp(g_last) + upd
    st_ref[...] = st

    y = _rms_rows(o_acc[...], nw_ref[...])
    g = g_ref[...]
    o_ref[...] = (y * (g * jax.nn.sigmoid(g))).astype(o_ref.dtype)


def hgrn2(proj_a, proj_b, lb, nw, batch, seq, ts, q_col0):
    ns = seq // ts
    d = LANES
    kern = functools.partial(_hgrn_kernel, ts=ts)
    return pl.pallas_call(
        kern,
        grid=(batch, HG_HEADS, ns),
        in_specs=[
            pl.BlockSpec((ts, d), lambda b, h, i: (b * ns + i, q_col0 + h)),
            pl.BlockSpec((ts, d), lambda b, h, i: (b * ns + i, h)),
            pl.BlockSpec((ts, d), lambda b, h, i: (b * ns + i, HG_HEADS + h)),
            pl.BlockSpec((ts, d), lambda b, h, i: (b * ns + i, 2 * HG_HEADS + h)),
            pl.BlockSpec((1, d), lambda b, h, i: (0, h)),
            pl.BlockSpec((1, d), lambda b, h, i: (0, 0)),
        ],
        out_specs=pl.BlockSpec((ts, d), lambda b, h, i: (b * ns + i, h)),
        out_shape=jax.ShapeDtypeStruct((batch * seq, HG_HEADS * d), BF16),
        scratch_shapes=[pltpu.VMEM((d, d), F32), pltpu.VMEM((ts, d), F32)],
        compiler_params=_params("parallel", "parallel", "arbitrary"),
        name="hgrn2",
    )(proj_a, proj_b, proj_b, proj_b, lb, nw.reshape(1, d))


def _mem_attn_kernel(mq_ref, k_ref, v_ref, w_ref, res_ref, o_ref, a_ref, *, scale):
    @pl.when(pl.program_id(1) == 0)
    def _():
        dh = a_ref.shape[1] // MEM_HEADS
        for hd in range(MEM_HEADS):
            sl = slice(hd * dh, (hd + 1) * dh)
            s = lax.dot_general(mq_ref[:, sl], k_ref[:, sl], (((1,), (1,)), ((), ())),
                                preferred_element_type=F32) * scale
            e = jnp.exp(s - jnp.max(s, axis=-1, keepdims=True))
            p = e / jnp.sum(e, axis=-1, keepdims=True)
            a_ref[:, sl] = jnp.dot(p.astype(BF16), v_ref[:, sl],
                                   preferred_element_type=F32).astype(a_ref.dtype)

    o_ref[...] = res_ref[...] + jnp.dot(a_ref[...], w_ref[...], preferred_element_type=F32)


def mem_attention_out(mq, mkv, w_mo, res, seq, n_mem, tm_pref=512, tn_pref=512):
    t, d = mq.shape
    tm, tn = _pick(seq, tm_pref), _pick(d, tn_pref)
    per_batch = seq // tm
    scale = (d // MEM_HEADS) ** -0.5
    return pl.pallas_call(
        functools.partial(_mem_attn_kernel, scale=scale),
        grid=(t // tm, d // tn),
        in_specs=[
            pl.BlockSpec((tm, d), lambda i, j: (i, 0)),
            pl.BlockSpec((n_mem, d), lambda i, j: (i // per_batch, 0)),
            pl.BlockSpec((n_mem, d), lambda i, j: (i // per_batch, 1)),
            pl.BlockSpec((d, tn), lambda i, j: (0, j)),
            pl.BlockSpec((tm, tn), lambda i, j: (i, j)),
        ],
        out_specs=pl.BlockSpec((tm, tn), lambda i, j: (i, j)),
        out_shape=jax.ShapeDtypeStruct((t, d), F32),
        scratch_shapes=[pltpu.VMEM((tm, d), BF16)],
        compiler_params=_params("parallel", "arbitrary"),
        name="mem_attention_out",
    )(mq, mkv, mkv, w_mo, res)


def _ffn_up_kernel(x_ref, halo_ref, nw_ref, wa_ref, wb_ref, cwa_ref, cwb_ref, cba_ref, cbb_ref,
                   o_ref, h_ref, ua_ref, ub_ref, *, tm, per_batch):
    hl = SUBLANES

    @pl.when(pl.program_id(1) == 0)
    def _():
        nw = nw_ref[...]
        first = (pl.program_id(0) % per_batch) == 0
        halo = _rms_rows(halo_ref[...], nw)
        h_ref[0:hl, :] = jnp.where(first, 0.0, halo).astype(h_ref.dtype)
        h_ref[hl:, :] = _rms_rows(x_ref[...], nw).astype(h_ref.dtype)

    h = h_ref[...]
    ua_ref[...] = jnp.dot(h, wa_ref[...], preferred_element_type=F32)
    ub_ref[...] = jnp.dot(h, wb_ref[...], preferred_element_type=F32)

    def conv(u_ref, cw_ref, cb_ref):
        acc = cb_ref[...]
        for j in range(CONV_WIDTH):
            off = hl - (CONV_WIDTH - 1) + j
            acc = acc + cw_ref[j:j + 1, :] * u_ref[off:off + tm, :]
        return acc

    a = conv(ua_ref, cwa_ref, cba_ref)
    b = conv(ub_ref, cwb_ref, cbb_ref)
    o_ref[...] = ((a * jax.nn.sigmoid(a)) * b).astype(o_ref.dtype)


def ffn_up(x, nw, w_up, conv_w, conv_b, seq, tm_pref=512, tf_pref=512):
    t, d = x.shape
    f = w_up.shape[1] // 2
    tm, tf = _pick(seq, tm_pref), _pick(f, tf_pref)
    nf = f // tf
    per_batch = seq // tm
    hl = SUBLANES
    rows_per_tile = tm // hl
    kern = functools.partial(_ffn_up_kernel, tm=tm, per_batch=per_batch)
    return pl.pallas_call(
        kern,
        grid=(t // tm, nf),
        in_specs=[
            pl.BlockSpec((tm, d), lambda i, j: (i, 0)),
            pl.BlockSpec((hl, d), lambda i, j: (jnp.maximum(i * rows_per_tile - 1, 0), 0)),
            pl.BlockSpec((1, d), lambda i, j: (0, 0)),
            pl.BlockSpec((d, tf), lambda i, j: (0, j)),
            pl.BlockSpec((d, tf), lambda i, j: (0, nf + j)),
            pl.BlockSpec((CONV_WIDTH, tf), lambda i, j: (0, j)),
            pl.BlockSpec((CONV_WIDTH, tf), lambda i, j: (0, nf + j)),
            pl.BlockSpec((1, tf), lambda i, j: (0, j)),
            pl.BlockSpec((1, tf), lambda i, j: (0, nf + j)),
        ],
        out_specs=pl.BlockSpec((tm, tf), lambda i, j: (i, j)),
        out_shape=jax.ShapeDtypeStruct((t, f), BF16),
        scratch_shapes=[
            pltpu.VMEM((tm + hl, d), BF16),
            pltpu.VMEM((tm + hl, tf), F32),
            pltpu.VMEM((tm + hl, tf), F32),
        ],
        compiler_params=_params("parallel", "arbitrary"),
        name="ffn_up",
    )(x, x, nw.reshape(1, d), w_up, w_up, conv_w, conv_w, conv_b.reshape(1, 2 * f), conv_b.reshape(1, 2 * f))


def _ffn_down_kernel(a_ref, w_ref, res_ref, nw_ref, o_ref):
    k = pl.program_id(1)

    @pl.when(k == 0)
    def _():
        o_ref[...] = res_ref[...]

    o_ref[...] += jnp.dot(a_ref[...], w_ref[...], preferred_element_type=F32)

    @pl.when(k == pl.num_programs(1) - 1)
    def _():
        o_ref[...] = _rms_rows(o_ref[...], nw_ref[...])


def ffn_down(act, w_down, res, nw, tm_pref=512, tk_pref=512):
    t, f = act.shape
    d = w_down.shape[1]
    tm, tk = _pick(t, tm_pref), _pick(f, tk_pref)
    return pl.pallas_call(
        _ffn_down_kernel,
        grid=(t // tm, f // tk),
        in_specs=[
            pl.BlockSpec((tm, tk), lambda i, k: (i, k)),
            pl.BlockSpec((tk, d), lambda i, k: (k, 0)),
            pl.BlockSpec((tm, d), lambda i, k: (i, 0)),
            pl.BlockSpec((1, d), lambda i, k: (0, 0)),
        ],
        out_specs=pl.BlockSpec((tm, d), lambda i, k: (i, 0)),
        out_shape=jax.ShapeDtypeStruct((t, d), F32),
        compiler_params=_params("parallel", "arbitrary"),
        name="ffn_down",
    )(act, w_down, res, nw.reshape(1, d))


def kernel(x, mem, w_in, w_out, norm_mix_w, lam_q1, lam_k1, lam_q2, lam_k2, da_subln_w,
           hg_lb_raw, hg_norm_w, rel_bias, norm_mem_w, mem_kv_norm_w, w_mq, w_mkv, w_mo,
           norm_ffn_w, w_up, conv_w, conv_b, w_down, final_norm_w):
    batch, seq, d_model = x.shape
    n_mem = mem.shape[1]
    depth = w_in.shape[0]
    assert depth == 1, "single-layer block"
    layer = 0
    t = batch * seq

    da_width = DA_HEADS * LANES
    hg_width = HG_HEADS * LANES
    assert w_in.shape[2] == 3 * da_width + 4 * hg_width
    assert da_subln_w.shape[1] == LANES and hg_norm_w.shape[1] == LANES
    assert lam_q1.shape[1] * 2 == LANES
    n_bf = 3 * da_width + hg_width

    tq = _pick(seq, 128)
    ts = _pick(seq, 128)

    lam_init = 0.8 - 0.6 * math.exp(-0.3 * layer)
    lam_vecs = jnp.stack([lam_q1[layer], lam_k1[layer], lam_q2[layer], lam_k2[layer]]).astype(F32)
    lam_row, lb = param_tables(lam_vecs, hg_lb_raw.astype(F32), lam_init, layer)

    ii = np.arange(tq)[:, None]
    jj = np.arange(tq)[None, :]
    buckets = np.stack([_t5_bucket_np(np.maximum(ii - jj, 0)), _t5_bucket_np(tq + ii - jj)])
    far = _t5_bucket_np(np.arange(tq + 1, max(seq, tq + 2)))
    assert (far == far[0]).all(), "bias must be constant beyond the sub-diagonal block"
    bias_t = bias_tiles(rel_bias.astype(F32), jnp.asarray(buckets))

    x2d = x.reshape(t, d_model)
    w_in_l = w_in[layer]
    q_scale = (LANES // 2) ** -0.5
    w_a = jnp.concatenate([w_in_l[:, :da_width] * q_scale, w_in_l[:, da_width:n_bf]], axis=1).astype(BF16)
    w_b = w_in_l[:, n_bf:].astype(BF16)
    proj_a = norm_matmul(x2d, norm_mix_w[layer], w_a, BF16, name="in_proj_a")
    proj_b = norm_matmul(x2d, norm_mix_w[layer], w_b, F32, name="in_proj_b")

    da_o = diff_attention(proj_a, rel_bias.astype(F32), bias_t, lam_row, da_subln_w[layer],
                          batch, seq, tq, int(far[0]), 1.0 - lam_init)
    hg_o = hgrn2(proj_a, proj_b, lb, hg_norm_w[layer], batch, seq, ts, 3 * DA_HEADS)
    x1 = mix_out_proj(da_o, hg_o, w_out[layer].astype(BF16), x2d)

    mq = norm_matmul(x1, norm_mem_w[layer], w_mq[layer].astype(BF16), BF16, name="mem_q_proj")
    mkv = norm_matmul(mem.reshape(batch * n_mem, d_model), mem_kv_norm_w[layer],
                      w_mkv[layer].astype(BF16), BF16, name="mem_kv_proj")
    x2 = mem_attention_out(mq, mkv, w_mo[layer].astype(BF16), x1, seq, n_mem)

    act = ffn_up(x2, norm_ffn_w[layer], w_up[layer].astype(BF16), conv_w[layer], conv_b[layer], seq)
    out = ffn_down(act, w_down[layer].astype(BF16), x2, final_norm_w)
    return out.reshape(batch, seq, d_model)
```

```python
import functools
import math

import numpy as np
import jax
import jax.numpy as jnp
from jax import lax
from jax.experimental import pallas as pl
from jax.experimental.pallas import tpu as pltpu

DA_HEADS = 8
HG_HEADS = 8
MEM_HEADS = 4
NUM_BUCKETS = 32
MAX_DISTANCE = 128
CONV_WIDTH = 3
EPS = 1e-6

LANES = 128
SUBLANES = 8
VMEM_LIMIT_BYTES = 56 * 1024 * 1024

HG_BLOCK = 16
HG_TILE = 128
DA_BLOCK = 512
NEG_BIG = -1e30

F32 = jnp.float32
BF16 = jnp.bfloat16


def _params(*sem):
    return pltpu.CompilerParams(dimension_semantics=sem, vmem_limit_bytes=VMEM_LIMIT_BYTES)


def _rms_rows(x, w):
    ms = jnp.mean(x * x, axis=-1, keepdims=True)
    return (x * lax.rsqrt(ms + EPS)) * w


def _pick(n, pref):
    t = min(pref, n)
    while n % t:
        t //= 2
    return t


def _norm_mm_kernel(x_ref, nw_ref, w_ref, o_ref, h_ref):
    @pl.when(pl.program_id(1) == 0)
    def _():
        h_ref[...] = _rms_rows(x_ref[...], nw_ref[...]).astype(h_ref.dtype)

    o_ref[...] = jnp.dot(h_ref[...], w_ref[...], preferred_element_type=F32).astype(o_ref.dtype)


def norm_matmul(x, nw, w, out_dtype, tm_pref=512, tn_pref=512, name="norm_matmul"):
    t, k = x.shape
    n = w.shape[1]
    tm, tn = _pick(t, tm_pref), _pick(n, tn_pref)
    return pl.pallas_call(
        _norm_mm_kernel,
        grid=(t // tm, n // tn),
        in_specs=[
            pl.BlockSpec((tm, k), lambda i, j: (i, 0)),
            pl.BlockSpec((1, k), lambda i, j: (0, 0)),
            pl.BlockSpec((k, tn), lambda i, j: (0, j)),
        ],
        out_specs=pl.BlockSpec((tm, tn), lambda i, j: (i, j)),
        out_shape=jax.ShapeDtypeStruct((t, n), out_dtype),
        scratch_shapes=[pltpu.VMEM((tm, k), BF16)],
        compiler_params=_params("parallel", "arbitrary"),
        name=name,
    )(x, nw.reshape(1, k), w)


def _mix_out_kernel(a0_ref, a1_ref, w0_ref, w1_ref, res_ref, o_ref):
    acc = res_ref[...]
    acc = acc + jnp.dot(a0_ref[...], w0_ref[...], preferred_element_type=F32)
    acc = acc + jnp.dot(a1_ref[...], w1_ref[...], preferred_element_type=F32)
    o_ref[...] = acc


def mix_out_proj(a0, a1, w, res, tm_pref=512, tn_pref=512):
    t, k0 = a0.shape
    k1 = a1.shape[1]
    assert k0 == k1 and w.shape[0] == k0 + k1
    n = w.shape[1]
    tm, tn = _pick(t, tm_pref), _pick(n, tn_pref)
    return pl.pallas_call(
        _mix_out_kernel,
        grid=(t // tm, n // tn),
        in_specs=[
            pl.BlockSpec((tm, k0), lambda i, j: (i, 0)),
            pl.BlockSpec((tm, k1), lambda i, j: (i, 0)),
            pl.BlockSpec((k0, tn), lambda i, j: (0, j)),
            pl.BlockSpec((k1, tn), lambda i, j: (1, j)),
            pl.BlockSpec((tm, tn), lambda i, j: (i, j)),
        ],
        out_specs=pl.BlockSpec((tm, tn), lambda i, j: (i, j)),
        out_shape=jax.ShapeDtypeStruct((t, n), F32),
        compiler_params=_params("parallel", "arbitrary"),
        name="mix_out_proj",
    )(a0, a1, w, w, res)


def _t5_bucket_np(n):
    max_exact = NUM_BUCKETS // 2
    nf = np.maximum(n, 1).astype(np.float32)
    large = max_exact + (np.log(nf / max_exact) / math.log(MAX_DISTANCE / max_exact)
                         * (NUM_BUCKETS - max_exact)).astype(np.int32)
    large = np.minimum(large, NUM_BUCKETS - 1)
    return np.where(n < max_exact, n, large).astype(np.int32)


def _tables_kernel(lamv_ref, lbraw_ref, lam_ref, lb_ref, *, lam_init, layer):
    v = lamv_ref[...]
    s1 = jnp.sum(v[0:1] * v[1:2], axis=-1, keepdims=True)
    s2 = jnp.sum(v[2:3] * v[3:4], axis=-1, keepdims=True)
    lam = jnp.exp(s1) - jnp.exp(s2) + lam_init
    lam_ref[...] = jnp.broadcast_to(lam, lam_ref.shape)
    raw = lbraw_ref[...]
    e = jnp.exp(raw - jnp.max(raw, axis=0, keepdims=True))
    sm = e / jnp.sum(e, axis=0, keepdims=True)
    lb_ref[...] = jnp.sum(sm[0:layer + 1], axis=0, keepdims=True)


def param_tables(lam_vecs, lb_raw, lam_init, layer):
    kw = lb_raw.shape[1]
    return pl.pallas_call(
        functools.partial(_tables_kernel, lam_init=lam_init, layer=layer),
        out_shape=(jax.ShapeDtypeStruct((1, LANES), F32), jax.ShapeDtypeStruct((1, kw), F32)),
        name="param_tables",
    )(lam_vecs, lb_raw)


def _bias_tiles_kernel(rb_ref, bk_ref, o_ref):
    h = pl.program_id(0)
    bk = bk_ref[...]
    acc = jnp.full(bk.shape, NEG_BIG, F32)
    for n in range(NUM_BUCKETS):
        acc = jnp.where(bk == n, rb_ref[n, h], acc)
    o_ref[0] = acc


def bias_tiles(rel_bias, buckets):
    nb, h = rel_bias.shape
    return pl.pallas_call(
        _bias_tiles_kernel,
        grid=(h,),
        in_specs=[
            pl.BlockSpec(memory_space=pltpu.SMEM),
            pl.BlockSpec(buckets.shape, lambda i: (0, 0, 0)),
        ],
        out_specs=pl.BlockSpec((1,) + buckets.shape, lambda i: (i, 0, 0, 0)),
        out_shape=jax.ShapeDtypeStruct((h,) + buckets.shape, F32),
        compiler_params=_params("arbitrary"),
        name="bias_tiles",
    )(rel_bias, buckets)


ONES_ROWS = 16


def _diff_attn_kernel(rb_ref, q_ref, k_ref, v_ref, bias_ref, lam_ref, nw_ref, o_ref,
                      vt_ref, m_ref, acc_ref, *, tq, far_bucket, out_scale):
    h = pl.program_id(1)
    i = pl.program_id(2)
    dv = v_ref.shape[1]
    dh = dv // 2
    nkv = v_ref.shape[0] // tq

    @pl.when(i == 0)
    def _():
        for c in range(nkv):
            vt_ref[c, 0:dv, :] = v_ref[c * tq:(c + 1) * tq, :].astype(F32).T.astype(BF16)
            vt_ref[c, dv:, :] = jnp.ones((ONES_ROWS, tq), BF16)

    q = q_ref[...]
    lane = lax.broadcasted_iota(jnp.int32, (1, dv), 1)
    qcat = jnp.concatenate([jnp.where(lane < dh, q, jnp.zeros_like(q)),
                            jnp.where(lane >= dh, q, jnp.zeros_like(q))], axis=0)
    far_bias = rb_ref[far_bucket, h]

    m_ref[...] = jnp.full(m_ref.shape, NEG_BIG, F32)
    acc_ref[...] = jnp.zeros(acc_ref.shape, F32)

    def step(j, tile):
        start = pl.multiple_of(j * tq, tq)
        kb = k_ref[pl.ds(start, tq), :]
        st = lax.dot_general(kb, qcat, (((1,), (1,)), ((), ())), preferred_element_type=F32)
        if tile is None:
            shift = far_bias
        else:
            b = bias_ref[0, tile]
            st = st + jnp.concatenate([b, b], axis=1)
            shift = 0.0
        m_old = m_ref[...]
        m_new = jnp.maximum(m_old, jnp.max(st, axis=0, keepdims=True) + shift)
        p = jnp.exp(st - (m_new - shift)).astype(BF16)
        alpha = jnp.exp(m_old - m_new)
        acc_ref[...] = alpha * acc_ref[...] + jnp.dot(vt_ref[j], p, preferred_element_type=F32)
        m_ref[...] = m_new

    def far_body(j, carry):
        step(j, None)
        return carry

    lax.fori_loop(0, jnp.maximum(i - 1, 0), far_body, 0)

    @pl.when(i >= 1)
    def _():
        step(i - 1, 1)

    step(i, 0)

    acc = acc_ref[...]
    o0 = acc[0:dv, 0:tq] / acc[dv:dv + 1, 0:tq]
    o1 = acc[0:dv, tq:] / acc[dv:dv + 1, tq:]
    o = o0 - lam_ref[0:1, 0:1] * o1
    ms = jnp.mean(o * o, axis=0, keepdims=True)
    y = (o * lax.rsqrt(ms + EPS)) * (nw_ref[...] * out_scale)
    o_ref[...] = y.T.astype(o_ref.dtype)


def diff_attention(proj, rel_bias, bias_t, lam_row, nw, batch, seq, tq, far_bucket, out_scale):
    nq = seq // tq
    dv = LANES
    kern = functools.partial(_diff_attn_kernel, tq=tq, far_bucket=far_bucket, out_scale=out_scale)
    return pl.pallas_call(
        kern,
        grid=(batch, DA_HEADS, nq),
        in_specs=[
            pl.BlockSpec(memory_space=pltpu.SMEM),
            pl.BlockSpec((tq, dv), lambda b, h, i: (b * nq + i, h)),
            pl.BlockSpec((seq, dv), lambda b, h, i: (b, DA_HEADS + h)),
            pl.BlockSpec((seq, dv), lambda b, h, i: (b, 2 * DA_HEADS + h)),
            pl.BlockSpec((1, 2, tq, tq), lambda b, h, i: (h, 0, 0, 0)),
            pl.BlockSpec((1, LANES), lambda b, h, i: (0, 0)),
            pl.BlockSpec((dv, 1), lambda b, h, i: (0, 0)),
        ],
        out_specs=pl.BlockSpec((tq, dv), lambda b, h, i: (b * nq + i, h)),
        out_shape=jax.ShapeDtypeStruct((batch * seq, DA_HEADS * dv), BF16),
        scratch_shapes=[
            pltpu.VMEM((nq, dv + ONES_ROWS, tq), BF16),
            pltpu.VMEM((1, 2 * tq), F32),
            pltpu.VMEM((dv + ONES_ROWS, 2 * tq), F32),
        ],
        compiler_params=_params("parallel", "parallel", "arbitrary"),
        name="diff_attention",
    )(rel_bias, proj, proj, proj, bias_t, lam_row, nw.reshape(dv, 1))


def _exact_bf16_dot(tri, x):
    hi = x.astype(BF16)
    r1 = x - hi.astype(F32)
    mid = r1.astype(BF16)
    lo = (r1 - mid.astype(F32)).astype(BF16)
    return (jnp.dot(tri, hi, preferred_element_type=F32)
            + jnp.dot(tri, mid, preferred_element_type=F32)
            + jnp.dot(tri, lo, preferred_element_type=F32))


def _hgrn_kernel(q_ref, f_ref, i_ref, g_ref, lb_ref, nw_ref, o_ref, st_ref, o_acc, *, ts):
    bs = HG_BLOCK

    @pl.when(pl.program_id(2) == 0)
    def _():
        st_ref[...] = jnp.zeros(st_ref.shape, F32)

    lb = lb_ref[...]
    f = lb + (1.0 - lb) * jax.nn.sigmoid(f_ref[...])
    logf = jnp.log(f)
    kk = 1.0 - f

    row = lax.broadcasted_iota(jnp.int32, (ts, ts), 0)
    col = lax.broadcasted_iota(jnp.int32, (ts, ts), 1)
    tri = ((row >= col) & ((row // bs) == (col // bs))).astype(BF16)
    gcum = _exact_bf16_dot(tri, logf)

    t3 = lax.broadcasted_iota(jnp.int32, (bs, bs, LANES), 0)
    s3 = lax.broadcasted_iota(jnp.int32, (bs, bs, LANES), 1)
    causal = t3 >= s3
    ones = jnp.ones((LANES, LANES), BF16)

    st = st_ref[...]
    for blk in range(ts // bs):
        sl = slice(blk * bs, (blk + 1) * bs)
        gb = gcum[sl]
        qb = q_ref[sl, :].astype(F32)
        kb = kk[sl]
        vb = i_ref[sl, :]
        g_last = gb[bs - 1:bs]
        qg = (qb * jnp.exp(gb)).astype(BF16)
        o = lax.dot_general(qg, st.astype(BF16), (((1,), (1,)), ((), ())), preferred_element_type=F32)
        diff = gb[:, None, :] - gb[None, :, :]
        dec = jnp.where(causal, jnp.exp(diff), 0.0)
        w = (qb[:, None, :] * kb[None, :, :]) * dec
        a_rep = jnp.dot(w.reshape(bs * bs, LANES).astype(BF16), ones, preferred_element_type=F32)
        o = o + jnp.sum(a_rep.reshape(bs, bs, LANES) * vb[None, :, :], axis=1)
        o_acc[sl, :] = o
        kd = (kb * jnp.exp(g_last - gb)).astype(BF16)
        upd = lax.dot_general(vb.astype(BF16), kd, (((0,), (0,)), ((), ())), preferred_element_type=F32)
        st = st * jnp.exp(g_last) + upd
    st_ref[...] = st

    y = _rms_rows(o_acc[...], nw_ref[...])
    g = g_ref[...]
    o_ref[...] = (y * (g * jax.nn.sigmoid(g))).astype(o_ref.dtype)


def hgrn2(proj_a, proj_b, lb, nw, batch, seq, ts, q_col0):
    ns = seq // ts
    d = LANES
    kern = functools.partial(_hgrn_kernel, ts=ts)
    return pl.pallas_call(
        kern,
        grid=(batch, HG_HEADS, ns),
        in_specs=[
            pl.BlockSpec((ts, d), lambda b, h, i: (b * ns + i, q_col0 + h)),
            pl.BlockSpec((ts, d), lambda b, h, i: (b * ns + i, h)),
            pl.BlockSpec((ts, d), lambda b, h, i: (b * ns + i, HG_HEADS + h)),
            pl.BlockSpec((ts, d), lambda b, h, i: (b * ns + i, 2 * HG_HEADS + h)),
            pl.BlockSpec((1, d), lambda b, h, i: (0, h)),
            pl.BlockSpec((1, d), lambda b, h, i: (0, 0)),
        ],
        out_specs=pl.BlockSpec((ts, d), lambda b, h, i: (b * ns + i, h)),
        out_shape=jax.ShapeDtypeStruct((batch * seq, HG_HEADS * d), BF16),
        scratch_shapes=[pltpu.VMEM((d, d), F32), pltpu.VMEM((ts, d), F32)],
        compiler_params=_params("parallel", "parallel", "arbitrary"),
        name="hgrn2",
    )(proj_a, proj_b, proj_b, proj_b, lb, nw.reshape(1, d))


def _mem_attn_kernel(mq_ref, k_ref, v_ref, w_ref, res_ref, o_ref, a_ref, *, scale):
    @pl.when(pl.program_id(1) == 0)
    def _():
        dh = a_ref.shape[1] // MEM_HEADS
        for hd in range(MEM_HEADS):
            sl = slice(hd * dh, (hd + 1) * dh)
            s = lax.dot_general(mq_ref[:, sl], k_ref[:, sl], (((1,), (1,)), ((), ())),
                                preferred_element_type=F32) * scale
            e = jnp.exp(s - jnp.max(s, axis=-1, keepdims=True))
            p = e / jnp.sum(e, axis=-1, keepdims=True)
            a_ref[:, sl] = jnp.dot(p.astype(BF16), v_ref[:, sl],
                                   preferred_element_type=F32).astype(a_ref.dtype)

    o_ref[...] = res_ref[...] + jnp.dot(a_ref[...], w_ref[...], preferred_element_type=F32)


def mem_attention_out(mq, mkv, w_mo, res, seq, n_mem, tm_pref=512, tn_pref=512):
    t, d = mq.shape
    tm, tn = _pick(seq, tm_pref), _pick(d, tn_pref)
    per_batch = seq // tm
    scale = (d // MEM_HEADS) ** -0.5
    return pl.pallas_call(
        functools.partial(_mem_attn_kernel, scale=scale),
        grid=(t // tm, d // tn),
        in_specs=[
            pl.BlockSpec((tm, d), lambda i, j: (i, 0)),
            pl.BlockSpec((n_mem, d), lambda i, j: (i // per_batch, 0)),
            pl.BlockSpec((n_mem, d), lambda i, j: (i // per_batch, 1)),
            pl.BlockSpec((d, tn), lambda i, j: (0, j)),
            pl.BlockSpec((tm, tn), lambda i, j: (i, j)),
        ],
        out_specs=pl.BlockSpec((tm, tn), lambda i, j: (i, j)),
        out_shape=jax.ShapeDtypeStruct((t, d), F32),
        scratch_shapes=[pltpu.VMEM((tm, d), BF16)],
        compiler_params=_params("parallel", "arbitrary"),
        name="mem_attention_out",
    )(mq, mkv, mkv, w_mo, res)


def _ffn_up_kernel(x_ref, halo_ref, nw_ref, wa_ref, wb_ref, cwa_ref, cwb_ref, cba_ref, cbb_ref,
                   o_ref, h_ref, ua_ref, ub_ref, *, tm, per_batch):
    hl = SUBLANES

    @pl.when(pl.program_id(1) == 0)
    def _():
        nw = nw_ref[...]
        first = (pl.program_id(0) % per_batch) == 0
        halo = _rms_rows(halo_ref[...], nw)
        h_ref[0:hl, :] = jnp.where(first, 0.0, halo).astype(h_ref.dtype)
        h_ref[hl:, :] = _rms_rows(x_ref[...], nw).astype(h_ref.dtype)

    h = h_ref[...]
    ua_ref[...] = jnp.dot(h, wa_ref[...], preferred_element_type=F32)
    ub_ref[...] = jnp.dot(h, wb_ref[...], preferred_element_type=F32)

    def conv(u_ref, cw_ref, cb_ref):
        acc = cb_ref[...]
        for j in range(CONV_WIDTH):
            off = hl - (CONV_WIDTH - 1) + j
            acc = acc + cw_ref[j:j + 1, :] * u_ref[off:off + tm, :]
        return acc

    a = conv(ua_ref, cwa_ref, cba_ref)
    b = conv(ub_ref, cwb_ref, cbb_ref)
    o_ref[...] = ((a * jax.nn.sigmoid(a)) * b).astype(o_ref.dtype)


def ffn_up(x, nw, w_up, conv_w, conv_b, seq, tm_pref=512, tf_pref=512):
    t, d = x.shape
    f = w_up.shape[1] // 2
    tm, tf = _pick(seq, tm_pref), _pick(f, tf_pref)
    nf = f // tf
    per_batch = seq // tm
    hl = SUBLANES
    rows_per_tile = tm // hl
    kern = functools.partial(_ffn_up_kernel, tm=tm, per_batch=per_batch)
    return pl.pallas_call(
        kern,
        grid=(t // tm, nf),
        in_specs=[
            pl.BlockSpec((tm, d), lambda i, j: (i, 0)),
            pl.BlockSpec((hl, d), lambda i, j: (jnp.maximum(i * rows_per_tile - 1, 0), 0)),
            pl.BlockSpec((1, d), lambda i, j: (0, 0)),
            pl.BlockSpec((d, tf), lambda i, j: (0, j)),
            pl.BlockSpec((d, tf), lambda i, j: (0, nf + j)),
            pl.BlockSpec((CONV_WIDTH, tf), lambda i, j: (0, j)),
            pl.BlockSpec((CONV_WIDTH, tf), lambda i, j: (0, nf + j)),
            pl.BlockSpec((1, tf), lambda i, j: (0, j)),
            pl.BlockSpec((1, tf), lambda i, j: (0, nf + j)),
        ],
        out_specs=pl.BlockSpec((tm, tf), lambda i, j: (i, j)),
        out_shape=jax.ShapeDtypeStruct((t, f), BF16),
        scratch_shapes=[
            pltpu.VMEM((tm + hl, d), BF16),
            pltpu.VMEM((tm + hl, tf), F32),
            pltpu.VMEM((tm + hl, tf), F32),
        ],
        compiler_params=_params("parallel", "arbitrary"),
        name="ffn_up",
    )(x, x, nw.reshape(1, d), w_up, w_up, conv_w, conv_w, conv_b.reshape(1, 2 * f), conv_b.reshape(1, 2 * f))


def _ffn_down_kernel(a_ref, w_ref, res_ref, nw_ref, o_ref):
    k = pl.program_id(1)

    @pl.when(k == 0)
    def _():
        o_ref[...] = res_ref[...]

    o_ref[...] += jnp.dot(a_ref[...], w_ref[...], preferred_element_type=F32)

    @pl.when(k == pl.num_programs(1) - 1)
    def _():
        o_ref[...] = _rms_rows(o_ref[...], nw_ref[...])


def ffn_down(act, w_down, res, nw, tm_pref=512, tk_pref=512):
    t, f = act.shape
    d = w_down.shape[1]
    tm, tk = _pick(t, tm_pref), _pick(f, tk_pref)
    return pl.pallas_call(
        _ffn_down_kernel,
        grid=(t // tm, f // tk),
        in_specs=[
            pl.BlockSpec((tm, tk), lambda i, k: (i, k)),
            pl.BlockSpec((tk, d), lambda i, k: (k, 0)),
            pl.BlockSpec((tm, d), lambda i, k: (i, 0)),
            pl.BlockSpec((1, d), lambda i, k: (0, 0)),
        ],
        out_specs=pl.BlockSpec((tm, d), lambda i, k: (i, 0)),
        out_shape=jax.ShapeDtypeStruct((t, d), F32),
        compiler_params=_params("parallel", "arbitrary"),
        name="ffn_down",
    )(act, w_down, res, nw.reshape(1, d))


def kernel(x, mem, w_in, w_out, norm_mix_w, lam_q1, lam_k1, lam_q2, lam_k2, da_subln_w,
           hg_lb_raw, hg_norm_w, rel_bias, norm_mem_w, mem_kv_norm_w, w_mq, w_mkv, w_mo,
           norm_ffn_w, w_up, conv_w, conv_b, w_down, final_norm_w):
    batch, seq, d_model = x.shape
    n_mem = mem.shape[1]
    depth = w_in.shape[0]
    assert depth == 1, "single-layer block"
    layer = 0
    t = batch * seq

    da_width = DA_HEADS * LANES
    hg_width = HG_HEADS * LANES
    assert w_in.shape[2] == 3 * da_width + 4 * hg_width
    assert da_subln_w.shape[1] == LANES and hg_norm_w.shape[1] == LANES
    assert lam_q1.shape[1] * 2 == LANES
    n_bf = 3 * da_width + hg_width

    tq = _pick(seq, DA_BLOCK)
    ts = _pick(seq, HG_TILE)

    lam_init = 0.8 - 0.6 * math.exp(-0.3 * layer)
    lam_vecs = jnp.stack([lam_q1[layer], lam_k1[layer], lam_q2[layer], lam_k2[layer]]).astype(F32)
    lam_row, lb = param_tables(lam_vecs, hg_lb_raw.astype(F32), lam_init, layer)

    kk = np.arange(tq)[:, None]
    qq = np.arange(tq)[None, :]
    buckets = np.stack([np.where(qq >= kk, _t5_bucket_np(np.maximum(qq - kk, 0)), -1),
                        _t5_bucket_np(tq + qq - kk)]).astype(np.int32)
    far = _t5_bucket_np(np.arange(tq + 1, max(seq, tq + 2)))
    assert (far == far[0]).all(), "bias must be constant beyond the sub-diagonal block"
    bias_t = bias_tiles(rel_bias.astype(F32), jnp.asarray(buckets))

    x2d = x.reshape(t, d_model)
    w_in_l = w_in[layer]
    q_scale = (LANES // 2) ** -0.5
    w_a = jnp.concatenate([w_in_l[:, :da_width] * q_scale, w_in_l[:, da_width:n_bf]], axis=1).astype(BF16)
    w_b = w_in_l[:, n_bf:].astype(BF16)
    proj_a = norm_matmul(x2d, norm_mix_w[layer], w_a, BF16, name="in_proj_a")
    proj_b = norm_matmul(x2d, norm_mix_w[layer], w_b, F32, name="in_proj_b")

    da_o = diff_attention(proj_a, rel_bias.astype(F32), bias_t, lam_row, da_subln_w[layer],
                          batch, seq, tq, int(far[0]), 1.0 - lam_init)
    hg_o = hgrn2(proj_a, proj_b, lb, hg_norm_w[layer], batch, seq, ts, 3 * DA_HEADS)
    x1 = mix_out_proj(da_o, hg_o, w_out[layer].astype(BF16), x2d)

    mq = norm_matmul(x1, norm_mem_w[layer], w_mq[layer].astype(BF16), BF16, name="mem_q_proj")
    mkv = norm_matmul(mem.reshape(batch * n_mem, d_model), mem_kv_norm_w[layer],
                      w_mkv[layer].astype(BF16), BF16, name="mem_kv_proj")
    x2 = mem_attention_out(mq, mkv, w_mo[layer].astype(BF16), x1, seq, n_mem)

    act = ffn_up(x2, norm_ffn_w[layer], w_up[layer].astype(BF16), conv_w[layer], conv_b[layer], seq)
    out = ffn_down(act, w_down[layer].astype(BF16), x2, final_norm_w)
    return out.reshape(batch, seq, d_model)
```

```python
import functools
import math

import numpy as np
import jax
import jax.numpy as jnp
from jax import lax
from jax.experimental import pallas as pl
from jax.experimental.pallas import tpu as pltpu

DA_HEADS = 8
HG_HEADS = 8
MEM_HEADS = 4
NUM_BUCKETS = 32
MAX_DISTANCE = 128
CONV_WIDTH = 3
EPS = 1e-6

LANES = 128
SUBLANES = 8
VMEM_LIMIT_BYTES = 56 * 1024 * 1024

HG_CHUNK = 128
HG_TILE = 256
HG_HEADS_PER_STEP = 2
HG_SMALL_LEVELS = 3
DA_BLOCK = 512
MM_TM = 1024
MM_TN = 1024
FFN_TF = 512
FFN_COL_CHUNK = 256
NEG_BIG = -1e30

F32 = jnp.float32
BF16 = jnp.bfloat16


def _params(*sem):
    return pltpu.CompilerParams(dimension_semantics=sem, vmem_limit_bytes=VMEM_LIMIT_BYTES)


def _rms_rows(x, w):
    ms = jnp.mean(x * x, axis=-1, keepdims=True)
    return (x * lax.rsqrt(ms + EPS)) * w


def _pick(n, pref):
    t = min(pref, n)
    while n % t:
        t //= 2
    return t


def _norm_mm_kernel(x_ref, nw_ref, w_ref, o_ref, h_ref):
    @pl.when(pl.program_id(1) == 0)
    def _():
        h_ref[...] = _rms_rows(x_ref[...], nw_ref[...]).astype(h_ref.dtype)

    o_ref[...] = jnp.dot(h_ref[...], w_ref[...], preferred_element_type=F32).astype(o_ref.dtype)


def norm_matmul(x, nw, w, out_dtype, name="norm_matmul"):
    t, k = x.shape
    n = w.shape[1]
    tm, tn = _pick(t, MM_TM), _pick(n, MM_TN)
    return pl.pallas_call(
        _norm_mm_kernel,
        grid=(t // tm, n // tn),
        in_specs=[
            pl.BlockSpec((tm, k), lambda i, j: (i, 0)),
            pl.BlockSpec((1, k), lambda i, j: (0, 0)),
            pl.BlockSpec((k, tn), lambda i, j: (0, j)),
        ],
        out_specs=pl.BlockSpec((tm, tn), lambda i, j: (i, j)),
        out_shape=jax.ShapeDtypeStruct((t, n), out_dtype),
        scratch_shapes=[pltpu.VMEM((tm, k), BF16)],
        compiler_params=_params("parallel", "arbitrary"),
        name=name,
    )(x, nw.reshape(1, k), w)


def _mix_out_kernel(a0_ref, a1_ref, w0_ref, w1_ref, res_ref, o_ref):
    acc = res_ref[...]
    acc = acc + jnp.dot(a0_ref[...], w0_ref[...], preferred_element_type=F32)
    acc = acc + jnp.dot(a1_ref[...], w1_ref[...], preferred_element_type=F32)
    o_ref[...] = acc


def mix_out_proj(a0, a1, w, res):
    t, k0 = a0.shape
    k1 = a1.shape[1]
    assert k0 == k1 and w.shape[0] == k0 + k1
    n = w.shape[1]
    tm, tn = _pick(t, MM_TM), _pick(n, MM_TN)
    return pl.pallas_call(
        _mix_out_kernel,
        grid=(t // tm, n // tn),
        in_specs=[
            pl.BlockSpec((tm, k0), lambda i, j: (i, 0)),
            pl.BlockSpec((tm, k1), lambda i, j: (i, 0)),
            pl.BlockSpec((k0, tn), lambda i, j: (0, j)),
            pl.BlockSpec((k1, tn), lambda i, j: (1, j)),
            pl.BlockSpec((tm, tn), lambda i, j: (i, j)),
        ],
        out_specs=pl.BlockSpec((tm, tn), lambda i, j: (i, j)),
        out_shape=jax.ShapeDtypeStruct((t, n), F32),
        compiler_params=_params("parallel", "arbitrary"),
        name="mix_out_proj",
    )(a0, a1, w, w, res)


def _t5_bucket_np(n):
    max_exact = NUM_BUCKETS // 2
    nf = np.maximum(n, 1).astype(np.float32)
    large = max_exact + (np.log(nf / max_exact) / math.log(MAX_DISTANCE / max_exact)
                         * (NUM_BUCKETS - max_exact)).astype(np.int32)
    large = np.minimum(large, NUM_BUCKETS - 1)
    return np.where(n < max_exact, n, large).astype(np.int32)


def _tables_kernel(lamv_ref, lbraw_ref, lam_ref, lb_ref, *, lam_init, layer):
    v = lamv_ref[...]
    s1 = jnp.sum(v[0:1] * v[1:2], axis=-1, keepdims=True)
    s2 = jnp.sum(v[2:3] * v[3:4], axis=-1, keepdims=True)
    lam = jnp.exp(s1) - jnp.exp(s2) + lam_init
    lam_ref[...] = jnp.broadcast_to(lam, lam_ref.shape)
    raw = lbraw_ref[...]
    e = jnp.exp(raw - jnp.max(raw, axis=0, keepdims=True))
    sm = e / jnp.sum(e, axis=0, keepdims=True)
    lb_ref[...] = jnp.sum(sm[0:layer + 1], axis=0, keepdims=True)


def param_tables(lam_vecs, lb_raw, lam_init, layer):
    kw = lb_raw.shape[1]
    return pl.pallas_call(
        functools.partial(_tables_kernel, lam_init=lam_init, layer=layer),
        out_shape=(jax.ShapeDtypeStruct((1, LANES), F32), jax.ShapeDtypeStruct((1, kw), F32)),
        name="param_tables",
    )(lam_vecs, lb_raw)


def _bias_tiles_kernel(rb_ref, bk_ref, o_ref):
    h = pl.program_id(0)
    bk = bk_ref[...]
    acc = jnp.full(bk.shape, NEG_BIG, F32)
    for n in range(NUM_BUCKETS):
        acc = jnp.where(bk == n, rb_ref[n, h], acc)
    o_ref[0] = acc


def bias_tiles(rel_bias, buckets):
    nb, h = rel_bias.shape
    return pl.pallas_call(
        _bias_tiles_kernel,
        grid=(h,),
        in_specs=[
            pl.BlockSpec(memory_space=pltpu.SMEM),
            pl.BlockSpec(buckets.shape, lambda i: (0, 0, 0)),
        ],
        out_specs=pl.BlockSpec((1,) + buckets.shape, lambda i: (i, 0, 0, 0)),
        out_shape=jax.ShapeDtypeStruct((h,) + buckets.shape, F32),
        compiler_params=_params("arbitrary"),
        name="bias_tiles",
    )(rel_bias, buckets)


ONES_ROWS = 16


def _diff_attn_kernel(rb_ref, q_ref, k_ref, v_ref, bias_ref, lam_ref, nw_ref, o_ref,
                      vt_ref, m_ref, acc_ref, *, tq, far_bucket, out_scale):
    h = pl.program_id(1)
    i = pl.program_id(2)
    dv = v_ref.shape[1]
    dh = dv // 2
    nkv = v_ref.shape[0] // tq

    @pl.when(i == 0)
    def _():
        for c in range(nkv):
            vt_ref[c, 0:dv, :] = v_ref[c * tq:(c + 1) * tq, :].astype(F32).T.astype(BF16)
            vt_ref[c, dv:, :] = jnp.ones((ONES_ROWS, tq), BF16)

    q = q_ref[...]
    lane = lax.broadcasted_iota(jnp.int32, (1, dv), 1)
    qcat = jnp.concatenate([jnp.where(lane < dh, q, jnp.zeros_like(q)),
                            jnp.where(lane >= dh, q, jnp.zeros_like(q))], axis=0)
    far_bias = rb_ref[far_bucket, h]

    m_ref[...] = jnp.full(m_ref.shape, NEG_BIG, F32)
    acc_ref[...] = jnp.zeros(acc_ref.shape, F32)

    def step(j, tile):
        start = pl.multiple_of(j * tq, tq)
        kb = k_ref[pl.ds(start, tq), :]
        st = lax.dot_general(kb, qcat, (((1,), (1,)), ((), ())), preferred_element_type=F32)
        if tile is None:
            shift = far_bias
        else:
            b = bias_ref[0, tile]
            st = st + jnp.concatenate([b, b], axis=1)
            shift = 0.0
        m_old = m_ref[...]
        m_new = jnp.maximum(m_old, jnp.max(st, axis=0, keepdims=True) + shift)
        p = jnp.exp(st - (m_new - shift)).astype(BF16)
        alpha = jnp.exp(m_old - m_new)
        acc_ref[...] = alpha * acc_ref[...] + jnp.dot(vt_ref[j], p, preferred_element_type=F32)
        m_ref[...] = m_new

    def far_body(j, carry):
        step(j, None)
        return carry

    lax.fori_loop(0, jnp.maximum(i - 1, 0), far_body, 0)

    @pl.when(i >= 1)
    def _():
        step(i - 1, 1)

    step(i, 0)

    acc = acc_ref[...]
    o0 = acc[0:dv, 0:tq] / acc[dv:dv + 1, 0:tq]
    o1 = acc[0:dv, tq:] / acc[dv:dv + 1, tq:]
    o = o0 - lam_ref[0:1, 0:1] * o1
    ms = jnp.mean(o * o, axis=0, keepdims=True)
    y = (o * lax.rsqrt(ms + EPS)) * (nw_ref[...] * out_scale)
    o_ref[...] = y.T.astype(o_ref.dtype)


def diff_attention(proj, rel_bias, bias_t, lam_row, nw, batch, seq, tq, far_bucket, out_scale):
    nq = seq // tq
    dv = LANES
    kern = functools.partial(_diff_attn_kernel, tq=tq, far_bucket=far_bucket, out_scale=out_scale)
    return pl.pallas_call(
        kern,
        grid=(batch, DA_HEADS, nq),
        in_specs=[
            pl.BlockSpec(memory_space=pltpu.SMEM),
            pl.BlockSpec((tq, dv), lambda b, h, i: (b * nq + i, h)),
            pl.BlockSpec((seq, dv), lambda b, h, i: (b, DA_HEADS + h)),
            pl.BlockSpec((seq, dv), lambda b, h, i: (b, 2 * DA_HEADS + h)),
            pl.BlockSpec((1, 2, tq, tq), lambda b, h, i: (h, 0, 0, 0)),
            pl.BlockSpec((1, LANES), lambda b, h, i: (0, 0)),
            pl.BlockSpec((dv, 1), lambda b, h, i: (0, 0)),
        ],
        out_specs=pl.BlockSpec((tq, dv), lambda b, h, i: (b * nq + i, h)),
        out_shape=jax.ShapeDtypeStruct((batch * seq, DA_HEADS * dv), BF16),
        scratch_shapes=[
            pltpu.VMEM((nq, dv + ONES_ROWS, tq), BF16),
            pltpu.VMEM((1, 2 * tq), F32),
            pltpu.VMEM((dv + ONES_ROWS, 2 * tq), F32),
        ],
        compiler_params=_params("parallel", "parallel", "arbitrary"),
        name="diff_attention",
    )(rel_bias, proj, proj, proj, bias_t, lam_row, nw.reshape(dv, 1))


def _exact_bf16_dot(tri3, x):
    hi = x.astype(BF16)
    r1 = x - hi.astype(F32)
    mid = r1.astype(BF16)
    lo = (r1 - mid.astype(F32)).astype(BF16)
    return jnp.dot(tri3, jnp.concatenate([hi, mid, lo], axis=0), preferred_element_type=F32)


def _hgrn_tables(c):
    n_lv = int(math.log2(c))
    r = np.arange(c)[None, :]
    t = np.arange(c)[:, None]
    blocks_d, blocks_e = [], []
    for lv in range(1, HG_SMALL_LEVELS + 1):
        h = 1 << (lv - 1)
        upper = (t // h) % 2 == 1
        mid_t = (t // h) * h
        blocks_d.append(upper & (r >= mid_t) & (r <= t))
        mid_s = (t // h + 1) * h
        if lv > 1:
            blocks_e.append((~upper) & (r > t) & (r < mid_s))
    stack = np.concatenate(blocks_d + blocks_e + [r <= t, r > t], axis=0).astype(np.float32)
    x = np.bitwise_xor(t, r)
    level = np.where(r == t, 0, np.floor(np.log2(np.maximum(x, 1))).astype(np.int32) + 1)
    level = np.where(r > t, -1, level).astype(np.int32)
    return stack, level, n_lv


def _hgrn_kernel(q_ref, f_ref, i_ref, g_ref, lb_ref, nw_ref, tst_ref, lvl_ref, o_ref, st_ref,
                 *, ts, c, n_lv):
    nt = (((1,), (1,)), ((), ()))
    d = LANES
    nh = q_ref.shape[1] // d
    nch = ts // c
    n_small = HG_SMALL_LEVELS

    @pl.when(pl.program_id(2) == 0)
    def _():
        st_ref[...] = jnp.zeros(st_ref.shape, F32)

    lb = lb_ref[...]
    f = lb + (1.0 - lb) * jax.nn.sigmoid(f_ref[...])
    logf = jnp.log(f)
    kk = 1.0 - f
    units = [(ch, hd) for ch in range(nch) for hd in range(nh)]
    lcat = jnp.concatenate([logf[ch * c:(ch + 1) * c, hd * d:(hd + 1) * d] for ch, hd in units], axis=1)
    seg = _exact_bf16_dot(tst_ref[...], lcat)
    lvl = lvl_ref[...]
    nw = nw_ref[...]

    st = [st_ref[hd] for hd in range(nh)]
    for u, (ch, hd) in enumerate(units):
        rows = slice(ch * c, (ch + 1) * c)
        cols = slice(hd * d, (hd + 1) * d)

        def blk(b, u=u):
            return seg[b * c:(b + 1) * c, u * d:(u + 1) * d]

        g_in = blk(2 * n_small - 1)
        g_out = blk(2 * n_small)
        q = q_ref[rows, cols]
        q32 = q.astype(F32)
        kc = kk[rows, cols]
        vb = i_ref[rows, cols].astype(BF16)
        a = jnp.where(lvl == 0, lax.dot_general(q, kc.astype(BF16), nt, preferred_element_type=F32), 0.0)
        for lv in range(1, n_lv + 1):
            if lv <= n_small:
                d_up = blk(lv - 1)
                d_lo = blk(n_small + lv - 2) if lv > 1 else None
            else:
                h = 1 << (lv - 1)
                ref = jnp.concatenate(
                    [jnp.broadcast_to(g_in[gi * 2 * h + h - 1:gi * 2 * h + h, :], (2 * h, d))
                     for gi in range(c // (2 * h))], axis=0)
                d_up = jnp.minimum(g_in - ref, 0.0)
                d_lo = jnp.minimum(ref - g_in, 0.0)
            qt = (q32 * jnp.exp(d_up)).astype(BF16)
            kt = (kc if d_lo is None else kc * jnp.exp(d_lo)).astype(BF16)
            a = jnp.where(lvl == lv, lax.dot_general(qt, kt, nt, preferred_element_type=F32), a)
        qg = (q32 * jnp.exp(g_in)).astype(BF16)
        o = (jnp.dot(a.astype(BF16), vb, preferred_element_type=F32)
             + lax.dot_general(qg, st[hd].astype(BF16), nt, preferred_element_type=F32))
        kd = (kc * jnp.exp(g_out)).astype(BF16)
        upd = lax.dot_general(vb, kd, (((0,), (0,)), ((), ())), preferred_element_type=F32)
        st[hd] = st[hd] * jnp.exp(g_in[c - 1:c]) + upd
        g = g_ref[rows, cols]
        o_ref[rows, cols] = (_rms_rows(o, nw) * (g * jax.nn.sigmoid(g))).astype(o_ref.dtype)
    for hd in range(nh):
        st_ref[hd] = st[hd]


def hgrn2(proj_a, proj_b, lb, nw, batch, seq, ts, q_col0):
    ns = seq // ts
    d = LANES
    nh = HG_HEADS_PER_STEP
    assert HG_HEADS % nh == 0 and q_col0 % nh == 0
    w = nh * d
    ng = HG_HEADS // nh
    c = min(HG_CHUNK, ts)
    stack, level, n_lv = _hgrn_tables(c)
    assert n_lv >= HG_SMALL_LEVELS
    kern = functools.partial(_hgrn_kernel, ts=ts, c=c, n_lv=n_lv)
    return pl.pallas_call(
        kern,
        grid=(batch, ng, ns),
        in_specs=[
            pl.BlockSpec((ts, w), lambda b, h, i: (b * ns + i, q_col0 // nh + h)),
            pl.BlockSpec((ts, w), lambda b, h, i: (b * ns + i, h)),
            pl.BlockSpec((ts, w), lambda b, h, i: (b * ns + i, ng + h)),
            pl.BlockSpec((ts, w), lambda b, h, i: (b * ns + i, 2 * ng + h)),
            pl.BlockSpec((1, w), lambda b, h, i: (0, h)),
            pl.BlockSpec((1, d), lambda b, h, i: (0, 0)),
            pl.BlockSpec((stack.shape[0], 3 * stack.shape[1]), lambda b, h, i: (0, 0)),
            pl.BlockSpec(level.shape, lambda b, h, i: (0, 0)),
        ],
        out_specs=pl.BlockSpec((ts, w), lambda b, h, i: (b * ns + i, h)),
        out_shape=jax.ShapeDtypeStruct((batch * seq, HG_HEADS * d), BF16),
        scratch_shapes=[pltpu.VMEM((nh, d, d), F32)],
        compiler_params=_params("parallel", "parallel", "arbitrary"),
        name="hgrn2",
    )(proj_a, proj_b, proj_b, proj_b, lb, nw.reshape(1, d),
      jnp.asarray(np.concatenate([stack] * 3, axis=1), BF16), jnp.asarray(level))


def _mem_attn_kernel(mq_ref, k_ref, v_ref, w_ref, res_ref, o_ref, a_ref, *, scale):
    @pl.when(pl.program_id(1) == 0)
    def _():
        dh = a_ref.shape[1] // MEM_HEADS
        for hd in range(MEM_HEADS):
            sl = slice(hd * dh, (hd + 1) * dh)
            s = lax.dot_general(mq_ref[:, sl], k_ref[:, sl], (((1,), (1,)), ((), ())),
                                preferred_element_type=F32) * scale
            e = jnp.exp(s - jnp.max(s, axis=-1, keepdims=True))
            p = e / jnp.sum(e, axis=-1, keepdims=True)
            a_ref[:, sl] = jnp.dot(p.astype(BF16), v_ref[:, sl],
                                   preferred_element_type=F32).astype(a_ref.dtype)

    o_ref[...] = res_ref[...] + jnp.dot(a_ref[...], w_ref[...], preferred_element_type=F32)


def mem_attention_out(mq, mkv, w_mo, res, seq, n_mem):
    t, d = mq.shape
    tm, tn = _pick(seq, MM_TM), _pick(d, MM_TN)
    per_batch = seq // tm
    scale = (d // MEM_HEADS) ** -0.5
    return pl.pallas_call(
        functools.partial(_mem_attn_kernel, scale=scale),
        grid=(t // tm, d // tn),
        in_specs=[
            pl.BlockSpec((tm, d), lambda i, j: (i, 0)),
            pl.BlockSpec((n_mem, d), lambda i, j: (i // per_batch, 0)),
            pl.BlockSpec((n_mem, d), lambda i, j: (i // per_batch, 1)),
            pl.BlockSpec((d, tn), lambda i, j: (0, j)),
            pl.BlockSpec((tm, tn), lambda i, j: (i, j)),
        ],
        out_specs=pl.BlockSpec((tm, tn), lambda i, j: (i, j)),
        out_shape=jax.ShapeDtypeStruct((t, d), F32),
        scratch_shapes=[pltpu.VMEM((tm, d), BF16)],
        compiler_params=_params("parallel", "arbitrary"),
        name="mem_attention_out",
    )(mq, mkv, mkv, w_mo, res)


def _ffn_up_kernel(x_ref, halo_ref, nw_ref, wa_ref, wb_ref, cwa_ref, cwb_ref, cba_ref, cbb_ref,
                   o_ref, h_ref, u_ref, *, tm, per_batch):
    hl = SUBLANES

    @pl.when(pl.program_id(1) == 0)
    def _():
        nw = nw_ref[...]
        first = (pl.program_id(0) % per_batch) == 0
        halo = _rms_rows(halo_ref[...], nw)
        h_ref[0:hl, :] = jnp.where(first, 0.0, halo).astype(h_ref.dtype)
        h_ref[hl:, :] = _rms_rows(x_ref[...], nw).astype(h_ref.dtype)

    h = h_ref[...]
    tf = o_ref.shape[1]
    cw = min(FFN_COL_CHUNK, tf)

    def cols(cc):
        return slice(cc * cw, (cc + 1) * cw)

    def matmuls(cc):
        u_ref[cc, 0] = jnp.dot(h, wa_ref[:, cols(cc)], preferred_element_type=F32)
        u_ref[cc, 1] = jnp.dot(h, wb_ref[:, cols(cc)], preferred_element_type=F32)

    def conv(cc, half, cw_ref, cb_ref):
        acc = cb_ref[:, cols(cc)]
        for j in range(CONV_WIDTH):
            off = hl - (CONV_WIDTH - 1) + j
            acc = acc + cw_ref[j:j + 1, cols(cc)] * u_ref[cc, half, off:off + tm, :]
        return acc

    def gate(cc):
        a = conv(cc, 0, cwa_ref, cba_ref)
        b = conv(cc, 1, cwb_ref, cbb_ref)
        o_ref[:, cols(cc)] = ((a * jax.nn.sigmoid(a)) * b).astype(o_ref.dtype)

    n_chunks = tf // cw
    matmuls(0)
    for cc in range(1, n_chunks):
        matmuls(cc)
        gate(cc - 1)
    gate(n_chunks - 1)


def ffn_up(x, nw, w_up, conv_w, conv_b, seq):
    t, d = x.shape
    f = w_up.shape[1] // 2
    tm, tf = _pick(seq, MM_TM), _pick(f, FFN_TF)
    nf = f // tf
    per_batch = seq // tm
    hl = SUBLANES
    rows_per_tile = tm // hl
    kern = functools.partial(_ffn_up_kernel, tm=tm, per_batch=per_batch)
    return pl.pallas_call(
        kern,
        grid=(t // tm, nf),
        in_specs=[
            pl.BlockSpec((tm, d), lambda i, j: (i, 0)),
            pl.BlockSpec((hl, d), lambda i, j: (jnp.maximum(i * rows_per_tile - 1, 0), 0)),
            pl.BlockSpec((1, d), lambda i, j: (0, 0)),
            pl.BlockSpec((d, tf), lambda i, j: (0, j)),
            pl.BlockSpec((d, tf), lambda i, j: (0, nf + j)),
            pl.BlockSpec((CONV_WIDTH, tf), lambda i, j: (0, j)),
            pl.BlockSpec((CONV_WIDTH, tf), lambda i, j: (0, nf + j)),
            pl.BlockSpec((1, tf), lambda i, j: (0, j)),
            pl.BlockSpec((1, tf), lambda i, j: (0, nf + j)),
        ],
        out_specs=pl.BlockSpec((tm, tf), lambda i, j: (i, j)),
        out_shape=jax.ShapeDtypeStruct((t, f), BF16),
        scratch_shapes=[
            pltpu.VMEM((tm + hl, d), BF16),
            pltpu.VMEM((tf // min(FFN_COL_CHUNK, tf), 2, tm + hl, min(FFN_COL_CHUNK, tf)), F32),
        ],
        compiler_params=_params("parallel", "arbitrary"),
        name="ffn_up",
    )(x, x, nw.reshape(1, d), w_up, w_up, conv_w, conv_w, conv_b.reshape(1, 2 * f), conv_b.reshape(1, 2 * f))


def _ffn_down_kernel(a_ref, w_ref, res_ref, nw_ref, o_ref):
    k = pl.program_id(1)

    @pl.when(k == 0)
    def _():
        o_ref[...] = res_ref[...]

    o_ref[...] += jnp.dot(a_ref[...], w_ref[...], preferred_element_type=F32)

    @pl.when(k == pl.num_programs(1) - 1)
    def _():
        o_ref[...] = _rms_rows(o_ref[...], nw_ref[...])


def ffn_down(act, w_down, res, nw):
    t, f = act.shape
    d = w_down.shape[1]
    tm, tk = _pick(t, MM_TM), _pick(f, FFN_TF)
    return pl.pallas_call(
        _ffn_down_kernel,
        grid=(t // tm, f // tk),
        in_specs=[
            pl.BlockSpec((tm, tk), lambda i, k: (i, k)),
            pl.BlockSpec((tk, d), lambda i, k: (k, 0)),
            pl.BlockSpec((tm, d), lambda i, k: (i, 0)),
            pl.BlockSpec((1, d), lambda i, k: (0, 0)),
        ],
        out_specs=pl.BlockSpec((tm, d), lambda i, k: (i, 0)),
        out_shape=jax.ShapeDtypeStruct((t, d), F32),
        compiler_params=_params("parallel", "arbitrary"),
        name="ffn_down",
    )(act, w_down, res, nw.reshape(1, d))


def kernel(x, mem, w_in, w_out, norm_mix_w, lam_q1, lam_k1, lam_q2, lam_k2, da_subln_w,
           hg_lb_raw, hg_norm_w, rel_bias, norm_mem_w, mem_kv_norm_w, w_mq, w_mkv, w_mo,
           norm_ffn_w, w_up, conv_w, conv_b, w_down, final_norm_w):
    batch, seq, d_model = x.shape
    n_mem = mem.shape[1]
    depth = w_in.shape[0]
    assert depth == 1, "single-layer block"
    layer = 0
    t = batch * seq

    da_width = DA_HEADS * LANES
    hg_width = HG_HEADS * LANES
    assert w_in.shape[2] == 3 * da_width + 4 * hg_width
    assert da_subln_w.shape[1] == LANES and hg_norm_w.shape[1] == LANES
    assert lam_q1.shape[1] * 2 == LANES
    n_bf = 3 * da_width + hg_width

    tq = _pick(seq, DA_BLOCK)
    ts = _pick(seq, HG_TILE)

    lam_init = 0.8 - 0.6 * math.exp(-0.3 * layer)
    lam_vecs = jnp.stack([lam_q1[layer], lam_k1[layer], lam_q2[layer], lam_k2[layer]]).astype(F32)
    lam_row, lb = param_tables(lam_vecs, hg_lb_raw.astype(F32), lam_init, layer)

    kk = np.arange(tq)[:, None]
    qq = np.arange(tq)[None, :]
    buckets = np.stack([np.where(qq >= kk, _t5_bucket_np(np.maximum(qq - kk, 0)), -1),
                        _t5_bucket_np(tq + qq - kk)]).astype(np.int32)
    far = _t5_bucket_np(np.arange(tq + 1, max(seq, tq + 2)))
    assert (far == far[0]).all(), "bias must be constant beyond the sub-diagonal block"
    bias_t = bias_tiles(rel_bias.astype(F32), jnp.asarray(buckets))

    x2d = x.reshape(t, d_model)
    w_in_l = w_in[layer]
    q_scale = (LANES // 2) ** -0.5
    w_a = jnp.concatenate([w_in_l[:, :da_width] * q_scale, w_in_l[:, da_width:n_bf]], axis=1).astype(BF16)
    w_b = w_in_l[:, n_bf:].astype(BF16)
    proj_a = norm_matmul(x2d, norm_mix_w[layer], w_a, BF16, name="in_proj_a")
    proj_b = norm_matmul(x2d, norm_mix_w[layer], w_b, F32, name="in_proj_b")

    da_o = diff_attention(proj_a, rel_bias.astype(F32), bias_t, lam_row, da_subln_w[layer],
                          batch, seq, tq, int(far[0]), 1.0 - lam_init)
    hg_o = hgrn2(proj_a, proj_b, lb, hg_norm_w[layer], batch, seq, ts, 3 * DA_HEADS)
    x1 = mix_out_proj(da_o, hg_o, w_out[layer].astype(BF16), x2d)

    mq = norm_matmul(x1, norm_mem_w[layer], w_mq[layer].astype(BF16), BF16, name="mem_q_proj")
    mkv = norm_matmul(mem.reshape(batch * n_mem, d_model), mem_kv_norm_w[layer],
                      w_mkv[layer].astype(BF16), BF16, name="mem_kv_proj")
    x2 = mem_attention_out(mq, mkv, w_mo[layer].astype(BF16), x1, seq, n_mem)

    act = ffn_up(x2, norm_ffn_w[layer], w_up[layer].astype(BF16), conv_w[layer], conv_b[layer], seq)
    out = ffn_down(act, w_down[layer].astype(BF16), x2, final_norm_w)
    return out.reshape(batch, seq, d_model)
```

```python
import functools
import math

import numpy as np
import jax
import jax.numpy as jnp
from jax import lax
from jax.experimental import pallas as pl
from jax.experimental.pallas import tpu as pltpu

DA_HEADS = 8
HG_HEADS = 8
MEM_HEADS = 4
NUM_BUCKETS = 32
MAX_DISTANCE = 128
CONV_WIDTH = 3
EPS = 1e-6

LANES = 128
SUBLANES = 8
VMEM_LIMIT_BYTES = 56 * 1024 * 1024

HG_CHUNK = 128
HG_TILE = 256
HG_HEADS_PER_STEP = 2
HG_SMALL_LEVELS = 3
DA_BLOCK = 512
DA_Q_SUB = 256
MM_TM = 1024
MM_TN = 1024
FFN_TF = 512
FFN_COL_CHUNK = 256
NEG_BIG = -1e30

F32 = jnp.float32
BF16 = jnp.bfloat16


def _params(*sem, flags=None):
    return pltpu.CompilerParams(dimension_semantics=sem, vmem_limit_bytes=VMEM_LIMIT_BYTES, flags=flags)


def _rms_rows(x, w):
    ms = jnp.mean(x * x, axis=-1, keepdims=True)
    return (x * lax.rsqrt(ms + EPS)) * w


def _pick(n, pref):
    t = min(pref, n)
    while n % t:
        t //= 2
    return t


def _norm_mm_kernel(x_ref, nw_ref, w_ref, o_ref, h_ref):
    @pl.when(pl.program_id(1) == 0)
    def _():
        h_ref[...] = _rms_rows(x_ref[...], nw_ref[...]).astype(h_ref.dtype)

    o_ref[...] = jnp.dot(h_ref[...], w_ref[...], preferred_element_type=F32).astype(o_ref.dtype)


def norm_matmul(x, nw, w, out_dtype, name="norm_matmul"):
    t, k = x.shape
    n = w.shape[1]
    tm, tn = _pick(t, MM_TM), _pick(n, MM_TN)
    return pl.pallas_call(
        _norm_mm_kernel,
        grid=(t // tm, n // tn),
        in_specs=[
            pl.BlockSpec((tm, k), lambda i, j: (i, 0)),
            pl.BlockSpec((1, k), lambda i, j: (0, 0)),
            pl.BlockSpec((k, tn), lambda i, j: (0, j)),
        ],
        out_specs=pl.BlockSpec((tm, tn), lambda i, j: (i, j)),
        out_shape=jax.ShapeDtypeStruct((t, n), out_dtype),
        scratch_shapes=[pltpu.VMEM((tm, k), BF16)],
        compiler_params=_params("parallel", "arbitrary"),
        name=name,
    )(x, nw.reshape(1, k), w)


def _mix_out_kernel(a0_ref, a1_ref, w0_ref, w1_ref, res_ref, o_ref):
    acc = res_ref[...]
    acc = acc + jnp.dot(a0_ref[...], w0_ref[...], preferred_element_type=F32)
    acc = acc + jnp.dot(a1_ref[...], w1_ref[...], preferred_element_type=F32)
    o_ref[...] = acc


def mix_out_proj(a0, a1, w, res):
    t, k0 = a0.shape
    k1 = a1.shape[1]
    assert k0 == k1 and w.shape[0] == k0 + k1
    n = w.shape[1]
    tm, tn = _pick(t, MM_TM), _pick(n, MM_TN)
    return pl.pallas_call(
        _mix_out_kernel,
        grid=(t // tm, n // tn),
        in_specs=[
            pl.BlockSpec((tm, k0), lambda i, j: (i, 0)),
            pl.BlockSpec((tm, k1), lambda i, j: (i, 0)),
            pl.BlockSpec((k0, tn), lambda i, j: (0, j)),
            pl.BlockSpec((k1, tn), lambda i, j: (1, j)),
            pl.BlockSpec((tm, tn), lambda i, j: (i, j)),
        ],
        out_specs=pl.BlockSpec((tm, tn), lambda i, j: (i, j)),
        out_shape=jax.ShapeDtypeStruct((t, n), F32),
        compiler_params=_params("parallel", "arbitrary"),
        name="mix_out_proj",
    )(a0, a1, w, w, res)


def _t5_bucket_np(n):
    max_exact = NUM_BUCKETS // 2
    nf = np.maximum(n, 1).astype(np.float32)
    large = max_exact + (np.log(nf / max_exact) / math.log(MAX_DISTANCE / max_exact)
                         * (NUM_BUCKETS - max_exact)).astype(np.int32)
    large = np.minimum(large, NUM_BUCKETS - 1)
    return np.where(n < max_exact, n, large).astype(np.int32)


def _tables_kernel(lamv_ref, lbraw_ref, lam_ref, lb_ref, *, lam_init, layer):
    v = lamv_ref[...]
    s1 = jnp.sum(v[0:1] * v[1:2], axis=-1, keepdims=True)
    s2 = jnp.sum(v[2:3] * v[3:4], axis=-1, keepdims=True)
    lam = jnp.exp(s1) - jnp.exp(s2) + lam_init
    lam_ref[...] = jnp.broadcast_to(lam, lam_ref.shape)
    raw = lbraw_ref[...]
    e = jnp.exp(raw - jnp.max(raw, axis=0, keepdims=True))
    sm = e / jnp.sum(e, axis=0, keepdims=True)
    lb_ref[...] = jnp.sum(sm[0:layer + 1], axis=0, keepdims=True)


def param_tables(lam_vecs, lb_raw, lam_init, layer):
    kw = lb_raw.shape[1]
    return pl.pallas_call(
        functools.partial(_tables_kernel, lam_init=lam_init, layer=layer),
        out_shape=(jax.ShapeDtypeStruct((1, LANES), F32), jax.ShapeDtypeStruct((1, kw), F32)),
        name="param_tables",
    )(lam_vecs, lb_raw)


def _bias_tiles_kernel(rb_ref, bk_ref, o_ref, *, tq, far_bucket):
    h = pl.program_id(0)
    bk = bk_ref[...]
    vec = jnp.full(bk.shape, NEG_BIG, F32)
    for n in range(NUM_BUCKETS):
        vec = jnp.where(bk == n, rb_ref[n, h], vec)
    for t in range(2):
        rows = jnp.broadcast_to(vec[t:t + 1], (tq, 2 * tq))
        o_ref[0, t] = pltpu.roll(rows, 0, 1, stride=1, stride_axis=0)[:, tq:]
    o_ref[0, FAR_TILE] = jnp.full((tq, tq), rb_ref[far_bucket, h], F32)


def bias_tiles(rel_bias, dist_buckets, tq, far_bucket):
    nb, h = rel_bias.shape
    return pl.pallas_call(
        functools.partial(_bias_tiles_kernel, tq=tq, far_bucket=far_bucket),
        grid=(h,),
        in_specs=[
            pl.BlockSpec(memory_space=pltpu.SMEM),
            pl.BlockSpec(dist_buckets.shape, lambda i: (0, 0)),
        ],
        out_specs=pl.BlockSpec((1, FAR_TILE + 1, tq, tq), lambda i: (i, 0, 0, 0)),
        out_shape=jax.ShapeDtypeStruct((h, FAR_TILE + 1, tq, tq), F32),
        compiler_params=_params("arbitrary"),
        name="bias_tiles",
    )(rel_bias, dist_buckets)


ONES_ROWS = 16
FAR_TILE = 2


def _diff_attn_kernel(q_ref, k_ref, v_ref, bias_ref, lam_ref, nw_ref, o_ref,
                      vt_ref, m_ref, acc_ref, st_ref, *, tq, out_scale):
    i = pl.program_id(2)
    dv = v_ref.shape[1]
    dh = dv // 2
    nkv = v_ref.shape[0] // tq

    @pl.when(i == 0)
    def _():
        for c in range(nkv):
            vt_ref[c, 0:dv, :] = v_ref[c * tq:(c + 1) * tq, :].astype(F32).T.astype(BF16)
            vt_ref[c, dv:, :] = jnp.ones((ONES_ROWS, tq), BF16)

    q = q_ref[...]
    lane = lax.broadcasted_iota(jnp.int32, (1, dv), 1)
    qm = (jnp.where(lane < dh, q, jnp.zeros_like(q)), jnp.where(lane >= dh, q, jnp.zeros_like(q)))

    m_ref[...] = jnp.full(m_ref.shape, NEG_BIG, F32)
    acc_ref[...] = jnp.zeros(acc_ref.shape, F32)

    qw = min(DA_Q_SUB, tq)
    chains = [(mp, slice(qs * qw, (qs + 1) * qw)) for mp in range(2) for qs in range(tq // qw)]

    def scores(n, j):
        mp, cols = chains[n]
        kb = k_ref[pl.ds(pl.multiple_of(j * tq, tq), tq), :]
        st_ref[n] = lax.dot_general(kb, qm[mp][cols, :], (((1,), (1,)), ((), ())),
                                    preferred_element_type=F32)

    def softmax_pv(n, vt, tile):
        mp, cols = chains[n]
        st = st_ref[n] + bias_ref[0, tile, :, cols]
        m_old = m_ref[mp, :, cols]
        m_new = jnp.maximum(m_old, jnp.max(st, axis=0, keepdims=True))
        p = jnp.exp(st - m_new).astype(BF16)
        alpha = jnp.exp(m_old - m_new)
        acc_ref[mp, :, cols] = (alpha * acc_ref[mp, :, cols]
                                + jnp.dot(vt, p, preferred_element_type=F32))
        m_ref[mp, :, cols] = m_new

    def step(j, carry):
        vt = vt_ref[j]
        tile = jnp.minimum(i - j, FAR_TILE)
        for n in range(len(chains)):
            if n + 1 < len(chains):
                scores(n + 1, j)
            else:
                scores(0, jnp.minimum(j + 1, i))
            softmax_pv(n, vt, tile)
        return carry

    scores(0, 0)
    lax.fori_loop(0, i + 1, step, 0)

    o0 = acc_ref[0, 0:dv, :] / acc_ref[0, dv:dv + 1, :]
    o1 = acc_ref[1, 0:dv, :] / acc_ref[1, dv:dv + 1, :]
    o = o0 - lam_ref[0:1, 0:1] * o1
    ms = jnp.mean(o * o, axis=0, keepdims=True)
    y = (o * lax.rsqrt(ms + EPS)) * (nw_ref[...] * out_scale)
    o_ref[...] = y.T.astype(o_ref.dtype)


def diff_attention(proj, bias_t, lam_row, nw, batch, seq, tq, out_scale):
    nq = seq // tq
    dv = LANES
    kern = functools.partial(_diff_attn_kernel, tq=tq, out_scale=out_scale)
    return pl.pallas_call(
        kern,
        grid=(batch, DA_HEADS, nq),
        in_specs=[
            pl.BlockSpec((tq, dv), lambda b, h, i: (b * nq + i, h)),
            pl.BlockSpec((seq, dv), lambda b, h, i: (b, DA_HEADS + h)),
            pl.BlockSpec((seq, dv), lambda b, h, i: (b, 2 * DA_HEADS + h)),
            pl.BlockSpec((1, FAR_TILE + 1, tq, tq), lambda b, h, i: (h, 0, 0, 0)),
            pl.BlockSpec((1, LANES), lambda b, h, i: (0, 0)),
            pl.BlockSpec((dv, 1), lambda b, h, i: (0, 0)),
        ],
        out_specs=pl.BlockSpec((tq, dv), lambda b, h, i: (b * nq + i, h)),
        out_shape=jax.ShapeDtypeStruct((batch * seq, DA_HEADS * dv), BF16),
        scratch_shapes=[
            pltpu.VMEM((nq, dv + ONES_ROWS, tq), BF16),
            pltpu.VMEM((2, 1, tq), F32),
            pltpu.VMEM((2, dv + ONES_ROWS, tq), F32),
            pltpu.VMEM((2 * tq // min(DA_Q_SUB, tq), tq, min(DA_Q_SUB, tq)), F32),
        ],
        compiler_params=_params("parallel", "parallel", "arbitrary"),
        name="diff_attention",
    )(proj, proj, proj, bias_t, lam_row, nw.reshape(dv, 1))


def _exact_bf16_dot(tri3, x):
    hi = x.astype(BF16)
    r1 = x - hi.astype(F32)
    mid = r1.astype(BF16)
    lo = (r1 - mid.astype(F32)).astype(BF16)
    return jnp.dot(tri3, jnp.concatenate([hi, mid, lo], axis=0), preferred_element_type=F32)


def _hgrn_tables(c):
    n_lv = int(math.log2(c))
    r = np.arange(c)[None, :]
    t = np.arange(c)[:, None]
    blocks_d, blocks_e = [], []
    for lv in range(1, HG_SMALL_LEVELS + 1):
        h = 1 << (lv - 1)
        upper = (t // h) % 2 == 1
        mid_t = (t // h) * h
        blocks_d.append(upper & (r >= mid_t) & (r <= t))
        mid_s = (t // h + 1) * h
        if lv > 1:
            blocks_e.append((~upper) & (r > t) & (r < mid_s))
    stack = np.concatenate(blocks_d + blocks_e + [r <= t, r > t], axis=0).astype(np.float32)
    x = np.bitwise_xor(t, r)
    level = np.where(r == t, 0, np.floor(np.log2(np.maximum(x, 1))).astype(np.int32) + 1)
    level = np.where(r > t, -1, level).astype(np.int32)
    return stack, level, n_lv


def _hgrn_kernel(q_ref, f_ref, i_ref, g_ref, lb_ref, nw_ref, tst_ref, lvl_ref, o_ref, st_ref,
                 *, ts, c, n_lv):
    nt = (((1,), (1,)), ((), ()))
    d = LANES
    nh = q_ref.shape[1] // d
    nch = ts // c
    n_small = HG_SMALL_LEVELS

    @pl.when(pl.program_id(2) == 0)
    def _():
        st_ref[...] = jnp.zeros(st_ref.shape, F32)

    lb = lb_ref[...]
    f = lb + (1.0 - lb) * jax.nn.sigmoid(f_ref[...])
    logf = jnp.log(f)
    kk = 1.0 - f
    units = [(ch, hd) for ch in range(nch) for hd in range(nh)]
    lcat = jnp.concatenate([logf[ch * c:(ch + 1) * c, hd * d:(hd + 1) * d] for ch, hd in units], axis=1)
    seg = _exact_bf16_dot(tst_ref[...], lcat)
    lvl = lvl_ref[...]
    nw = nw_ref[...]

    st = [st_ref[hd] for hd in range(nh)]
    for u, (ch, hd) in enumerate(units):
        rows = slice(ch * c, (ch + 1) * c)
        cols = slice(hd * d, (hd + 1) * d)

        def blk(b, u=u):
            return seg[b * c:(b + 1) * c, u * d:(u + 1) * d]

        g_in = blk(2 * n_small - 1)
        g_out = blk(2 * n_small)
        q = q_ref[rows, cols]
        q32 = q.astype(F32)
        kc = kk[rows, cols]
        vb = i_ref[rows, cols].astype(BF16)
        a = jnp.where(lvl == 0, lax.dot_general(q, kc.astype(BF16), nt, preferred_element_type=F32), 0.0)
        for lv in range(1, n_lv + 1):
            if lv <= n_small:
                d_up = blk(lv - 1)
                d_lo = blk(n_small + lv - 2) if lv > 1 else None
            else:
                h = 1 << (lv - 1)
                ref = jnp.concatenate(
                    [jnp.broadcast_to(g_in[gi * 2 * h + h - 1:gi * 2 * h + h, :], (2 * h, d))
                     for gi in range(c // (2 * h))], axis=0)
                d_up = jnp.minimum(g_in - ref, 0.0)
                d_lo = jnp.minimum(ref - g_in, 0.0)
            qt = (q32 * jnp.exp(d_up)).astype(BF16)
            kt = (kc if d_lo is None else kc * jnp.exp(d_lo)).astype(BF16)
            a = jnp.where(lvl == lv, lax.dot_general(qt, kt, nt, preferred_element_type=F32), a)
        qg = (q32 * jnp.exp(g_in)).astype(BF16)
        o = (jnp.dot(a.astype(BF16), vb, preferred_element_type=F32)
             + lax.dot_general(qg, st[hd].astype(BF16), nt, preferred_element_type=F32))
        kd = (kc * jnp.exp(g_out)).astype(BF16)
        upd = lax.dot_general(vb, kd, (((0,), (0,)), ((), ())), preferred_element_type=F32)
        st[hd] = st[hd] * jnp.exp(g_in[c - 1:c]) + upd
        g = g_ref[rows, cols]
        o_ref[rows, cols] = (_rms_rows(o, nw) * (g * jax.nn.sigmoid(g))).astype(o_ref.dtype)
    for hd in range(nh):
        st_ref[hd] = st[hd]


def hgrn2(proj_a, proj_b, lb, nw, batch, seq, ts, q_col0):
    ns = seq // ts
    d = LANES
    nh = HG_HEADS_PER_STEP
    assert HG_HEADS % nh == 0 and q_col0 % nh == 0
    w = nh * d
    ng = HG_HEADS // nh
    c = min(HG_CHUNK, ts)
    stack, level, n_lv = _hgrn_tables(c)
    assert n_lv >= HG_SMALL_LEVELS
    kern = functools.partial(_hgrn_kernel, ts=ts, c=c, n_lv=n_lv)
    return pl.pallas_call(
        kern,
        grid=(batch, ng, ns),
        in_specs=[
            pl.BlockSpec((ts, w), lambda b, h, i: (b * ns + i, q_col0 // nh + h)),
            pl.BlockSpec((ts, w), lambda b, h, i: (b * ns + i, h)),
            pl.BlockSpec((ts, w), lambda b, h, i: (b * ns + i, ng + h)),
            pl.BlockSpec((ts, w), lambda b, h, i: (b * ns + i, 2 * ng + h)),
            pl.BlockSpec((1, w), lambda b, h, i: (0, h)),
            pl.BlockSpec((1, d), lambda b, h, i: (0, 0)),
            pl.BlockSpec((stack.shape[0], 3 * stack.shape[1]), lambda b, h, i: (0, 0)),
            pl.BlockSpec(level.shape, lambda b, h, i: (0, 0)),
        ],
        out_specs=pl.BlockSpec((ts, w), lambda b, h, i: (b * ns + i, h)),
        out_shape=jax.ShapeDtypeStruct((batch * seq, HG_HEADS * d), BF16),
        scratch_shapes=[pltpu.VMEM((nh, d, d), F32)],
        compiler_params=_params("parallel", "parallel", "arbitrary"),
        name="hgrn2",
    )(proj_a, proj_b, proj_b, proj_b, lb, nw.reshape(1, d),
      jnp.asarray(np.concatenate([stack] * 3, axis=1), BF16), jnp.asarray(level))


def _mem_attn_kernel(mq_ref, k_ref, v_ref, w_ref, res_ref, o_ref, a_ref, *, scale):
    @pl.when(pl.program_id(1) == 0)
    def _():
        dh = a_ref.shape[1] // MEM_HEADS
        for hd in range(MEM_HEADS):
            sl = slice(hd * dh, (hd + 1) * dh)
            s = lax.dot_general(mq_ref[:, sl], k_ref[:, sl], (((1,), (1,)), ((), ())),
                                preferred_element_type=F32) * scale
            e = jnp.exp(s - jnp.max(s, axis=-1, keepdims=True))
            p = e / jnp.sum(e, axis=-1, keepdims=True)
            a_ref[:, sl] = jnp.dot(p.astype(BF16), v_ref[:, sl],
                                   preferred_element_type=F32).astype(a_ref.dtype)

    o_ref[...] = res_ref[...] + jnp.dot(a_ref[...], w_ref[...], preferred_element_type=F32)


def mem_attention_out(mq, mkv, w_mo, res, seq, n_mem):
    t, d = mq.shape
    tm, tn = _pick(seq, MM_TM), _pick(d, MM_TN)
    per_batch = seq // tm
    scale = (d // MEM_HEADS) ** -0.5
    return pl.pallas_call(
        functools.partial(_mem_attn_kernel, scale=scale),
        grid=(t // tm, d // tn),
        in_specs=[
            pl.BlockSpec((tm, d), lambda i, j: (i, 0)),
            pl.BlockSpec((n_mem, d), lambda i, j: (i // per_batch, 0)),
            pl.BlockSpec((n_mem, d), lambda i, j: (i // per_batch, 1)),
            pl.BlockSpec((d, tn), lambda i, j: (0, j)),
            pl.BlockSpec((tm, tn), lambda i, j: (i, j)),
        ],
        out_specs=pl.BlockSpec((tm, tn), lambda i, j: (i, j)),
        out_shape=jax.ShapeDtypeStruct((t, d), F32),
        scratch_shapes=[pltpu.VMEM((tm, d), BF16)],
        compiler_params=_params("parallel", "arbitrary"),
        name="mem_attention_out",
    )(mq, mkv, mkv, w_mo, res)


def _ffn_up_kernel(x_ref, halo_ref, nw_ref, wa_ref, wb_ref, cwa_ref, cwb_ref, cba_ref, cbb_ref,
                   o_ref, h_ref, u_ref, *, tm, per_batch):
    hl = SUBLANES

    @pl.when(pl.program_id(1) == 0)
    def _():
        nw = nw_ref[...]
        first = (pl.program_id(0) % per_batch) == 0
        halo = _rms_rows(halo_ref[...], nw)
        h_ref[0:hl, :] = jnp.where(first, 0.0, halo).astype(h_ref.dtype)
        h_ref[hl:, :] = _rms_rows(x_ref[...], nw).astype(h_ref.dtype)

    h = h_ref[...]
    tf = o_ref.shape[1]
    cw = min(FFN_COL_CHUNK, tf)

    def cols(cc):
        return slice(cc * cw, (cc + 1) * cw)

    def matmuls(cc):
        u_ref[cc, 0] = jnp.dot(h, wa_ref[:, cols(cc)], preferred_element_type=F32)
        u_ref[cc, 1] = jnp.dot(h, wb_ref[:, cols(cc)], preferred_element_type=F32)

    def conv(cc, half, cw_ref, cb_ref):
        acc = cb_ref[:, cols(cc)]
        for j in range(CONV_WIDTH):
            off = hl - (CONV_WIDTH - 1) + j
            acc = acc + cw_ref[j:j + 1, cols(cc)] * u_ref[cc, half, off:off + tm, :]
        return acc

    def gate(cc):
        a = conv(cc, 0, cwa_ref, cba_ref)
        b = conv(cc, 1, cwb_ref, cbb_ref)
        o_ref[:, cols(cc)] = ((a * jax.nn.sigmoid(a)) * b).astype(o_ref.dtype)

    n_chunks = tf // cw
    matmuls(0)
    for cc in range(1, n_chunks):
        matmuls(cc)
        gate(cc - 1)
    gate(n_chunks - 1)


def ffn_up(x, nw, w_up, conv_w, conv_b, seq):
    t, d = x.shape
    f = w_up.shape[1] // 2
    tm, tf = _pick(seq, MM_TM), _pick(f, FFN_TF)
    nf = f // tf
    per_batch = seq // tm
    hl = SUBLANES
    rows_per_tile = tm // hl
    kern = functools.partial(_ffn_up_kernel, tm=tm, per_batch=per_batch)
    return pl.pallas_call(
        kern,
        grid=(t // tm, nf),
        in_specs=[
            pl.BlockSpec((tm, d), lambda i, j: (i, 0)),
            pl.BlockSpec((hl, d), lambda i, j: (jnp.maximum(i * rows_per_tile - 1, 0), 0)),
            pl.BlockSpec((1, d), lambda i, j: (0, 0)),
            pl.BlockSpec((d, tf), lambda i, j: (0, j)),
            pl.BlockSpec((d, tf), lambda i, j: (0, nf + j)),
            pl.BlockSpec((CONV_WIDTH, tf), lambda i, j: (0, j)),
            pl.BlockSpec((CONV_WIDTH, tf), lambda i, j: (0, nf + j)),
            pl.BlockSpec((1, tf), lambda i, j: (0, j)),
            pl.BlockSpec((1, tf), lambda i, j: (0, nf + j)),
        ],
        out_specs=pl.BlockSpec((tm, tf), lambda i, j: (i, j)),
        out_shape=jax.ShapeDtypeStruct((t, f), BF16),
        scratch_shapes=[
            pltpu.VMEM((tm + hl, d), BF16),
            pltpu.VMEM((tf // min(FFN_COL_CHUNK, tf), 2, tm + hl, min(FFN_COL_CHUNK, tf)), F32),
        ],
        compiler_params=_params("parallel", "arbitrary"),
        name="ffn_up",
    )(x, x, nw.reshape(1, d), w_up, w_up, conv_w, conv_w, conv_b.reshape(1, 2 * f), conv_b.reshape(1, 2 * f))


def _ffn_down_kernel(a_ref, w_ref, res_ref, nw_ref, o_ref):
    k = pl.program_id(1)

    @pl.when(k == 0)
    def _():
        o_ref[...] = res_ref[...]

    o_ref[...] += jnp.dot(a_ref[...], w_ref[...], preferred_element_type=F32)

    @pl.when(k == pl.num_programs(1) - 1)
    def _():
        o_ref[...] = _rms_rows(o_ref[...], nw_ref[...])


def ffn_down(act, w_down, res, nw):
    t, f = act.shape
    d = w_down.shape[1]
    tm, tk = _pick(t, MM_TM), _pick(f, FFN_TF)
    return pl.pallas_call(
        _ffn_down_kernel,
        grid=(t // tm, f // tk),
        in_specs=[
            pl.BlockSpec((tm, tk), lambda i, k: (i, k)),
            pl.BlockSpec((tk, d), lambda i, k: (k, 0)),
            pl.BlockSpec((tm, d), lambda i, k: (i, 0)),
            pl.BlockSpec((1, d), lambda i, k: (0, 0)),
        ],
        out_specs=pl.BlockSpec((tm, d), lambda i, k: (i, 0)),
        out_shape=jax.ShapeDtypeStruct((t, d), F32),
        compiler_params=_params("parallel", "arbitrary"),
        name="ffn_down",
    )(act, w_down, res, nw.reshape(1, d))


def kernel(x, mem, w_in, w_out, norm_mix_w, lam_q1, lam_k1, lam_q2, lam_k2, da_subln_w,
           hg_lb_raw, hg_norm_w, rel_bias, norm_mem_w, mem_kv_norm_w, w_mq, w_mkv, w_mo,
           norm_ffn_w, w_up, conv_w, conv_b, w_down, final_norm_w):
    batch, seq, d_model = x.shape
    n_mem = mem.shape[1]
    depth = w_in.shape[0]
    assert depth == 1, "single-layer block"
    layer = 0
    t = batch * seq

    da_width = DA_HEADS * LANES
    hg_width = HG_HEADS * LANES
    assert w_in.shape[2] == 3 * da_width + 4 * hg_width
    assert da_subln_w.shape[1] == LANES and hg_norm_w.shape[1] == LANES
    assert lam_q1.shape[1] * 2 == LANES
    n_bf = 3 * da_width + hg_width

    tq = _pick(seq, DA_BLOCK)
    ts = _pick(seq, HG_TILE)

    lam_init = 0.8 - 0.6 * math.exp(-0.3 * layer)
    lam_vecs = jnp.stack([lam_q1[layer], lam_k1[layer], lam_q2[layer], lam_k2[layer]]).astype(F32)
    lam_row, lb = param_tables(lam_vecs, hg_lb_raw.astype(F32), lam_init, layer)

    far = _t5_bucket_np(np.arange(tq + 1, max(seq, tq + 2)))
    assert (far == far[0]).all(), "bias must be constant beyond the sub-diagonal block"
    c = np.arange(2 * tq)
    dist_buckets = np.stack([np.where(c >= tq, _t5_bucket_np(np.maximum(c - tq, 0)), -1),
                             _t5_bucket_np(c)]).astype(np.int32)
    bias_t = bias_tiles(rel_bias.astype(F32), jnp.asarray(dist_buckets), tq, int(far[0]))

    x2d = x.reshape(t, d_model)
    w_in_l = w_in[layer]
    q_scale = (LANES // 2) ** -0.5
    w_a = jnp.concatenate([w_in_l[:, :da_width] * q_scale, w_in_l[:, da_width:n_bf]], axis=1).astype(BF16)
    w_b = w_in_l[:, n_bf:].astype(BF16)
    proj_a = norm_matmul(x2d, norm_mix_w[layer], w_a, BF16, name="in_proj_a")
    proj_b = norm_matmul(x2d, norm_mix_w[layer], w_b, F32, name="in_proj_b")

    da_o = diff_attention(proj_a, bias_t, lam_row, da_subln_w[layer], batch, seq, tq, 1.0 - lam_init)
    hg_o = hgrn2(proj_a, proj_b, lb, hg_norm_w[layer], batch, seq, ts, 3 * DA_HEADS)
    x1 = mix_out_proj(da_o, hg_o, w_out[layer].astype(BF16), x2d)

    mq = norm_matmul(x1, norm_mem_w[layer], w_mq[layer].astype(BF16), BF16, name="mem_q_proj")
    mkv = norm_matmul(mem.reshape(batch * n_mem, d_model), mem_kv_norm_w[layer],
                      w_mkv[layer].astype(BF16), BF16, name="mem_kv_proj")
    x2 = mem_attention_out(mq, mkv, w_mo[layer].astype(BF16), x1, seq, n_mem)

    act = ffn_up(x2, norm_ffn_w[layer], w_up[layer].astype(BF16), conv_w[layer], conv_b[layer], seq)
    out = ffn_down(act, w_down[layer].astype(BF16), x2, final_norm_w)
    return out.reshape(batch, seq, d_model)
```

```python
import functools
import math

import numpy as np
import jax
import jax.numpy as jnp
from jax import lax
from jax.experimental import pallas as pl
from jax.experimental.pallas import tpu as pltpu

DA_HEADS = 8
HG_HEADS = 8
MEM_HEADS = 4
NUM_BUCKETS = 32
MAX_DISTANCE = 128
CONV_WIDTH = 3
EPS = 1e-6

LANES = 128
SUBLANES = 8
VMEM_LIMIT_BYTES = 56 * 1024 * 1024

HG_CHUNK = 128
HG_TILE = 256
HG_HEADS_PER_STEP = 2
HG_SMALL_LEVELS = 3
DA_BLOCK = 512
DA_Q_SUB = 256
MM_TM = 1024
MM_TN = 1024
FFN_TF = 512
FFN_COL_CHUNK = 256
RESIDENT_TM = 512
FFN_DOWN_TM = RESIDENT_TM
NEG_BIG = -1e30

F32 = jnp.float32
BF16 = jnp.bfloat16


def _params(*sem, flags=None):
    return pltpu.CompilerParams(dimension_semantics=sem, vmem_limit_bytes=VMEM_LIMIT_BYTES, flags=flags)


def _rms_rows(x, w):
    ms = jnp.mean(x * x, axis=-1, keepdims=True)
    return (x * lax.rsqrt(ms + EPS)) * w


def _pick(n, pref):
    t = min(pref, n)
    while n % t:
        t //= 2
    return t


def _norm_mm_kernel(x_ref, nw_ref, w_ref, o_ref, h_ref):
    @pl.when(pl.program_id(1) == 0)
    def _():
        h_ref[...] = _rms_rows(x_ref[...], nw_ref[...]).astype(h_ref.dtype)

    o_ref[...] = jnp.dot(h_ref[...], w_ref[...], preferred_element_type=F32).astype(o_ref.dtype)


def norm_matmul(x, nw, w, out_dtype, name="norm_matmul"):
    t, k = x.shape
    n = w.shape[1]
    tm, tn = _pick(t, MM_TM), _pick(n, MM_TN)
    return pl.pallas_call(
        _norm_mm_kernel,
        grid=(t // tm, n // tn),
        in_specs=[
            pl.BlockSpec((tm, k), lambda i, j: (i, 0)),
            pl.BlockSpec((1, k), lambda i, j: (0, 0)),
            pl.BlockSpec((k, tn), lambda i, j: (0, j)),
        ],
        out_specs=pl.BlockSpec((tm, tn), lambda i, j: (i, j)),
        out_shape=jax.ShapeDtypeStruct((t, n), out_dtype),
        scratch_shapes=[pltpu.VMEM((tm, k), BF16)],
        compiler_params=_params("parallel", "arbitrary"),
        name=name,
    )(x, nw.reshape(1, k), w)


def _in_proj_kernel(x_ref, nw_ref, w_ref, oa_ref, ob_ref, h_ref, *, na_tiles):
    j = pl.program_id(1)

    @pl.when(j == 0)
    def _():
        h_ref[...] = _rms_rows(x_ref[...], nw_ref[...]).astype(h_ref.dtype)

    y = jnp.dot(h_ref[...], w_ref[...], preferred_element_type=F32)

    @pl.when(j < na_tiles)
    def _():
        oa_ref[...] = y.astype(oa_ref.dtype)

    @pl.when(j >= na_tiles)
    def _():
        ob_ref[...] = y


def in_proj(x, nw, w, n_a):
    t, k = x.shape
    n = w.shape[1]
    tm, tn = _pick(t, MM_TM), math.gcd(_pick(n_a, MM_TN), _pick(n - n_a, MM_TN))
    na_tiles = n_a // tn
    return pl.pallas_call(
        functools.partial(_in_proj_kernel, na_tiles=na_tiles),
        grid=(t // tm, n // tn),
        in_specs=[
            pl.BlockSpec((tm, k), lambda i, j: (i, 0)),
            pl.BlockSpec((1, k), lambda i, j: (0, 0)),
            pl.BlockSpec((k, tn), lambda i, j: (0, j)),
        ],
        out_specs=[
            pl.BlockSpec((tm, tn), lambda i, j: (i, jnp.minimum(j, na_tiles - 1))),
            pl.BlockSpec((tm, tn), lambda i, j: (i, jnp.maximum(j - na_tiles, 0))),
        ],
        out_shape=(jax.ShapeDtypeStruct((t, n_a), BF16), jax.ShapeDtypeStruct((t, n - n_a), F32)),
        scratch_shapes=[pltpu.VMEM((tm, k), BF16)],
        compiler_params=_params("parallel", "arbitrary"),
        name="in_proj",
    )(x, nw.reshape(1, k), w)


def _mix_mem_kernel(a0_ref, a1_ref, w0_ref, w1_ref, x_ref, nw_ref, wq_ref, k_ref, v_ref, wo_ref,
                    o_ref, att_ref, *, scale):
    x1 = x_ref[...]
    x1 = x1 + jnp.dot(a0_ref[...], w0_ref[...], preferred_element_type=F32)
    x1 = x1 + jnp.dot(a1_ref[...], w1_ref[...], preferred_element_type=F32)
    hq = _rms_rows(x1, nw_ref[...]).astype(BF16)
    mq = jnp.dot(hq, wq_ref[...], preferred_element_type=F32).astype(BF16)
    dh = mq.shape[1] // MEM_HEADS
    for hd in range(MEM_HEADS):
        sl = slice(hd * dh, (hd + 1) * dh)
        s = lax.dot_general(mq[:, sl], k_ref[:, sl], (((1,), (1,)), ((), ())),
                            preferred_element_type=F32) * scale
        e = jnp.exp(s - jnp.max(s, axis=-1, keepdims=True))
        p = e / jnp.sum(e, axis=-1, keepdims=True)
        att_ref[:, sl] = jnp.dot(p.astype(BF16), v_ref[:, sl],
                                 preferred_element_type=F32).astype(att_ref.dtype)
    o_ref[...] = x1 + jnp.dot(att_ref[...], wo_ref[...], preferred_element_type=F32)


def mix_out_mem_attention(a0, a1, w_out, x, nw_mem, w_mq, mkv, w_mo, seq, n_mem):
    t, k0 = a0.shape
    k1 = a1.shape[1]
    d = x.shape[1]
    assert k0 == k1 and w_out.shape == (k0 + k1, d)
    tm = _pick(seq, RESIDENT_TM)
    per_batch = seq // tm
    scale = (d // MEM_HEADS) ** -0.5
    once = pl.Buffered(1)
    return pl.pallas_call(
        functools.partial(_mix_mem_kernel, scale=scale),
        grid=(t // tm,),
        in_specs=[
            pl.BlockSpec((tm, k0), lambda i: (i, 0)),
            pl.BlockSpec((tm, k1), lambda i: (i, 0)),
            pl.BlockSpec((k0, d), lambda i: (0, 0), pipeline_mode=once),
            pl.BlockSpec((k1, d), lambda i: (1, 0), pipeline_mode=once),
            pl.BlockSpec((tm, d), lambda i: (i, 0)),
            pl.BlockSpec((1, d), lambda i: (0, 0)),
            pl.BlockSpec((d, d), lambda i: (0, 0), pipeline_mode=once),
            pl.BlockSpec((n_mem, d), lambda i: (i // per_batch, 0), pipeline_mode=once),
            pl.BlockSpec((n_mem, d), lambda i: (i // per_batch, 1), pipeline_mode=once),
            pl.BlockSpec((d, d), lambda i: (0, 0), pipeline_mode=once),
        ],
        out_specs=pl.BlockSpec((tm, d), lambda i: (i, 0)),
        out_shape=jax.ShapeDtypeStruct((t, d), F32),
        scratch_shapes=[pltpu.VMEM((tm, d), BF16)],
        compiler_params=_params("parallel"),
        name="mix_out_mem_attention",
    )(a0, a1, w_out, w_out, x, nw_mem.reshape(1, d), w_mq, mkv, mkv, w_mo)


def _t5_bucket_np(n):
    max_exact = NUM_BUCKETS // 2
    nf = np.maximum(n, 1).astype(np.float32)
    large = max_exact + (np.log(nf / max_exact) / math.log(MAX_DISTANCE / max_exact)
                         * (NUM_BUCKETS - max_exact)).astype(np.int32)
    large = np.minimum(large, NUM_BUCKETS - 1)
    return np.where(n < max_exact, n, large).astype(np.int32)


def _tables_kernel(lamv_ref, lbraw_ref, lam_ref, lb_ref, *, lam_init, layer):
    v = lamv_ref[...]
    s1 = jnp.sum(v[0:1] * v[1:2], axis=-1, keepdims=True)
    s2 = jnp.sum(v[2:3] * v[3:4], axis=-1, keepdims=True)
    lam = jnp.exp(s1) - jnp.exp(s2) + lam_init
    lam_ref[...] = jnp.broadcast_to(lam, lam_ref.shape)
    raw = lbraw_ref[...]
    e = jnp.exp(raw - jnp.max(raw, axis=0, keepdims=True))
    sm = e / jnp.sum(e, axis=0, keepdims=True)
    lb_ref[...] = jnp.sum(sm[0:layer + 1], axis=0, keepdims=True)


def param_tables(lam_vecs, lb_raw, lam_init, layer):
    kw = lb_raw.shape[1]
    return pl.pallas_call(
        functools.partial(_tables_kernel, lam_init=lam_init, layer=layer),
        out_shape=(jax.ShapeDtypeStruct((1, LANES), F32), jax.ShapeDtypeStruct((1, kw), F32)),
        name="param_tables",
    )(lam_vecs, lb_raw)


def _bias_tiles_kernel(rb_ref, bk_ref, o_ref, *, tq, far_bucket):
    h = pl.program_id(0)
    bk = bk_ref[...]
    vec = jnp.full(bk.shape, NEG_BIG, F32)
    for n in range(NUM_BUCKETS):
        vec = jnp.where(bk == n, rb_ref[n, h], vec)
    for t in range(2):
        rows = jnp.broadcast_to(vec[t:t + 1], (tq, 2 * tq))
        o_ref[0, t] = pltpu.roll(rows, 0, 1, stride=1, stride_axis=0)[:, tq:]
    o_ref[0, FAR_TILE] = jnp.full((tq, tq), rb_ref[far_bucket, h], F32)


def bias_tiles(rel_bias, dist_buckets, tq, far_bucket):
    nb, h = rel_bias.shape
    return pl.pallas_call(
        functools.partial(_bias_tiles_kernel, tq=tq, far_bucket=far_bucket),
        grid=(h,),
        in_specs=[
            pl.BlockSpec(memory_space=pltpu.SMEM),
            pl.BlockSpec(dist_buckets.shape, lambda i: (0, 0)),
        ],
        out_specs=pl.BlockSpec((1, FAR_TILE + 1, tq, tq), lambda i: (i, 0, 0, 0)),
        out_shape=jax.ShapeDtypeStruct((h, FAR_TILE + 1, tq, tq), F32),
        compiler_params=_params("arbitrary"),
        name="bias_tiles",
    )(rel_bias, dist_buckets)


ONES_ROWS = 16
FAR_TILE = 2


def _diff_attn_kernel(q_ref, k_ref, v_ref, bias_ref, lam_ref, nw_ref, o_ref,
                      vt_ref, m_ref, acc_ref, st_ref, *, tq, out_scale):
    i = pl.program_id(2)
    dv = v_ref.shape[1]
    dh = dv // 2
    nkv = v_ref.shape[0] // tq

    @pl.when(i == 0)
    def _():
        for c in range(nkv):
            vt_ref[c, 0:dv, :] = v_ref[c * tq:(c + 1) * tq, :].astype(F32).T.astype(BF16)
            vt_ref[c, dv:, :] = jnp.ones((ONES_ROWS, tq), BF16)

    q = q_ref[...]
    lane = lax.broadcasted_iota(jnp.int32, (1, dv), 1)
    qm = (jnp.where(lane < dh, q, jnp.zeros_like(q)), jnp.where(lane >= dh, q, jnp.zeros_like(q)))

    m_ref[...] = jnp.full(m_ref.shape, NEG_BIG, F32)
    acc_ref[...] = jnp.zeros(acc_ref.shape, F32)

    qw = min(DA_Q_SUB, tq)
    chains = [(mp, slice(qs * qw, (qs + 1) * qw)) for mp in range(2) for qs in range(tq // qw)]

    def scores(n, j):
        mp, cols = chains[n]
        kb = k_ref[pl.ds(pl.multiple_of(j * tq, tq), tq), :]
        st_ref[n] = lax.dot_general(kb, qm[mp][cols, :], (((1,), (1,)), ((), ())),
                                    preferred_element_type=F32)

    def softmax_pv(n, vt, tile):
        mp, cols = chains[n]
        st = st_ref[n] + bias_ref[0, tile, :, cols]
        m_old = m_ref[mp, :, cols]
        m_new = jnp.maximum(m_old, jnp.max(st, axis=0, keepdims=True))
        p = jnp.exp(st - m_new).astype(BF16)
        alpha = jnp.exp(m_old - m_new)
        acc_ref[mp, :, cols] = (alpha * acc_ref[mp, :, cols]
                                + jnp.dot(vt, p, preferred_element_type=F32))
        m_ref[mp, :, cols] = m_new

    def step(j, carry):
        vt = vt_ref[j]
        tile = jnp.minimum(i - j, FAR_TILE)
        for n in range(len(chains)):
            if n + 1 < len(chains):
                scores(n + 1, j)
            else:
                scores(0, jnp.minimum(j + 1, i))
            softmax_pv(n, vt, tile)
        return carry

    scores(0, 0)
    lax.fori_loop(0, i + 1, step, 0)

    o0 = acc_ref[0, 0:dv, :] / acc_ref[0, dv:dv + 1, :]
    o1 = acc_ref[1, 0:dv, :] / acc_ref[1, dv:dv + 1, :]
    o = o0 - lam_ref[0:1, 0:1] * o1
    ms = jnp.mean(o * o, axis=0, keepdims=True)
    y = (o * lax.rsqrt(ms + EPS)) * (nw_ref[...] * out_scale)
    o_ref[...] = y.T.astype(o_ref.dtype)


def diff_attention(proj, bias_t, lam_row, nw, batch, seq, tq, out_scale):
    nq = seq // tq
    dv = LANES
    kern = functools.partial(_diff_attn_kernel, tq=tq, out_scale=out_scale)
    return pl.pallas_call(
        kern,
        grid=(batch, DA_HEADS, nq),
        in_specs=[
            pl.BlockSpec((tq, dv), lambda b, h, i: (b * nq + i, h)),
            pl.BlockSpec((seq, dv), lambda b, h, i: (b, DA_HEADS + h)),
            pl.BlockSpec((seq, dv), lambda b, h, i: (b, 2 * DA_HEADS + h)),
            pl.BlockSpec((1, FAR_TILE + 1, tq, tq), lambda b, h, i: (h, 0, 0, 0)),
            pl.BlockSpec((1, LANES), lambda b, h, i: (0, 0)),
            pl.BlockSpec((dv, 1), lambda b, h, i: (0, 0)),
        ],
        out_specs=pl.BlockSpec((tq, dv), lambda b, h, i: (b * nq + i, h)),
        out_shape=jax.ShapeDtypeStruct((batch * seq, DA_HEADS * dv), BF16),
        scratch_shapes=[
            pltpu.VMEM((nq, dv + ONES_ROWS, tq), BF16),
            pltpu.VMEM((2, 1, tq), F32),
            pltpu.VMEM((2, dv + ONES_ROWS, tq), F32),
            pltpu.VMEM((2 * tq // min(DA_Q_SUB, tq), tq, min(DA_Q_SUB, tq)), F32),
        ],
        compiler_params=_params("parallel", "parallel", "arbitrary"),
        name="diff_attention",
    )(proj, proj, proj, bias_t, lam_row, nw.reshape(dv, 1))


def _exact_bf16_dot(tri3, x):
    hi = x.astype(BF16)
    r1 = x - hi.astype(F32)
    mid = r1.astype(BF16)
    lo = (r1 - mid.astype(F32)).astype(BF16)
    return jnp.dot(tri3, jnp.concatenate([hi, mid, lo], axis=0), preferred_element_type=F32)


def _hgrn_tables(c):
    n_lv = int(math.log2(c))
    r = np.arange(c)[None, :]
    t = np.arange(c)[:, None]
    blocks_d, blocks_e = [], []
    for lv in range(1, HG_SMALL_LEVELS + 1):
        h = 1 << (lv - 1)
        upper = (t // h) % 2 == 1
        mid_t = (t // h) * h
        blocks_d.append(upper & (r >= mid_t) & (r <= t))
        mid_s = (t // h + 1) * h
        if lv > 1:
            blocks_e.append((~upper) & (r > t) & (r < mid_s))
    stack = np.concatenate(blocks_d + blocks_e + [r <= t, r > t], axis=0).astype(np.float32)
    x = np.bitwise_xor(t, r)
    level = np.where(r == t, 0, np.floor(np.log2(np.maximum(x, 1))).astype(np.int32) + 1)
    level = np.where(r > t, -1, level).astype(np.int32)
    return stack, level, n_lv


def _hgrn_kernel(q_ref, f_ref, i_ref, g_ref, lb_ref, nw_ref, tst_ref, lvl_ref, o_ref, st_ref,
                 *, ts, c, n_lv):
    nt = (((1,), (1,)), ((), ()))
    d = LANES
    nh = q_ref.shape[1] // d
    nch = ts // c
    n_small = HG_SMALL_LEVELS

    @pl.when(pl.program_id(2) == 0)
    def _():
        st_ref[...] = jnp.zeros(st_ref.shape, F32)

    lb = lb_ref[...]
    f = lb + (1.0 - lb) * jax.nn.sigmoid(f_ref[...])
    logf = jnp.log(f)
    kk = 1.0 - f
    units = [(ch, hd) for ch in range(nch) for hd in range(nh)]
    lcat = jnp.concatenate([logf[ch * c:(ch + 1) * c, hd * d:(hd + 1) * d] for ch, hd in units], axis=1)
    seg = _exact_bf16_dot(tst_ref[...], lcat)
    lvl = lvl_ref[...]
    nw = nw_ref[...]

    st = [st_ref[hd] for hd in range(nh)]
    for u, (ch, hd) in enumerate(units):
        rows = slice(ch * c, (ch + 1) * c)
        cols = slice(hd * d, (hd + 1) * d)

        def blk(b, u=u):
            return seg[b * c:(b + 1) * c, u * d:(u + 1) * d]

        g_in = blk(2 * n_small - 1)
        g_out = blk(2 * n_small)
        q = q_ref[rows, cols]
        q32 = q.astype(F32)
        kc = kk[rows, cols]
        vb = i_ref[rows, cols].astype(BF16)
        a = jnp.where(lvl == 0, lax.dot_general(q, kc.astype(BF16), nt, preferred_element_type=F32), 0.0)
        for lv in range(1, n_lv + 1):
            if lv <= n_small:
                d_up = blk(lv - 1)
                d_lo = blk(n_small + lv - 2) if lv > 1 else None
            else:
                h = 1 << (lv - 1)
                ref = jnp.concatenate(
                    [jnp.broadcast_to(g_in[gi * 2 * h + h - 1:gi * 2 * h + h, :], (2 * h, d))
                     for gi in range(c // (2 * h))], axis=0)
                d_up = jnp.minimum(g_in - ref, 0.0)
                d_lo = jnp.minimum(ref - g_in, 0.0)
            qt = (q32 * jnp.exp(d_up)).astype(BF16)
            kt = (kc if d_lo is None else kc * jnp.exp(d_lo)).astype(BF16)
            a = jnp.where(lvl == lv, lax.dot_general(qt, kt, nt, preferred_element_type=F32), a)
        qg = (q32 * jnp.exp(g_in)).astype(BF16)
        o = (jnp.dot(a.astype(BF16), vb, preferred_element_type=F32)
             + lax.dot_general(qg, st[hd].astype(BF16), nt, preferred_element_type=F32))
        kd = (kc * jnp.exp(g_out)).astype(BF16)
        upd = lax.dot_general(vb, kd, (((0,), (0,)), ((), ())), preferred_element_type=F32)
        st[hd] = st[hd] * jnp.exp(g_in[c - 1:c]) + upd
        g = g_ref[rows, cols]
        o_ref[rows, cols] = (_rms_rows(o, nw) * (g * jax.nn.sigmoid(g))).astype(o_ref.dtype)
    for hd in range(nh):
        st_ref[hd] = st[hd]


def hgrn2(proj_a, proj_b, lb, nw, batch, seq, ts, q_col0):
    ns = seq // ts
    d = LANES
    nh = HG_HEADS_PER_STEP
    assert HG_HEADS % nh == 0 and q_col0 % nh == 0
    w = nh * d
    ng = HG_HEADS // nh
    c = min(HG_CHUNK, ts)
    stack, level, n_lv = _hgrn_tables(c)
    assert n_lv >= HG_SMALL_LEVELS
    kern = functools.partial(_hgrn_kernel, ts=ts, c=c, n_lv=n_lv)
    return pl.pallas_call(
        kern,
        grid=(batch, ng, ns),
        in_specs=[
            pl.BlockSpec((ts, w), lambda b, h, i: (b * ns + i, q_col0 // nh + h)),
            pl.BlockSpec((ts, w), lambda b, h, i: (b * ns + i, h)),
            pl.BlockSpec((ts, w), lambda b, h, i: (b * ns + i, ng + h)),
            pl.BlockSpec((ts, w), lambda b, h, i: (b * ns + i, 2 * ng + h)),
            pl.BlockSpec((1, w), lambda b, h, i: (0, h)),
            pl.BlockSpec((1, d), lambda b, h, i: (0, 0)),
            pl.BlockSpec((stack.shape[0], 3 * stack.shape[1]), lambda b, h, i: (0, 0)),
            pl.BlockSpec(level.shape, lambda b, h, i: (0, 0)),
        ],
        out_specs=pl.BlockSpec((ts, w), lambda b, h, i: (b * ns + i, h)),
        out_shape=jax.ShapeDtypeStruct((batch * seq, HG_HEADS * d), BF16),
        scratch_shapes=[pltpu.VMEM((nh, d, d), F32)],
        compiler_params=_params("parallel", "parallel", "arbitrary"),
        name="hgrn2",
    )(proj_a, proj_b, proj_b, proj_b, lb, nw.reshape(1, d),
      jnp.asarray(np.concatenate([stack] * 3, axis=1), BF16), jnp.asarray(level))


def _ffn_up_kernel(x_ref, halo_ref, nw_ref, wa_ref, wb_ref, cwa_ref, cwb_ref, cba_ref, cbb_ref,
                   o_ref, h_ref, u_ref, *, tm, per_batch):
    hl = SUBLANES

    @pl.when(pl.program_id(1) == 0)
    def _():
        nw = nw_ref[...]
        first = (pl.program_id(0) % per_batch) == 0
        halo = _rms_rows(halo_ref[...], nw)
        h_ref[0:hl, :] = jnp.where(first, 0.0, halo).astype(h_ref.dtype)
        h_ref[hl:, :] = _rms_rows(x_ref[...], nw).astype(h_ref.dtype)

    h = h_ref[...]
    tf = o_ref.shape[1]
    cw = min(FFN_COL_CHUNK, tf)

    def cols(cc):
        return slice(cc * cw, (cc + 1) * cw)

    def matmuls(cc):
        u_ref[cc, 0] = jnp.dot(h, wa_ref[:, cols(cc)], preferred_element_type=F32)
        u_ref[cc, 1] = jnp.dot(h, wb_ref[:, cols(cc)], preferred_element_type=F32)

    def conv(cc, half, cw_ref, cb_ref):
        acc = cb_ref[:, cols(cc)]
        for j in range(CONV_WIDTH):
            off = hl - (CONV_WIDTH - 1) + j
            acc = acc + cw_ref[j:j + 1, cols(cc)] * u_ref[cc, half, off:off + tm, :]
        return acc

    def gate(cc):
        a = conv(cc, 0, cwa_ref, cba_ref)
        b = conv(cc, 1, cwb_ref, cbb_ref)
        o_ref[:, cols(cc)] = ((a * jax.nn.sigmoid(a)) * b).astype(o_ref.dtype)

    for cc in range(tf // cw):
        matmuls(cc)
        gate(cc)


def ffn_up(x, nw, w_up, conv_w, conv_b, seq):
    t, d = x.shape
    f = w_up.shape[1] // 2
    tm, tf = _pick(seq, MM_TM), _pick(f, FFN_TF)
    nf = f // tf
    per_batch = seq // tm
    hl = SUBLANES
    rows_per_tile = tm // hl
    cw = min(FFN_COL_CHUNK, tf)
    kern = functools.partial(_ffn_up_kernel, tm=tm, per_batch=per_batch)
    return pl.pallas_call(
        kern,
        grid=(t // tm, nf),
        in_specs=[
            pl.BlockSpec((tm, d), lambda i, j: (i, 0)),
            pl.BlockSpec((hl, d), lambda i, j: (jnp.maximum(i * rows_per_tile - 1, 0), 0)),
            pl.BlockSpec((1, d), lambda i, j: (0, 0)),
            pl.BlockSpec((d, tf), lambda i, j: (0, j)),
            pl.BlockSpec((d, tf), lambda i, j: (0, nf + j)),
            pl.BlockSpec((CONV_WIDTH, tf), lambda i, j: (0, j)),
            pl.BlockSpec((CONV_WIDTH, tf), lambda i, j: (0, nf + j)),
            pl.BlockSpec((1, tf), lambda i, j: (0, j)),
            pl.BlockSpec((1, tf), lambda i, j: (0, nf + j)),
        ],
        out_specs=pl.BlockSpec((tm, tf), lambda i, j: (i, j)),
        out_shape=jax.ShapeDtypeStruct((t, f), BF16),
        scratch_shapes=[
            pltpu.VMEM((tm + hl, d), BF16),
            pltpu.VMEM((tf // cw, 2, tm + hl, cw), F32),
        ],
        compiler_params=_params("parallel", "arbitrary"),
        name="ffn_up",
    )(x, x, nw.reshape(1, d), w_up, w_up, conv_w, conv_w, conv_b.reshape(1, 2 * f), conv_b.reshape(1, 2 * f))


def _ffn_down_kernel(a_ref, w_ref, res_ref, nw_ref, o_ref):
    y = res_ref[...] + jnp.dot(a_ref[...], w_ref[...], preferred_element_type=F32)
    o_ref[...] = _rms_rows(y, nw_ref[...])


def ffn_down(act, w_down, res, nw):
    t, f = act.shape
    d = w_down.shape[1]
    tm = _pick(t, FFN_DOWN_TM)
    return pl.pallas_call(
        _ffn_down_kernel,
        grid=(t // tm,),
        in_specs=[
            pl.BlockSpec((tm, f), lambda i: (i, 0)),
            pl.BlockSpec((f, d), lambda i: (0, 0), pipeline_mode=pl.Buffered(1)),
            pl.BlockSpec((tm, d), lambda i: (i, 0)),
            pl.BlockSpec((1, d), lambda i: (0, 0)),
        ],
        out_specs=pl.BlockSpec((tm, d), lambda i: (i, 0)),
        out_shape=jax.ShapeDtypeStruct((t, d), F32),
        compiler_params=_params("parallel"),
        name="ffn_down",
    )(act, w_down, res, nw.reshape(1, d))


def kernel(x, mem, w_in, w_out, norm_mix_w, lam_q1, lam_k1, lam_q2, lam_k2, da_subln_w,
           hg_lb_raw, hg_norm_w, rel_bias, norm_mem_w, mem_kv_norm_w, w_mq, w_mkv, w_mo,
           norm_ffn_w, w_up, conv_w, conv_b, w_down, final_norm_w):
    batch, seq, d_model = x.shape
    n_mem = mem.shape[1]
    depth = w_in.shape[0]
    assert depth == 1, "single-layer block"
    layer = 0
    t = batch * seq

    da_width = DA_HEADS * LANES
    hg_width = HG_HEADS * LANES
    assert w_in.shape[2] == 3 * da_width + 4 * hg_width
    assert da_subln_w.shape[1] == LANES and hg_norm_w.shape[1] == LANES
    assert lam_q1.shape[1] * 2 == LANES
    n_bf = 3 * da_width + hg_width

    tq = _pick(seq, DA_BLOCK)
    ts = _pick(seq, HG_TILE)

    lam_init = 0.8 - 0.6 * math.exp(-0.3 * layer)
    lam_vecs = jnp.stack([lam_q1[layer], lam_k1[layer], lam_q2[layer], lam_k2[layer]]).astype(F32)
    lam_row, lb = param_tables(lam_vecs, hg_lb_raw.astype(F32), lam_init, layer)

    far = _t5_bucket_np(np.arange(tq + 1, max(seq, tq + 2)))
    assert (far == far[0]).all(), "bias must be constant beyond the sub-diagonal block"
    c = np.arange(2 * tq)
    dist_buckets = np.stack([np.where(c >= tq, _t5_bucket_np(np.maximum(c - tq, 0)), -1),
                             _t5_bucket_np(c)]).astype(np.int32)
    bias_t = bias_tiles(rel_bias.astype(F32), jnp.asarray(dist_buckets), tq, int(far[0]))

    x2d = x.reshape(t, d_model)
    w_in_l = w_in[layer]
    q_scale = (LANES // 2) ** -0.5
    w_all = jnp.concatenate([w_in_l[:, :da_width] * q_scale, w_in_l[:, da_width:]], axis=1).astype(BF16)
    proj_a, proj_b = in_proj(x2d, norm_mix_w[layer], w_all, n_bf)

    da_o = diff_attention(proj_a, bias_t, lam_row, da_subln_w[layer], batch, seq, tq, 1.0 - lam_init)
    hg_o = hgrn2(proj_a, proj_b, lb, hg_norm_w[layer], batch, seq, ts, 3 * DA_HEADS)

    mkv = norm_matmul(mem.reshape(batch * n_mem, d_model), mem_kv_norm_w[layer],
                      w_mkv[layer].astype(BF16), BF16, name="mem_kv_proj")
    x2 = mix_out_mem_attention(da_o, hg_o, w_out[layer].astype(BF16), x2d, norm_mem_w[layer],
                               w_mq[layer].astype(BF16), mkv, w_mo[layer].astype(BF16), seq, n_mem)

    act = ffn_up(x2, norm_ffn_w[layer], w_up[layer].astype(BF16), conv_w[layer], conv_b[layer], seq)
    out = ffn_down(act, w_down[layer].astype(BF16), x2, final_norm_w)
    return out.reshape(batch, seq, d_model)
```

```python
import functools
import math

import numpy as np
import jax
import jax.numpy as jnp
from jax import lax
from jax.experimental import pallas as pl
from jax.experimental.pallas import tpu as pltpu

DA_HEADS = 8
HG_HEADS = 8
MEM_HEADS = 4
NUM_BUCKETS = 32
MAX_DISTANCE = 128
CONV_WIDTH = 3
EPS = 1e-6

LANES = 128
SUBLANES = 8
VMEM_LIMIT_BYTES = 56 * 1024 * 1024

HG_CHUNK = 128
HG_TILE = 256
HG_HEADS_PER_STEP = 2
HG_SMALL_LEVELS = 3
DA_BLOCK = 512
DA_Q_SUB = 256
MM_TM = 1024
MM_TN = 1024
FFN_TF = 512
FFN_COL_CHUNK = 256
RESIDENT_TM = 512
FFN_DOWN_TM = RESIDENT_TM
NEG_BIG = -1e30

F32 = jnp.float32
BF16 = jnp.bfloat16


def _params(*sem, flags=None):
    return pltpu.CompilerParams(dimension_semantics=sem, vmem_limit_bytes=VMEM_LIMIT_BYTES, flags=flags)


def _rms_rows(x, w):
    ms = jnp.mean(x * x, axis=-1, keepdims=True)
    return (x * lax.rsqrt(ms + EPS)) * w


def _pick(n, pref):
    t = min(pref, n)
    while n % t:
        t //= 2
    return t


def _norm_mm_kernel(x_ref, nw_ref, w_ref, o_ref, h_ref):
    @pl.when(pl.program_id(1) == 0)
    def _():
        h_ref[...] = _rms_rows(x_ref[...], nw_ref[...]).astype(h_ref.dtype)

    o_ref[...] = jnp.dot(h_ref[...], w_ref[...], preferred_element_type=F32).astype(o_ref.dtype)


def norm_matmul(x, nw, w, out_dtype, name="norm_matmul"):
    t, k = x.shape
    n = w.shape[1]
    tm, tn = _pick(t, MM_TM), _pick(n, MM_TN)
    return pl.pallas_call(
        _norm_mm_kernel,
        grid=(t // tm, n // tn),
        in_specs=[
            pl.BlockSpec((tm, k), lambda i, j: (i, 0)),
            pl.BlockSpec((1, k), lambda i, j: (0, 0)),
            pl.BlockSpec((k, tn), lambda i, j: (0, j)),
        ],
        out_specs=pl.BlockSpec((tm, tn), lambda i, j: (i, j)),
        out_shape=jax.ShapeDtypeStruct((t, n), out_dtype),
        scratch_shapes=[pltpu.VMEM((tm, k), BF16)],
        compiler_params=_params("parallel", "arbitrary"),
        name=name,
    )(x, nw.reshape(1, k), w)


BF16_ROWS = 16


def _with_cast_riders(body, n_in, n_out, n_riders):
    def kern(*refs):
        ins, rest = refs[:n_in], refs[n_in:]
        rid_in, rest = rest[:n_riders], rest[n_riders:]
        outs, rest = rest[:n_out], rest[n_out:]
        rid_out, scratch = rest[:n_riders], rest[n_riders:]
        for src, dst in zip(rid_in, rid_out):
            dst[...] = src[...].astype(dst.dtype)
        body(*ins, *outs, *scratch)
    return kern


def _rider_specs(riders, grid):
    n_steps = math.prod(grid)

    def step(*g):
        idx = 0
        for gi, n in zip(g, grid):
            idx = idx * n + gi
        return idx

    specs, shapes = [], []
    for a in riders:
        rows = a.shape[0] // n_steps
        assert rows * n_steps == a.shape[0] and rows % BF16_ROWS == 0, (a.shape, grid)
        specs.append(pl.BlockSpec((rows, a.shape[1]), lambda *g: (step(*g), 0)))
        shapes.append(jax.ShapeDtypeStruct(a.shape, BF16))
    return specs, shapes


def _in_proj_kernel(x_ref, nw_ref, w_ref, cs_ref, oa_ref, ob_ref, h_ref, *, na_tiles):
    j = pl.program_id(1)

    @pl.when(j == 0)
    def _():
        h_ref[...] = _rms_rows(x_ref[...], nw_ref[...]).astype(h_ref.dtype)

    y = jnp.dot(h_ref[...], w_ref[...], preferred_element_type=F32)

    @pl.when(j < na_tiles)
    def _():
        oa_ref[...] = (y * cs_ref[...]).astype(oa_ref.dtype)

    @pl.when(j >= na_tiles)
    def _():
        ob_ref[...] = y


def in_proj(x, nw, w, n_a, col_scale):
    t, k = x.shape
    n = w.shape[1]
    tm, tn = _pick(t, MM_TM), math.gcd(_pick(n_a, MM_TN), _pick(n - n_a, MM_TN))
    na_tiles = n_a // tn
    return pl.pallas_call(
        functools.partial(_in_proj_kernel, na_tiles=na_tiles),
        grid=(t // tm, n // tn),
        in_specs=[
            pl.BlockSpec((tm, k), lambda i, j: (i, 0)),
            pl.BlockSpec((1, k), lambda i, j: (0, 0)),
            pl.BlockSpec((k, tn), lambda i, j: (0, j)),
            pl.BlockSpec((1, tn), lambda i, j: (0, jnp.minimum(j, na_tiles - 1))),
        ],
        out_specs=[
            pl.BlockSpec((tm, tn), lambda i, j: (i, jnp.minimum(j, na_tiles - 1))),
            pl.BlockSpec((tm, tn), lambda i, j: (i, jnp.maximum(j - na_tiles, 0))),
        ],
        out_shape=(jax.ShapeDtypeStruct((t, n_a), BF16), jax.ShapeDtypeStruct((t, n - n_a), F32)),
        scratch_shapes=[pltpu.VMEM((tm, k), BF16)],
        compiler_params=_params("parallel", "arbitrary"),
        name="in_proj",
    )(x, nw.reshape(1, k), w, col_scale)


def _mix_mem_kernel(a0_ref, a1_ref, w0_ref, w1_ref, x_ref, nw_ref, wq_ref, k_ref, v_ref, wo_ref,
                    o_ref, att_ref, *, scale):
    x1 = x_ref[...]
    x1 = x1 + jnp.dot(a0_ref[...], w0_ref[...], preferred_element_type=F32)
    x1 = x1 + jnp.dot(a1_ref[...], w1_ref[...], preferred_element_type=F32)
    hq = _rms_rows(x1, nw_ref[...]).astype(BF16)
    mq = jnp.dot(hq, wq_ref[...], preferred_element_type=F32).astype(BF16)
    dh = mq.shape[1] // MEM_HEADS
    for hd in range(MEM_HEADS):
        sl = slice(hd * dh, (hd + 1) * dh)
        s = lax.dot_general(mq[:, sl], k_ref[:, sl], (((1,), (1,)), ((), ())),
                            preferred_element_type=F32) * scale
        e = jnp.exp(s - jnp.max(s, axis=-1, keepdims=True))
        p = e / jnp.sum(e, axis=-1, keepdims=True)
        att_ref[:, sl] = jnp.dot(p.astype(BF16), v_ref[:, sl],
                                 preferred_element_type=F32).astype(att_ref.dtype)
    o_ref[...] = x1 + jnp.dot(att_ref[...], wo_ref[...], preferred_element_type=F32)


def mix_out_mem_attention(a0, a1, w_out, x, nw_mem, w_mq, mkv, w_mo, seq, n_mem):
    t, k0 = a0.shape
    k1 = a1.shape[1]
    d = x.shape[1]
    assert k0 == k1 and w_out.shape == (k0 + k1, d)
    tm = _pick(seq, RESIDENT_TM)
    per_batch = seq // tm
    scale = (d // MEM_HEADS) ** -0.5
    once = pl.Buffered(1)
    return pl.pallas_call(
        functools.partial(_mix_mem_kernel, scale=scale),
        grid=(t // tm,),
        in_specs=[
            pl.BlockSpec((tm, k0), lambda i: (i, 0)),
            pl.BlockSpec((tm, k1), lambda i: (i, 0)),
            pl.BlockSpec((k0, d), lambda i: (0, 0), pipeline_mode=once),
            pl.BlockSpec((k1, d), lambda i: (1, 0), pipeline_mode=once),
            pl.BlockSpec((tm, d), lambda i: (i, 0)),
            pl.BlockSpec((1, d), lambda i: (0, 0)),
            pl.BlockSpec((d, d), lambda i: (0, 0), pipeline_mode=once),
            pl.BlockSpec((n_mem, d), lambda i: (i // per_batch, 0), pipeline_mode=once),
            pl.BlockSpec((n_mem, d), lambda i: (i // per_batch, 1), pipeline_mode=once),
            pl.BlockSpec((d, d), lambda i: (0, 0), pipeline_mode=once),
        ],
        out_specs=pl.BlockSpec((tm, d), lambda i: (i, 0)),
        out_shape=jax.ShapeDtypeStruct((t, d), F32),
        scratch_shapes=[pltpu.VMEM((tm, d), BF16)],
        compiler_params=_params("parallel"),
        name="mix_out_mem_attention",
    )(a0, a1, w_out, w_out, x, nw_mem.reshape(1, d), w_mq, mkv, mkv, w_mo)


def _t5_bucket_np(n):
    max_exact = NUM_BUCKETS // 2
    nf = np.maximum(n, 1).astype(np.float32)
    large = max_exact + (np.log(nf / max_exact) / math.log(MAX_DISTANCE / max_exact)
                         * (NUM_BUCKETS - max_exact)).astype(np.int32)
    large = np.minimum(large, NUM_BUCKETS - 1)
    return np.where(n < max_exact, n, large).astype(np.int32)


def _tables_kernel(lamv_ref, lbraw_ref, lam_ref, lb_ref, *, lam_init, layer):
    v = lamv_ref[...]
    s1 = jnp.sum(v[0:1] * v[1:2], axis=-1, keepdims=True)
    s2 = jnp.sum(v[2:3] * v[3:4], axis=-1, keepdims=True)
    lam = jnp.exp(s1) - jnp.exp(s2) + lam_init
    lam_ref[...] = jnp.broadcast_to(lam, lam_ref.shape)
    raw = lbraw_ref[...]
    e = jnp.exp(raw - jnp.max(raw, axis=0, keepdims=True))
    sm = e / jnp.sum(e, axis=0, keepdims=True)
    lb_ref[...] = jnp.sum(sm[0:layer + 1], axis=0, keepdims=True)


def param_tables(lam_vecs, lb_raw, lam_init, layer):
    kw = lb_raw.shape[1]
    return pl.pallas_call(
        functools.partial(_tables_kernel, lam_init=lam_init, layer=layer),
        out_shape=(jax.ShapeDtypeStruct((1, LANES), F32), jax.ShapeDtypeStruct((1, kw), F32)),
        name="param_tables",
    )(lam_vecs, lb_raw)


def _bias_tiles_kernel(rb_ref, bk_ref, o_ref, *, tq, far_bucket):
    h = pl.program_id(0)
    bk = bk_ref[...]
    vec = jnp.full(bk.shape, NEG_BIG, F32)
    for n in range(NUM_BUCKETS):
        vec = jnp.where(bk == n, rb_ref[n, h], vec)
    for t in range(2):
        rows = jnp.broadcast_to(vec[t:t + 1], (tq, 2 * tq))
        o_ref[0, t] = pltpu.roll(rows, 0, 1, stride=1, stride_axis=0)[:, tq:]
    o_ref[0, FAR_TILE] = jnp.full((tq, tq), rb_ref[far_bucket, h], F32)


def bias_tiles(rel_bias, dist_buckets, tq, far_bucket):
    nb, h = rel_bias.shape
    return pl.pallas_call(
        functools.partial(_bias_tiles_kernel, tq=tq, far_bucket=far_bucket),
        grid=(h,),
        in_specs=[
            pl.BlockSpec(memory_space=pltpu.SMEM),
            pl.BlockSpec(dist_buckets.shape, lambda i: (0, 0)),
        ],
        out_specs=pl.BlockSpec((1, FAR_TILE + 1, tq, tq), lambda i: (i, 0, 0, 0)),
        out_shape=jax.ShapeDtypeStruct((h, FAR_TILE + 1, tq, tq), F32),
        compiler_params=_params("arbitrary"),
        name="bias_tiles",
    )(rel_bias, dist_buckets)


ONES_ROWS = 16
FAR_TILE = 2


def _diff_attn_kernel(q_ref, k_ref, v_ref, bias_ref, lam_ref, nw_ref, o_ref,
                      vt_ref, m_ref, acc_ref, st_ref, *, tq, out_scale):
    i = pl.program_id(2)
    dv = v_ref.shape[1]
    dh = dv // 2
    nkv = v_ref.shape[0] // tq

    @pl.when(i == 0)
    def _():
        for c in range(nkv):
            vt_ref[c, 0:dv, :] = v_ref[c * tq:(c + 1) * tq, :].astype(F32).T.astype(BF16)
            vt_ref[c, dv:, :] = jnp.ones((ONES_ROWS, tq), BF16)

    q = q_ref[...]
    lane = lax.broadcasted_iota(jnp.int32, (1, dv), 1)
    qm = (jnp.where(lane < dh, q, jnp.zeros_like(q)), jnp.where(lane >= dh, q, jnp.zeros_like(q)))

    m_ref[...] = jnp.full(m_ref.shape, NEG_BIG, F32)
    acc_ref[...] = jnp.zeros(acc_ref.shape, F32)

    qw = min(DA_Q_SUB, tq)
    chains = [(mp, slice(qs * qw, (qs + 1) * qw)) for mp in range(2) for qs in range(tq // qw)]

    def scores(n, j):
        mp, cols = chains[n]
        kb = k_ref[pl.ds(pl.multiple_of(j * tq, tq), tq), :]
        st_ref[n] = lax.dot_general(kb, qm[mp][cols, :], (((1,), (1,)), ((), ())),
                                    preferred_element_type=F32)

    def softmax_pv(n, vt, tile):
        mp, cols = chains[n]
        st = st_ref[n] + bias_ref[0, tile, :, cols]
        m_old = m_ref[mp, :, cols]
        m_new = jnp.maximum(m_old, jnp.max(st, axis=0, keepdims=True))
        p = jnp.exp(st - m_new).astype(BF16)
        alpha = jnp.exp(m_old - m_new)
        acc_ref[mp, :, cols] = (alpha * acc_ref[mp, :, cols]
                                + jnp.dot(vt, p, preferred_element_type=F32))
        m_ref[mp, :, cols] = m_new

    def step(j, carry):
        vt = vt_ref[j]
        tile = jnp.minimum(i - j, FAR_TILE)
        for n in range(len(chains)):
            if n + 1 < len(chains):
                scores(n + 1, j)
            else:
                scores(0, jnp.minimum(j + 1, i))
            softmax_pv(n, vt, tile)
        return carry

    scores(0, 0)
    lax.fori_loop(0, i + 1, step, 0)

    o0 = acc_ref[0, 0:dv, :] / acc_ref[0, dv:dv + 1, :]
    o1 = acc_ref[1, 0:dv, :] / acc_ref[1, dv:dv + 1, :]
    o = o0 - lam_ref[0:1, 0:1] * o1
    ms = jnp.mean(o * o, axis=0, keepdims=True)
    y = (o * lax.rsqrt(ms + EPS)) * (nw_ref[...] * out_scale)
    o_ref[...] = y.T.astype(o_ref.dtype)


def diff_attention(proj, bias_t, lam_row, nw, batch, seq, tq, out_scale, riders=()):
    nq = seq // tq
    dv = LANES
    grid = (batch, DA_HEADS, nq)
    rider_specs, rider_shapes = _rider_specs(riders, grid)
    kern = _with_cast_riders(functools.partial(_diff_attn_kernel, tq=tq, out_scale=out_scale),
                             n_in=6, n_out=1, n_riders=len(riders))
    return pl.pallas_call(
        kern,
        grid=grid,
        in_specs=[
            pl.BlockSpec((tq, dv), lambda b, h, i: (b * nq + i, h)),
            pl.BlockSpec((seq, dv), lambda b, h, i: (b, DA_HEADS + h)),
            pl.BlockSpec((seq, dv), lambda b, h, i: (b, 2 * DA_HEADS + h)),
            pl.BlockSpec((1, FAR_TILE + 1, tq, tq), lambda b, h, i: (h, 0, 0, 0)),
            pl.BlockSpec((1, LANES), lambda b, h, i: (0, 0)),
            pl.BlockSpec((dv, 1), lambda b, h, i: (0, 0)),
        ] + rider_specs,
        out_specs=[pl.BlockSpec((tq, dv), lambda b, h, i: (b * nq + i, h))] + rider_specs,
        out_shape=[jax.ShapeDtypeStruct((batch * seq, DA_HEADS * dv), BF16)] + rider_shapes,
        scratch_shapes=[
            pltpu.VMEM((nq, dv + ONES_ROWS, tq), BF16),
            pltpu.VMEM((2, 1, tq), F32),
            pltpu.VMEM((2, dv + ONES_ROWS, tq), F32),
            pltpu.VMEM((2 * tq // min(DA_Q_SUB, tq), tq, min(DA_Q_SUB, tq)), F32),
        ],
        compiler_params=_params("parallel", "parallel", "arbitrary"),
        name="diff_attention",
    )(proj, proj, proj, bias_t, lam_row, nw.reshape(dv, 1), *riders)


def _exact_bf16_dot(tri3, x):
    hi = x.astype(BF16)
    r1 = x - hi.astype(F32)
    mid = r1.astype(BF16)
    lo = (r1 - mid.astype(F32)).astype(BF16)
    return jnp.dot(tri3, jnp.concatenate([hi, mid, lo], axis=0), preferred_element_type=F32)


def _hgrn_tables(c):
    n_lv = int(math.log2(c))
    r = np.arange(c)[None, :]
    t = np.arange(c)[:, None]
    blocks_d, blocks_e = [], []
    for lv in range(2, HG_SMALL_LEVELS + 1):
        h = 1 << (lv - 1)
        upper = (t // h) % 2 == 1
        mid_t = (t // h) * h
        blocks_d.append(upper & (r >= mid_t) & (r <= t))
        mid_s = (t // h + 1) * h
        blocks_e.append((~upper) & (r > t) & (r < mid_s))
    stack = np.concatenate(blocks_d + blocks_e + [r <= t], axis=0).astype(np.float32)
    x = np.bitwise_xor(t, r)
    level = np.where(r == t, 0, np.floor(np.log2(np.maximum(x, 1))).astype(np.int32) + 1)
    level = np.where(r > t, -1, level).astype(np.int32)
    return stack, level, n_lv


def _hgrn_kernel(q_ref, f_ref, i_ref, g_ref, lb_ref, nw_ref, tst_ref, lvl_ref, o_ref, st_ref,
                 *, ts, c, n_lv):
    nt = (((1,), (1,)), ((), ()))
    d = LANES
    nh = q_ref.shape[1] // d
    nch = ts // c
    n_small = HG_SMALL_LEVELS

    @pl.when(pl.program_id(2) == 0)
    def _():
        st_ref[...] = jnp.zeros(st_ref.shape, F32)

    lb = lb_ref[...]
    f = lb + (1.0 - lb) * jax.nn.sigmoid(f_ref[...])
    logf = jnp.log(f)
    kk = 1.0 - f
    units = [(ch, hd) for ch in range(nch) for hd in range(nh)]
    lcat = jnp.concatenate([logf[ch * c:(ch + 1) * c, hd * d:(hd + 1) * d] for ch, hd in units], axis=1)
    seg = _exact_bf16_dot(tst_ref[...], lcat)
    lvl = lvl_ref[...]
    nw = nw_ref[...]

    st = [st_ref[hd] for hd in range(nh)]
    for u, (ch, hd) in enumerate(units):
        rows = slice(ch * c, (ch + 1) * c)
        cols = slice(hd * d, (hd + 1) * d)

        def blk(b, u=u):
            return seg[b * c:(b + 1) * c, u * d:(u + 1) * d]

        g_in = blk(2 * n_small - 2)
        g_last = g_in[c - 1:c]
        g_out = g_last - g_in
        q = q_ref[rows, cols]
        q32 = q.astype(F32)
        kc = kk[rows, cols]
        kb = kc.astype(BF16)
        vb = i_ref[rows, cols].astype(BF16)
        a = jnp.where(lvl == 0, lax.dot_general(q, kb, nt, preferred_element_type=F32), 0.0)
        qt = (q32 * f[rows, cols]).astype(BF16)
        a = jnp.where(lvl == 1, lax.dot_general(qt, kb, nt, preferred_element_type=F32), a)
        for lv in range(2, n_lv + 1):
            if lv <= n_small:
                d_up = blk(lv - 2)
                d_lo = blk(n_small + lv - 3)
            else:
                h = 1 << (lv - 1)
                ref = jnp.concatenate(
                    [jnp.broadcast_to(g_in[gi * 2 * h + h - 1:gi * 2 * h + h, :], (2 * h, d))
                     for gi in range(c // (2 * h))], axis=0)
                d_up = jnp.minimum(g_in - ref, 0.0)
                d_lo = jnp.minimum(ref - g_in, 0.0)
            qt = (q32 * jnp.exp(d_up)).astype(BF16)
            kt = (kc * jnp.exp(d_lo)).astype(BF16)
            a = jnp.where(lvl == lv, lax.dot_general(qt, kt, nt, preferred_element_type=F32), a)
        qg = (q32 * jnp.exp(g_in)).astype(BF16)
        o = (jnp.dot(a.astype(BF16), vb, preferred_element_type=F32)
             + lax.dot_general(qg, st[hd].astype(BF16), nt, preferred_element_type=F32))
        kd = (kc * jnp.exp(g_out)).astype(BF16)
        upd = lax.dot_general(vb, kd, (((0,), (0,)), ((), ())), preferred_element_type=F32)
        st[hd] = st[hd] * jnp.exp(g_last) + upd
        g = g_ref[rows, cols]
        o_ref[rows, cols] = (_rms_rows(o, nw) * (g * jax.nn.sigmoid(g))).astype(o_ref.dtype)
    for hd in range(nh):
        st_ref[hd] = st[hd]


def hgrn2(proj_a, proj_b, lb, nw, batch, seq, ts, q_col0, riders=()):
    ns = seq // ts
    d = LANES
    nh = HG_HEADS_PER_STEP
    assert HG_HEADS % nh == 0 and q_col0 % nh == 0
    w = nh * d
    ng = HG_HEADS // nh
    c = min(HG_CHUNK, ts)
    stack, level, n_lv = _hgrn_tables(c)
    assert n_lv >= HG_SMALL_LEVELS
    grid = (batch, ng, ns)
    rider_specs, rider_shapes = _rider_specs(riders, grid)
    kern = _with_cast_riders(functools.partial(_hgrn_kernel, ts=ts, c=c, n_lv=n_lv),
                             n_in=8, n_out=1, n_riders=len(riders))
    return pl.pallas_call(
        kern,
        grid=grid,
        in_specs=[
            pl.BlockSpec((ts, w), lambda b, h, i: (b * ns + i, q_col0 // nh + h)),
            pl.BlockSpec((ts, w), lambda b, h, i: (b * ns + i, h)),
            pl.BlockSpec((ts, w), lambda b, h, i: (b * ns + i, ng + h)),
            pl.BlockSpec((ts, w), lambda b, h, i: (b * ns + i, 2 * ng + h)),
            pl.BlockSpec((1, w), lambda b, h, i: (0, h)),
            pl.BlockSpec((1, d), lambda b, h, i: (0, 0)),
            pl.BlockSpec((stack.shape[0], 3 * stack.shape[1]), lambda b, h, i: (0, 0)),
            pl.BlockSpec(level.shape, lambda b, h, i: (0, 0)),
        ] + rider_specs,
        out_specs=[pl.BlockSpec((ts, w), lambda b, h, i: (b * ns + i, h))] + rider_specs,
        out_shape=[jax.ShapeDtypeStruct((batch * seq, HG_HEADS * d), BF16)] + rider_shapes,
        scratch_shapes=[pltpu.VMEM((nh, d, d), F32)],
        compiler_params=_params("parallel", "parallel", "arbitrary"),
        name="hgrn2",
    )(proj_a, proj_b, proj_b, proj_b, lb, nw.reshape(1, d),
      jnp.asarray(np.concatenate([stack] * 3, axis=1), BF16), jnp.asarray(level), *riders)


def _ffn_up_kernel(x_ref, halo_ref, nw_ref, wa_ref, wb_ref, cwa_ref, cwb_ref, cba_ref, cbb_ref,
                   o_ref, h_ref, u_ref, *, tm, per_batch):
    hl = SUBLANES

    @pl.when(pl.program_id(1) == 0)
    def _():
        nw = nw_ref[...]
        first = (pl.program_id(0) % per_batch) == 0
        halo = _rms_rows(halo_ref[...], nw)
        h_ref[0:hl, :] = jnp.where(first, 0.0, halo).astype(h_ref.dtype)
        h_ref[hl:, :] = _rms_rows(x_ref[...], nw).astype(h_ref.dtype)

    h = h_ref[...]
    tf = o_ref.shape[1]
    cw = min(FFN_COL_CHUNK, tf)

    def cols(cc):
        return slice(cc * cw, (cc + 1) * cw)

    def matmuls(cc):
        u_ref[cc, 0] = jnp.dot(h, wa_ref[:, cols(cc)], preferred_element_type=F32)
        u_ref[cc, 1] = jnp.dot(h, wb_ref[:, cols(cc)], preferred_element_type=F32)

    def conv(cc, half, cw_ref, cb_ref):
        acc = cb_ref[:, cols(cc)]
        for j in range(CONV_WIDTH):
            off = hl - (CONV_WIDTH - 1) + j
            acc = acc + cw_ref[j:j + 1, cols(cc)] * u_ref[cc, half, off:off + tm, :]
        return acc

    def gate(cc):
        a = conv(cc, 0, cwa_ref, cba_ref)
        b = conv(cc, 1, cwb_ref, cbb_ref)
        o_ref[:, cols(cc)] = ((a * jax.nn.sigmoid(a)) * b).astype(o_ref.dtype)

    for cc in range(tf // cw):
        matmuls(cc)
        gate(cc)


def ffn_up(x, nw, w_up, conv_w, conv_b, seq):
    t, d = x.shape
    f = w_up.shape[1] // 2
    tm, tf = _pick(seq, MM_TM), _pick(f, FFN_TF)
    nf = f // tf
    per_batch = seq // tm
    hl = SUBLANES
    rows_per_tile = tm // hl
    cw = min(FFN_COL_CHUNK, tf)
    kern = functools.partial(_ffn_up_kernel, tm=tm, per_batch=per_batch)
    return pl.pallas_call(
        kern,
        grid=(t // tm, nf),
        in_specs=[
            pl.BlockSpec((tm, d), lambda i, j: (i, 0)),
            pl.BlockSpec((hl, d), lambda i, j: (jnp.maximum(i * rows_per_tile - 1, 0), 0)),
            pl.BlockSpec((1, d), lambda i, j: (0, 0)),
            pl.BlockSpec((d, tf), lambda i, j: (0, j)),
            pl.BlockSpec((d, tf), lambda i, j: (0, nf + j)),
            pl.BlockSpec((CONV_WIDTH, tf), lambda i, j: (0, j)),
            pl.BlockSpec((CONV_WIDTH, tf), lambda i, j: (0, nf + j)),
            pl.BlockSpec((1, tf), lambda i, j: (0, j)),
            pl.BlockSpec((1, tf), lambda i, j: (0, nf + j)),
        ],
        out_specs=pl.BlockSpec((tm, tf), lambda i, j: (i, j)),
        out_shape=jax.ShapeDtypeStruct((t, f), BF16),
        scratch_shapes=[
            pltpu.VMEM((tm + hl, d), BF16),
            pltpu.VMEM((tf // cw, 2, tm + hl, cw), F32),
        ],
        compiler_params=_params("parallel", "arbitrary"),
        name="ffn_up",
    )(x, x, nw.reshape(1, d), w_up, w_up, conv_w, conv_w, conv_b.reshape(1, 2 * f), conv_b.reshape(1, 2 * f))


def _ffn_down_kernel(a_ref, w_ref, res_ref, nw_ref, o_ref):
    y = res_ref[...] + jnp.dot(a_ref[...], w_ref[...], preferred_element_type=F32)
    o_ref[...] = _rms_rows(y, nw_ref[...])


def ffn_down(act, w_down, res, nw):
    t, f = act.shape
    d = w_down.shape[1]
    tm = _pick(t, FFN_DOWN_TM)
    return pl.pallas_call(
        _ffn_down_kernel,
        grid=(t // tm,),
        in_specs=[
            pl.BlockSpec((tm, f), lambda i: (i, 0)),
            pl.BlockSpec((f, d), lambda i: (0, 0), pipeline_mode=pl.Buffered(1)),
            pl.BlockSpec((tm, d), lambda i: (i, 0)),
            pl.BlockSpec((1, d), lambda i: (0, 0)),
        ],
        out_specs=pl.BlockSpec((tm, d), lambda i: (i, 0)),
        out_shape=jax.ShapeDtypeStruct((t, d), F32),
        compiler_params=_params("parallel"),
        name="ffn_down",
    )(act, w_down, res, nw.reshape(1, d))


def kernel(x, mem, w_in, w_out, norm_mix_w, lam_q1, lam_k1, lam_q2, lam_k2, da_subln_w,
           hg_lb_raw, hg_norm_w, rel_bias, norm_mem_w, mem_kv_norm_w, w_mq, w_mkv, w_mo,
           norm_ffn_w, w_up, conv_w, conv_b, w_down, final_norm_w):
    batch, seq, d_model = x.shape
    n_mem = mem.shape[1]
    depth = w_in.shape[0]
    assert depth == 1, "single-layer block"
    layer = 0
    t = batch * seq

    da_width = DA_HEADS * LANES
    hg_width = HG_HEADS * LANES
    assert w_in.shape[2] == 3 * da_width + 4 * hg_width
    assert da_subln_w.shape[1] == LANES and hg_norm_w.shape[1] == LANES
    assert lam_q1.shape[1] * 2 == LANES
    n_bf = 3 * da_width + hg_width

    tq = _pick(seq, DA_BLOCK)
    ts = _pick(seq, HG_TILE)

    lam_init = 0.8 - 0.6 * math.exp(-0.3 * layer)
    lam_vecs = jnp.stack([lam_q1[layer], lam_k1[layer], lam_q2[layer], lam_k2[layer]]).astype(F32)
    lam_row, lb = param_tables(lam_vecs, hg_lb_raw.astype(F32), lam_init, layer)

    far = _t5_bucket_np(np.arange(tq + 1, max(seq, tq + 2)))
    assert (far == far[0]).all(), "bias must be constant beyond the sub-diagonal block"
    c = np.arange(2 * tq)
    dist_buckets = np.stack([np.where(c >= tq, _t5_bucket_np(np.maximum(c - tq, 0)), -1),
                             _t5_bucket_np(c)]).astype(np.int32)
    bias_t = bias_tiles(rel_bias.astype(F32), jnp.asarray(dist_buckets), tq, int(far[0]))

    x2d = x.reshape(t, d_model)
    q_scale = (LANES // 2) ** -0.5
    col_scale = np.ones((1, n_bf), np.float32)
    col_scale[:, :da_width] = q_scale
    proj_a, proj_b = in_proj(x2d, norm_mix_w[layer], w_in[layer].astype(BF16), n_bf, jnp.asarray(col_scale))

    d_ff = w_down.shape[1]
    w_down_rows = w_down[layer].reshape(d_model, d_ff)
    da_o, w_up_bf, w_mkv_bf = diff_attention(
        proj_a, bias_t, lam_row, da_subln_w[layer], batch, seq, tq, 1.0 - lam_init,
        riders=(w_up[layer], w_mkv[layer]))
    hg_o, w_down_bf, w_out_bf, w_mq_bf, w_mo_bf = hgrn2(
        proj_a, proj_b, lb, hg_norm_w[layer], batch, seq, ts, 3 * DA_HEADS,
        riders=(w_down_rows, w_out[layer], w_mq[layer], w_mo[layer]))
    w_down_bf = w_down_bf.reshape(d_ff, d_model)

    mkv = norm_matmul(mem.reshape(batch * n_mem, d_model), mem_kv_norm_w[layer], w_mkv_bf, BF16,
                      name="mem_kv_proj")
    x2 = mix_out_mem_attention(da_o, hg_o, w_out_bf, x2d, norm_mem_w[layer], w_mq_bf, mkv, w_mo_bf,
                               seq, n_mem)

    act = ffn_up(x2, norm_ffn_w[layer], w_up_bf, conv_w[layer], conv_b[layer], seq)
    out = ffn_down(act, w_down_bf, x2, final_norm_w)
    return out.reshape(batch, seq, d_model)
```

```python
import functools
import math

import numpy as np
import jax
import jax.numpy as jnp
from jax import lax
from jax.experimental import pallas as pl
from jax.experimental.pallas import tpu as pltpu

DA_HEADS = 8
HG_HEADS = 8
MEM_HEADS = 4
NUM_BUCKETS = 32
MAX_DISTANCE = 128
CONV_WIDTH = 3
EPS = 1e-6

LANES = 128
SUBLANES = 8
VMEM_LIMIT_BYTES = 56 * 1024 * 1024

HG_CHUNK = 128
HG_TILE = 256
HG_HEADS_PER_STEP = 2
HG_SMALL_LEVELS = 3
DA_BLOCK = 512
DA_Q_SUB = 256
MM_TM = 1024
MM_TN = 1024
FFN_TF = 512
FFN_COL_CHUNK = 256
RESIDENT_TM = 512
FFN_DOWN_TM = RESIDENT_TM
NEG_BIG = -1e30

F32 = jnp.float32
BF16 = jnp.bfloat16


def _params(*sem, flags=None):
    return pltpu.CompilerParams(dimension_semantics=sem, vmem_limit_bytes=VMEM_LIMIT_BYTES, flags=flags)


def _rms_rows(x, w):
    ms = jnp.mean(x * x, axis=-1, keepdims=True)
    return (x * lax.rsqrt(ms + EPS)) * w


def _pick(n, pref):
    t = min(pref, n)
    while n % t:
        t //= 2
    return t


def _norm_mm_kernel(x_ref, nw_ref, w_ref, o_ref, h_ref):
    @pl.when(pl.program_id(1) == 0)
    def _():
        h_ref[...] = _rms_rows(x_ref[...], nw_ref[...]).astype(h_ref.dtype)

    o_ref[...] = jnp.dot(h_ref[...], w_ref[...], preferred_element_type=F32).astype(o_ref.dtype)


def norm_matmul(x, nw, w, out_dtype, name="norm_matmul"):
    t, k = x.shape
    n = w.shape[1]
    tm, tn = _pick(t, MM_TM), _pick(n, MM_TN)
    return pl.pallas_call(
        _norm_mm_kernel,
        grid=(t // tm, n // tn),
        in_specs=[
            pl.BlockSpec((tm, k), lambda i, j: (i, 0)),
            pl.BlockSpec((1, k), lambda i, j: (0, 0)),
            pl.BlockSpec((k, tn), lambda i, j: (0, j)),
        ],
        out_specs=pl.BlockSpec((tm, tn), lambda i, j: (i, j)),
        out_shape=jax.ShapeDtypeStruct((t, n), out_dtype),
        scratch_shapes=[pltpu.VMEM((tm, k), BF16)],
        compiler_params=_params("parallel", "arbitrary"),
        name=name,
    )(x, nw.reshape(1, k), w)


BF16_ROWS = 16


def _with_cast_riders(body, n_in, n_out, n_riders):
    def kern(*refs):
        ins, rest = refs[:n_in], refs[n_in:]
        rid_in, rest = rest[:n_riders], rest[n_riders:]
        outs, rest = rest[:n_out], rest[n_out:]
        rid_out, scratch = rest[:n_riders], rest[n_riders:]
        for src, dst in zip(rid_in, rid_out):
            dst[...] = src[...].astype(dst.dtype)
        body(*ins, *outs, *scratch)
    return kern


def _rider_specs(riders, layer, grid):
    n_steps = math.prod(grid)

    def step(*g):
        idx = 0
        for gi, n in zip(g, grid):
            idx = idx * n + gi
        return idx

    in_specs, out_specs, shapes = [], [], []
    for a in riders:
        _, rows, cols = a.shape
        n_slabs = math.gcd(n_steps, rows // BF16_ROWS)
        assert rows % BF16_ROWS == 0 and n_steps % n_slabs == 0, (a.shape, grid)
        hold = n_steps // n_slabs
        in_specs.append(pl.BlockSpec((None, rows // n_slabs, cols),
                                     lambda *g, hold=hold: (layer, step(*g) // hold, 0)))
        out_specs.append(pl.BlockSpec((rows // n_slabs, cols),
                                      lambda *g, hold=hold: (step(*g) // hold, 0)))
        shapes.append(jax.ShapeDtypeStruct((rows, cols), BF16))
    return in_specs, out_specs, shapes


def _in_proj_kernel(x_ref, nw_ref, w_ref, cs_ref, oa_ref, ob_ref, h_ref, *, na_tiles):
    j = pl.program_id(1)

    @pl.when(j == 0)
    def _():
        h_ref[...] = _rms_rows(x_ref[...], nw_ref[...]).astype(h_ref.dtype)

    y = jnp.dot(h_ref[...], w_ref[...], preferred_element_type=F32)

    @pl.when(j < na_tiles)
    def _():
        oa_ref[...] = (y * cs_ref[...]).astype(oa_ref.dtype)

    @pl.when(j >= na_tiles)
    def _():
        ob_ref[...] = y


def in_proj(x, nw, w, n_a, col_scale):
    t, k = x.shape
    n = w.shape[1]
    tm, tn = _pick(t, MM_TM), math.gcd(_pick(n_a, MM_TN), _pick(n - n_a, MM_TN))
    na_tiles = n_a // tn
    return pl.pallas_call(
        functools.partial(_in_proj_kernel, na_tiles=na_tiles),
        grid=(t // tm, n // tn),
        in_specs=[
            pl.BlockSpec((tm, k), lambda i, j: (i, 0)),
            pl.BlockSpec((1, k), lambda i, j: (0, 0)),
            pl.BlockSpec((k, tn), lambda i, j: (0, j)),
            pl.BlockSpec((1, tn), lambda i, j: (0, jnp.minimum(j, na_tiles - 1))),
        ],
        out_specs=[
            pl.BlockSpec((tm, tn), lambda i, j: (i, jnp.minimum(j, na_tiles - 1))),
            pl.BlockSpec((tm, tn), lambda i, j: (i, jnp.maximum(j - na_tiles, 0))),
        ],
        out_shape=(jax.ShapeDtypeStruct((t, n_a), BF16), jax.ShapeDtypeStruct((t, n - n_a), F32)),
        scratch_shapes=[pltpu.VMEM((tm, k), BF16)],
        compiler_params=_params("parallel", "arbitrary"),
        name="in_proj",
    )(x, nw.reshape(1, k), w, col_scale)


def _mix_mem_kernel(a0_ref, a1_ref, w0_ref, w1_ref, x_ref, nw_ref, wq_ref, k_ref, v_ref, wo_ref,
                    o_ref, att_ref, *, scale):
    x1 = x_ref[...]
    x1 = x1 + jnp.dot(a0_ref[...], w0_ref[...], preferred_element_type=F32)
    x1 = x1 + jnp.dot(a1_ref[...], w1_ref[...], preferred_element_type=F32)
    hq = _rms_rows(x1, nw_ref[...]).astype(BF16)
    mq = jnp.dot(hq, wq_ref[...], preferred_element_type=F32).astype(BF16)
    dh = mq.shape[1] // MEM_HEADS
    for hd in range(MEM_HEADS):
        sl = slice(hd * dh, (hd + 1) * dh)
        s = lax.dot_general(mq[:, sl], k_ref[:, sl], (((1,), (1,)), ((), ())),
                            preferred_element_type=F32) * scale
        e = jnp.exp(s - jnp.max(s, axis=-1, keepdims=True))
        p = e / jnp.sum(e, axis=-1, keepdims=True)
        att_ref[:, sl] = jnp.dot(p.astype(BF16), v_ref[:, sl],
                                 preferred_element_type=F32).astype(att_ref.dtype)
    o_ref[...] = x1 + jnp.dot(att_ref[...], wo_ref[...], preferred_element_type=F32)


def mix_out_mem_attention(a0, a1, w_out, x, nw_mem, w_mq, mkv, w_mo, seq, n_mem):
    t, k0 = a0.shape
    k1 = a1.shape[1]
    d = x.shape[1]
    assert k0 == k1 and w_out.shape == (k0 + k1, d)
    tm = _pick(seq, RESIDENT_TM)
    per_batch = seq // tm
    scale = (d // MEM_HEADS) ** -0.5
    once = pl.Buffered(1)
    return pl.pallas_call(
        functools.partial(_mix_mem_kernel, scale=scale),
        grid=(t // tm,),
        in_specs=[
            pl.BlockSpec((tm, k0), lambda i: (i, 0)),
            pl.BlockSpec((tm, k1), lambda i: (i, 0)),
            pl.BlockSpec((k0, d), lambda i: (0, 0), pipeline_mode=once),
            pl.BlockSpec((k1, d), lambda i: (1, 0), pipeline_mode=once),
            pl.BlockSpec((tm, d), lambda i: (i, 0)),
            pl.BlockSpec((1, d), lambda i: (0, 0)),
            pl.BlockSpec((d, d), lambda i: (0, 0), pipeline_mode=once),
            pl.BlockSpec((n_mem, d), lambda i: (i // per_batch, 0), pipeline_mode=once),
            pl.BlockSpec((n_mem, d), lambda i: (i // per_batch, 1), pipeline_mode=once),
            pl.BlockSpec((d, d), lambda i: (0, 0), pipeline_mode=once),
        ],
        out_specs=pl.BlockSpec((tm, d), lambda i: (i, 0)),
        out_shape=jax.ShapeDtypeStruct((t, d), F32),
        scratch_shapes=[pltpu.VMEM((tm, d), BF16)],
        compiler_params=_params("parallel"),
        name="mix_out_mem_attention",
    )(a0, a1, w_out, w_out, x, nw_mem.reshape(1, d), w_mq, mkv, mkv, w_mo)


def _t5_bucket_np(n):
    max_exact = NUM_BUCKETS // 2
    nf = np.maximum(n, 1).astype(np.float32)
    large = max_exact + (np.log(nf / max_exact) / math.log(MAX_DISTANCE / max_exact)
                         * (NUM_BUCKETS - max_exact)).astype(np.int32)
    large = np.minimum(large, NUM_BUCKETS - 1)
    return np.where(n < max_exact, n, large).astype(np.int32)


def _tables_kernel(lamv_ref, lbraw_ref, lam_ref, lb_ref, *, lam_init, layer):
    v = lamv_ref[...]
    s1 = jnp.sum(v[0:1] * v[1:2], axis=-1, keepdims=True)
    s2 = jnp.sum(v[2:3] * v[3:4], axis=-1, keepdims=True)
    lam = jnp.exp(s1) - jnp.exp(s2) + lam_init
    lam_ref[...] = jnp.broadcast_to(lam, lam_ref.shape)
    raw = lbraw_ref[...]
    e = jnp.exp(raw - jnp.max(raw, axis=0, keepdims=True))
    sm = e / jnp.sum(e, axis=0, keepdims=True)
    lb_ref[...] = jnp.sum(sm[0:layer + 1], axis=0, keepdims=True)


def param_tables(lam_vecs, lb_raw, lam_init, layer):
    kw = lb_raw.shape[1]
    return pl.pallas_call(
        functools.partial(_tables_kernel, lam_init=lam_init, layer=layer),
        out_shape=(jax.ShapeDtypeStruct((1, LANES), F32), jax.ShapeDtypeStruct((1, kw), F32)),
        name="param_tables",
    )(lam_vecs, lb_raw)


def _bias_tiles_kernel(rb_ref, bk_ref, o_ref, *, tq, far_bucket):
    h = pl.program_id(0)
    bk = bk_ref[...]
    vec = jnp.full(bk.shape, NEG_BIG, F32)
    for n in range(NUM_BUCKETS):
        vec = jnp.where(bk == n, rb_ref[n, h], vec)
    for t in range(2):
        rows = jnp.broadcast_to(vec[t:t + 1], (tq, 2 * tq))
        o_ref[0, t] = pltpu.roll(rows, 0, 1, stride=1, stride_axis=0)[:, tq:]
    o_ref[0, FAR_TILE] = jnp.full((tq, tq), rb_ref[far_bucket, h], F32)


def bias_tiles(rel_bias, dist_buckets, tq, far_bucket):
    nb, h = rel_bias.shape
    return pl.pallas_call(
        functools.partial(_bias_tiles_kernel, tq=tq, far_bucket=far_bucket),
        grid=(h,),
        in_specs=[
            pl.BlockSpec(memory_space=pltpu.SMEM),
            pl.BlockSpec(dist_buckets.shape, lambda i: (0, 0)),
        ],
        out_specs=pl.BlockSpec((1, FAR_TILE + 1, tq, tq), lambda i: (i, 0, 0, 0)),
        out_shape=jax.ShapeDtypeStruct((h, FAR_TILE + 1, tq, tq), F32),
        compiler_params=_params("arbitrary"),
        name="bias_tiles",
    )(rel_bias, dist_buckets)


ONES_ROWS = 16
FAR_TILE = 2


def _diff_attn_kernel(q_ref, k_ref, v_ref, bias_ref, lam_ref, nw_ref, o_ref,
                      vt_ref, m_ref, acc_ref, st_ref, *, tq, out_scale):
    i = pl.program_id(2)
    dv = v_ref.shape[1]
    dh = dv // 2
    nkv = v_ref.shape[0] // tq

    @pl.when(i == 0)
    def _():
        for c in range(nkv):
            vt_ref[c, 0:dv, :] = v_ref[c * tq:(c + 1) * tq, :].astype(F32).T.astype(BF16)
            vt_ref[c, dv:, :] = jnp.ones((ONES_ROWS, tq), BF16)

    q = q_ref[...]
    lane = lax.broadcasted_iota(jnp.int32, (1, dv), 1)
    qm = (jnp.where(lane < dh, q, jnp.zeros_like(q)), jnp.where(lane >= dh, q, jnp.zeros_like(q)))

    m_ref[...] = jnp.full(m_ref.shape, NEG_BIG, F32)
    acc_ref[...] = jnp.zeros(acc_ref.shape, F32)

    qw = min(DA_Q_SUB, tq)
    chains = [(mp, slice(qs * qw, (qs + 1) * qw)) for mp in range(2) for qs in range(tq // qw)]

    def scores(n, j):
        mp, cols = chains[n]
        kb = k_ref[pl.ds(pl.multiple_of(j * tq, tq), tq), :]
        st_ref[n] = lax.dot_general(kb, qm[mp][cols, :], (((1,), (1,)), ((), ())),
                                    preferred_element_type=F32)

    def softmax_pv(n, vt, tile):
        mp, cols = chains[n]
        st = st_ref[n] + bias_ref[0, tile, :, cols]
        m_old = m_ref[mp, :, cols]
        m_new = jnp.maximum(m_old, jnp.max(st, axis=0, keepdims=True))
        p = jnp.exp(st - m_new).astype(BF16)
        alpha = jnp.exp(m_old - m_new)
        acc_ref[mp, :, cols] = (alpha * acc_ref[mp, :, cols]
                                + jnp.dot(vt, p, preferred_element_type=F32))
        m_ref[mp, :, cols] = m_new

    def step(j, carry):
        vt = vt_ref[j]
        tile = jnp.minimum(i - j, FAR_TILE)
        for n in range(len(chains)):
            if n + 1 < len(chains):
                scores(n + 1, j)
            else:
                scores(0, jnp.minimum(j + 1, i))
            softmax_pv(n, vt, tile)
        return carry

    scores(0, 0)
    lax.fori_loop(0, i + 1, step, 0)

    o0 = acc_ref[0, 0:dv, :] / acc_ref[0, dv:dv + 1, :]
    o1 = acc_ref[1, 0:dv, :] / acc_ref[1, dv:dv + 1, :]
    o = o0 - lam_ref[0:1, 0:1] * o1
    ms = jnp.mean(o * o, axis=0, keepdims=True)
    y = (o * lax.rsqrt(ms + EPS)) * (nw_ref[...] * out_scale)
    o_ref[...] = y.T.astype(o_ref.dtype)


def diff_attention(proj, bias_t, lam_row, nw, batch, seq, tq, out_scale, riders=(), layer=0):
    nq = seq // tq
    dv = LANES
    grid = (batch, DA_HEADS, nq)
    rider_in, rider_out, rider_shapes = _rider_specs(riders, layer, grid)
    kern = _with_cast_riders(functools.partial(_diff_attn_kernel, tq=tq, out_scale=out_scale),
                             n_in=6, n_out=1, n_riders=len(riders))
    return pl.pallas_call(
        kern,
        grid=grid,
        in_specs=[
            pl.BlockSpec((tq, dv), lambda b, h, i: (b * nq + i, h)),
            pl.BlockSpec((seq, dv), lambda b, h, i: (b, DA_HEADS + h)),
            pl.BlockSpec((seq, dv), lambda b, h, i: (b, 2 * DA_HEADS + h)),
            pl.BlockSpec((1, FAR_TILE + 1, tq, tq), lambda b, h, i: (h, 0, 0, 0)),
            pl.BlockSpec((1, LANES), lambda b, h, i: (0, 0)),
            pl.BlockSpec((dv, 1), lambda b, h, i: (0, 0)),
        ] + rider_in,
        out_specs=[pl.BlockSpec((tq, dv), lambda b, h, i: (b * nq + i, h))] + rider_out,
        out_shape=[jax.ShapeDtypeStruct((batch * seq, DA_HEADS * dv), BF16)] + rider_shapes,
        scratch_shapes=[
            pltpu.VMEM((nq, dv + ONES_ROWS, tq), BF16),
            pltpu.VMEM((2, 1, tq), F32),
            pltpu.VMEM((2, dv + ONES_ROWS, tq), F32),
            pltpu.VMEM((2 * tq // min(DA_Q_SUB, tq), tq, min(DA_Q_SUB, tq)), F32),
        ],
        compiler_params=_params("parallel", "parallel", "arbitrary"),
        name="diff_attention",
    )(proj, proj, proj, bias_t, lam_row, nw.reshape(dv, 1), *riders)


def _exact_bf16_dot(tri3, x):
    hi = x.astype(BF16)
    r1 = x - hi.astype(F32)
    mid = r1.astype(BF16)
    lo = (r1 - mid.astype(F32)).astype(BF16)
    return jnp.dot(tri3, jnp.concatenate([hi, mid, lo], axis=0), preferred_element_type=F32)


def _hgrn_tables(c):
    n_lv = int(math.log2(c))
    r = np.arange(c)[None, :]
    t = np.arange(c)[:, None]
    blocks_d, blocks_e = [], []
    for lv in range(2, HG_SMALL_LEVELS + 1):
        h = 1 << (lv - 1)
        upper = (t // h) % 2 == 1
        mid_t = (t // h) * h
        blocks_d.append(upper & (r >= mid_t) & (r <= t))
        mid_s = (t // h + 1) * h
        blocks_e.append((~upper) & (r > t) & (r < mid_s))
    stack = np.concatenate(blocks_d + blocks_e + [r <= t], axis=0).astype(np.float32)
    x = np.bitwise_xor(t, r)
    level = np.where(r == t, 0, np.floor(np.log2(np.maximum(x, 1))).astype(np.int32) + 1)
    level = np.where(r > t, -1, level).astype(np.int32)
    return stack, level, n_lv


def _hgrn_kernel(q_ref, f_ref, i_ref, g_ref, lb_ref, nw_ref, tst_ref, lvl_ref, o_ref, st_ref,
                 *, ts, c, n_lv):
    nt = (((1,), (1,)), ((), ()))
    d = LANES
    nh = q_ref.shape[1] // d
    nch = ts // c
    n_small = HG_SMALL_LEVELS

    @pl.when(pl.program_id(2) == 0)
    def _():
        st_ref[...] = jnp.zeros(st_ref.shape, F32)

    lb = lb_ref[...]
    f = lb + (1.0 - lb) * jax.nn.sigmoid(f_ref[...])
    logf = jnp.log(f)
    kk = 1.0 - f
    units = [(ch, hd) for ch in range(nch) for hd in range(nh)]
    lcat = jnp.concatenate([logf[ch * c:(ch + 1) * c, hd * d:(hd + 1) * d] for ch, hd in units], axis=1)
    seg = _exact_bf16_dot(tst_ref[...], lcat)
    lvl = lvl_ref[...]
    nw = nw_ref[...]

    st = [st_ref[hd] for hd in range(nh)]
    for u, (ch, hd) in enumerate(units):
        rows = slice(ch * c, (ch + 1) * c)
        cols = slice(hd * d, (hd + 1) * d)

        def blk(b, u=u):
            return seg[b * c:(b + 1) * c, u * d:(u + 1) * d]

        g_in = blk(2 * n_small - 2)
        g_last = g_in[c - 1:c]
        g_out = g_last - g_in
        q = q_ref[rows, cols]
        q32 = q.astype(F32)
        kc = kk[rows, cols]
        kb = kc.astype(BF16)
        vb = i_ref[rows, cols].astype(BF16)
        a = jnp.where(lvl == 0, lax.dot_general(q, kb, nt, preferred_element_type=F32), 0.0)
        qt = (q32 * f[rows, cols]).astype(BF16)
        a = jnp.where(lvl == 1, lax.dot_general(qt, kb, nt, preferred_element_type=F32), a)
        for lv in range(2, n_lv + 1):
            if lv <= n_small:
                d_up = blk(lv - 2)
                d_lo = blk(n_small + lv - 3)
            else:
                h = 1 << (lv - 1)
                ref = jnp.concatenate(
                    [jnp.broadcast_to(g_in[gi * 2 * h + h - 1:gi * 2 * h + h, :], (2 * h, d))
                     for gi in range(c // (2 * h))], axis=0)
                d_up = jnp.minimum(g_in - ref, 0.0)
                d_lo = jnp.minimum(ref - g_in, 0.0)
            qt = (q32 * jnp.exp(d_up)).astype(BF16)
            kt = (kc * jnp.exp(d_lo)).astype(BF16)
            a = jnp.where(lvl == lv, lax.dot_general(qt, kt, nt, preferred_element_type=F32), a)
        qg = (q32 * jnp.exp(g_in)).astype(BF16)
        o = (jnp.dot(a.astype(BF16), vb, preferred_element_type=F32)
             + lax.dot_general(qg, st[hd].astype(BF16), nt, preferred_element_type=F32))
        kd = (kc * jnp.exp(g_out)).astype(BF16)
        upd = lax.dot_general(vb, kd, (((0,), (0,)), ((), ())), preferred_element_type=F32)
        st[hd] = st[hd] * jnp.exp(g_last) + upd
        g = g_ref[rows, cols]
        o_ref[rows, cols] = (_rms_rows(o, nw) * (g * jax.nn.sigmoid(g))).astype(o_ref.dtype)
    for hd in range(nh):
        st_ref[hd] = st[hd]


def hgrn2(proj_a, proj_b, lb, nw, batch, seq, ts, q_col0, riders=(), layer=0):
    ns = seq // ts
    d = LANES
    nh = HG_HEADS_PER_STEP
    assert HG_HEADS % nh == 0 and q_col0 % nh == 0
    w = nh * d
    ng = HG_HEADS // nh
    c = min(HG_CHUNK, ts)
    stack, level, n_lv = _hgrn_tables(c)
    assert n_lv >= HG_SMALL_LEVELS
    grid = (batch, ng, ns)
    rider_in, rider_out, rider_shapes = _rider_specs(riders, layer, grid)
    kern = _with_cast_riders(functools.partial(_hgrn_kernel, ts=ts, c=c, n_lv=n_lv),
                             n_in=8, n_out=1, n_riders=len(riders))
    return pl.pallas_call(
        kern,
        grid=grid,
        in_specs=[
            pl.BlockSpec((ts, w), lambda b, h, i: (b * ns + i, q_col0 // nh + h)),
            pl.BlockSpec((ts, w), lambda b, h, i: (b * ns + i, h)),
            pl.BlockSpec((ts, w), lambda b, h, i: (b * ns + i, ng + h)),
            pl.BlockSpec((ts, w), lambda b, h, i: (b * ns + i, 2 * ng + h)),
            pl.BlockSpec((1, w), lambda b, h, i: (0, h)),
            pl.BlockSpec((1, d), lambda b, h, i: (0, 0)),
            pl.BlockSpec((stack.shape[0], 3 * stack.shape[1]), lambda b, h, i: (0, 0)),
            pl.BlockSpec(level.shape, lambda b, h, i: (0, 0)),
        ] + rider_in,
        out_specs=[pl.BlockSpec((ts, w), lambda b, h, i: (b * ns + i, h))] + rider_out,
        out_shape=[jax.ShapeDtypeStruct((batch * seq, HG_HEADS * d), BF16)] + rider_shapes,
        scratch_shapes=[pltpu.VMEM((nh, d, d), F32)],
        compiler_params=_params("parallel", "parallel", "arbitrary"),
        name="hgrn2",
    )(proj_a, proj_b, proj_b, proj_b, lb, nw.reshape(1, d),
      jnp.asarray(np.concatenate([stack] * 3, axis=1), BF16), jnp.asarray(level), *riders)


def _ffn_up_kernel(x_ref, halo_ref, nw_ref, wa_ref, wb_ref, cwa_ref, cwb_ref, cba_ref, cbb_ref,
                   o_ref, h_ref, u_ref, *, tm, per_batch):
    hl = SUBLANES

    @pl.when(pl.program_id(1) == 0)
    def _():
        nw = nw_ref[...]
        first = (pl.program_id(0) % per_batch) == 0
        halo = _rms_rows(halo_ref[...], nw)
        h_ref[0:hl, :] = jnp.where(first, 0.0, halo).astype(h_ref.dtype)
        h_ref[hl:, :] = _rms_rows(x_ref[...], nw).astype(h_ref.dtype)

    h = h_ref[...]
    tf = o_ref.shape[1]
    cw = min(FFN_COL_CHUNK, tf)

    def cols(cc):
        return slice(cc * cw, (cc + 1) * cw)

    def matmuls(cc):
        u_ref[cc, 0] = jnp.dot(h, wa_ref[:, cols(cc)], preferred_element_type=F32)
        u_ref[cc, 1] = jnp.dot(h, wb_ref[:, cols(cc)], preferred_element_type=F32)

    def conv(cc, half, cw_ref, cb_ref):
        acc = cb_ref[:, cols(cc)]
        for j in range(CONV_WIDTH):
            off = hl - (CONV_WIDTH - 1) + j
            acc = acc + cw_ref[j:j + 1, cols(cc)] * u_ref[cc, half, off:off + tm, :]
        return acc

    def gate(cc):
        a = conv(cc, 0, cwa_ref, cba_ref)
        b = conv(cc, 1, cwb_ref, cbb_ref)
        o_ref[:, cols(cc)] = ((a * jax.nn.sigmoid(a)) * b).astype(o_ref.dtype)

    for cc in range(tf // cw):
        matmuls(cc)
        gate(cc)


def ffn_up(x, nw, w_up, conv_w, conv_b, seq):
    t, d = x.shape
    f = w_up.shape[1] // 2
    tm, tf = _pick(seq, MM_TM), _pick(f, FFN_TF)
    nf = f // tf
    per_batch = seq // tm
    hl = SUBLANES
    rows_per_tile = tm // hl
    cw = min(FFN_COL_CHUNK, tf)
    kern = functools.partial(_ffn_up_kernel, tm=tm, per_batch=per_batch)
    return pl.pallas_call(
        kern,
        grid=(t // tm, nf),
        in_specs=[
            pl.BlockSpec((tm, d), lambda i, j: (i, 0)),
            pl.BlockSpec((hl, d), lambda i, j: (jnp.maximum(i * rows_per_tile - 1, 0), 0)),
            pl.BlockSpec((1, d), lambda i, j: (0, 0)),
            pl.BlockSpec((d, tf), lambda i, j: (0, j)),
            pl.BlockSpec((d, tf), lambda i, j: (0, nf + j)),
            pl.BlockSpec((CONV_WIDTH, tf), lambda i, j: (0, j)),
            pl.BlockSpec((CONV_WIDTH, tf), lambda i, j: (0, nf + j)),
            pl.BlockSpec((1, tf), lambda i, j: (0, j)),
            pl.BlockSpec((1, tf), lambda i, j: (0, nf + j)),
        ],
        out_specs=pl.BlockSpec((tm, tf), lambda i, j: (i, j)),
        out_shape=jax.ShapeDtypeStruct((t, f), BF16),
        scratch_shapes=[
            pltpu.VMEM((tm + hl, d), BF16),
            pltpu.VMEM((tf // cw, 2, tm + hl, cw), F32),
        ],
        compiler_params=_params("parallel", "arbitrary"),
        name="ffn_up",
    )(x, x, nw.reshape(1, d), w_up, w_up, conv_w, conv_w, conv_b.reshape(1, 2 * f), conv_b.reshape(1, 2 * f))


def _ffn_down_kernel(a_ref, w_ref, res_ref, nw_ref, o_ref):
    y = res_ref[...] + jnp.dot(a_ref[...], w_ref[...], preferred_element_type=F32)
    o_ref[...] = _rms_rows(y, nw_ref[...])


def ffn_down(act, w_down, res, nw):
    t, f = act.shape
    d = w_down.shape[1]
    tm = _pick(t, FFN_DOWN_TM)
    return pl.pallas_call(
        _ffn_down_kernel,
        grid=(t // tm,),
        in_specs=[
            pl.BlockSpec((tm, f), lambda i: (i, 0)),
            pl.BlockSpec((f, d), lambda i: (0, 0), pipeline_mode=pl.Buffered(1)),
            pl.BlockSpec((tm, d), lambda i: (i, 0)),
            pl.BlockSpec((1, d), lambda i: (0, 0)),
        ],
        out_specs=pl.BlockSpec((tm, d), lambda i: (i, 0)),
        out_shape=jax.ShapeDtypeStruct((t, d), F32),
        compiler_params=_params("parallel"),
        name="ffn_down",
    )(act, w_down, res, nw.reshape(1, d))


def kernel(x, mem, w_in, w_out, norm_mix_w, lam_q1, lam_k1, lam_q2, lam_k2, da_subln_w,
           hg_lb_raw, hg_norm_w, rel_bias, norm_mem_w, mem_kv_norm_w, w_mq, w_mkv, w_mo,
           norm_ffn_w, w_up, conv_w, conv_b, w_down, final_norm_w):
    batch, seq, d_model = x.shape
    n_mem = mem.shape[1]
    depth = w_in.shape[0]
    assert depth == 1, "single-layer block"
    layer = 0
    t = batch * seq

    da_width = DA_HEADS * LANES
    hg_width = HG_HEADS * LANES
    assert w_in.shape[2] == 3 * da_width + 4 * hg_width
    assert da_subln_w.shape[1] == LANES and hg_norm_w.shape[1] == LANES
    assert lam_q1.shape[1] * 2 == LANES
    n_bf = 3 * da_width + hg_width

    tq = _pick(seq, DA_BLOCK)
    ts = _pick(seq, HG_TILE)

    lam_init = 0.8 - 0.6 * math.exp(-0.3 * layer)
    lam_vecs = jnp.stack([lam_q1[layer], lam_k1[layer], lam_q2[layer], lam_k2[layer]]).astype(F32)
    lam_row, lb = param_tables(lam_vecs, hg_lb_raw.astype(F32), lam_init, layer)

    far = _t5_bucket_np(np.arange(tq + 1, max(seq, tq + 2)))
    assert (far == far[0]).all(), "bias must be constant beyond the sub-diagonal block"
    c = np.arange(2 * tq)
    dist_buckets = np.stack([np.where(c >= tq, _t5_bucket_np(np.maximum(c - tq, 0)), -1),
                             _t5_bucket_np(c)]).astype(np.int32)
    bias_t = bias_tiles(rel_bias.astype(F32), jnp.asarray(dist_buckets), tq, int(far[0]))

    x2d = x.reshape(t, d_model)
    q_scale = (LANES // 2) ** -0.5
    col_scale = np.ones((1, n_bf), np.float32)
    col_scale[:, :da_width] = q_scale
    proj_a, proj_b = in_proj(x2d, norm_mix_w[layer], w_in[layer].astype(BF16), n_bf, jnp.asarray(col_scale))

    da_o, w_up_bf, w_mkv_bf = diff_attention(
        proj_a, bias_t, lam_row, da_subln_w[layer], batch, seq, tq, 1.0 - lam_init,
        riders=(w_up, w_mkv), layer=layer)
    hg_o, w_down_bf, w_out_bf, w_mq_bf, w_mo_bf = hgrn2(
        proj_a, proj_b, lb, hg_norm_w[layer], batch, seq, ts, 3 * DA_HEADS,
        riders=(w_down, w_out, w_mq, w_mo), layer=layer)

    mkv = norm_matmul(mem.reshape(batch * n_mem, d_model), mem_kv_norm_w[layer], w_mkv_bf, BF16,
                      name="mem_kv_proj")
    x2 = mix_out_mem_attention(da_o, hg_o, w_out_bf, x2d, norm_mem_w[layer], w_mq_bf, mkv, w_mo_bf,
                               seq, n_mem)

    act = ffn_up(x2, norm_ffn_w[layer], w_up_bf, conv_w[layer], conv_b[layer], seq)
    out = ffn_down(act, w_down_bf, x2, final_norm_w)
    return out.reshape(batch, seq, d_model)
```

```python
import functools
import math

import numpy as np
import jax
import jax.numpy as jnp
from jax import lax
from jax.experimental import pallas as pl
from jax.experimental.pallas import tpu as pltpu

DA_HEADS = 8
HG_HEADS = 8
MEM_HEADS = 4
NUM_BUCKETS = 32
MAX_DISTANCE = 128
CONV_WIDTH = 3
EPS = 1e-6

LANES = 128
SUBLANES = 8
VMEM_LIMIT_BYTES = 56 * 1024 * 1024

HG_CHUNK = 128
HG_TILE = 512
HG_HEADS_PER_STEP = 4
HG_SMALL_LEVELS = 3
DA_BLOCK = 512
DA_Q_SUB = 512
MM_TM = 1024
MM_TN = 1024
FFN_TF = 512
FFN_COL_CHUNK = 256
RESIDENT_TM = 512
FFN_DOWN_TM = RESIDENT_TM
NEG_BIG = -1e30

F32 = jnp.float32
BF16 = jnp.bfloat16


def _params(*sem, flags=None):
    return pltpu.CompilerParams(dimension_semantics=sem, vmem_limit_bytes=VMEM_LIMIT_BYTES, flags=flags)


def _rms_rows(x, w):
    ms = jnp.mean(x * x, axis=-1, keepdims=True)
    return (x * lax.rsqrt(ms + EPS)) * w


def _pick(n, pref):
    t = min(pref, n)
    while n % t:
        t //= 2
    return t


def _norm_mm_kernel(x_ref, nw_ref, w_ref, o_ref, h_ref):
    @pl.when(pl.program_id(1) == 0)
    def _():
        h_ref[...] = _rms_rows(x_ref[...], nw_ref[...]).astype(h_ref.dtype)

    o_ref[...] = jnp.dot(h_ref[...], w_ref[...], preferred_element_type=F32).astype(o_ref.dtype)


def norm_matmul(x, nw, w, out_dtype, name="norm_matmul"):
    t, k = x.shape
    n = w.shape[1]
    tm, tn = _pick(t, MM_TM), _pick(n, MM_TN)
    return pl.pallas_call(
        _norm_mm_kernel,
        grid=(t // tm, n // tn),
        in_specs=[
            pl.BlockSpec((tm, k), lambda i, j: (i, 0)),
            pl.BlockSpec((1, k), lambda i, j: (0, 0)),
            pl.BlockSpec((k, tn), lambda i, j: (0, j)),
        ],
        out_specs=pl.BlockSpec((tm, tn), lambda i, j: (i, j)),
        out_shape=jax.ShapeDtypeStruct((t, n), out_dtype),
        scratch_shapes=[pltpu.VMEM((tm, k), BF16)],
        compiler_params=_params("parallel", "arbitrary"),
        name=name,
    )(x, nw.reshape(1, k), w)


BF16_ROWS = 16


def _with_cast_riders(body, n_in, n_out, n_riders):
    def kern(*refs):
        ins, rest = refs[:n_in], refs[n_in:]
        rid_in, rest = rest[:n_riders], rest[n_riders:]
        outs, rest = rest[:n_out], rest[n_out:]
        rid_out, scratch = rest[:n_riders], rest[n_riders:]
        for src, dst in zip(rid_in, rid_out):
            dst[...] = src[...].astype(dst.dtype)
        body(*ins, *outs, *scratch)
    return kern


def _rider_specs(riders, layer, grid):
    n_steps = math.prod(grid)

    def step(*g):
        idx = 0
        for gi, n in zip(g, grid):
            idx = idx * n + gi
        return idx

    in_specs, out_specs, shapes = [], [], []
    for a in riders:
        _, rows, cols = a.shape
        n_slabs = math.gcd(n_steps, rows // BF16_ROWS)
        assert rows % BF16_ROWS == 0 and n_steps % n_slabs == 0, (a.shape, grid)
        hold = n_steps // n_slabs
        in_specs.append(pl.BlockSpec((None, rows // n_slabs, cols),
                                     lambda *g, hold=hold: (layer, step(*g) // hold, 0)))
        out_specs.append(pl.BlockSpec((rows // n_slabs, cols),
                                      lambda *g, hold=hold: (step(*g) // hold, 0)))
        shapes.append(jax.ShapeDtypeStruct((rows, cols), BF16))
    return in_specs, out_specs, shapes


def _in_proj_kernel(x_ref, nw_ref, w_ref, cs_ref, oa_ref, ob_ref, h_ref, *, na_tiles):
    j = pl.program_id(1)

    @pl.when(j == 0)
    def _():
        h_ref[...] = _rms_rows(x_ref[...], nw_ref[...]).astype(h_ref.dtype)

    y = jnp.dot(h_ref[...], w_ref[...], preferred_element_type=F32)

    @pl.when(j < na_tiles)
    def _():
        oa_ref[...] = (y * cs_ref[...]).astype(oa_ref.dtype)

    @pl.when(j >= na_tiles)
    def _():
        ob_ref[...] = y


def in_proj(x, nw, w, n_a, col_scale):
    t, k = x.shape
    n = w.shape[1]
    tm, tn = _pick(t, MM_TM), math.gcd(_pick(n_a, MM_TN), _pick(n - n_a, MM_TN))
    na_tiles = n_a // tn
    return pl.pallas_call(
        functools.partial(_in_proj_kernel, na_tiles=na_tiles),
        grid=(t // tm, n // tn),
        in_specs=[
            pl.BlockSpec((tm, k), lambda i, j: (i, 0)),
            pl.BlockSpec((1, k), lambda i, j: (0, 0)),
            pl.BlockSpec((k, tn), lambda i, j: (0, j)),
            pl.BlockSpec((1, tn), lambda i, j: (0, jnp.minimum(j, na_tiles - 1))),
        ],
        out_specs=[
            pl.BlockSpec((tm, tn), lambda i, j: (i, jnp.minimum(j, na_tiles - 1))),
            pl.BlockSpec((tm, tn), lambda i, j: (i, jnp.maximum(j - na_tiles, 0))),
        ],
        out_shape=(jax.ShapeDtypeStruct((t, n_a), BF16), jax.ShapeDtypeStruct((t, n - n_a), F32)),
        scratch_shapes=[pltpu.VMEM((tm, k), BF16)],
        compiler_params=_params("parallel", "arbitrary"),
        name="in_proj",
    )(x, nw.reshape(1, k), w, col_scale)


def _mix_mem_kernel(a0_ref, a1_ref, w0_ref, w1_ref, x_ref, nw_ref, wq_ref, k_ref, v_ref, wo_ref,
                    o_ref, att_ref, *, scale):
    x1 = x_ref[...]
    x1 = x1 + jnp.dot(a0_ref[...], w0_ref[...], preferred_element_type=F32)
    x1 = x1 + jnp.dot(a1_ref[...], w1_ref[...], preferred_element_type=F32)
    hq = _rms_rows(x1, nw_ref[...]).astype(BF16)
    mq = jnp.dot(hq, wq_ref[...], preferred_element_type=F32).astype(BF16)
    dh = mq.shape[1] // MEM_HEADS
    for hd in range(MEM_HEADS):
        sl = slice(hd * dh, (hd + 1) * dh)
        s = lax.dot_general(mq[:, sl], k_ref[:, sl], (((1,), (1,)), ((), ())),
                            preferred_element_type=F32) * scale
        e = jnp.exp(s - jnp.max(s, axis=-1, keepdims=True))
        p = e / jnp.sum(e, axis=-1, keepdims=True)
        att_ref[:, sl] = jnp.dot(p.astype(BF16), v_ref[:, sl],
                                 preferred_element_type=F32).astype(att_ref.dtype)
    o_ref[...] = x1 + jnp.dot(att_ref[...], wo_ref[...], preferred_element_type=F32)


def mix_out_mem_attention(a0, a1, w_out, x, nw_mem, w_mq, mkv, w_mo, seq, n_mem):
    t, k0 = a0.shape
    k1 = a1.shape[1]
    d = x.shape[1]
    assert k0 == k1 and w_out.shape == (k0 + k1, d)
    tm = _pick(seq, RESIDENT_TM)
    per_batch = seq // tm
    scale = (d // MEM_HEADS) ** -0.5
    once = pl.Buffered(1)
    return pl.pallas_call(
        functools.partial(_mix_mem_kernel, scale=scale),
        grid=(t // tm,),
        in_specs=[
            pl.BlockSpec((tm, k0), lambda i: (i, 0)),
            pl.BlockSpec((tm, k1), lambda i: (i, 0)),
            pl.BlockSpec((k0, d), lambda i: (0, 0), pipeline_mode=once),
            pl.BlockSpec((k1, d), lambda i: (1, 0), pipeline_mode=once),
            pl.BlockSpec((tm, d), lambda i: (i, 0)),
            pl.BlockSpec((1, d), lambda i: (0, 0)),
            pl.BlockSpec((d, d), lambda i: (0, 0), pipeline_mode=once),
            pl.BlockSpec((n_mem, d), lambda i: (i // per_batch, 0), pipeline_mode=once),
            pl.BlockSpec((n_mem, d), lambda i: (i // per_batch, 1), pipeline_mode=once),
            pl.BlockSpec((d, d), lambda i: (0, 0), pipeline_mode=once),
        ],
        out_specs=pl.BlockSpec((tm, d), lambda i: (i, 0)),
        out_shape=jax.ShapeDtypeStruct((t, d), F32),
        scratch_shapes=[pltpu.VMEM((tm, d), BF16)],
        compiler_params=_params("parallel"),
        name="mix_out_mem_attention",
    )(a0, a1, w_out, w_out, x, nw_mem.reshape(1, d), w_mq, mkv, mkv, w_mo)


def _t5_bucket_np(n):
    max_exact = NUM_BUCKETS // 2
    nf = np.maximum(n, 1).astype(np.float32)
    large = max_exact + (np.log(nf / max_exact) / math.log(MAX_DISTANCE / max_exact)
                         * (NUM_BUCKETS - max_exact)).astype(np.int32)
    large = np.minimum(large, NUM_BUCKETS - 1)
    return np.where(n < max_exact, n, large).astype(np.int32)


def _tables_kernel(lamv_ref, lbraw_ref, lam_ref, lb_ref, *, lam_init, layer):
    v = lamv_ref[...]
    s1 = jnp.sum(v[0:1] * v[1:2], axis=-1, keepdims=True)
    s2 = jnp.sum(v[2:3] * v[3:4], axis=-1, keepdims=True)
    lam = jnp.exp(s1) - jnp.exp(s2) + lam_init
    lam_ref[...] = jnp.broadcast_to(lam, lam_ref.shape)
    raw = lbraw_ref[...]
    e = jnp.exp(raw - jnp.max(raw, axis=0, keepdims=True))
    sm = e / jnp.sum(e, axis=0, keepdims=True)
    lb_ref[...] = jnp.sum(sm[0:layer + 1], axis=0, keepdims=True)


def param_tables(lam_vecs, lb_raw, lam_init, layer):
    kw = lb_raw.shape[1]
    return pl.pallas_call(
        functools.partial(_tables_kernel, lam_init=lam_init, layer=layer),
        out_shape=(jax.ShapeDtypeStruct((1, LANES), F32), jax.ShapeDtypeStruct((1, kw), F32)),
        name="param_tables",
    )(lam_vecs, lb_raw)


def _bias_tiles_kernel(rb_ref, bk_ref, o_ref, *, tq, far_bucket):
    h = pl.program_id(0)
    bk = bk_ref[...]
    vec = jnp.full(bk.shape, NEG_BIG, F32)
    for n in range(NUM_BUCKETS):
        vec = jnp.where(bk == n, rb_ref[n, h], vec)
    for t in range(2):
        rows = jnp.broadcast_to(vec[t:t + 1], (tq, 2 * tq))
        o_ref[0, t] = pltpu.roll(rows, 0, 1, stride=1, stride_axis=0)[:, tq:]
    o_ref[0, FAR_TILE] = jnp.full((tq, tq), rb_ref[far_bucket, h], F32)


def bias_tiles(rel_bias, dist_buckets, tq, far_bucket):
    nb, h = rel_bias.shape
    return pl.pallas_call(
        functools.partial(_bias_tiles_kernel, tq=tq, far_bucket=far_bucket),
        grid=(h,),
        in_specs=[
            pl.BlockSpec(memory_space=pltpu.SMEM),
            pl.BlockSpec(dist_buckets.shape, lambda i: (0, 0)),
        ],
        out_specs=pl.BlockSpec((1, FAR_TILE + 1, tq, tq), lambda i: (i, 0, 0, 0)),
        out_shape=jax.ShapeDtypeStruct((h, FAR_TILE + 1, tq, tq), F32),
        compiler_params=_params("arbitrary"),
        name="bias_tiles",
    )(rel_bias, dist_buckets)


ONES_ROWS = 16
FAR_TILE = 2


def _diff_attn_kernel(q_ref, k_ref, v_ref, bias_ref, lam_ref, nw_ref, o_ref,
                      vt_ref, m_ref, acc_ref, st_ref, *, tq, out_scale):
    i = pl.program_id(2)
    dv = v_ref.shape[1]
    dh = dv // 2
    nkv = v_ref.shape[0] // tq

    @pl.when(i == 0)
    def _():
        for c in range(nkv):
            vt_ref[c, 0:dv, :] = v_ref[c * tq:(c + 1) * tq, :].astype(F32).T.astype(BF16)
            vt_ref[c, dv:, :] = jnp.ones((ONES_ROWS, tq), BF16)

    q = q_ref[...]
    lane = lax.broadcasted_iota(jnp.int32, (1, dv), 1)
    qm = (jnp.where(lane < dh, q, jnp.zeros_like(q)), jnp.where(lane >= dh, q, jnp.zeros_like(q)))

    m_ref[...] = jnp.full(m_ref.shape, NEG_BIG, F32)
    acc_ref[...] = jnp.zeros(acc_ref.shape, F32)

    qw = min(DA_Q_SUB, tq)
    chains = [(mp, slice(qs * qw, (qs + 1) * qw)) for mp in range(2) for qs in range(tq // qw)]

    def scores(n, j):
        mp, cols = chains[n]
        kb = k_ref[pl.ds(pl.multiple_of(j * tq, tq), tq), :]
        st_ref[n] = lax.dot_general(kb, qm[mp][cols, :], (((1,), (1,)), ((), ())),
                                    preferred_element_type=F32)

    def softmax_pv(n, vt, tile):
        mp, cols = chains[n]
        st = st_ref[n] + bias_ref[0, tile, :, cols]
        m_old = m_ref[mp, :, cols]
        m_new = jnp.maximum(m_old, jnp.max(st, axis=0, keepdims=True))
        p = jnp.exp(st - m_new).astype(BF16)
        alpha = jnp.exp(m_old - m_new)
        acc_ref[mp, :, cols] = (alpha * acc_ref[mp, :, cols]
                                + jnp.dot(vt, p, preferred_element_type=F32))
        m_ref[mp, :, cols] = m_new

    def step(j, carry):
        vt = vt_ref[j]
        tile = jnp.minimum(i - j, FAR_TILE)
        for n in range(len(chains)):
            if n + 1 < len(chains):
                scores(n + 1, j)
            else:
                scores(0, jnp.minimum(j + 1, i))
            softmax_pv(n, vt, tile)
        return carry

    scores(0, 0)
    lax.fori_loop(0, i + 1, step, 0)

    o0 = acc_ref[0, 0:dv, :] / acc_ref[0, dv:dv + 1, :]
    o1 = acc_ref[1, 0:dv, :] / acc_ref[1, dv:dv + 1, :]
    o = o0 - lam_ref[0:1, 0:1] * o1
    ms = jnp.mean(o * o, axis=0, keepdims=True)
    y = (o * lax.rsqrt(ms + EPS)) * (nw_ref[...] * out_scale)
    o_ref[...] = y.T.astype(o_ref.dtype)


def diff_attention(proj, bias_t, lam_row, nw, batch, seq, tq, out_scale, riders=(), layer=0):
    nq = seq // tq
    dv = LANES
    grid = (batch, DA_HEADS, nq)
    rider_in, rider_out, rider_shapes = _rider_specs(riders, layer, grid)
    kern = _with_cast_riders(functools.partial(_diff_attn_kernel, tq=tq, out_scale=out_scale),
                             n_in=6, n_out=1, n_riders=len(riders))
    return pl.pallas_call(
        kern,
        grid=grid,
        in_specs=[
            pl.BlockSpec((tq, dv), lambda b, h, i: (b * nq + i, h)),
            pl.BlockSpec((seq, dv), lambda b, h, i: (b, DA_HEADS + h)),
            pl.BlockSpec((seq, dv), lambda b, h, i: (b, 2 * DA_HEADS + h)),
            pl.BlockSpec((1, FAR_TILE + 1, tq, tq), lambda b, h, i: (h, 0, 0, 0)),
            pl.BlockSpec((1, LANES), lambda b, h, i: (0, 0)),
            pl.BlockSpec((dv, 1), lambda b, h, i: (0, 0)),
        ] + rider_in,
        out_specs=[pl.BlockSpec((tq, dv), lambda b, h, i: (b * nq + i, h))] + rider_out,
        out_shape=[jax.ShapeDtypeStruct((batch * seq, DA_HEADS * dv), BF16)] + rider_shapes,
        scratch_shapes=[
            pltpu.VMEM((nq, dv + ONES_ROWS, tq), BF16),
            pltpu.VMEM((2, 1, tq), F32),
            pltpu.VMEM((2, dv + ONES_ROWS, tq), F32),
            pltpu.VMEM((2 * tq // min(DA_Q_SUB, tq), tq, min(DA_Q_SUB, tq)), F32),
        ],
        compiler_params=_params("parallel", "parallel", "arbitrary"),
        name="diff_attention",
    )(proj, proj, proj, bias_t, lam_row, nw.reshape(dv, 1), *riders)


def _exact_bf16_dot(tri3, x):
    hi = x.astype(BF16)
    r1 = x - hi.astype(F32)
    mid = r1.astype(BF16)
    lo = (r1 - mid.astype(F32)).astype(BF16)
    return jnp.dot(tri3, jnp.concatenate([hi, mid, lo], axis=0), preferred_element_type=F32)


def _hgrn_tables(c):
    n_lv = int(math.log2(c))
    r = np.arange(c)[None, :]
    t = np.arange(c)[:, None]
    blocks_d, blocks_e = [], []
    for lv in range(2, HG_SMALL_LEVELS + 1):
        h = 1 << (lv - 1)
        upper = (t // h) % 2 == 1
        mid_t = (t // h) * h
        blocks_d.append(upper & (r >= mid_t) & (r <= t))
        mid_s = (t // h + 1) * h
        blocks_e.append((~upper) & (r > t) & (r < mid_s))
    stack = np.concatenate(blocks_d + blocks_e + [r <= t], axis=0).astype(np.float32)
    x = np.bitwise_xor(t, r)
    level = np.where(r == t, 0, np.floor(np.log2(np.maximum(x, 1))).astype(np.int32) + 1)
    level = np.where(r > t, -1, level).astype(np.int32)
    return stack, level, n_lv


def _hgrn_kernel(q_ref, f_ref, i_ref, g_ref, lb_ref, nw_ref, tst_ref, lvl_ref, o_ref, st_ref,
                 *, ts, c, n_lv):
    nt = (((1,), (1,)), ((), ()))
    d = LANES
    nh = q_ref.shape[1] // d
    nch = ts // c
    n_small = HG_SMALL_LEVELS

    @pl.when(pl.program_id(2) == 0)
    def _():
        st_ref[...] = jnp.zeros(st_ref.shape, F32)

    lb = lb_ref[...]
    f = lb + (1.0 - lb) * jax.nn.sigmoid(f_ref[...])
    logf = jnp.log(f)
    kk = 1.0 - f
    units = [(ch, hd) for ch in range(nch) for hd in range(nh)]
    lcat = jnp.concatenate([logf[ch * c:(ch + 1) * c, hd * d:(hd + 1) * d] for ch, hd in units], axis=1)
    seg = _exact_bf16_dot(tst_ref[...], lcat)
    lvl = lvl_ref[...]
    nw = nw_ref[...]

    st = [st_ref[hd] for hd in range(nh)]
    for u, (ch, hd) in enumerate(units):
        rows = slice(ch * c, (ch + 1) * c)
        cols = slice(hd * d, (hd + 1) * d)

        def blk(b, u=u):
            return seg[b * c:(b + 1) * c, u * d:(u + 1) * d]

        g_in = blk(2 * n_small - 2)
        g_last = g_in[c - 1:c]
        g_out = g_last - g_in
        q = q_ref[rows, cols]
        q32 = q.astype(F32)
        kc = kk[rows, cols]
        kb = kc.astype(BF16)
        vb = i_ref[rows, cols].astype(BF16)
        a = jnp.where(lvl == 0, lax.dot_general(q, kb, nt, preferred_element_type=F32), 0.0)
        qt = (q32 * f[rows, cols]).astype(BF16)
        a = jnp.where(lvl == 1, lax.dot_general(qt, kb, nt, preferred_element_type=F32), a)
        for lv in range(2, n_lv + 1):
            if lv <= n_small:
                d_up = blk(lv - 2)
                d_lo = blk(n_small + lv - 3)
            else:
                h = 1 << (lv - 1)
                ref = jnp.concatenate(
                    [jnp.broadcast_to(g_in[gi * 2 * h + h - 1:gi * 2 * h + h, :], (2 * h, d))
                     for gi in range(c // (2 * h))], axis=0)
                d_up = jnp.minimum(g_in - ref, 0.0)
                d_lo = jnp.minimum(ref - g_in, 0.0)
            qt = (q32 * jnp.exp(d_up)).astype(BF16)
            kt = (kc * jnp.exp(d_lo)).astype(BF16)
            a = jnp.where(lvl == lv, lax.dot_general(qt, kt, nt, preferred_element_type=F32), a)
        qg = (q32 * jnp.exp(g_in)).astype(BF16)
        o = (jnp.dot(a.astype(BF16), vb, preferred_element_type=F32)
             + lax.dot_general(qg, st[hd].astype(BF16), nt, preferred_element_type=F32))
        kd = (kc * jnp.exp(g_out)).astype(BF16)
        upd = lax.dot_general(vb, kd, (((0,), (0,)), ((), ())), preferred_element_type=F32)
        st[hd] = st[hd] * jnp.exp(g_last) + upd
        g = g_ref[rows, cols]
        o_ref[rows, cols] = (_rms_rows(o, nw) * (g * jax.nn.sigmoid(g))).astype(o_ref.dtype)
    for hd in range(nh):
        st_ref[hd] = st[hd]


def hgrn2(proj_a, proj_b, lb, nw, batch, seq, ts, q_col0, riders=(), layer=0):
    ns = seq // ts
    d = LANES
    nh = math.gcd(HG_HEADS_PER_STEP, HG_HEADS)
    assert q_col0 % nh == 0
    w = nh * d
    ng = HG_HEADS // nh
    c = min(HG_CHUNK, ts)
    stack, level, n_lv = _hgrn_tables(c)
    assert n_lv >= HG_SMALL_LEVELS
    grid = (batch, ng, ns)
    rider_in, rider_out, rider_shapes = _rider_specs(riders, layer, grid)
    kern = _with_cast_riders(functools.partial(_hgrn_kernel, ts=ts, c=c, n_lv=n_lv),
                             n_in=8, n_out=1, n_riders=len(riders))
    return pl.pallas_call(
        kern,
        grid=grid,
        in_specs=[
            pl.BlockSpec((ts, w), lambda b, h, i: (b * ns + i, q_col0 // nh + h)),
            pl.BlockSpec((ts, w), lambda b, h, i: (b * ns + i, h)),
            pl.BlockSpec((ts, w), lambda b, h, i: (b * ns + i, ng + h)),
            pl.BlockSpec((ts, w), lambda b, h, i: (b * ns + i, 2 * ng + h)),
            pl.BlockSpec((1, w), lambda b, h, i: (0, h)),
            pl.BlockSpec((1, d), lambda b, h, i: (0, 0)),
            pl.BlockSpec((stack.shape[0], 3 * stack.shape[1]), lambda b, h, i: (0, 0)),
            pl.BlockSpec(level.shape, lambda b, h, i: (0, 0)),
        ] + rider_in,
        out_specs=[pl.BlockSpec((ts, w), lambda b, h, i: (b * ns + i, h))] + rider_out,
        out_shape=[jax.ShapeDtypeStruct((batch * seq, HG_HEADS * d), BF16)] + rider_shapes,
        scratch_shapes=[pltpu.VMEM((nh, d, d), F32)],
        compiler_params=_params("parallel", "parallel", "arbitrary"),
        name="hgrn2",
    )(proj_a, proj_b, proj_b, proj_b, lb, nw.reshape(1, d),
      jnp.asarray(np.concatenate([stack] * 3, axis=1), BF16), jnp.asarray(level), *riders)


def _ffn_up_kernel(x_ref, halo_ref, nw_ref, wa_ref, wb_ref, cwa_ref, cwb_ref, cba_ref, cbb_ref,
                   o_ref, h_ref, u_ref, *, tm, per_batch):
    hl = SUBLANES

    @pl.when(pl.program_id(1) == 0)
    def _():
        nw = nw_ref[...]
        first = (pl.program_id(0) % per_batch) == 0
        halo = _rms_rows(halo_ref[...], nw)
        h_ref[0:hl, :] = jnp.where(first, 0.0, halo).astype(h_ref.dtype)
        h_ref[hl:, :] = _rms_rows(x_ref[...], nw).astype(h_ref.dtype)

    h = h_ref[...]
    tf = o_ref.shape[1]
    cw = min(FFN_COL_CHUNK, tf)

    def cols(cc):
        return slice(cc * cw, (cc + 1) * cw)

    def matmuls(cc):
        u_ref[cc, 0] = jnp.dot(h, wa_ref[:, cols(cc)], preferred_element_type=F32)
        u_ref[cc, 1] = jnp.dot(h, wb_ref[:, cols(cc)], preferred_element_type=F32)

    def conv(cc, half, cw_ref, cb_ref):
        acc = cb_ref[:, cols(cc)]
        for j in range(CONV_WIDTH):
            off = hl - (CONV_WIDTH - 1) + j
            acc = acc + cw_ref[j:j + 1, cols(cc)] * u_ref[cc, half, off:off + tm, :]
        return acc

    def gate(cc):
        a = conv(cc, 0, cwa_ref, cba_ref)
        b = conv(cc, 1, cwb_ref, cbb_ref)
        o_ref[:, cols(cc)] = ((a * jax.nn.sigmoid(a)) * b).astype(o_ref.dtype)

    for cc in range(tf // cw):
        matmuls(cc)
        gate(cc)


def ffn_up(x, nw, w_up, conv_w, conv_b, seq):
    t, d = x.shape
    f = w_up.shape[1] // 2
    tm, tf = _pick(seq, MM_TM), _pick(f, FFN_TF)
    nf = f // tf
    per_batch = seq // tm
    hl = SUBLANES
    rows_per_tile = tm // hl
    cw = min(FFN_COL_CHUNK, tf)
    kern = functools.partial(_ffn_up_kernel, tm=tm, per_batch=per_batch)
    return pl.pallas_call(
        kern,
        grid=(t // tm, nf),
        in_specs=[
            pl.BlockSpec((tm, d), lambda i, j: (i, 0)),
            pl.BlockSpec((hl, d), lambda i, j: (jnp.maximum(i * rows_per_tile - 1, 0), 0)),
            pl.BlockSpec((1, d), lambda i, j: (0, 0)),
            pl.BlockSpec((d, tf), lambda i, j: (0, j)),
            pl.BlockSpec((d, tf), lambda i, j: (0, nf + j)),
            pl.BlockSpec((CONV_WIDTH, tf), lambda i, j: (0, j)),
            pl.BlockSpec((CONV_WIDTH, tf), lambda i, j: (0, nf + j)),
            pl.BlockSpec((1, tf), lambda i, j: (0, j)),
            pl.BlockSpec((1, tf), lambda i, j: (0, nf + j)),
        ],
        out_specs=pl.BlockSpec((tm, tf), lambda i, j: (i, j)),
        out_shape=jax.ShapeDtypeStruct((t, f), BF16),
        scratch_shapes=[
            pltpu.VMEM((tm + hl, d), BF16),
            pltpu.VMEM((tf // cw, 2, tm + hl, cw), F32),
        ],
        compiler_params=_params("parallel", "arbitrary"),
        name="ffn_up",
    )(x, x, nw.reshape(1, d), w_up, w_up, conv_w, conv_w, conv_b.reshape(1, 2 * f), conv_b.reshape(1, 2 * f))


def _ffn_down_kernel(a_ref, w_ref, res_ref, nw_ref, o_ref):
    y = res_ref[...] + jnp.dot(a_ref[...], w_ref[...], preferred_element_type=F32)
    o_ref[...] = _rms_rows(y, nw_ref[...])


def ffn_down(act, w_down, res, nw):
    t, f = act.shape
    d = w_down.shape[1]
    tm = _pick(t, FFN_DOWN_TM)
    return pl.pallas_call(
        _ffn_down_kernel,
        grid=(t // tm,),
        in_specs=[
            pl.BlockSpec((tm, f), lambda i: (i, 0)),
            pl.BlockSpec((f, d), lambda i: (0, 0), pipeline_mode=pl.Buffered(1)),
            pl.BlockSpec((tm, d), lambda i: (i, 0)),
            pl.BlockSpec((1, d), lambda i: (0, 0)),
        ],
        out_specs=pl.BlockSpec((tm, d), lambda i: (i, 0)),
        out_shape=jax.ShapeDtypeStruct((t, d), F32),
        compiler_params=_params("parallel"),
        name="ffn_down",
    )(act, w_down, res, nw.reshape(1, d))


def kernel(x, mem, w_in, w_out, norm_mix_w, lam_q1, lam_k1, lam_q2, lam_k2, da_subln_w,
           hg_lb_raw, hg_norm_w, rel_bias, norm_mem_w, mem_kv_norm_w, w_mq, w_mkv, w_mo,
           norm_ffn_w, w_up, conv_w, conv_b, w_down, final_norm_w):
    batch, seq, d_model = x.shape
    n_mem = mem.shape[1]
    depth = w_in.shape[0]
    assert depth == 1, "single-layer block"
    layer = 0
    t = batch * seq

    da_width = DA_HEADS * LANES
    hg_width = HG_HEADS * LANES
    assert w_in.shape[2] == 3 * da_width + 4 * hg_width
    assert da_subln_w.shape[1] == LANES and hg_norm_w.shape[1] == LANES
    assert lam_q1.shape[1] * 2 == LANES
    n_bf = 3 * da_width + hg_width

    tq = _pick(seq, DA_BLOCK)
    ts = _pick(seq, HG_TILE)

    lam_init = 0.8 - 0.6 * math.exp(-0.3 * layer)
    lam_vecs = jnp.stack([lam_q1[layer], lam_k1[layer], lam_q2[layer], lam_k2[layer]]).astype(F32)
    lam_row, lb = param_tables(lam_vecs, hg_lb_raw.astype(F32), lam_init, layer)

    far = _t5_bucket_np(np.arange(tq + 1, max(seq, tq + 2)))
    assert (far == far[0]).all(), "bias must be constant beyond the sub-diagonal block"
    c = np.arange(2 * tq)
    dist_buckets = np.stack([np.where(c >= tq, _t5_bucket_np(np.maximum(c - tq, 0)), -1),
                             _t5_bucket_np(c)]).astype(np.int32)
    bias_t = bias_tiles(rel_bias.astype(F32), jnp.asarray(dist_buckets), tq, int(far[0]))

    x2d = x.reshape(t, d_model)
    q_scale = (LANES // 2) ** -0.5
    col_scale = np.ones((1, n_bf), np.float32)
    col_scale[:, :da_width] = q_scale
    proj_a, proj_b = in_proj(x2d, norm_mix_w[layer], w_in[layer].astype(BF16), n_bf, jnp.asarray(col_scale))

    da_o, w_up_bf, w_mkv_bf = diff_attention(
        proj_a, bias_t, lam_row, da_subln_w[layer], batch, seq, tq, 1.0 - lam_init,
        riders=(w_up, w_mkv), layer=layer)
    hg_o, w_down_bf, w_out_bf, w_mq_bf, w_mo_bf = hgrn2(
        proj_a, proj_b, lb, hg_norm_w[layer], batch, seq, ts, 3 * DA_HEADS,
        riders=(w_down, w_out, w_mq, w_mo), layer=layer)

    mkv = norm_matmul(mem.reshape(batch * n_mem, d_model), mem_kv_norm_w[layer], w_mkv_bf, BF16,
                      name="mem_kv_proj")
    x2 = mix_out_mem_attention(da_o, hg_o, w_out_bf, x2d, norm_mem_w[layer], w_mq_bf, mkv, w_mo_bf,
                               seq, n_mem)

    act = ffn_up(x2, norm_ffn_w[layer], w_up_bf, conv_w[layer], conv_b[layer], seq)
    out = ffn_down(act, w_down_bf, x2, final_norm_w)
    return out.reshape(batch, seq, d_model)
```

```python
import functools
import math

import numpy as np
import jax
import jax.numpy as jnp
from jax import lax
from jax.experimental import pallas as pl
from jax.experimental.pallas import tpu as pltpu

DA_HEADS = 8
HG_HEADS = 8
MEM_HEADS = 4
NUM_BUCKETS = 32
MAX_DISTANCE = 128
CONV_WIDTH = 3
EPS = 1e-6

LANES = 128
SUBLANES = 8
VMEM_LIMIT_BYTES = 56 * 1024 * 1024

HG_CHUNK = 128
HG_TILE = 512
HG_HEADS_PER_STEP = 4
HG_SMALL_LEVELS = 3
DA_BLOCK = 512
DA_Q_SUB = 512
MM_TM = 1024
MM_TN = 1024
FFN_TF = 512
FFN_COL_CHUNK = 256
RESIDENT_TM = 512
FFN_DOWN_TM = RESIDENT_TM
NEG_BIG = -1e30

F32 = jnp.float32
BF16 = jnp.bfloat16


def _params(*sem, flags=None):
    return pltpu.CompilerParams(dimension_semantics=sem, vmem_limit_bytes=VMEM_LIMIT_BYTES, flags=flags)


def _rms_rows(x, w):
    ms = jnp.mean(x * x, axis=-1, keepdims=True)
    return (x * lax.rsqrt(ms + EPS)) * w


def _pick(n, pref):
    t = min(pref, n)
    while n % t:
        t //= 2
    return t


def _norm_mm_kernel(x_ref, nw_ref, w_ref, o_ref, h_ref):
    @pl.when(pl.program_id(1) == 0)
    def _():
        h_ref[...] = _rms_rows(x_ref[...], nw_ref[...]).astype(h_ref.dtype)

    o_ref[...] = jnp.dot(h_ref[...], w_ref[...], preferred_element_type=F32).astype(o_ref.dtype)


def norm_matmul(x, nw, w, out_dtype, name="norm_matmul"):
    t, k = x.shape
    n = w.shape[1]
    tm, tn = _pick(t, MM_TM), _pick(n, MM_TN)
    return pl.pallas_call(
        _norm_mm_kernel,
        grid=(t // tm, n // tn),
        in_specs=[
            pl.BlockSpec((tm, k), lambda i, j: (i, 0)),
            pl.BlockSpec((1, k), lambda i, j: (0, 0)),
            pl.BlockSpec((k, tn), lambda i, j: (0, j)),
        ],
        out_specs=pl.BlockSpec((tm, tn), lambda i, j: (i, j)),
        out_shape=jax.ShapeDtypeStruct((t, n), out_dtype),
        scratch_shapes=[pltpu.VMEM((tm, k), BF16)],
        compiler_params=_params("parallel", "arbitrary"),
        name=name,
    )(x, nw.reshape(1, k), w)


BF16_ROWS = 16


def _with_cast_riders(body, n_in, n_out, n_riders):
    def kern(*refs):
        ins, rest = refs[:n_in], refs[n_in:]
        rid_in, rest = rest[:n_riders], rest[n_riders:]
        outs, rest = rest[:n_out], rest[n_out:]
        rid_out, scratch = rest[:n_riders], rest[n_riders:]
        for src, dst in zip(rid_in, rid_out):
            dst[...] = src[...].astype(dst.dtype)
        body(*ins, *outs, *scratch)
    return kern


def _rider_specs(riders, layer, grid):
    n_steps = math.prod(grid)

    def step(*g):
        idx = 0
        for gi, n in zip(g, grid):
            idx = idx * n + gi
        return idx

    in_specs, out_specs, shapes = [], [], []
    for a in riders:
        _, rows, cols = a.shape
        n_slabs = math.gcd(n_steps, rows // BF16_ROWS)
        assert rows % BF16_ROWS == 0 and n_steps % n_slabs == 0, (a.shape, grid)
        hold = n_steps // n_slabs
        in_specs.append(pl.BlockSpec((None, rows // n_slabs, cols),
                                     lambda *g, hold=hold: (layer, step(*g) // hold, 0)))
        out_specs.append(pl.BlockSpec((rows // n_slabs, cols),
                                      lambda *g, hold=hold: (step(*g) // hold, 0)))
        shapes.append(jax.ShapeDtypeStruct((rows, cols), BF16))
    return in_specs, out_specs, shapes


def _in_proj_kernel(x_ref, nw_ref, w_ref, cs_ref, oa_ref, ob_ref, h_ref, *, na_tiles):
    j = pl.program_id(1)

    @pl.when(j == 0)
    def _():
        h_ref[...] = _rms_rows(x_ref[...], nw_ref[...]).astype(h_ref.dtype)

    y = jnp.dot(h_ref[...], w_ref[...], preferred_element_type=F32)

    heads = oa_ref.shape[0]

    @pl.when(j < na_tiles)
    def _():
        ys = y * cs_ref[...]
        for hh in range(heads):
            oa_ref[hh] = ys[:, hh * LANES:(hh + 1) * LANES].astype(oa_ref.dtype)

    @pl.when(j >= na_tiles)
    def _():
        for hh in range(heads):
            ob_ref[hh] = y[:, hh * LANES:(hh + 1) * LANES]


def in_proj(x, nw, w, n_a, col_scale):
    t, k = x.shape
    n = w.shape[1]
    tm, tn = _pick(t, MM_TM), math.gcd(_pick(n_a, MM_TN), _pick(n - n_a, MM_TN))
    na_tiles = n_a // tn
    hpt = tn // LANES
    return pl.pallas_call(
        functools.partial(_in_proj_kernel, na_tiles=na_tiles),
        grid=(t // tm, n // tn),
        in_specs=[
            pl.BlockSpec((tm, k), lambda i, j: (i, 0)),
            pl.BlockSpec((1, k), lambda i, j: (0, 0)),
            pl.BlockSpec((k, tn), lambda i, j: (0, j)),
            pl.BlockSpec((1, tn), lambda i, j: (0, jnp.minimum(j, na_tiles - 1))),
        ],
        out_specs=[
            pl.BlockSpec((hpt, tm, LANES), lambda i, j: (jnp.minimum(j, na_tiles - 1), i, 0)),
            pl.BlockSpec((hpt, tm, LANES), lambda i, j: (jnp.maximum(j - na_tiles, 0), i, 0)),
        ],
        out_shape=(jax.ShapeDtypeStruct((n_a // LANES, t, LANES), BF16),
                   jax.ShapeDtypeStruct(((n - n_a) // LANES, t, LANES), F32)),
        scratch_shapes=[pltpu.VMEM((tm, k), BF16)],
        compiler_params=_params("parallel", "arbitrary"),
        name="in_proj",
    )(x, nw.reshape(1, k), w, col_scale)


def _mix_mem_kernel(a0_ref, a1_ref, w0_ref, w1_ref, x_ref, nw_ref, wq_ref, k_ref, v_ref, wo_ref,
                    o_ref, att_ref, *, scale):
    def heads_on_lanes(a_ref):
        return jnp.concatenate([a_ref[hh] for hh in range(a_ref.shape[0])], axis=1)

    x1 = x_ref[...]
    x1 = x1 + jnp.dot(heads_on_lanes(a0_ref), w0_ref[...], preferred_element_type=F32)
    x1 = x1 + jnp.dot(heads_on_lanes(a1_ref), w1_ref[...], preferred_element_type=F32)
    hq = _rms_rows(x1, nw_ref[...]).astype(BF16)
    mq = jnp.dot(hq, wq_ref[...], preferred_element_type=F32).astype(BF16)
    dh = mq.shape[1] // MEM_HEADS
    for hd in range(MEM_HEADS):
        sl = slice(hd * dh, (hd + 1) * dh)
        s = lax.dot_general(mq[:, sl], k_ref[:, sl], (((1,), (1,)), ((), ())),
                            preferred_element_type=F32) * scale
        e = jnp.exp(s - jnp.max(s, axis=-1, keepdims=True))
        p = e / jnp.sum(e, axis=-1, keepdims=True)
        att_ref[:, sl] = jnp.dot(p.astype(BF16), v_ref[:, sl],
                                 preferred_element_type=F32).astype(att_ref.dtype)
    o_ref[...] = x1 + jnp.dot(att_ref[...], wo_ref[...], preferred_element_type=F32)


def mix_out_mem_attention(a0, a1, w_out, x, nw_mem, w_mq, mkv, w_mo, seq, n_mem):
    h0, t, _ = a0.shape
    h1 = a1.shape[0]
    k0, k1 = h0 * LANES, h1 * LANES
    d = x.shape[1]
    assert k0 == k1 and w_out.shape == (k0 + k1, d)
    tm = _pick(seq, RESIDENT_TM)
    per_batch = seq // tm
    scale = (d // MEM_HEADS) ** -0.5
    once = pl.Buffered(1)
    return pl.pallas_call(
        functools.partial(_mix_mem_kernel, scale=scale),
        grid=(t // tm,),
        in_specs=[
            pl.BlockSpec((h0, tm, LANES), lambda i: (0, i, 0)),
            pl.BlockSpec((h1, tm, LANES), lambda i: (0, i, 0)),
            pl.BlockSpec((k0, d), lambda i: (0, 0), pipeline_mode=once),
            pl.BlockSpec((k1, d), lambda i: (1, 0), pipeline_mode=once),
            pl.BlockSpec((tm, d), lambda i: (i, 0)),
            pl.BlockSpec((1, d), lambda i: (0, 0)),
            pl.BlockSpec((d, d), lambda i: (0, 0), pipeline_mode=once),
            pl.BlockSpec((n_mem, d), lambda i: (i // per_batch, 0), pipeline_mode=once),
            pl.BlockSpec((n_mem, d), lambda i: (i // per_batch, 1), pipeline_mode=once),
            pl.BlockSpec((d, d), lambda i: (0, 0), pipeline_mode=once),
        ],
        out_specs=pl.BlockSpec((tm, d), lambda i: (i, 0)),
        out_shape=jax.ShapeDtypeStruct((t, d), F32),
        scratch_shapes=[pltpu.VMEM((tm, d), BF16)],
        compiler_params=_params("parallel"),
        name="mix_out_mem_attention",
    )(a0, a1, w_out, w_out, x, nw_mem.reshape(1, d), w_mq, mkv, mkv, w_mo)


def _t5_bucket_np(n):
    max_exact = NUM_BUCKETS // 2
    nf = np.maximum(n, 1).astype(np.float32)
    large = max_exact + (np.log(nf / max_exact) / math.log(MAX_DISTANCE / max_exact)
                         * (NUM_BUCKETS - max_exact)).astype(np.int32)
    large = np.minimum(large, NUM_BUCKETS - 1)
    return np.where(n < max_exact, n, large).astype(np.int32)


def _tables_kernel(lamv_ref, lbraw_ref, lam_ref, lb_ref, *, lam_init, layer):
    v = lamv_ref[...]
    s1 = jnp.sum(v[0:1] * v[1:2], axis=-1, keepdims=True)
    s2 = jnp.sum(v[2:3] * v[3:4], axis=-1, keepdims=True)
    lam = jnp.exp(s1) - jnp.exp(s2) + lam_init
    lam_ref[...] = jnp.broadcast_to(lam, lam_ref.shape)
    raw = lbraw_ref[...]
    e = jnp.exp(raw - jnp.max(raw, axis=0, keepdims=True))
    sm = e / jnp.sum(e, axis=0, keepdims=True)
    lb_ref[...] = jnp.sum(sm[0:layer + 1], axis=0, keepdims=True)


def param_tables(lam_vecs, lb_raw, lam_init, layer):
    kw = lb_raw.shape[1]
    return pl.pallas_call(
        functools.partial(_tables_kernel, lam_init=lam_init, layer=layer),
        out_shape=(jax.ShapeDtypeStruct((1, LANES), F32), jax.ShapeDtypeStruct((1, kw), F32)),
        name="param_tables",
    )(lam_vecs, lb_raw)


def _bias_tiles_kernel(rb_ref, bk_ref, o_ref, *, tq, far_bucket):
    h = pl.program_id(0)
    bk = bk_ref[...]
    vec = jnp.full(bk.shape, NEG_BIG, F32)
    for n in range(NUM_BUCKETS):
        vec = jnp.where(bk == n, rb_ref[n, h], vec)
    for t in range(2):
        rows = jnp.broadcast_to(vec[t:t + 1], (tq, 2 * tq))
        o_ref[0, t] = pltpu.roll(rows, 0, 1, stride=1, stride_axis=0)[:, tq:]
    o_ref[0, FAR_TILE] = jnp.full((tq, tq), rb_ref[far_bucket, h], F32)


def bias_tiles(rel_bias, dist_buckets, tq, far_bucket):
    nb, h = rel_bias.shape
    return pl.pallas_call(
        functools.partial(_bias_tiles_kernel, tq=tq, far_bucket=far_bucket),
        grid=(h,),
        in_specs=[
            pl.BlockSpec(memory_space=pltpu.SMEM),
            pl.BlockSpec(dist_buckets.shape, lambda i: (0, 0)),
        ],
        out_specs=pl.BlockSpec((1, FAR_TILE + 1, tq, tq), lambda i: (i, 0, 0, 0)),
        out_shape=jax.ShapeDtypeStruct((h, FAR_TILE + 1, tq, tq), F32),
        compiler_params=_params("arbitrary"),
        name="bias_tiles",
    )(rel_bias, dist_buckets)


ONES_ROWS = 16
FAR_TILE = 2


def _diff_attn_kernel(q_ref, k_ref, v_ref, bias_ref, lam_ref, nw_ref, o_ref,
                      vt_ref, m_ref, acc_ref, st_ref, *, tq, out_scale):
    i = pl.program_id(2)
    dv = v_ref.shape[1]
    dh = dv // 2
    nkv = v_ref.shape[0] // tq

    @pl.when(i == 0)
    def _():
        for c in range(nkv):
            vt_ref[c, 0:dv, :] = v_ref[c * tq:(c + 1) * tq, :].astype(F32).T.astype(BF16)
            vt_ref[c, dv:, :] = jnp.ones((ONES_ROWS, tq), BF16)

    q = q_ref[...]
    lane = lax.broadcasted_iota(jnp.int32, (1, dv), 1)
    qm = (jnp.where(lane < dh, q, jnp.zeros_like(q)), jnp.where(lane >= dh, q, jnp.zeros_like(q)))

    m_ref[...] = jnp.full(m_ref.shape, NEG_BIG, F32)
    acc_ref[...] = jnp.zeros(acc_ref.shape, F32)

    qw = min(DA_Q_SUB, tq)
    chains = [(mp, slice(qs * qw, (qs + 1) * qw)) for mp in range(2) for qs in range(tq // qw)]

    def scores(n, j):
        mp, cols = chains[n]
        kb = k_ref[pl.ds(pl.multiple_of(j * tq, tq), tq), :]
        st_ref[n] = lax.dot_general(kb, qm[mp][cols, :], (((1,), (1,)), ((), ())),
                                    preferred_element_type=F32)

    def softmax_pv(n, vt, tile):
        mp, cols = chains[n]
        st = st_ref[n] + bias_ref[0, tile, :, cols]
        m_old = m_ref[mp, :, cols]
        m_new = jnp.maximum(m_old, jnp.max(st, axis=0, keepdims=True))
        p = jnp.exp(st - m_new).astype(BF16)
        alpha = jnp.exp(m_old - m_new)
        acc_ref[mp, :, cols] = (alpha * acc_ref[mp, :, cols]
                                + jnp.dot(vt, p, preferred_element_type=F32))
        m_ref[mp, :, cols] = m_new

    def step(j, carry):
        vt = vt_ref[j]
        tile = jnp.minimum(i - j, FAR_TILE)
        for n in range(len(chains)):
            if n + 1 < len(chains):
                scores(n + 1, j)
            else:
                scores(0, jnp.minimum(j + 1, i))
            softmax_pv(n, vt, tile)
        return carry

    scores(0, 0)
    lax.fori_loop(0, i + 1, step, 0)

    o0 = acc_ref[0, 0:dv, :] / acc_ref[0, dv:dv + 1, :]
    o1 = acc_ref[1, 0:dv, :] / acc_ref[1, dv:dv + 1, :]
    o = o0 - lam_ref[0:1, 0:1] * o1
    ms = jnp.mean(o * o, axis=0, keepdims=True)
    y = (o * lax.rsqrt(ms + EPS)) * (nw_ref[...] * out_scale)
    o_ref[...] = y.T.astype(o_ref.dtype)


def diff_attention(proj, bias_t, lam_row, nw, batch, seq, tq, out_scale, riders=(), layer=0):
    nq = seq // tq
    dv = LANES
    grid = (batch, DA_HEADS, nq)
    rider_in, rider_out, rider_shapes = _rider_specs(riders, layer, grid)
    kern = _with_cast_riders(functools.partial(_diff_attn_kernel, tq=tq, out_scale=out_scale),
                             n_in=6, n_out=1, n_riders=len(riders))
    return pl.pallas_call(
        kern,
        grid=grid,
        in_specs=[
            pl.BlockSpec((None, tq, dv), lambda b, h, i: (h, b * nq + i, 0)),
            pl.BlockSpec((None, seq, dv), lambda b, h, i: (DA_HEADS + h, b, 0)),
            pl.BlockSpec((None, seq, dv), lambda b, h, i: (2 * DA_HEADS + h, b, 0)),
            pl.BlockSpec((1, FAR_TILE + 1, tq, tq), lambda b, h, i: (h, 0, 0, 0)),
            pl.BlockSpec((1, LANES), lambda b, h, i: (0, 0)),
            pl.BlockSpec((dv, 1), lambda b, h, i: (0, 0)),
        ] + rider_in,
        out_specs=[pl.BlockSpec((None, tq, dv), lambda b, h, i: (h, b * nq + i, 0))] + rider_out,
        out_shape=[jax.ShapeDtypeStruct((DA_HEADS, batch * seq, dv), BF16)] + rider_shapes,
        scratch_shapes=[
            pltpu.VMEM((nq, dv + ONES_ROWS, tq), BF16),
            pltpu.VMEM((2, 1, tq), F32),
            pltpu.VMEM((2, dv + ONES_ROWS, tq), F32),
            pltpu.VMEM((2 * tq // min(DA_Q_SUB, tq), tq, min(DA_Q_SUB, tq)), F32),
        ],
        compiler_params=_params("parallel", "parallel", "arbitrary"),
        name="diff_attention",
    )(proj, proj, proj, bias_t, lam_row, nw.reshape(dv, 1), *riders)


def _exact_bf16_dot(tri3, x):
    hi = x.astype(BF16)
    r1 = x - hi.astype(F32)
    mid = r1.astype(BF16)
    lo = (r1 - mid.astype(F32)).astype(BF16)
    return jnp.dot(tri3, jnp.concatenate([hi, mid, lo], axis=0), preferred_element_type=F32)


def _hgrn_tables(c):
    n_lv = int(math.log2(c))
    r = np.arange(c)[None, :]
    t = np.arange(c)[:, None]
    blocks_d, blocks_e = [], []
    for lv in range(2, HG_SMALL_LEVELS + 1):
        h = 1 << (lv - 1)
        upper = (t // h) % 2 == 1
        mid_t = (t // h) * h
        blocks_d.append(upper & (r >= mid_t) & (r <= t))
        mid_s = (t // h + 1) * h
        blocks_e.append((~upper) & (r > t) & (r < mid_s))
    stack = np.concatenate(blocks_d + blocks_e + [r <= t], axis=0).astype(np.float32)
    x = np.bitwise_xor(t, r)
    level = np.where(r == t, 0, np.floor(np.log2(np.maximum(x, 1))).astype(np.int32) + 1)
    level = np.where(r > t, -1, level).astype(np.int32)
    return stack, level, n_lv


def _hgrn_kernel(q_ref, f_ref, i_ref, g_ref, lb_ref, nw_ref, tst_ref, lvl_ref, o_ref, st_ref,
                 *, ts, c, n_lv):
    nt = (((1,), (1,)), ((), ()))
    d = LANES
    nh = q_ref.shape[0]
    nch = ts // c
    n_small = HG_SMALL_LEVELS

    @pl.when(pl.program_id(2) == 0)
    def _():
        st_ref[...] = jnp.zeros(st_ref.shape, F32)

    f, logf, kk = [], [], []
    for hd in range(nh):
        lb = lb_ref[:, hd * d:(hd + 1) * d]
        f.append(lb + (1.0 - lb) * jax.nn.sigmoid(f_ref[hd]))
        logf.append(jnp.log(f[hd]))
        kk.append(1.0 - f[hd])
    units = [(ch, hd) for ch in range(nch) for hd in range(nh)]
    lcat = jnp.concatenate([logf[hd][ch * c:(ch + 1) * c] for ch, hd in units], axis=1)
    seg = _exact_bf16_dot(tst_ref[...], lcat)
    lvl = lvl_ref[...]
    nw = nw_ref[...]

    st = [st_ref[hd] for hd in range(nh)]
    for u, (ch, hd) in enumerate(units):
        rows = slice(ch * c, (ch + 1) * c)

        def blk(b, u=u):
            return seg[b * c:(b + 1) * c, u * d:(u + 1) * d]

        g_in = blk(2 * n_small - 2)
        g_last = g_in[c - 1:c]
        g_out = g_last - g_in
        q = q_ref[hd, rows, :]
        q32 = q.astype(F32)
        kc = kk[hd][rows]
        kb = kc.astype(BF16)
        vb = i_ref[hd, rows, :].astype(BF16)
        a = jnp.where(lvl == 0, lax.dot_general(q, kb, nt, preferred_element_type=F32), 0.0)
        qt = (q32 * f[hd][rows]).astype(BF16)
        a = jnp.where(lvl == 1, lax.dot_general(qt, kb, nt, preferred_element_type=F32), a)
        for lv in range(2, n_lv + 1):
            if lv <= n_small:
                d_up = blk(lv - 2)
                d_lo = blk(n_small + lv - 3)
            else:
                h = 1 << (lv - 1)
                ref = jnp.concatenate(
                    [jnp.broadcast_to(g_in[gi * 2 * h + h - 1:gi * 2 * h + h, :], (2 * h, d))
                     for gi in range(c // (2 * h))], axis=0)
                d_up = jnp.minimum(g_in - ref, 0.0)
                d_lo = jnp.minimum(ref - g_in, 0.0)
            qt = (q32 * jnp.exp(d_up)).astype(BF16)
            kt = (kc * jnp.exp(d_lo)).astype(BF16)
            a = jnp.where(lvl == lv, lax.dot_general(qt, kt, nt, preferred_element_type=F32), a)
        qg = (q32 * jnp.exp(g_in)).astype(BF16)
        o = (jnp.dot(a.astype(BF16), vb, preferred_element_type=F32)
             + lax.dot_general(qg, st[hd].astype(BF16), nt, preferred_element_type=F32))
        kd = (kc * jnp.exp(g_out)).astype(BF16)
        upd = lax.dot_general(vb, kd, (((0,), (0,)), ((), ())), preferred_element_type=F32)
        st[hd] = st[hd] * jnp.exp(g_last) + upd
        g = g_ref[hd, rows, :]
        o_ref[hd, rows, :] = (_rms_rows(o, nw) * (g * jax.nn.sigmoid(g))).astype(o_ref.dtype)
    for hd in range(nh):
        st_ref[hd] = st[hd]


def hgrn2(proj_a, proj_b, lb, nw, batch, seq, ts, q_col0, riders=(), layer=0):
    ns = seq // ts
    d = LANES
    nh = math.gcd(HG_HEADS_PER_STEP, HG_HEADS)
    assert q_col0 % nh == 0
    w = nh * d
    ng = HG_HEADS // nh
    c = min(HG_CHUNK, ts)
    stack, level, n_lv = _hgrn_tables(c)
    assert n_lv >= HG_SMALL_LEVELS
    grid = (batch, ng, ns)
    rider_in, rider_out, rider_shapes = _rider_specs(riders, layer, grid)
    kern = _with_cast_riders(functools.partial(_hgrn_kernel, ts=ts, c=c, n_lv=n_lv),
                             n_in=8, n_out=1, n_riders=len(riders))
    return pl.pallas_call(
        kern,
        grid=grid,
        in_specs=[
            pl.BlockSpec((nh, ts, d), lambda b, h, i: (q_col0 // nh + h, b * ns + i, 0)),
            pl.BlockSpec((nh, ts, d), lambda b, h, i: (h, b * ns + i, 0)),
            pl.BlockSpec((nh, ts, d), lambda b, h, i: (ng + h, b * ns + i, 0)),
            pl.BlockSpec((nh, ts, d), lambda b, h, i: (2 * ng + h, b * ns + i, 0)),
            pl.BlockSpec((1, w), lambda b, h, i: (0, h)),
            pl.BlockSpec((1, d), lambda b, h, i: (0, 0)),
            pl.BlockSpec((stack.shape[0], 3 * stack.shape[1]), lambda b, h, i: (0, 0)),
            pl.BlockSpec(level.shape, lambda b, h, i: (0, 0)),
        ] + rider_in,
        out_specs=[pl.BlockSpec((nh, ts, d), lambda b, h, i: (h, b * ns + i, 0))] + rider_out,
        out_shape=[jax.ShapeDtypeStruct((HG_HEADS, batch * seq, d), BF16)] + rider_shapes,
        scratch_shapes=[pltpu.VMEM((nh, d, d), F32)],
        compiler_params=_params("parallel", "parallel", "arbitrary"),
        name="hgrn2",
    )(proj_a, proj_b, proj_b, proj_b, lb, nw.reshape(1, d),
      jnp.asarray(np.concatenate([stack] * 3, axis=1), BF16), jnp.asarray(level), *riders)


def _ffn_up_kernel(x_ref, halo_ref, nw_ref, wa_ref, wb_ref, cwa_ref, cwb_ref, cba_ref, cbb_ref,
                   o_ref, h_ref, u_ref, *, tm, per_batch):
    hl = SUBLANES

    @pl.when(pl.program_id(1) == 0)
    def _():
        nw = nw_ref[...]
        first = (pl.program_id(0) % per_batch) == 0
        halo = _rms_rows(halo_ref[...], nw)
        h_ref[0:hl, :] = jnp.where(first, 0.0, halo).astype(h_ref.dtype)
        h_ref[hl:, :] = _rms_rows(x_ref[...], nw).astype(h_ref.dtype)

    h = h_ref[...]
    tf = o_ref.shape[1]
    cw = min(FFN_COL_CHUNK, tf)

    def cols(cc):
        return slice(cc * cw, (cc + 1) * cw)

    def matmuls(cc):
        u_ref[cc, 0] = jnp.dot(h, wa_ref[:, cols(cc)], preferred_element_type=F32)
        u_ref[cc, 1] = jnp.dot(h, wb_ref[:, cols(cc)], preferred_element_type=F32)

    def conv(cc, half, cw_ref, cb_ref):
        acc = cb_ref[:, cols(cc)]
        for j in range(CONV_WIDTH):
            off = hl - (CONV_WIDTH - 1) + j
            acc = acc + cw_ref[j:j + 1, cols(cc)] * u_ref[cc, half, off:off + tm, :]
        return acc

    def gate(cc):
        a = conv(cc, 0, cwa_ref, cba_ref)
        b = conv(cc, 1, cwb_ref, cbb_ref)
        o_ref[:, cols(cc)] = ((a * jax.nn.sigmoid(a)) * b).astype(o_ref.dtype)

    for cc in range(tf // cw):
        matmuls(cc)
        gate(cc)


def ffn_up(x, nw, w_up, conv_w, conv_b, seq):
    t, d = x.shape
    f = w_up.shape[1] // 2
    tm, tf = _pick(seq, MM_TM), _pick(f, FFN_TF)
    nf = f // tf
    per_batch = seq // tm
    hl = SUBLANES
    rows_per_tile = tm // hl
    cw = min(FFN_COL_CHUNK, tf)
    kern = functools.partial(_ffn_up_kernel, tm=tm, per_batch=per_batch)
    return pl.pallas_call(
        kern,
        grid=(t // tm, nf),
        in_specs=[
            pl.BlockSpec((tm, d), lambda i, j: (i, 0)),
            pl.BlockSpec((hl, d), lambda i, j: (jnp.maximum(i * rows_per_tile - 1, 0), 0)),
            pl.BlockSpec((1, d), lambda i, j: (0, 0)),
            pl.BlockSpec((d, tf), lambda i, j: (0, j)),
            pl.BlockSpec((d, tf), lambda i, j: (0, nf + j)),
            pl.BlockSpec((CONV_WIDTH, tf), lambda i, j: (0, j)),
            pl.BlockSpec((CONV_WIDTH, tf), lambda i, j: (0, nf + j)),
            pl.BlockSpec((1, tf), lambda i, j: (0, j)),
            pl.BlockSpec((1, tf), lambda i, j: (0, nf + j)),
        ],
        out_specs=pl.BlockSpec((tm, tf), lambda i, j: (i, j)),
        out_shape=jax.ShapeDtypeStruct((t, f), BF16),
        scratch_shapes=[
            pltpu.VMEM((tm + hl, d), BF16),
            pltpu.VMEM((tf // cw, 2, tm + hl, cw), F32),
        ],
        compiler_params=_params("parallel", "arbitrary"),
        name="ffn_up",
    )(x, x, nw.reshape(1, d), w_up, w_up, conv_w, conv_w, conv_b.reshape(1, 2 * f), conv_b.reshape(1, 2 * f))


def _ffn_down_kernel(a_ref, w_ref, res_ref, nw_ref, o_ref):
    y = res_ref[...] + jnp.dot(a_ref[...], w_ref[...], preferred_element_type=F32)
    o_ref[...] = _rms_rows(y, nw_ref[...])


def ffn_down(act, w_down, res, nw):
    t, f = act.shape
    d = w_down.shape[1]
    tm = _pick(t, FFN_DOWN_TM)
    return pl.pallas_call(
        _ffn_down_kernel,
        grid=(t // tm,),
        in_specs=[
            pl.BlockSpec((tm, f), lambda i: (i, 0)),
            pl.BlockSpec((f, d), lambda i: (0, 0), pipeline_mode=pl.Buffered(1)),
            pl.BlockSpec((tm, d), lambda i: (i, 0)),
            pl.BlockSpec((1, d), lambda i: (0, 0)),
        ],
        out_specs=pl.BlockSpec((tm, d), lambda i: (i, 0)),
        out_shape=jax.ShapeDtypeStruct((t, d), F32),
        compiler_params=_params("parallel"),
        name="ffn_down",
    )(act, w_down, res, nw.reshape(1, d))


def kernel(x, mem, w_in, w_out, norm_mix_w, lam_q1, lam_k1, lam_q2, lam_k2, da_subln_w,
           hg_lb_raw, hg_norm_w, rel_bias, norm_mem_w, mem_kv_norm_w, w_mq, w_mkv, w_mo,
           norm_ffn_w, w_up, conv_w, conv_b, w_down, final_norm_w):
    batch, seq, d_model = x.shape
    n_mem = mem.shape[1]
    depth = w_in.shape[0]
    assert depth == 1, "single-layer block"
    layer = 0
    t = batch * seq

    da_width = DA_HEADS * LANES
    hg_width = HG_HEADS * LANES
    assert w_in.shape[2] == 3 * da_width + 4 * hg_width
    assert da_subln_w.shape[1] == LANES and hg_norm_w.shape[1] == LANES
    assert lam_q1.shape[1] * 2 == LANES
    n_bf = 3 * da_width + hg_width

    tq = _pick(seq, DA_BLOCK)
    ts = _pick(seq, HG_TILE)

    lam_init = 0.8 - 0.6 * math.exp(-0.3 * layer)
    lam_vecs = jnp.stack([lam_q1[layer], lam_k1[layer], lam_q2[layer], lam_k2[layer]]).astype(F32)
    lam_row, lb = param_tables(lam_vecs, hg_lb_raw.astype(F32), lam_init, layer)

    far = _t5_bucket_np(np.arange(tq + 1, max(seq, tq + 2)))
    assert (far == far[0]).all(), "bias must be constant beyond the sub-diagonal block"
    c = np.arange(2 * tq)
    dist_buckets = np.stack([np.where(c >= tq, _t5_bucket_np(np.maximum(c - tq, 0)), -1),
                             _t5_bucket_np(c)]).astype(np.int32)
    bias_t = bias_tiles(rel_bias.astype(F32), jnp.asarray(dist_buckets), tq, int(far[0]))

    x2d = x.reshape(t, d_model)
    q_scale = (LANES // 2) ** -0.5
    col_scale = np.ones((1, n_bf), np.float32)
    col_scale[:, :da_width] = q_scale
    proj_a, proj_b = in_proj(x2d, norm_mix_w[layer], w_in[layer].astype(BF16), n_bf, jnp.asarray(col_scale))

    da_o, w_up_bf, w_mkv_bf = diff_attention(
        proj_a, bias_t, lam_row, da_subln_w[layer], batch, seq, tq, 1.0 - lam_init,
        riders=(w_up, w_mkv), layer=layer)
    hg_o, w_down_bf, w_out_bf, w_mq_bf, w_mo_bf = hgrn2(
        proj_a, proj_b, lb, hg_norm_w[layer], batch, seq, ts, 3 * DA_HEADS,
        riders=(w_down, w_out, w_mq, w_mo), layer=layer)

    mkv = norm_matmul(mem.reshape(batch * n_mem, d_model), mem_kv_norm_w[layer], w_mkv_bf, BF16,
                      name="mem_kv_proj")
    x2 = mix_out_mem_attention(da_o, hg_o, w_out_bf, x2d, norm_mem_w[layer], w_mq_bf, mkv, w_mo_bf,
                               seq, n_mem)

    act = ffn_up(x2, norm_ffn_w[layer], w_up_bf, conv_w[layer], conv_b[layer], seq)
    out = ffn_down(act, w_down_bf, x2, final_norm_w)
    return out.reshape(batch, seq, d_model)
```

```python
import functools
import math

import numpy as np
import jax
import jax.numpy as jnp
from jax import lax
from jax.experimental import pallas as pl
from jax.experimental.pallas import tpu as pltpu

DA_HEADS = 8
HG_HEADS = 8
MEM_HEADS = 4
NUM_BUCKETS = 32
MAX_DISTANCE = 128
CONV_WIDTH = 3
EPS = 1e-6

LANES = 128
SUBLANES = 8
VMEM_LIMIT_BYTES = 56 * 1024 * 1024

HG_CHUNK = 128
HG_TILE = 512
HG_HEADS_PER_STEP = 4
HG_SMALL_LEVELS = 3
DA_BLOCK = 512
DA_Q_SUB = 512
MM_TM = 1024
MM_TN = 1024
FFN_TF = 512
FFN_COL_CHUNK = 256
RESIDENT_TM = 512
FFN_DOWN_TM = RESIDENT_TM
NEG_BIG = -1e30

F32 = jnp.float32
BF16 = jnp.bfloat16


def _params(*sem, flags=None):
    return pltpu.CompilerParams(dimension_semantics=sem, vmem_limit_bytes=VMEM_LIMIT_BYTES, flags=flags)


def _rms_rows(x, w):
    ms = jnp.mean(x * x, axis=-1, keepdims=True)
    return (x * lax.rsqrt(ms + EPS)) * w


def _pick(n, pref):
    t = min(pref, n)
    while n % t:
        t //= 2
    return t


def _norm_mm_kernel(x_ref, nw_ref, w_ref, o_ref, h_ref):
    @pl.when(pl.program_id(1) == 0)
    def _():
        h_ref[...] = _rms_rows(x_ref[...], nw_ref[...]).astype(h_ref.dtype)

    o_ref[...] = jnp.dot(h_ref[...], w_ref[...], preferred_element_type=F32).astype(o_ref.dtype)


def norm_matmul(x, nw, w, out_dtype, name="norm_matmul"):
    t, k = x.shape
    n = w.shape[1]
    tm, tn = _pick(t, MM_TM), _pick(n, MM_TN)
    return pl.pallas_call(
        _norm_mm_kernel,
        grid=(t // tm, n // tn),
        in_specs=[
            pl.BlockSpec((tm, k), lambda i, j: (i, 0)),
            pl.BlockSpec((1, k), lambda i, j: (0, 0)),
            pl.BlockSpec((k, tn), lambda i, j: (0, j)),
        ],
        out_specs=pl.BlockSpec((tm, tn), lambda i, j: (i, j)),
        out_shape=jax.ShapeDtypeStruct((t, n), out_dtype),
        scratch_shapes=[pltpu.VMEM((tm, k), BF16)],
        compiler_params=_params("parallel", "arbitrary"),
        name=name,
    )(x, nw.reshape(1, k), w)


BF16_ROWS = 16


def _with_cast_riders(body, n_in, n_out, n_riders):
    def kern(*refs):
        ins, rest = refs[:n_in], refs[n_in:]
        rid_in, rest = rest[:n_riders], rest[n_riders:]
        outs, rest = rest[:n_out], rest[n_out:]
        rid_out, scratch = rest[:n_riders], rest[n_riders:]
        for src, dst in zip(rid_in, rid_out):
            dst[...] = src[...].astype(dst.dtype)
        body(*ins, *outs, *scratch)
    return kern


def _rider_specs(riders, layer, grid):
    n_steps = math.prod(grid)

    def step(*g):
        idx = 0
        for gi, n in zip(g, grid):
            idx = idx * n + gi
        return idx

    in_specs, out_specs, shapes = [], [], []
    for a in riders:
        _, rows, cols = a.shape
        n_slabs = math.gcd(n_steps, rows // BF16_ROWS)
        assert rows % BF16_ROWS == 0 and n_steps % n_slabs == 0, (a.shape, grid)
        hold = n_steps // n_slabs
        in_specs.append(pl.BlockSpec((None, rows // n_slabs, cols),
                                     lambda *g, hold=hold: (layer, step(*g) // hold, 0)))
        out_specs.append(pl.BlockSpec((rows // n_slabs, cols),
                                      lambda *g, hold=hold: (step(*g) // hold, 0)))
        shapes.append(jax.ShapeDtypeStruct((rows, cols), BF16))
    return in_specs, out_specs, shapes


def _in_proj_kernel(x_ref, nw_ref, w_ref, cs_ref, oa_ref, ob_ref, h_ref, *, na_tiles):
    j = pl.program_id(1)

    @pl.when(j == 0)
    def _():
        h_ref[...] = _rms_rows(x_ref[...], nw_ref[...]).astype(h_ref.dtype)

    y = jnp.dot(h_ref[...], w_ref[...], preferred_element_type=F32)

    heads = oa_ref.shape[0]

    @pl.when(j < na_tiles)
    def _():
        ys = y * cs_ref[...]
        for hh in range(heads):
            oa_ref[hh] = ys[:, hh * LANES:(hh + 1) * LANES].astype(oa_ref.dtype)

    @pl.when(j >= na_tiles)
    def _():
        for hh in range(heads):
            ob_ref[hh] = y[:, hh * LANES:(hh + 1) * LANES]


def in_proj(x, nw, w, n_a, col_scale):
    t, k = x.shape
    n = w.shape[1]
    tm, tn = _pick(t, MM_TM), math.gcd(_pick(n_a, MM_TN), _pick(n - n_a, MM_TN))
    na_tiles = n_a // tn
    hpt = tn // LANES
    return pl.pallas_call(
        functools.partial(_in_proj_kernel, na_tiles=na_tiles),
        grid=(t // tm, n // tn),
        in_specs=[
            pl.BlockSpec((tm, k), lambda i, j: (i, 0)),
            pl.BlockSpec((1, k), lambda i, j: (0, 0)),
            pl.BlockSpec((k, tn), lambda i, j: (0, j)),
            pl.BlockSpec((1, tn), lambda i, j: (0, jnp.minimum(j, na_tiles - 1))),
        ],
        out_specs=[
            pl.BlockSpec((hpt, tm, LANES), lambda i, j: (jnp.minimum(j, na_tiles - 1), i, 0)),
            pl.BlockSpec((hpt, tm, LANES), lambda i, j: (jnp.maximum(j - na_tiles, 0), i, 0)),
        ],
        out_shape=(jax.ShapeDtypeStruct((n_a // LANES, t, LANES), BF16),
                   jax.ShapeDtypeStruct(((n - n_a) // LANES, t, LANES), F32)),
        scratch_shapes=[pltpu.VMEM((tm, k), BF16)],
        compiler_params=_params("parallel", "arbitrary"),
        name="in_proj",
    )(x, nw.reshape(1, k), w, col_scale)


def _mix_mem_kernel(a0_ref, a1_ref, w0_ref, w1_ref, x_ref, nw_ref, wq_ref, k_ref, v_ref, wo_ref,
                    o_ref, att_ref, *, scale):
    def heads_on_lanes(a_ref):
        return jnp.concatenate([a_ref[hh] for hh in range(a_ref.shape[0])], axis=1)

    x1 = x_ref[...]
    x1 = x1 + jnp.dot(heads_on_lanes(a0_ref), w0_ref[...], preferred_element_type=F32)
    x1 = x1 + jnp.dot(heads_on_lanes(a1_ref), w1_ref[...], preferred_element_type=F32)
    hq = _rms_rows(x1, nw_ref[...]).astype(BF16)
    mq = jnp.dot(hq, wq_ref[...], preferred_element_type=F32).astype(BF16)
    dh = mq.shape[1] // MEM_HEADS
    for hd in range(MEM_HEADS):
        sl = slice(hd * dh, (hd + 1) * dh)
        s = lax.dot_general(mq[:, sl], k_ref[:, sl], (((1,), (1,)), ((), ())),
                            preferred_element_type=F32) * scale
        e = jnp.exp(s - jnp.max(s, axis=-1, keepdims=True))
        p = e / jnp.sum(e, axis=-1, keepdims=True)
        att_ref[:, sl] = jnp.dot(p.astype(BF16), v_ref[:, sl],
                                 preferred_element_type=F32).astype(att_ref.dtype)
    o_ref[...] = x1 + jnp.dot(att_ref[...], wo_ref[...], preferred_element_type=F32)


def mix_out_mem_attention(a0, a1, w_out, x, nw_mem, w_mq, mkv, w_mo, seq, n_mem):
    h0, t, _ = a0.shape
    h1 = a1.shape[0]
    k0, k1 = h0 * LANES, h1 * LANES
    d = x.shape[1]
    assert k0 == k1 and w_out.shape == (k0 + k1, d)
    tm = _pick(seq, RESIDENT_TM)
    per_batch = seq // tm
    scale = (d // MEM_HEADS) ** -0.5
    once = pl.Buffered(1)
    return pl.pallas_call(
        functools.partial(_mix_mem_kernel, scale=scale),
        grid=(t // tm,),
        in_specs=[
            pl.BlockSpec((h0, tm, LANES), lambda i: (0, i, 0)),
            pl.BlockSpec((h1, tm, LANES), lambda i: (0, i, 0)),
            pl.BlockSpec((k0, d), lambda i: (0, 0), pipeline_mode=once),
            pl.BlockSpec((k1, d), lambda i: (1, 0), pipeline_mode=once),
            pl.BlockSpec((tm, d), lambda i: (i, 0)),
            pl.BlockSpec((1, d), lambda i: (0, 0)),
            pl.BlockSpec((d, d), lambda i: (0, 0), pipeline_mode=once),
            pl.BlockSpec((n_mem, d), lambda i: (i // per_batch, 0), pipeline_mode=once),
            pl.BlockSpec((n_mem, d), lambda i: (i // per_batch, 1), pipeline_mode=once),
            pl.BlockSpec((d, d), lambda i: (0, 0), pipeline_mode=once),
        ],
        out_specs=pl.BlockSpec((tm, d), lambda i: (i, 0)),
        out_shape=jax.ShapeDtypeStruct((t, d), F32),
        scratch_shapes=[pltpu.VMEM((tm, d), BF16)],
        compiler_params=_params("parallel"),
        name="mix_out_mem_attention",
    )(a0, a1, w_out, w_out, x, nw_mem.reshape(1, d), w_mq, mkv, mkv, w_mo)


def _t5_bucket_np(n):
    max_exact = NUM_BUCKETS // 2
    nf = np.maximum(n, 1).astype(np.float32)
    large = max_exact + (np.log(nf / max_exact) / math.log(MAX_DISTANCE / max_exact)
                         * (NUM_BUCKETS - max_exact)).astype(np.int32)
    large = np.minimum(large, NUM_BUCKETS - 1)
    return np.where(n < max_exact, n, large).astype(np.int32)


def _tables_kernel(lamv_ref, lbraw_ref, lam_ref, lb_ref, *, lam_init, layer):
    v = lamv_ref[...]
    s1 = jnp.sum(v[0:1] * v[1:2], axis=-1, keepdims=True)
    s2 = jnp.sum(v[2:3] * v[3:4], axis=-1, keepdims=True)
    lam = jnp.exp(s1) - jnp.exp(s2) + lam_init
    lam_ref[...] = jnp.broadcast_to(lam, lam_ref.shape)
    raw = lbraw_ref[...]
    e = jnp.exp(raw - jnp.max(raw, axis=0, keepdims=True))
    sm = e / jnp.sum(e, axis=0, keepdims=True)
    lb_ref[...] = jnp.sum(sm[0:layer + 1], axis=0, keepdims=True)


def param_tables(lam_vecs, lb_raw, lam_init, layer):
    kw = lb_raw.shape[1]
    return pl.pallas_call(
        functools.partial(_tables_kernel, lam_init=lam_init, layer=layer),
        out_shape=(jax.ShapeDtypeStruct((1, LANES), F32), jax.ShapeDtypeStruct((1, kw), F32)),
        name="param_tables",
    )(lam_vecs, lb_raw)


def _bias_tiles_kernel(rb_ref, bk_ref, o_ref, *, tq, far_bucket):
    h = pl.program_id(0)
    bk = bk_ref[...]
    vec = jnp.full(bk.shape, NEG_BIG, F32)
    for n in range(NUM_BUCKETS):
        vec = jnp.where(bk == n, rb_ref[n, h], vec)
    for t in range(2):
        rows = jnp.broadcast_to(vec[t:t + 1], (tq, 2 * tq))
        o_ref[0, t] = pltpu.roll(rows, 0, 1, stride=1, stride_axis=0)[:, tq:]
    o_ref[0, FAR_TILE] = jnp.full((tq, tq), rb_ref[far_bucket, h], F32)


def bias_tiles(rel_bias, dist_buckets, tq, far_bucket):
    nb, h = rel_bias.shape
    return pl.pallas_call(
        functools.partial(_bias_tiles_kernel, tq=tq, far_bucket=far_bucket),
        grid=(h,),
        in_specs=[
            pl.BlockSpec(memory_space=pltpu.SMEM),
            pl.BlockSpec(dist_buckets.shape, lambda i: (0, 0)),
        ],
        out_specs=pl.BlockSpec((1, FAR_TILE + 1, tq, tq), lambda i: (i, 0, 0, 0)),
        out_shape=jax.ShapeDtypeStruct((h, FAR_TILE + 1, tq, tq), F32),
        compiler_params=_params("arbitrary"),
        name="bias_tiles",
    )(rel_bias, dist_buckets)


ONES_ROWS = 16
FAR_TILE = 2


def _diff_attn_kernel(q_ref, k_ref, v_ref, bias_ref, lam_ref, nw_ref, o_ref,
                      vt_ref, m_ref, acc_ref, st_ref, *, tq, out_scale):
    i = pl.program_id(2)
    dv = v_ref.shape[1]
    dh = dv // 2
    nkv = v_ref.shape[0] // tq

    @pl.when(i == 0)
    def _():
        for c in range(nkv):
            vt_ref[c, 0:dv, :] = v_ref[c * tq:(c + 1) * tq, :].astype(F32).T.astype(BF16)
            vt_ref[c, dv:, :] = jnp.ones((ONES_ROWS, tq), BF16)

    q = q_ref[...]
    lane = lax.broadcasted_iota(jnp.int32, (1, dv), 1)
    qm = (jnp.where(lane < dh, q, jnp.zeros_like(q)), jnp.where(lane >= dh, q, jnp.zeros_like(q)))

    m_ref[...] = jnp.full(m_ref.shape, NEG_BIG, F32)
    acc_ref[...] = jnp.zeros(acc_ref.shape, F32)

    qw = min(DA_Q_SUB, tq)
    chains = [(mp, slice(qs * qw, (qs + 1) * qw)) for mp in range(2) for qs in range(tq // qw)]

    def scores(n, j):
        mp, cols = chains[n]
        kb = k_ref[pl.ds(pl.multiple_of(j * tq, tq), tq), :]
        st_ref[n] = lax.dot_general(kb, qm[mp][cols, :], (((1,), (1,)), ((), ())),
                                    preferred_element_type=F32)

    def softmax_pv(n, vt, tile):
        mp, cols = chains[n]
        st = st_ref[n] + bias_ref[0, tile, :, cols]
        m_old = m_ref[mp, :, cols]
        m_new = jnp.maximum(m_old, jnp.max(st, axis=0, keepdims=True))
        p = jnp.exp((st - m_new).astype(BF16))
        alpha = jnp.exp(m_old - m_new)
        acc_ref[mp, :, cols] = (alpha * acc_ref[mp, :, cols]
                                + jnp.dot(vt, p, preferred_element_type=F32))
        m_ref[mp, :, cols] = m_new

    def step(j, carry):
        vt = vt_ref[j]
        tile = jnp.minimum(i - j, FAR_TILE)
        for n in range(len(chains)):
            if n + 1 < len(chains):
                scores(n + 1, j)
            else:
                scores(0, jnp.minimum(j + 1, i))
            softmax_pv(n, vt, tile)
        return carry

    scores(0, 0)
    lax.fori_loop(0, i + 1, step, 0)

    o0 = acc_ref[0, 0:dv, :] / acc_ref[0, dv:dv + 1, :]
    o1 = acc_ref[1, 0:dv, :] / acc_ref[1, dv:dv + 1, :]
    o = o0 - lam_ref[0:1, 0:1] * o1
    ms = jnp.mean(o * o, axis=0, keepdims=True)
    y = (o * lax.rsqrt(ms + EPS)) * (nw_ref[...] * out_scale)
    o_ref[...] = y.T.astype(o_ref.dtype)


def diff_attention(proj, bias_t, lam_row, nw, batch, seq, tq, out_scale, riders=(), layer=0):
    nq = seq // tq
    dv = LANES
    grid = (batch, DA_HEADS, nq)
    rider_in, rider_out, rider_shapes = _rider_specs(riders, layer, grid)
    kern = _with_cast_riders(functools.partial(_diff_attn_kernel, tq=tq, out_scale=out_scale),
                             n_in=6, n_out=1, n_riders=len(riders))
    return pl.pallas_call(
        kern,
        grid=grid,
        in_specs=[
            pl.BlockSpec((None, tq, dv), lambda b, h, i: (h, b * nq + i, 0)),
            pl.BlockSpec((None, seq, dv), lambda b, h, i: (DA_HEADS + h, b, 0)),
            pl.BlockSpec((None, seq, dv), lambda b, h, i: (2 * DA_HEADS + h, b, 0)),
            pl.BlockSpec((1, FAR_TILE + 1, tq, tq), lambda b, h, i: (h, 0, 0, 0)),
            pl.BlockSpec((1, LANES), lambda b, h, i: (0, 0)),
            pl.BlockSpec((dv, 1), lambda b, h, i: (0, 0)),
        ] + rider_in,
        out_specs=[pl.BlockSpec((None, tq, dv), lambda b, h, i: (h, b * nq + i, 0))] + rider_out,
        out_shape=[jax.ShapeDtypeStruct((DA_HEADS, batch * seq, dv), BF16)] + rider_shapes,
        scratch_shapes=[
            pltpu.VMEM((nq, dv + ONES_ROWS, tq), BF16),
            pltpu.VMEM((2, 1, tq), F32),
            pltpu.VMEM((2, dv + ONES_ROWS, tq), F32),
            pltpu.VMEM((2 * tq // min(DA_Q_SUB, tq), tq, min(DA_Q_SUB, tq)), F32),
        ],
        compiler_params=_params("parallel", "parallel", "arbitrary"),
        name="diff_attention",
    )(proj, proj, proj, bias_t, lam_row, nw.reshape(dv, 1), *riders)


def _exact_bf16_dot(tri3, x):
    hi = x.astype(BF16)
    r1 = x - hi.astype(F32)
    mid = r1.astype(BF16)
    lo = (r1 - mid.astype(F32)).astype(BF16)
    return jnp.dot(tri3, jnp.concatenate([hi, mid, lo], axis=0), preferred_element_type=F32)


def _hgrn_tables(c):
    n_lv = int(math.log2(c))
    r = np.arange(c)[None, :]
    t = np.arange(c)[:, None]
    blocks_d, blocks_e = [], []
    for lv in range(2, HG_SMALL_LEVELS + 1):
        h = 1 << (lv - 1)
        upper = (t // h) % 2 == 1
        mid_t = (t // h) * h
        blocks_d.append(upper & (r >= mid_t) & (r <= t))
        mid_s = (t // h + 1) * h
        blocks_e.append((~upper) & (r > t) & (r < mid_s))
    stack = np.concatenate(blocks_d + blocks_e + [r <= t], axis=0).astype(np.float32)
    x = np.bitwise_xor(t, r)
    level = np.where(r == t, 0, np.floor(np.log2(np.maximum(x, 1))).astype(np.int32) + 1)
    level = np.where(r > t, -1, level).astype(np.int32)
    return stack, level, n_lv


def _hgrn_kernel(q_ref, f_ref, i_ref, g_ref, lb_ref, nw_ref, tst_ref, lvl_ref, o_ref, st_ref,
                 *, ts, c, n_lv):
    nt = (((1,), (1,)), ((), ()))
    d = LANES
    nh = q_ref.shape[0]
    nch = ts // c
    n_small = HG_SMALL_LEVELS

    @pl.when(pl.program_id(2) == 0)
    def _():
        st_ref[...] = jnp.zeros(st_ref.shape, F32)

    f, logf, kk = [], [], []
    for hd in range(nh):
        lb = lb_ref[:, hd * d:(hd + 1) * d]
        f.append(lb + (1.0 - lb) * jax.nn.sigmoid(f_ref[hd]))
        logf.append(jnp.log(f[hd]))
        kk.append(1.0 - f[hd])
    units = [(ch, hd) for ch in range(nch) for hd in range(nh)]
    lcat = jnp.concatenate([logf[hd][ch * c:(ch + 1) * c] for ch, hd in units], axis=1)
    seg = _exact_bf16_dot(tst_ref[...], lcat)
    lvl = lvl_ref[...]
    nw = nw_ref[...]

    st = [st_ref[hd] for hd in range(nh)]
    for u, (ch, hd) in enumerate(units):
        rows = slice(ch * c, (ch + 1) * c)

        def blk(b, u=u):
            return seg[b * c:(b + 1) * c, u * d:(u + 1) * d]

        g_in = blk(2 * n_small - 2)
        g_last = g_in[c - 1:c]
        g_out = g_last - g_in
        q = q_ref[hd, rows, :]
        q32 = q.astype(F32)
        kc = kk[hd][rows]
        kb = kc.astype(BF16)
        vb = i_ref[hd, rows, :].astype(BF16)
        a = jnp.where(lvl == 0, lax.dot_general(q, kb, nt, preferred_element_type=F32), 0.0)
        qt = (q32 * f[hd][rows]).astype(BF16)
        a = jnp.where(lvl == 1, lax.dot_general(qt, kb, nt, preferred_element_type=F32), a)
        for lv in range(2, n_lv + 1):
            if lv <= n_small:
                d_up = blk(lv - 2)
                d_lo = blk(n_small + lv - 3)
            else:
                h = 1 << (lv - 1)
                ref = jnp.concatenate(
                    [jnp.broadcast_to(g_in[gi * 2 * h + h - 1:gi * 2 * h + h, :], (2 * h, d))
                     for gi in range(c // (2 * h))], axis=0)
                d_up = jnp.minimum(g_in - ref, 0.0)
                d_lo = jnp.minimum(ref - g_in, 0.0)
            qt = (q32 * jnp.exp(d_up)).astype(BF16)
            kt = (kc * jnp.exp(d_lo)).astype(BF16)
            a = jnp.where(lvl == lv, lax.dot_general(qt, kt, nt, preferred_element_type=F32), a)
        qg = (q32 * jnp.exp(g_in)).astype(BF16)
        o = (jnp.dot(a.astype(BF16), vb, preferred_element_type=F32)
             + lax.dot_general(qg, st[hd].astype(BF16), nt, preferred_element_type=F32))
        kd = (kc * jnp.exp(g_out)).astype(BF16)
        upd = lax.dot_general(vb, kd, (((0,), (0,)), ((), ())), preferred_element_type=F32)
        st[hd] = st[hd] * jnp.exp(g_last) + upd
        g = g_ref[hd, rows, :]
        o_ref[hd, rows, :] = (_rms_rows(o, nw) * (g * jax.nn.sigmoid(g))).astype(o_ref.dtype)
    for hd in range(nh):
        st_ref[hd] = st[hd]


def hgrn2(proj_a, proj_b, lb, nw, batch, seq, ts, q_col0, riders=(), layer=0):
    ns = seq // ts
    d = LANES
    nh = math.gcd(HG_HEADS_PER_STEP, HG_HEADS)
    assert q_col0 % nh == 0
    w = nh * d
    ng = HG_HEADS // nh
    c = min(HG_CHUNK, ts)
    stack, level, n_lv = _hgrn_tables(c)
    assert n_lv >= HG_SMALL_LEVELS
    grid = (batch, ng, ns)
    rider_in, rider_out, rider_shapes = _rider_specs(riders, layer, grid)
    kern = _with_cast_riders(functools.partial(_hgrn_kernel, ts=ts, c=c, n_lv=n_lv),
                             n_in=8, n_out=1, n_riders=len(riders))
    return pl.pallas_call(
        kern,
        grid=grid,
        in_specs=[
            pl.BlockSpec((nh, ts, d), lambda b, h, i: (q_col0 // nh + h, b * ns + i, 0)),
            pl.BlockSpec((nh, ts, d), lambda b, h, i: (h, b * ns + i, 0)),
            pl.BlockSpec((nh, ts, d), lambda b, h, i: (ng + h, b * ns + i, 0)),
            pl.BlockSpec((nh, ts, d), lambda b, h, i: (2 * ng + h, b * ns + i, 0)),
            pl.BlockSpec((1, w), lambda b, h, i: (0, h)),
            pl.BlockSpec((1, d), lambda b, h, i: (0, 0)),
            pl.BlockSpec((stack.shape[0], 3 * stack.shape[1]), lambda b, h, i: (0, 0)),
            pl.BlockSpec(level.shape, lambda b, h, i: (0, 0)),
        ] + rider_in,
        out_specs=[pl.BlockSpec((nh, ts, d), lambda b, h, i: (h, b * ns + i, 0))] + rider_out,
        out_shape=[jax.ShapeDtypeStruct((HG_HEADS, batch * seq, d), BF16)] + rider_shapes,
        scratch_shapes=[pltpu.VMEM((nh, d, d), F32)],
        compiler_params=_params("parallel", "parallel", "arbitrary"),
        name="hgrn2",
    )(proj_a, proj_b, proj_b, proj_b, lb, nw.reshape(1, d),
      jnp.asarray(np.concatenate([stack] * 3, axis=1), BF16), jnp.asarray(level), *riders)


def _ffn_up_kernel(x_ref, halo_ref, nw_ref, wa_ref, wb_ref, cwa_ref, cwb_ref, cba_ref, cbb_ref,
                   o_ref, h_ref, u_ref, *, tm, per_batch):
    hl = SUBLANES

    @pl.when(pl.program_id(1) == 0)
    def _():
        nw = nw_ref[...]
        first = (pl.program_id(0) % per_batch) == 0
        halo = _rms_rows(halo_ref[...], nw)
        h_ref[0:hl, :] = jnp.where(first, 0.0, halo).astype(h_ref.dtype)
        h_ref[hl:, :] = _rms_rows(x_ref[...], nw).astype(h_ref.dtype)

    h = h_ref[...]
    tf = o_ref.shape[1]
    cw = min(FFN_COL_CHUNK, tf)

    def cols(cc):
        return slice(cc * cw, (cc + 1) * cw)

    def matmuls(cc):
        u_ref[cc, 0] = jnp.dot(h, wa_ref[:, cols(cc)], preferred_element_type=F32)
        u_ref[cc, 1] = jnp.dot(h, wb_ref[:, cols(cc)], preferred_element_type=F32)

    def conv(cc, half, cw_ref, cb_ref):
        acc = cb_ref[:, cols(cc)][None]
        for j in range(CONV_WIDTH):
            off = hl - (CONV_WIDTH - 1) + j
            rows = u_ref[cc, half, off:off + tm, :].reshape(tm // hl, hl, cw)
            acc = acc + cw_ref[j, :, cols(cc)][None] * rows
        return acc.reshape(tm, cw)

    def gate(cc):
        a = conv(cc, 0, cwa_ref, cba_ref)
        b = conv(cc, 1, cwb_ref, cbb_ref)
        o_ref[:, cols(cc)] = ((a * jax.nn.sigmoid(a)) * b).astype(o_ref.dtype)

    for cc in range(tf // cw):
        matmuls(cc)
        gate(cc)


def ffn_up(x, nw, w_up, conv_w, conv_b, seq):
    t, d = x.shape
    f = w_up.shape[1] // 2
    tm, tf = _pick(seq, MM_TM), _pick(f, FFN_TF)
    nf = f // tf
    per_batch = seq // tm
    hl = SUBLANES
    rows_per_tile = tm // hl
    cw = min(FFN_COL_CHUNK, tf)
    conv_w8 = jnp.broadcast_to(conv_w[:, None, :], (CONV_WIDTH, hl, 2 * f))
    conv_b8 = jnp.broadcast_to(conv_b.reshape(1, 2 * f), (hl, 2 * f))
    kern = functools.partial(_ffn_up_kernel, tm=tm, per_batch=per_batch)
    return pl.pallas_call(
        kern,
        grid=(t // tm, nf),
        in_specs=[
            pl.BlockSpec((tm, d), lambda i, j: (i, 0)),
            pl.BlockSpec((hl, d), lambda i, j: (jnp.maximum(i * rows_per_tile - 1, 0), 0)),
            pl.BlockSpec((1, d), lambda i, j: (0, 0)),
            pl.BlockSpec((d, tf), lambda i, j: (0, j)),
            pl.BlockSpec((d, tf), lambda i, j: (0, nf + j)),
            pl.BlockSpec((CONV_WIDTH, hl, tf), lambda i, j: (0, 0, j)),
            pl.BlockSpec((CONV_WIDTH, hl, tf), lambda i, j: (0, 0, nf + j)),
            pl.BlockSpec((hl, tf), lambda i, j: (0, j)),
            pl.BlockSpec((hl, tf), lambda i, j: (0, nf + j)),
        ],
        out_specs=pl.BlockSpec((tm, tf), lambda i, j: (i, j)),
        out_shape=jax.ShapeDtypeStruct((t, f), BF16),
        scratch_shapes=[
            pltpu.VMEM((tm + hl, d), BF16),
            pltpu.VMEM((tf // cw, 2, tm + hl, cw), F32),
        ],
        compiler_params=_params("parallel", "arbitrary"),
        name="ffn_up",
    )(x, x, nw.reshape(1, d), w_up, w_up, conv_w8, conv_w8, conv_b8, conv_b8)


def _ffn_down_kernel(a_ref, w_ref, res_ref, nw_ref, o_ref):
    y = res_ref[...] + jnp.dot(a_ref[...], w_ref[...], preferred_element_type=F32)
    o_ref[...] = _rms_rows(y, nw_ref[...])


def ffn_down(act, w_down, res, nw):
    t, f = act.shape
    d = w_down.shape[1]
    tm = _pick(t, FFN_DOWN_TM)
    return pl.pallas_call(
        _ffn_down_kernel,
        grid=(t // tm,),
        in_specs=[
            pl.BlockSpec((tm, f), lambda i: (i, 0)),
            pl.BlockSpec((f, d), lambda i: (0, 0), pipeline_mode=pl.Buffered(1)),
            pl.BlockSpec((tm, d), lambda i: (i, 0)),
            pl.BlockSpec((1, d), lambda i: (0, 0)),
        ],
        out_specs=pl.BlockSpec((tm, d), lambda i: (i, 0)),
        out_shape=jax.ShapeDtypeStruct((t, d), F32),
        compiler_params=_params("parallel"),
        name="ffn_down",
    )(act, w_down, res, nw.reshape(1, d))


def kernel(x, mem, w_in, w_out, norm_mix_w, lam_q1, lam_k1, lam_q2, lam_k2, da_subln_w,
           hg_lb_raw, hg_norm_w, rel_bias, norm_mem_w, mem_kv_norm_w, w_mq, w_mkv, w_mo,
           norm_ffn_w, w_up, conv_w, conv_b, w_down, final_norm_w):
    batch, seq, d_model = x.shape
    n_mem = mem.shape[1]
    depth = w_in.shape[0]
    assert depth == 1, "single-layer block"
    layer = 0
    t = batch * seq

    da_width = DA_HEADS * LANES
    hg_width = HG_HEADS * LANES
    assert w_in.shape[2] == 3 * da_width + 4 * hg_width
    assert da_subln_w.shape[1] == LANES and hg_norm_w.shape[1] == LANES
    assert lam_q1.shape[1] * 2 == LANES
    n_bf = 3 * da_width + hg_width

    tq = _pick(seq, DA_BLOCK)
    ts = _pick(seq, HG_TILE)

    lam_init = 0.8 - 0.6 * math.exp(-0.3 * layer)
    lam_vecs = jnp.stack([lam_q1[layer], lam_k1[layer], lam_q2[layer], lam_k2[layer]]).astype(F32)
    lam_row, lb = param_tables(lam_vecs, hg_lb_raw.astype(F32), lam_init, layer)

    far = _t5_bucket_np(np.arange(tq + 1, max(seq, tq + 2)))
    assert (far == far[0]).all(), "bias must be constant beyond the sub-diagonal block"
    c = np.arange(2 * tq)
    dist_buckets = np.stack([np.where(c >= tq, _t5_bucket_np(np.maximum(c - tq, 0)), -1),
                             _t5_bucket_np(c)]).astype(np.int32)
    bias_t = bias_tiles(rel_bias.astype(F32), jnp.asarray(dist_buckets), tq, int(far[0]))

    x2d = x.reshape(t, d_model)
    q_scale = (LANES // 2) ** -0.5
    col_scale = np.ones((1, n_bf), np.float32)
    col_scale[:, :da_width] = q_scale
    proj_a, proj_b = in_proj(x2d, norm_mix_w[layer], w_in[layer].astype(BF16), n_bf, jnp.asarray(col_scale))

    da_o, w_up_bf, w_mkv_bf = diff_attention(
        proj_a, bias_t, lam_row, da_subln_w[layer], batch, seq, tq, 1.0 - lam_init,
        riders=(w_up, w_mkv), layer=layer)
    hg_o, w_down_bf, w_out_bf, w_mq_bf, w_mo_bf = hgrn2(
        proj_a, proj_b, lb, hg_norm_w[layer], batch, seq, ts, 3 * DA_HEADS,
        riders=(w_down, w_out, w_mq, w_mo), layer=layer)

    mkv = norm_matmul(mem.reshape(batch * n_mem, d_model), mem_kv_norm_w[layer], w_mkv_bf, BF16,
                      name="mem_kv_proj")
    x2 = mix_out_mem_attention(da_o, hg_o, w_out_bf, x2d, norm_mem_w[layer], w_mq_bf, mkv, w_mo_bf,
                               seq, n_mem)

    act = ffn_up(x2, norm_ffn_w[layer], w_up_bf, conv_w[layer], conv_b[layer], seq)
    out = ffn_down(act, w_down_bf, x2, final_norm_w)
    return out.reshape(batch, seq, d_model)
```

```python
import functools
import math

import numpy as np
import jax
import jax.numpy as jnp
from jax import lax
from jax.experimental import pallas as pl
from jax.experimental.pallas import tpu as pltpu

DA_HEADS = 8
HG_HEADS = 8
MEM_HEADS = 4
NUM_BUCKETS = 32
MAX_DISTANCE = 128
CONV_WIDTH = 3
EPS = 1e-6

LANES = 128
SUBLANES = 8
VMEM_LIMIT_BYTES = 56 * 1024 * 1024

HG_CHUNK = 128
HG_TILE = 512
HG_HEADS_PER_STEP = 4
HG_SMALL_LEVELS = 3
DA_BLOCK = 512
DA_Q_SUB = 512
MM_TM = 1024
MM_TN = 1024
FFN_TF = 512
FFN_COL_CHUNK = 256
RESIDENT_TM = 512
FFN_DOWN_TM = RESIDENT_TM
NEG_BIG = -1e30

F32 = jnp.float32
BF16 = jnp.bfloat16


def _params(*sem, flags=None):
    return pltpu.CompilerParams(dimension_semantics=sem, vmem_limit_bytes=VMEM_LIMIT_BYTES, flags=flags)


def _rms_rows(x, w):
    ms = jnp.mean(x * x, axis=-1, keepdims=True)
    return (x * lax.rsqrt(ms + EPS)) * w


def _pick(n, pref):
    t = min(pref, n)
    while n % t:
        t //= 2
    return t


def _norm_mm_kernel(x_ref, nw_ref, w_ref, o_ref, h_ref):
    @pl.when(pl.program_id(1) == 0)
    def _():
        h_ref[...] = _rms_rows(x_ref[...], nw_ref[...]).astype(h_ref.dtype)

    o_ref[...] = jnp.dot(h_ref[...], w_ref[...], preferred_element_type=F32).astype(o_ref.dtype)


def norm_matmul(x, nw, w, out_dtype, name="norm_matmul"):
    t, k = x.shape
    n = w.shape[1]
    tm, tn = _pick(t, MM_TM), _pick(n, MM_TN)
    return pl.pallas_call(
        _norm_mm_kernel,
        grid=(t // tm, n // tn),
        in_specs=[
            pl.BlockSpec((tm, k), lambda i, j: (i, 0)),
            pl.BlockSpec((1, k), lambda i, j: (0, 0)),
            pl.BlockSpec((k, tn), lambda i, j: (0, j)),
        ],
        out_specs=pl.BlockSpec((tm, tn), lambda i, j: (i, j)),
        out_shape=jax.ShapeDtypeStruct((t, n), out_dtype),
        scratch_shapes=[pltpu.VMEM((tm, k), BF16)],
        compiler_params=_params("parallel", "arbitrary"),
        name=name,
    )(x, nw.reshape(1, k), w)


BF16_ROWS = 16


def _with_cast_riders(body, n_in, n_out, n_riders):
    def kern(*refs):
        ins, rest = refs[:n_in], refs[n_in:]
        rid_in, rest = rest[:n_riders], rest[n_riders:]
        outs, rest = rest[:n_out], rest[n_out:]
        rid_out, scratch = rest[:n_riders], rest[n_riders:]
        for src, dst in zip(rid_in, rid_out):
            dst[...] = src[...].astype(dst.dtype)
        body(*ins, *outs, *scratch)
    return kern


def _rider_specs(riders, layer, grid):
    n_steps = math.prod(grid)

    def step(*g):
        idx = 0
        for gi, n in zip(g, grid):
            idx = idx * n + gi
        return idx

    in_specs, out_specs, shapes = [], [], []
    for a in riders:
        _, rows, cols = a.shape
        n_slabs = math.gcd(n_steps, rows // BF16_ROWS)
        assert rows % BF16_ROWS == 0 and n_steps % n_slabs == 0, (a.shape, grid)
        hold = n_steps // n_slabs
        in_specs.append(pl.BlockSpec((None, rows // n_slabs, cols),
                                     lambda *g, hold=hold: (layer, step(*g) // hold, 0)))
        out_specs.append(pl.BlockSpec((rows // n_slabs, cols),
                                      lambda *g, hold=hold: (step(*g) // hold, 0)))
        shapes.append(jax.ShapeDtypeStruct((rows, cols), BF16))
    return in_specs, out_specs, shapes


def _in_proj_kernel(x_ref, nw_ref, w_ref, cs_ref, oa_ref, ob_ref, h_ref, *, na_tiles):
    j = pl.program_id(1)

    @pl.when(j == 0)
    def _():
        h_ref[...] = _rms_rows(x_ref[...], nw_ref[...]).astype(h_ref.dtype)

    y = jnp.dot(h_ref[...], w_ref[...], preferred_element_type=F32)

    @pl.when(j < na_tiles)
    def _():
        oa_ref[...] = (y * cs_ref[...]).astype(oa_ref.dtype)

    @pl.when(j >= na_tiles)
    def _():
        ob_ref[...] = y


def in_proj(x, nw, w, n_a, col_scale):
    t, k = x.shape
    n = w.shape[1]
    tm, tn = _pick(t, MM_TM), math.gcd(_pick(n_a, MM_TN), _pick(n - n_a, MM_TN))
    na_tiles = n_a // tn
    return pl.pallas_call(
        functools.partial(_in_proj_kernel, na_tiles=na_tiles),
        grid=(t // tm, n // tn),
        in_specs=[
            pl.BlockSpec((tm, k), lambda i, j: (i, 0)),
            pl.BlockSpec((1, k), lambda i, j: (0, 0)),
            pl.BlockSpec((k, tn), lambda i, j: (0, j)),
            pl.BlockSpec((1, tn), lambda i, j: (0, jnp.minimum(j, na_tiles - 1))),
        ],
        out_specs=[
            pl.BlockSpec((tm, tn), lambda i, j: (i, jnp.minimum(j, na_tiles - 1))),
            pl.BlockSpec((tm, tn), lambda i, j: (i, jnp.maximum(j - na_tiles, 0))),
        ],
        out_shape=(jax.ShapeDtypeStruct((t, n_a), BF16), jax.ShapeDtypeStruct((t, n - n_a), F32)),
        scratch_shapes=[pltpu.VMEM((tm, k), BF16)],
        compiler_params=_params("parallel", "arbitrary"),
        name="in_proj",
    )(x, nw.reshape(1, k), w, col_scale)


def _mix_mem_kernel(a0_ref, a1_ref, w0_ref, w1_ref, x_ref, nw_ref, wq_ref, k_ref, v_ref, wo_ref,
                    o_ref, att_ref, *, scale):
    x1 = x_ref[...]
    x1 = x1 + jnp.dot(a0_ref[...], w0_ref[...], preferred_element_type=F32)
    x1 = x1 + jnp.dot(a1_ref[...], w1_ref[...], preferred_element_type=F32)
    hq = _rms_rows(x1, nw_ref[...]).astype(BF16)
    mq = jnp.dot(hq, wq_ref[...], preferred_element_type=F32).astype(BF16)
    dh = mq.shape[1] // MEM_HEADS
    for hd in range(MEM_HEADS):
        sl = slice(hd * dh, (hd + 1) * dh)
        s = lax.dot_general(mq[:, sl], k_ref[:, sl], (((1,), (1,)), ((), ())),
                            preferred_element_type=F32) * scale
        e = jnp.exp(s - jnp.max(s, axis=-1, keepdims=True))
        p = e / jnp.sum(e, axis=-1, keepdims=True)
        att_ref[:, sl] = jnp.dot(p.astype(BF16), v_ref[:, sl],
                                 preferred_element_type=F32).astype(att_ref.dtype)
    o_ref[...] = x1 + jnp.dot(att_ref[...], wo_ref[...], preferred_element_type=F32)


def mix_out_mem_attention(a0, a1, w_out, x, nw_mem, w_mq, mkv, w_mo, seq, n_mem):
    t, k0 = a0.shape
    k1 = a1.shape[1]
    d = x.shape[1]
    assert k0 == k1 and w_out.shape == (k0 + k1, d)
    tm = _pick(seq, RESIDENT_TM)
    per_batch = seq // tm
    scale = (d // MEM_HEADS) ** -0.5
    once = pl.Buffered(1)
    return pl.pallas_call(
        functools.partial(_mix_mem_kernel, scale=scale),
        grid=(t // tm,),
        in_specs=[
            pl.BlockSpec((tm, k0), lambda i: (i, 0)),
            pl.BlockSpec((tm, k1), lambda i: (i, 0)),
            pl.BlockSpec((k0, d), lambda i: (0, 0), pipeline_mode=once),
            pl.BlockSpec((k1, d), lambda i: (1, 0), pipeline_mode=once),
            pl.BlockSpec((tm, d), lambda i: (i, 0)),
            pl.BlockSpec((1, d), lambda i: (0, 0)),
            pl.BlockSpec((d, d), lambda i: (0, 0), pipeline_mode=once),
            pl.BlockSpec((n_mem, d), lambda i: (i // per_batch, 0), pipeline_mode=once),
            pl.BlockSpec((n_mem, d), lambda i: (i // per_batch, 1), pipeline_mode=once),
            pl.BlockSpec((d, d), lambda i: (0, 0), pipeline_mode=once),
        ],
        out_specs=pl.BlockSpec((tm, d), lambda i: (i, 0)),
        out_shape=jax.ShapeDtypeStruct((t, d), F32),
        scratch_shapes=[pltpu.VMEM((tm, d), BF16)],
        compiler_params=_params("parallel"),
        name="mix_out_mem_attention",
    )(a0, a1, w_out, w_out, x, nw_mem.reshape(1, d), w_mq, mkv, mkv, w_mo)


def _t5_bucket_np(n):
    max_exact = NUM_BUCKETS // 2
    nf = np.maximum(n, 1).astype(np.float32)
    large = max_exact + (np.log(nf / max_exact) / math.log(MAX_DISTANCE / max_exact)
                         * (NUM_BUCKETS - max_exact)).astype(np.int32)
    large = np.minimum(large, NUM_BUCKETS - 1)
    return np.where(n < max_exact, n, large).astype(np.int32)


def _tables_kernel(lamv_ref, lbraw_ref, lam_ref, lb_ref, *, lam_init, layer):
    v = lamv_ref[...]
    s1 = jnp.sum(v[0:1] * v[1:2], axis=-1, keepdims=True)
    s2 = jnp.sum(v[2:3] * v[3:4], axis=-1, keepdims=True)
    lam = jnp.exp(s1) - jnp.exp(s2) + lam_init
    lam_ref[...] = jnp.broadcast_to(lam, lam_ref.shape)
    raw = lbraw_ref[...]
    e = jnp.exp(raw - jnp.max(raw, axis=0, keepdims=True))
    sm = e / jnp.sum(e, axis=0, keepdims=True)
    lb_ref[...] = jnp.sum(sm[0:layer + 1], axis=0, keepdims=True)


def param_tables(lam_vecs, lb_raw, lam_init, layer):
    kw = lb_raw.shape[1]
    return pl.pallas_call(
        functools.partial(_tables_kernel, lam_init=lam_init, layer=layer),
        out_shape=(jax.ShapeDtypeStruct((1, LANES), F32), jax.ShapeDtypeStruct((1, kw), F32)),
        name="param_tables",
    )(lam_vecs, lb_raw)


def _bias_tiles_kernel(rb_ref, bk_ref, o_ref, *, tq, far_bucket):
    h = pl.program_id(0)
    bk = bk_ref[...]
    vec = jnp.full(bk.shape, NEG_BIG, F32)
    for n in range(NUM_BUCKETS):
        vec = jnp.where(bk == n, rb_ref[n, h], vec)
    for t in range(2):
        rows = jnp.broadcast_to(vec[t:t + 1], (tq, 2 * tq))
        o_ref[0, t] = pltpu.roll(rows, 0, 1, stride=1, stride_axis=0)[:, tq:]
    o_ref[0, FAR_TILE] = jnp.full((tq, tq), rb_ref[far_bucket, h], F32)


def bias_tiles(rel_bias, dist_buckets, tq, far_bucket):
    nb, h = rel_bias.shape
    return pl.pallas_call(
        functools.partial(_bias_tiles_kernel, tq=tq, far_bucket=far_bucket),
        grid=(h,),
        in_specs=[
            pl.BlockSpec(memory_space=pltpu.SMEM),
            pl.BlockSpec(dist_buckets.shape, lambda i: (0, 0)),
        ],
        out_specs=pl.BlockSpec((1, FAR_TILE + 1, tq, tq), lambda i: (i, 0, 0, 0)),
        out_shape=jax.ShapeDtypeStruct((h, FAR_TILE + 1, tq, tq), F32),
        compiler_params=_params("arbitrary"),
        name="bias_tiles",
    )(rel_bias, dist_buckets)


ONES_ROWS = 16
FAR_TILE = 2


def _diff_attn_kernel(q_ref, k_ref, v_ref, bias_ref, lam_ref, nw_ref, o_ref,
                      vt_ref, m_ref, acc_ref, st_ref, *, tq, out_scale):
    i = pl.program_id(2)
    dv = v_ref.shape[1]
    dh = dv // 2
    nkv = v_ref.shape[0] // tq

    @pl.when(i == 0)
    def _():
        for c in range(nkv):
            vt_ref[c, 0:dv, :] = v_ref[c * tq:(c + 1) * tq, :].astype(F32).T.astype(BF16)
            vt_ref[c, dv:, :] = jnp.ones((ONES_ROWS, tq), BF16)

    q = q_ref[...]
    lane = lax.broadcasted_iota(jnp.int32, (1, dv), 1)
    qm = (jnp.where(lane < dh, q, jnp.zeros_like(q)), jnp.where(lane >= dh, q, jnp.zeros_like(q)))

    m_ref[...] = jnp.full(m_ref.shape, NEG_BIG, F32)
    acc_ref[...] = jnp.zeros(acc_ref.shape, F32)

    qw = min(DA_Q_SUB, tq)
    chains = [(mp, slice(qs * qw, (qs + 1) * qw)) for mp in range(2) for qs in range(tq // qw)]

    def scores(n, j):
        mp, cols = chains[n]
        kb = k_ref[pl.ds(pl.multiple_of(j * tq, tq), tq), :]
        st_ref[n] = lax.dot_general(kb, qm[mp][cols, :], (((1,), (1,)), ((), ())),
                                    preferred_element_type=F32)

    def softmax_pv(n, vt, tile):
        mp, cols = chains[n]
        st = st_ref[n] + bias_ref[0, tile, :, cols]
        m_old = m_ref[mp, :, cols]
        m_new = jnp.maximum(m_old, jnp.max(st, axis=0, keepdims=True))
        p = jnp.exp(st - m_new).astype(BF16)
        alpha = jnp.exp(m_old - m_new)
        acc_ref[mp, :, cols] = (alpha * acc_ref[mp, :, cols]
                                + jnp.dot(vt, p, preferred_element_type=F32))
        m_ref[mp, :, cols] = m_new

    def step(j, carry):
        vt = vt_ref[j]
        tile = jnp.minimum(i - j, FAR_TILE)
        for n in range(len(chains)):
            if n + 1 < len(chains):
                scores(n + 1, j)
            else:
                scores(0, jnp.minimum(j + 1, i))
            softmax_pv(n, vt, tile)
        return carry

    scores(0, 0)
    lax.fori_loop(0, i + 1, step, 0)

    o0 = acc_ref[0, 0:dv, :] / acc_ref[0, dv:dv + 1, :]
    o1 = acc_ref[1, 0:dv, :] / acc_ref[1, dv:dv + 1, :]
    o = o0 - lam_ref[0:1, 0:1] * o1
    ms = jnp.mean(o * o, axis=0, keepdims=True)
    y = (o * lax.rsqrt(ms + EPS)) * (nw_ref[...] * out_scale)
    o_ref[...] = y.T.astype(o_ref.dtype)


def diff_attention(proj, bias_t, lam_row, nw, batch, seq, tq, out_scale, riders=(), layer=0):
    nq = seq // tq
    dv = LANES
    grid = (batch, DA_HEADS, nq)
    rider_in, rider_out, rider_shapes = _rider_specs(riders, layer, grid)
    kern = _with_cast_riders(functools.partial(_diff_attn_kernel, tq=tq, out_scale=out_scale),
                             n_in=6, n_out=1, n_riders=len(riders))
    return pl.pallas_call(
        kern,
        grid=grid,
        in_specs=[
            pl.BlockSpec((tq, dv), lambda b, h, i: (b * nq + i, h)),
            pl.BlockSpec((seq, dv), lambda b, h, i: (b, DA_HEADS + h)),
            pl.BlockSpec((seq, dv), lambda b, h, i: (b, 2 * DA_HEADS + h)),
            pl.BlockSpec((1, FAR_TILE + 1, tq, tq), lambda b, h, i: (h, 0, 0, 0)),
            pl.BlockSpec((1, LANES), lambda b, h, i: (0, 0)),
            pl.BlockSpec((dv, 1), lambda b, h, i: (0, 0)),
        ] + rider_in,
        out_specs=[pl.BlockSpec((tq, dv), lambda b, h, i: (b * nq + i, h))] + rider_out,
        out_shape=[jax.ShapeDtypeStruct((batch * seq, DA_HEADS * dv), BF16)] + rider_shapes,
        scratch_shapes=[
            pltpu.VMEM((nq, dv + ONES_ROWS, tq), BF16),
            pltpu.VMEM((2, 1, tq), F32),
            pltpu.VMEM((2, dv + ONES_ROWS, tq), F32),
            pltpu.VMEM((2 * tq // min(DA_Q_SUB, tq), tq, min(DA_Q_SUB, tq)), F32),
        ],
        compiler_params=_params("parallel", "parallel", "arbitrary"),
        name="diff_attention",
    )(proj, proj, proj, bias_t, lam_row, nw.reshape(dv, 1), *riders)


def _exact_bf16_dot(tri3, x):
    hi = x.astype(BF16)
    r1 = x - hi.astype(F32)
    mid = r1.astype(BF16)
    lo = (r1 - mid.astype(F32)).astype(BF16)
    return jnp.dot(tri3, jnp.concatenate([hi, mid, lo], axis=0), preferred_element_type=F32)


def _hgrn_tables(c):
    n_lv = int(math.log2(c))
    r = np.arange(c)[None, :]
    t = np.arange(c)[:, None]
    blocks_d, blocks_e = [], []
    for lv in range(2, HG_SMALL_LEVELS + 1):
        h = 1 << (lv - 1)
        upper = (t // h) % 2 == 1
        mid_t = (t // h) * h
        blocks_d.append(upper & (r >= mid_t) & (r <= t))
        mid_s = (t // h + 1) * h
        blocks_e.append((~upper) & (r > t) & (r < mid_s))
    stack = np.concatenate(blocks_d + blocks_e + [r <= t], axis=0).astype(np.float32)
    x = np.bitwise_xor(t, r)
    level = np.where(r == t, 0, np.floor(np.log2(np.maximum(x, 1))).astype(np.int32) + 1)
    level = np.where(r > t, -1, level).astype(np.int32)
    return stack, level, n_lv


def _hgrn_kernel(q_ref, f_ref, i_ref, g_ref, lb_ref, nw_ref, tst_ref, lvl_ref, o_ref, st_ref,
                 *, ts, c, n_lv):
    nt = (((1,), (1,)), ((), ()))
    d = LANES
    nh = q_ref.shape[1] // d
    nch = ts // c
    n_small = HG_SMALL_LEVELS

    @pl.when(pl.program_id(2) == 0)
    def _():
        st_ref[...] = jnp.zeros(st_ref.shape, F32)

    lb = lb_ref[...]
    f = lb + (1.0 - lb) * jax.nn.sigmoid(f_ref[...])
    logf = jnp.log(f)
    kk = 1.0 - f
    units = [(ch, hd) for ch in range(nch) for hd in range(nh)]
    lcat = jnp.concatenate([logf[ch * c:(ch + 1) * c, hd * d:(hd + 1) * d] for ch, hd in units], axis=1)
    seg = _exact_bf16_dot(tst_ref[...], lcat)
    lvl = lvl_ref[...]
    nw = nw_ref[...]

    st = [st_ref[hd] for hd in range(nh)]
    for u, (ch, hd) in enumerate(units):
        rows = slice(ch * c, (ch + 1) * c)
        cols = slice(hd * d, (hd + 1) * d)

        def blk(b, u=u):
            return seg[b * c:(b + 1) * c, u * d:(u + 1) * d]

        g_in = blk(2 * n_small - 2)
        g_last = g_in[c - 1:c]
        g_out = g_last - g_in
        q = q_ref[rows, cols]
        q32 = q.astype(F32)
        kc = kk[rows, cols]
        kb = kc.astype(BF16)
        vb = i_ref[rows, cols].astype(BF16)
        a = jnp.where(lvl == 0, lax.dot_general(q, kb, nt, preferred_element_type=F32), 0.0)
        qt = (q32 * f[rows, cols]).astype(BF16)
        a = jnp.where(lvl == 1, lax.dot_general(qt, kb, nt, preferred_element_type=F32), a)
        for lv in range(2, n_lv + 1):
            if lv <= n_small:
                d_up = blk(lv - 2)
                d_lo = blk(n_small + lv - 3)
            else:
                h = 1 << (lv - 1)
                ref = jnp.concatenate(
                    [jnp.broadcast_to(g_in[gi * 2 * h + h - 1:gi * 2 * h + h, :], (2 * h, d))
                     for gi in range(c // (2 * h))], axis=0)
                d_up = jnp.minimum(g_in - ref, 0.0)
                d_lo = jnp.minimum(ref - g_in, 0.0)
            qt = (q32 * jnp.exp(d_up)).astype(BF16)
            kt = (kc * jnp.exp(d_lo)).astype(BF16)
            a = jnp.where(lvl == lv, lax.dot_general(qt, kt, nt, preferred_element_type=F32), a)
        qg = (q32 * jnp.exp(g_in)).astype(BF16)
        o = (jnp.dot(a.astype(BF16), vb, preferred_element_type=F32)
             + lax.dot_general(qg, st[hd].astype(BF16), nt, preferred_element_type=F32))
        kd = (kc * jnp.exp(g_out)).astype(BF16)
        upd = lax.dot_general(vb, kd, (((0,), (0,)), ((), ())), preferred_element_type=F32)
        st[hd] = st[hd] * jnp.exp(g_last) + upd
        g = g_ref[rows, cols]
        o_ref[rows, cols] = (_rms_rows(o, nw) * (g * jax.nn.sigmoid(g))).astype(o_ref.dtype)
    for hd in range(nh):
        st_ref[hd] = st[hd]


def hgrn2(proj_a, proj_b, lb, nw, batch, seq, ts, q_col0, riders=(), layer=0):
    ns = seq // ts
    d = LANES
    nh = math.gcd(HG_HEADS_PER_STEP, HG_HEADS)
    assert q_col0 % nh == 0
    w = nh * d
    ng = HG_HEADS // nh
    c = min(HG_CHUNK, ts)
    stack, level, n_lv = _hgrn_tables(c)
    assert n_lv >= HG_SMALL_LEVELS
    grid = (batch, ng, ns)
    rider_in, rider_out, rider_shapes = _rider_specs(riders, layer, grid)
    kern = _with_cast_riders(functools.partial(_hgrn_kernel, ts=ts, c=c, n_lv=n_lv),
                             n_in=8, n_out=1, n_riders=len(riders))
    return pl.pallas_call(
        kern,
        grid=grid,
        in_specs=[
            pl.BlockSpec((ts, w), lambda b, h, i: (b * ns + i, q_col0 // nh + h)),
            pl.BlockSpec((ts, w), lambda b, h, i: (b * ns + i, h)),
            pl.BlockSpec((ts, w), lambda b, h, i: (b * ns + i, ng + h)),
            pl.BlockSpec((ts, w), lambda b, h, i: (b * ns + i, 2 * ng + h)),
            pl.BlockSpec((1, w), lambda b, h, i: (0, h)),
            pl.BlockSpec((1, d), lambda b, h, i: (0, 0)),
            pl.BlockSpec((stack.shape[0], 3 * stack.shape[1]), lambda b, h, i: (0, 0)),
            pl.BlockSpec(level.shape, lambda b, h, i: (0, 0)),
        ] + rider_in,
        out_specs=[pl.BlockSpec((ts, w), lambda b, h, i: (b * ns + i, h))] + rider_out,
        out_shape=[jax.ShapeDtypeStruct((batch * seq, HG_HEADS * d), BF16)] + rider_shapes,
        scratch_shapes=[pltpu.VMEM((nh, d, d), F32)],
        compiler_params=_params("parallel", "parallel", "arbitrary"),
        name="hgrn2",
    )(proj_a, proj_b, proj_b, proj_b, lb, nw.reshape(1, d),
      jnp.asarray(np.concatenate([stack] * 3, axis=1), BF16), jnp.asarray(level), *riders)


def _ffn_up_kernel(x_ref, halo_ref, nw_ref, wa_ref, wb_ref, cwa_ref, cwb_ref, cba_ref, cbb_ref,
                   o_ref, h_ref, u_ref, *, tm, per_batch):
    hl = SUBLANES

    @pl.when(pl.program_id(1) == 0)
    def _():
        nw = nw_ref[...]
        first = (pl.program_id(0) % per_batch) == 0
        halo = _rms_rows(halo_ref[...], nw)
        h_ref[0:hl, :] = jnp.where(first, 0.0, halo).astype(h_ref.dtype)
        h_ref[hl:, :] = _rms_rows(x_ref[...], nw).astype(h_ref.dtype)

    h = h_ref[...]
    tf = o_ref.shape[1]
    cw = min(FFN_COL_CHUNK, tf)

    def cols(cc):
        return slice(cc * cw, (cc + 1) * cw)

    def matmuls(cc):
        u_ref[cc, 0] = jnp.dot(h, wa_ref[:, cols(cc)], preferred_element_type=F32)
        u_ref[cc, 1] = jnp.dot(h, wb_ref[:, cols(cc)], preferred_element_type=F32)

    def conv(cc, half, cw_ref, cb_ref):
        acc = cb_ref[:, cols(cc)][None]
        for j in range(CONV_WIDTH):
            off = hl - (CONV_WIDTH - 1) + j
            rows = u_ref[cc, half, off:off + tm, :].reshape(tm // hl, hl, cw)
            acc = acc + cw_ref[j, :, cols(cc)][None] * rows
        return acc.reshape(tm, cw)

    def gate(cc):
        a = conv(cc, 0, cwa_ref, cba_ref)
        b = conv(cc, 1, cwb_ref, cbb_ref)
        o_ref[:, cols(cc)] = ((a * jax.nn.sigmoid(a)) * b).astype(o_ref.dtype)

    for cc in range(tf // cw):
        matmuls(cc)
        gate(cc)


def ffn_up(x, nw, w_up, conv_w, conv_b, seq):
    t, d = x.shape
    f = w_up.shape[1] // 2
    tm, tf = _pick(seq, MM_TM), _pick(f, FFN_TF)
    nf = f // tf
    per_batch = seq // tm
    hl = SUBLANES
    rows_per_tile = tm // hl
    cw = min(FFN_COL_CHUNK, tf)
    conv_w8 = jnp.broadcast_to(conv_w[:, None, :], (CONV_WIDTH, hl, 2 * f))
    conv_b8 = jnp.broadcast_to(conv_b.reshape(1, 2 * f), (hl, 2 * f))
    kern = functools.partial(_ffn_up_kernel, tm=tm, per_batch=per_batch)
    return pl.pallas_call(
        kern,
        grid=(t // tm, nf),
        in_specs=[
            pl.BlockSpec((tm, d), lambda i, j: (i, 0)),
            pl.BlockSpec((hl, d), lambda i, j: (jnp.maximum(i * rows_per_tile - 1, 0), 0)),
            pl.BlockSpec((1, d), lambda i, j: (0, 0)),
            pl.BlockSpec((d, tf), lambda i, j: (0, j)),
            pl.BlockSpec((d, tf), lambda i, j: (0, nf + j)),
            pl.BlockSpec((CONV_WIDTH, hl, tf), lambda i, j: (0, 0, j)),
            pl.BlockSpec((CONV_WIDTH, hl, tf), lambda i, j: (0, 0, nf + j)),
            pl.BlockSpec((hl, tf), lambda i, j: (0, j)),
            pl.BlockSpec((hl, tf), lambda i, j: (0, nf + j)),
        ],
        out_specs=pl.BlockSpec((tm, tf), lambda i, j: (i, j)),
        out_shape=jax.ShapeDtypeStruct((t, f), BF16),
        scratch_shapes=[
            pltpu.VMEM((tm + hl, d), BF16),
            pltpu.VMEM((tf // cw, 2, tm + hl, cw), F32),
        ],
        compiler_params=_params("parallel", "arbitrary"),
        name="ffn_up",
    )(x, x, nw.reshape(1, d), w_up, w_up, conv_w8, conv_w8, conv_b8, conv_b8)


def _ffn_down_kernel(a_ref, w_ref, res_ref, nw_ref, o_ref):
    y = res_ref[...] + jnp.dot(a_ref[...], w_ref[...], preferred_element_type=F32)
    o_ref[...] = _rms_rows(y, nw_ref[...])


def ffn_down(act, w_down, res, nw):
    t, f = act.shape
    d = w_down.shape[1]
    tm = _pick(t, FFN_DOWN_TM)
    return pl.pallas_call(
        _ffn_down_kernel,
        grid=(t // tm,),
        in_specs=[
            pl.BlockSpec((tm, f), lambda i: (i, 0)),
            pl.BlockSpec((f, d), lambda i: (0, 0), pipeline_mode=pl.Buffered(1)),
            pl.BlockSpec((tm, d), lambda i: (i, 0)),
            pl.BlockSpec((1, d), lambda i: (0, 0)),
        ],
        out_specs=pl.BlockSpec((tm, d), lambda i: (i, 0)),
        out_shape=jax.ShapeDtypeStruct((t, d), F32),
        compiler_params=_params("parallel"),
        name="ffn_down",
    )(act, w_down, res, nw.reshape(1, d))


def kernel(x, mem, w_in, w_out, norm_mix_w, lam_q1, lam_k1, lam_q2, lam_k2, da_subln_w,
           hg_lb_raw, hg_norm_w, rel_bias, norm_mem_w, mem_kv_norm_w, w_mq, w_mkv, w_mo,
           norm_ffn_w, w_up, conv_w, conv_b, w_down, final_norm_w):
    batch, seq, d_model = x.shape
    n_mem = mem.shape[1]
    depth = w_in.shape[0]
    assert depth == 1, "single-layer block"
    layer = 0
    t = batch * seq

    da_width = DA_HEADS * LANES
    hg_width = HG_HEADS * LANES
    assert w_in.shape[2] == 3 * da_width + 4 * hg_width
    assert da_subln_w.shape[1] == LANES and hg_norm_w.shape[1] == LANES
    assert lam_q1.shape[1] * 2 == LANES
    n_bf = 3 * da_width + hg_width

    tq = _pick(seq, DA_BLOCK)
    ts = _pick(seq, HG_TILE)

    lam_init = 0.8 - 0.6 * math.exp(-0.3 * layer)
    lam_vecs = jnp.stack([lam_q1[layer], lam_k1[layer], lam_q2[layer], lam_k2[layer]]).astype(F32)
    lam_row, lb = param_tables(lam_vecs, hg_lb_raw.astype(F32), lam_init, layer)

    far = _t5_bucket_np(np.arange(tq + 1, max(seq, tq + 2)))
    assert (far == far[0]).all(), "bias must be constant beyond the sub-diagonal block"
    c = np.arange(2 * tq)
    dist_buckets = np.stack([np.where(c >= tq, _t5_bucket_np(np.maximum(c - tq, 0)), -1),
                             _t5_bucket_np(c)]).astype(np.int32)
    bias_t = bias_tiles(rel_bias.astype(F32), jnp.asarray(dist_buckets), tq, int(far[0]))

    x2d = x.reshape(t, d_model)
    q_scale = (LANES // 2) ** -0.5
    col_scale = np.ones((1, n_bf), np.float32)
    col_scale[:, :da_width] = q_scale
    proj_a, proj_b = in_proj(x2d, norm_mix_w[layer], w_in[layer].astype(BF16), n_bf, jnp.asarray(col_scale))

    da_o, w_up_bf, w_mkv_bf = diff_attention(
        proj_a, bias_t, lam_row, da_subln_w[layer], batch, seq, tq, 1.0 - lam_init,
        riders=(w_up, w_mkv), layer=layer)
    hg_o, w_down_bf, w_out_bf, w_mq_bf, w_mo_bf = hgrn2(
        proj_a, proj_b, lb, hg_norm_w[layer], batch, seq, ts, 3 * DA_HEADS,
        riders=(w_down, w_out, w_mq, w_mo), layer=layer)

    mkv = norm_matmul(mem.reshape(batch * n_mem, d_model), mem_kv_norm_w[layer], w_mkv_bf, BF16,
                      name="mem_kv_proj")
    x2 = mix_out_mem_attention(da_o, hg_o, w_out_bf, x2d, norm_mem_w[layer], w_mq_bf, mkv, w_mo_bf,
                               seq, n_mem)

    act = ffn_up(x2, norm_ffn_w[layer], w_up_bf, conv_w[layer], conv_b[layer], seq)
    out = ffn_down(act, w_down_bf, x2, final_norm_w)
    return out.reshape(batch, seq, d_model)
```

```python
import functools
import math

import numpy as np
import jax
import jax.numpy as jnp
from jax import lax
from jax.experimental import pallas as pl
from jax.experimental.pallas import tpu as pltpu

DA_HEADS = 8
HG_HEADS = 8
MEM_HEADS = 4
NUM_BUCKETS = 32
MAX_DISTANCE = 128
CONV_WIDTH = 3
EPS = 1e-6

LANES = 128
SUBLANES = 8
VMEM_LIMIT_BYTES = 56 * 1024 * 1024

HG_CHUNK = 128
HG_TILE = 512
HG_HEADS_PER_STEP = 4
HG_SMALL_LEVELS = 3
DA_BLOCK = 512
DA_Q_SUB = 512
MM_TM = 1024
MM_TN = 1024
FFN_TF = 512
FFN_COL_CHUNK = 256
RESIDENT_TM = 512
FFN_DOWN_TM = RESIDENT_TM
NEG_BIG = -1e30

F32 = jnp.float32
BF16 = jnp.bfloat16


def _params(*sem, flags=None):
    return pltpu.CompilerParams(dimension_semantics=sem, vmem_limit_bytes=VMEM_LIMIT_BYTES, flags=flags)


def _rms_rows(x, w):
    ms = jnp.mean(x * x, axis=-1, keepdims=True)
    return (x * lax.rsqrt(ms + EPS)) * w


def _pick(n, pref):
    t = min(pref, n)
    while n % t:
        t //= 2
    return t


def _norm_mm_kernel(x_ref, nw_ref, w_ref, o_ref, h_ref):
    @pl.when(pl.program_id(1) == 0)
    def _():
        h_ref[...] = _rms_rows(x_ref[...], nw_ref[...]).astype(h_ref.dtype)

    o_ref[...] = jnp.dot(h_ref[...], w_ref[...], preferred_element_type=F32).astype(o_ref.dtype)


def norm_matmul(x, nw, w, out_dtype, name="norm_matmul"):
    t, k = x.shape
    n = w.shape[1]
    tm, tn = _pick(t, MM_TM), _pick(n, MM_TN)
    return pl.pallas_call(
        _norm_mm_kernel,
        grid=(t // tm, n // tn),
        in_specs=[
            pl.BlockSpec((tm, k), lambda i, j: (i, 0)),
            pl.BlockSpec((1, k), lambda i, j: (0, 0)),
            pl.BlockSpec((k, tn), lambda i, j: (0, j)),
        ],
        out_specs=pl.BlockSpec((tm, tn), lambda i, j: (i, j)),
        out_shape=jax.ShapeDtypeStruct((t, n), out_dtype),
        scratch_shapes=[pltpu.VMEM((tm, k), BF16)],
        compiler_params=_params("parallel", "arbitrary"),
        name=name,
    )(x, nw.reshape(1, k), w)


BF16_ROWS = 16


def _with_cast_riders(body, n_in, n_out, n_riders):
    def kern(*refs):
        ins, rest = refs[:n_in], refs[n_in:]
        rid_in, rest = rest[:n_riders], rest[n_riders:]
        outs, rest = rest[:n_out], rest[n_out:]
        rid_out, scratch = rest[:n_riders], rest[n_riders:]
        for src, dst in zip(rid_in, rid_out):
            dst[...] = src[...].astype(dst.dtype)
        body(*ins, *outs, *scratch)
    return kern


def _rider_specs(riders, layer, grid):
    n_steps = math.prod(grid)

    def step(*g):
        idx = 0
        for gi, n in zip(g, grid):
            idx = idx * n + gi
        return idx

    in_specs, out_specs, shapes = [], [], []
    for a in riders:
        _, rows, cols = a.shape
        n_slabs = math.gcd(n_steps, rows // BF16_ROWS)
        assert rows % BF16_ROWS == 0 and n_steps % n_slabs == 0, (a.shape, grid)
        hold = n_steps // n_slabs
        in_specs.append(pl.BlockSpec((None, rows // n_slabs, cols),
                                     lambda *g, hold=hold: (layer, step(*g) // hold, 0)))
        out_specs.append(pl.BlockSpec((rows // n_slabs, cols),
                                      lambda *g, hold=hold: (step(*g) // hold, 0)))
        shapes.append(jax.ShapeDtypeStruct((rows, cols), BF16))
    return in_specs, out_specs, shapes


def _in_proj_kernel(x_ref, nw_ref, w_ref, cs_ref, oa_ref, ob_ref, h_ref, *, na_tiles):
    j = pl.program_id(1)

    @pl.when(j == 0)
    def _():
        h_ref[...] = _rms_rows(x_ref[...], nw_ref[...]).astype(h_ref.dtype)

    y = jnp.dot(h_ref[...], w_ref[...], preferred_element_type=F32)

    @pl.when(j < na_tiles)
    def _():
        oa_ref[...] = (y * cs_ref[...]).astype(oa_ref.dtype)

    @pl.when(j >= na_tiles)
    def _():
        ob_ref[...] = y


def in_proj(x, nw, w, n_a, col_scale):
    t, k = x.shape
    n = w.shape[1]
    tm, tn = _pick(t, MM_TM), math.gcd(_pick(n_a, MM_TN), _pick(n - n_a, MM_TN))
    na_tiles = n_a // tn
    return pl.pallas_call(
        functools.partial(_in_proj_kernel, na_tiles=na_tiles),
        grid=(t // tm, n // tn),
        in_specs=[
            pl.BlockSpec((tm, k), lambda i, j: (i, 0)),
            pl.BlockSpec((1, k), lambda i, j: (0, 0)),
            pl.BlockSpec((k, tn), lambda i, j: (0, j)),
            pl.BlockSpec((1, tn), lambda i, j: (0, jnp.minimum(j, na_tiles - 1))),
        ],
        out_specs=[
            pl.BlockSpec((tm, tn), lambda i, j: (i, jnp.minimum(j, na_tiles - 1))),
            pl.BlockSpec((tm, tn), lambda i, j: (i, jnp.maximum(j - na_tiles, 0))),
        ],
        out_shape=(jax.ShapeDtypeStruct((t, n_a), BF16), jax.ShapeDtypeStruct((t, n - n_a), F32)),
        scratch_shapes=[pltpu.VMEM((tm, k), BF16)],
        compiler_params=_params("parallel", "arbitrary"),
        name="in_proj",
    )(x, nw.reshape(1, k), w, col_scale)


def _mix_mem_kernel(a0_ref, a1_ref, w0_ref, w1_ref, x_ref, nw_ref, wq_ref, k_ref, v_ref, wo_ref,
                    o_ref, att_ref, *, scale):
    x1 = x_ref[...]
    x1 = x1 + jnp.dot(a0_ref[...], w0_ref[...], preferred_element_type=F32)
    x1 = x1 + jnp.dot(a1_ref[...], w1_ref[...], preferred_element_type=F32)
    hq = _rms_rows(x1, nw_ref[...]).astype(BF16)
    mq = jnp.dot(hq, wq_ref[...], preferred_element_type=F32).astype(BF16)
    dh = mq.shape[1] // MEM_HEADS
    for hd in range(MEM_HEADS):
        sl = slice(hd * dh, (hd + 1) * dh)
        s = lax.dot_general(mq[:, sl], k_ref[:, sl], (((1,), (1,)), ((), ())),
                            preferred_element_type=F32) * scale
        e = jnp.exp(s - jnp.max(s, axis=-1, keepdims=True))
        p = e / jnp.sum(e, axis=-1, keepdims=True)
        att_ref[:, sl] = jnp.dot(p.astype(BF16), v_ref[:, sl],
                                 preferred_element_type=F32).astype(att_ref.dtype)
    o_ref[...] = x1 + jnp.dot(att_ref[...], wo_ref[...], preferred_element_type=F32)


def mix_out_mem_attention(a0, a1, w_out, x, nw_mem, w_mq, mkv, w_mo, seq, n_mem):
    t, k0 = a0.shape
    k1 = a1.shape[1]
    d = x.shape[1]
    assert k0 == k1 and w_out.shape == (k0 + k1, d)
    tm = _pick(seq, RESIDENT_TM)
    per_batch = seq // tm
    scale = (d // MEM_HEADS) ** -0.5
    once = pl.Buffered(1)
    return pl.pallas_call(
        functools.partial(_mix_mem_kernel, scale=scale),
        grid=(t // tm,),
        in_specs=[
            pl.BlockSpec((tm, k0), lambda i: (i, 0)),
            pl.BlockSpec((tm, k1), lambda i: (i, 0)),
            pl.BlockSpec((k0, d), lambda i: (0, 0), pipeline_mode=once),
            pl.BlockSpec((k1, d), lambda i: (1, 0), pipeline_mode=once),
            pl.BlockSpec((tm, d), lambda i: (i, 0)),
            pl.BlockSpec((1, d), lambda i: (0, 0)),
            pl.BlockSpec((d, d), lambda i: (0, 0), pipeline_mode=once),
            pl.BlockSpec((n_mem, d), lambda i: (i // per_batch, 0), pipeline_mode=once),
            pl.BlockSpec((n_mem, d), lambda i: (i // per_batch, 1), pipeline_mode=once),
            pl.BlockSpec((d, d), lambda i: (0, 0), pipeline_mode=once),
        ],
        out_specs=pl.BlockSpec((tm, d), lambda i: (i, 0)),
        out_shape=jax.ShapeDtypeStruct((t, d), F32),
        scratch_shapes=[pltpu.VMEM((tm, d), BF16)],
        compiler_params=_params("parallel"),
        name="mix_out_mem_attention",
    )(a0, a1, w_out, w_out, x, nw_mem.reshape(1, d), w_mq, mkv, mkv, w_mo)


def _t5_bucket_np(n):
    max_exact = NUM_BUCKETS // 2
    nf = np.maximum(n, 1).astype(np.float32)
    large = max_exact + (np.log(nf / max_exact) / math.log(MAX_DISTANCE / max_exact)
                         * (NUM_BUCKETS - max_exact)).astype(np.int32)
    large = np.minimum(large, NUM_BUCKETS - 1)
    return np.where(n < max_exact, n, large).astype(np.int32)


def _tables_kernel(lamv_ref, lbraw_ref, lam_ref, lb_ref, *, lam_init, layer):
    v = lamv_ref[...]
    s1 = jnp.sum(v[0:1] * v[1:2], axis=-1, keepdims=True)
    s2 = jnp.sum(v[2:3] * v[3:4], axis=-1, keepdims=True)
    lam = jnp.exp(s1) - jnp.exp(s2) + lam_init
    lam_ref[...] = jnp.broadcast_to(lam, lam_ref.shape)
    raw = lbraw_ref[...]
    e = jnp.exp(raw - jnp.max(raw, axis=0, keepdims=True))
    sm = e / jnp.sum(e, axis=0, keepdims=True)
    lb_ref[...] = jnp.sum(sm[0:layer + 1], axis=0, keepdims=True)


def param_tables(lam_vecs, lb_raw, lam_init, layer):
    kw = lb_raw.shape[1]
    return pl.pallas_call(
        functools.partial(_tables_kernel, lam_init=lam_init, layer=layer),
        out_shape=(jax.ShapeDtypeStruct((1, LANES), F32), jax.ShapeDtypeStruct((1, kw), F32)),
        name="param_tables",
    )(lam_vecs, lb_raw)


def _bias_tiles_kernel(rb_ref, bk_ref, o_ref, *, tq, far_bucket):
    h = pl.program_id(0)
    bk = bk_ref[...]
    vec = jnp.full(bk.shape, NEG_BIG, F32)
    for n in range(NUM_BUCKETS):
        vec = jnp.where(bk == n, rb_ref[n, h], vec)
    for t in range(2):
        rows = jnp.broadcast_to(vec[t:t + 1], (tq, 2 * tq))
        o_ref[0, t] = pltpu.roll(rows, 0, 1, stride=1, stride_axis=0)[:, tq:]
    o_ref[0, FAR_TILE] = jnp.full((tq, tq), rb_ref[far_bucket, h], F32)


def bias_tiles(rel_bias, dist_buckets, tq, far_bucket):
    nb, h = rel_bias.shape
    return pl.pallas_call(
        functools.partial(_bias_tiles_kernel, tq=tq, far_bucket=far_bucket),
        grid=(h,),
        in_specs=[
            pl.BlockSpec(memory_space=pltpu.SMEM),
            pl.BlockSpec(dist_buckets.shape, lambda i: (0, 0)),
        ],
        out_specs=pl.BlockSpec((1, FAR_TILE + 1, tq, tq), lambda i: (i, 0, 0, 0)),
        out_shape=jax.ShapeDtypeStruct((h, FAR_TILE + 1, tq, tq), F32),
        compiler_params=_params("arbitrary"),
        name="bias_tiles",
    )(rel_bias, dist_buckets)


ONES_ROWS = 16
FAR_TILE = 2


def _diff_attn_kernel(q_ref, k_ref, v_ref, bias_ref, lam_ref, nw_ref, o_ref,
                      vt_ref, m_ref, acc_ref, st_ref, *, tq, out_scale):
    dv = v_ref.shape[1]
    dh = dv // 2
    nq = v_ref.shape[0] // tq

    for c in range(nq):
        vt_ref[c, 0:dv, :] = v_ref[c * tq:(c + 1) * tq, :].astype(F32).T.astype(BF16)
        vt_ref[c, dv:, :] = jnp.ones((ONES_ROWS, tq), BF16)

    lane = lax.broadcasted_iota(jnp.int32, (1, dv), 1)
    qw = min(DA_Q_SUB, tq)
    chains = [(mp, slice(qs * qw, (qs + 1) * qw)) for mp in range(2) for qs in range(tq // qw)]

    def query_block(i, carry):
        rows = pl.ds(pl.multiple_of(i * tq, tq), tq)
        q = q_ref[rows, :]
        qm = (jnp.where(lane < dh, q, jnp.zeros_like(q)), jnp.where(lane >= dh, q, jnp.zeros_like(q)))

        m_ref[...] = jnp.full(m_ref.shape, NEG_BIG, F32)
        acc_ref[...] = jnp.zeros(acc_ref.shape, F32)

        def scores(n, j):
            mp, cols = chains[n]
            kb = k_ref[pl.ds(pl.multiple_of(j * tq, tq), tq), :]
            st_ref[n] = lax.dot_general(kb, qm[mp][cols, :], (((1,), (1,)), ((), ())),
                                        preferred_element_type=F32)

        def softmax_pv(n, vt, tile):
            mp, cols = chains[n]
            st = st_ref[n] + bias_ref[0, tile, :, cols]
            m_old = m_ref[mp, :, cols]
            m_new = jnp.maximum(m_old, jnp.max(st, axis=0, keepdims=True))
            p = jnp.exp(st - m_new).astype(BF16)
            alpha = jnp.exp(m_old - m_new)
            acc_ref[mp, :, cols] = (alpha * acc_ref[mp, :, cols]
                                    + jnp.dot(vt, p, preferred_element_type=F32))
            m_ref[mp, :, cols] = m_new

        def step(j, c2):
            vt = vt_ref[j]
            tile = jnp.minimum(i - j, FAR_TILE)
            for n in range(len(chains)):
                if n + 1 < len(chains):
                    scores(n + 1, j)
                else:
                    scores(0, jnp.minimum(j + 1, i))
                softmax_pv(n, vt, tile)
            return c2

        scores(0, 0)
        lax.fori_loop(0, i + 1, step, 0)

        o0 = acc_ref[0, 0:dv, :] / acc_ref[0, dv:dv + 1, :]
        o1 = acc_ref[1, 0:dv, :] / acc_ref[1, dv:dv + 1, :]
        o = o0 - lam_ref[0:1, 0:1] * o1
        ms = jnp.mean(o * o, axis=0, keepdims=True)
        y = (o * lax.rsqrt(ms + EPS)) * (nw_ref[...] * out_scale)
        o_ref[rows, :] = y.T.astype(o_ref.dtype)
        return carry

    lax.fori_loop(0, nq, query_block, 0)


def diff_attention(proj, bias_t, lam_row, nw, batch, seq, tq, out_scale, riders=(), layer=0):
    nq = seq // tq
    dv = LANES
    grid = (batch, DA_HEADS)
    rider_in, rider_out, rider_shapes = _rider_specs(riders, layer, grid)
    kern = _with_cast_riders(functools.partial(_diff_attn_kernel, tq=tq, out_scale=out_scale),
                             n_in=6, n_out=1, n_riders=len(riders))
    return pl.pallas_call(
        kern,
        grid=grid,
        in_specs=[
            pl.BlockSpec((seq, dv), lambda b, h: (b, h)),
            pl.BlockSpec((seq, dv), lambda b, h: (b, DA_HEADS + h)),
            pl.BlockSpec((seq, dv), lambda b, h: (b, 2 * DA_HEADS + h)),
            pl.BlockSpec((1, FAR_TILE + 1, tq, tq), lambda b, h: (h, 0, 0, 0)),
            pl.BlockSpec((1, LANES), lambda b, h: (0, 0)),
            pl.BlockSpec((dv, 1), lambda b, h: (0, 0)),
        ] + rider_in,
        out_specs=[pl.BlockSpec((seq, dv), lambda b, h: (b, h))] + rider_out,
        out_shape=[jax.ShapeDtypeStruct((batch * seq, DA_HEADS * dv), BF16)] + rider_shapes,
        scratch_shapes=[
            pltpu.VMEM((nq, dv + ONES_ROWS, tq), BF16),
            pltpu.VMEM((2, 1, tq), F32),
            pltpu.VMEM((2, dv + ONES_ROWS, tq), F32),
            pltpu.VMEM((2 * tq // min(DA_Q_SUB, tq), tq, min(DA_Q_SUB, tq)), F32),
        ],
        compiler_params=_params("parallel", "parallel"),
        name="diff_attention",
    )(proj, proj, proj, bias_t, lam_row, nw.reshape(dv, 1), *riders)


def _exact_bf16_dot(tri3, x):
    hi = x.astype(BF16)
    r1 = x - hi.astype(F32)
    mid = r1.astype(BF16)
    lo = (r1 - mid.astype(F32)).astype(BF16)
    return jnp.dot(tri3, jnp.concatenate([hi, mid, lo], axis=0), preferred_element_type=F32)


def _hgrn_tables(c):
    n_lv = int(math.log2(c))
    r = np.arange(c)[None, :]
    t = np.arange(c)[:, None]
    blocks_d, blocks_e = [], []
    for lv in range(2, HG_SMALL_LEVELS + 1):
        h = 1 << (lv - 1)
        upper = (t // h) % 2 == 1
        mid_t = (t // h) * h
        blocks_d.append(upper & (r >= mid_t) & (r <= t))
        mid_s = (t // h + 1) * h
        blocks_e.append((~upper) & (r > t) & (r < mid_s))
    stack = np.concatenate(blocks_d + blocks_e + [r <= t], axis=0).astype(np.float32)
    x = np.bitwise_xor(t, r)
    level = np.where(r == t, 0, np.floor(np.log2(np.maximum(x, 1))).astype(np.int32) + 1)
    level = np.where(r > t, -1, level).astype(np.int32)
    return stack, level, n_lv


def _hgrn_kernel(q_ref, f_ref, i_ref, g_ref, lb_ref, nw_ref, tst_ref, lvl_ref, o_ref, st_ref,
                 *, ts, c, n_lv):
    nt = (((1,), (1,)), ((), ()))
    d = LANES
    nh = q_ref.shape[1] // d
    nch = ts // c
    n_small = HG_SMALL_LEVELS

    @pl.when(pl.program_id(2) == 0)
    def _():
        st_ref[...] = jnp.zeros(st_ref.shape, F32)

    lb = lb_ref[...]
    f = lb + (1.0 - lb) * jax.nn.sigmoid(f_ref[...])
    logf = jnp.log(f)
    kk = 1.0 - f
    units = [(ch, hd) for ch in range(nch) for hd in range(nh)]
    lcat = jnp.concatenate([logf[ch * c:(ch + 1) * c, hd * d:(hd + 1) * d] for ch, hd in units], axis=1)
    seg = _exact_bf16_dot(tst_ref[...], lcat)
    lvl = lvl_ref[...]
    nw = nw_ref[...]

    st = [st_ref[hd] for hd in range(nh)]
    for u, (ch, hd) in enumerate(units):
        rows = slice(ch * c, (ch + 1) * c)
        cols = slice(hd * d, (hd + 1) * d)

        def blk(b, u=u):
            return seg[b * c:(b + 1) * c, u * d:(u + 1) * d]

        g_in = blk(2 * n_small - 2)
        g_last = g_in[c - 1:c]
        g_out = g_last - g_in
        q = q_ref[rows, cols]
        q32 = q.astype(F32)
        kc = kk[rows, cols]
        kb = kc.astype(BF16)
        vb = i_ref[rows, cols].astype(BF16)
        a = jnp.where(lvl == 0, lax.dot_general(q, kb, nt, preferred_element_type=F32), 0.0)
        qt = (q32 * f[rows, cols]).astype(BF16)
        a = jnp.where(lvl == 1, lax.dot_general(qt, kb, nt, preferred_element_type=F32), a)
        for lv in range(2, n_lv + 1):
            if lv <= n_small:
                d_up = blk(lv - 2)
                d_lo = blk(n_small + lv - 3)
            else:
                h = 1 << (lv - 1)
                ref = jnp.concatenate(
                    [jnp.broadcast_to(g_in[gi * 2 * h + h - 1:gi * 2 * h + h, :], (2 * h, d))
                     for gi in range(c // (2 * h))], axis=0)
                d_up = jnp.minimum(g_in - ref, 0.0)
                d_lo = jnp.minimum(ref - g_in, 0.0)
            qt = (q32 * jnp.exp(d_up)).astype(BF16)
            kt = (kc * jnp.exp(d_lo)).astype(BF16)
            a = jnp.where(lvl == lv, lax.dot_general(qt, kt, nt, preferred_element_type=F32), a)
        qg = (q32 * jnp.exp(g_in)).astype(BF16)
        o = (jnp.dot(a.astype(BF16), vb, preferred_element_type=F32)
             + lax.dot_general(qg, st[hd].astype(BF16), nt, preferred_element_type=F32))
        kd = (kc * jnp.exp(g_out)).astype(BF16)
        upd = lax.dot_general(vb, kd, (((0,), (0,)), ((), ())), preferred_element_type=F32)
        st[hd] = st[hd] * jnp.exp(g_last) + upd
        g = g_ref[rows, cols]
        o_ref[rows, cols] = (_rms_rows(o, nw) * (g * jax.nn.sigmoid(g))).astype(o_ref.dtype)
    for hd in range(nh):
        st_ref[hd] = st[hd]


def hgrn2(proj_a, proj_b, lb, nw, batch, seq, ts, q_col0, riders=(), layer=0):
    ns = seq // ts
    d = LANES
    nh = math.gcd(HG_HEADS_PER_STEP, HG_HEADS)
    assert q_col0 % nh == 0
    w = nh * d
    ng = HG_HEADS // nh
    c = min(HG_CHUNK, ts)
    stack, level, n_lv = _hgrn_tables(c)
    assert n_lv >= HG_SMALL_LEVELS
    grid = (batch, ng, ns)
    rider_in, rider_out, rider_shapes = _rider_specs(riders, layer, grid)
    kern = _with_cast_riders(functools.partial(_hgrn_kernel, ts=ts, c=c, n_lv=n_lv),
                             n_in=8, n_out=1, n_riders=len(riders))
    return pl.pallas_call(
        kern,
        grid=grid,
        in_specs=[
            pl.BlockSpec((ts, w), lambda b, h, i: (b * ns + i, q_col0 // nh + h)),
            pl.BlockSpec((ts, w), lambda b, h, i: (b * ns + i, h)),
            pl.BlockSpec((ts, w), lambda b, h, i: (b * ns + i, ng + h)),
            pl.BlockSpec((ts, w), lambda b, h, i: (b * ns + i, 2 * ng + h)),
            pl.BlockSpec((1, w), lambda b, h, i: (0, h)),
            pl.BlockSpec((1, d), lambda b, h, i: (0, 0)),
            pl.BlockSpec((stack.shape[0], 3 * stack.shape[1]), lambda b, h, i: (0, 0)),
            pl.BlockSpec(level.shape, lambda b, h, i: (0, 0)),
        ] + rider_in,
        out_specs=[pl.BlockSpec((ts, w), lambda b, h, i: (b * ns + i, h))] + rider_out,
        out_shape=[jax.ShapeDtypeStruct((batch * seq, HG_HEADS * d), BF16)] + rider_shapes,
        scratch_shapes=[pltpu.VMEM((nh, d, d), F32)],
        compiler_params=_params("parallel", "parallel", "arbitrary"),
        name="hgrn2",
    )(proj_a, proj_b, proj_b, proj_b, lb, nw.reshape(1, d),
      jnp.asarray(np.concatenate([stack] * 3, axis=1), BF16), jnp.asarray(level), *riders)


def _ffn_up_kernel(x_ref, halo_ref, nw_ref, wa_ref, wb_ref, cwa_ref, cwb_ref, cba_ref, cbb_ref,
                   o_ref, h_ref, u_ref, *, tm, per_batch):
    hl = SUBLANES

    @pl.when(pl.program_id(1) == 0)
    def _():
        nw = nw_ref[...]
        first = (pl.program_id(0) % per_batch) == 0
        halo = _rms_rows(halo_ref[...], nw)
        h_ref[0:hl, :] = jnp.where(first, 0.0, halo).astype(h_ref.dtype)
        h_ref[hl:, :] = _rms_rows(x_ref[...], nw).astype(h_ref.dtype)

    h = h_ref[...]
    tf = o_ref.shape[1]
    cw = min(FFN_COL_CHUNK, tf)

    def cols(cc):
        return slice(cc * cw, (cc + 1) * cw)

    def matmuls(cc):
        u_ref[cc, 0] = jnp.dot(h, wa_ref[:, cols(cc)], preferred_element_type=F32)
        u_ref[cc, 1] = jnp.dot(h, wb_ref[:, cols(cc)], preferred_element_type=F32)

    def conv(cc, half, cw_ref, cb_ref):
        acc = cb_ref[:, cols(cc)][None]
        for j in range(CONV_WIDTH):
            off = hl - (CONV_WIDTH - 1) + j
            rows = u_ref[cc, half, off:off + tm, :].reshape(tm // hl, hl, cw)
            acc = acc + cw_ref[j, :, cols(cc)][None] * rows
        return acc.reshape(tm, cw)

    def gate(cc):
        a = conv(cc, 0, cwa_ref, cba_ref)
        b = conv(cc, 1, cwb_ref, cbb_ref)
        o_ref[:, cols(cc)] = ((a * jax.nn.sigmoid(a)) * b).astype(o_ref.dtype)

    for cc in range(tf // cw):
        matmuls(cc)
        gate(cc)


def ffn_up(x, nw, w_up, conv_w, conv_b, seq):
    t, d = x.shape
    f = w_up.shape[1] // 2
    tm, tf = _pick(seq, MM_TM), _pick(f, FFN_TF)
    nf = f // tf
    per_batch = seq // tm
    hl = SUBLANES
    rows_per_tile = tm // hl
    cw = min(FFN_COL_CHUNK, tf)
    conv_w8 = jnp.broadcast_to(conv_w[:, None, :], (CONV_WIDTH, hl, 2 * f))
    conv_b8 = jnp.broadcast_to(conv_b.reshape(1, 2 * f), (hl, 2 * f))
    kern = functools.partial(_ffn_up_kernel, tm=tm, per_batch=per_batch)
    return pl.pallas_call(
        kern,
        grid=(t // tm, nf),
        in_specs=[
            pl.BlockSpec((tm, d), lambda i, j: (i, 0)),
            pl.BlockSpec((hl, d), lambda i, j: (jnp.maximum(i * rows_per_tile - 1, 0), 0)),
            pl.BlockSpec((1, d), lambda i, j: (0, 0)),
            pl.BlockSpec((d, tf), lambda i, j: (0, j)),
            pl.BlockSpec((d, tf), lambda i, j: (0, nf + j)),
            pl.BlockSpec((CONV_WIDTH, hl, tf), lambda i, j: (0, 0, j)),
            pl.BlockSpec((CONV_WIDTH, hl, tf), lambda i, j: (0, 0, nf + j)),
            pl.BlockSpec((hl, tf), lambda i, j: (0, j)),
            pl.BlockSpec((hl, tf), lambda i, j: (0, nf + j)),
        ],
        out_specs=pl.BlockSpec((tm, tf), lambda i, j: (i, j)),
        out_shape=jax.ShapeDtypeStruct((t, f), BF16),
        scratch_shapes=[
            pltpu.VMEM((tm + hl, d), BF16),
            pltpu.VMEM((tf // cw, 2, tm + hl, cw), F32),
        ],
        compiler_params=_params("parallel", "arbitrary"),
        name="ffn_up",
    )(x, x, nw.reshape(1, d), w_up, w_up, conv_w8, conv_w8, conv_b8, conv_b8)


def _ffn_down_kernel(a_ref, w_ref, res_ref, nw_ref, o_ref):
    y = res_ref[...] + jnp.dot(a_ref[...], w_ref[...], preferred_element_type=F32)
    o_ref[...] = _rms_rows(y, nw_ref[...])


def ffn_down(act, w_down, res, nw):
    t, f = act.shape
    d = w_down.shape[1]
    tm = _pick(t, FFN_DOWN_TM)
    return pl.pallas_call(
        _ffn_down_kernel,
        grid=(t // tm,),
        in_specs=[
            pl.BlockSpec((tm, f), lambda i: (i, 0)),
            pl.BlockSpec((f, d), lambda i: (0, 0), pipeline_mode=pl.Buffered(1)),
            pl.BlockSpec((tm, d), lambda i: (i, 0)),
            pl.BlockSpec((1, d), lambda i: (0, 0)),
        ],
        out_specs=pl.BlockSpec((tm, d), lambda i: (i, 0)),
        out_shape=jax.ShapeDtypeStruct((t, d), F32),
        compiler_params=_params("parallel"),
        name="ffn_down",
    )(act, w_down, res, nw.reshape(1, d))


def kernel(x, mem, w_in, w_out, norm_mix_w, lam_q1, lam_k1, lam_q2, lam_k2, da_subln_w,
           hg_lb_raw, hg_norm_w, rel_bias, norm_mem_w, mem_kv_norm_w, w_mq, w_mkv, w_mo,
           norm_ffn_w, w_up, conv_w, conv_b, w_down, final_norm_w):
    batch, seq, d_model = x.shape
    n_mem = mem.shape[1]
    depth = w_in.shape[0]
    assert depth == 1, "single-layer block"
    layer = 0
    t = batch * seq

    da_width = DA_HEADS * LANES
    hg_width = HG_HEADS * LANES
    assert w_in.shape[2] == 3 * da_width + 4 * hg_width
    assert da_subln_w.shape[1] == LANES and hg_norm_w.shape[1] == LANES
    assert lam_q1.shape[1] * 2 == LANES
    n_bf = 3 * da_width + hg_width

    tq = _pick(seq, DA_BLOCK)
    ts = _pick(seq, HG_TILE)

    lam_init = 0.8 - 0.6 * math.exp(-0.3 * layer)
    lam_vecs = jnp.stack([lam_q1[layer], lam_k1[layer], lam_q2[layer], lam_k2[layer]]).astype(F32)
    lam_row, lb = param_tables(lam_vecs, hg_lb_raw.astype(F32), lam_init, layer)

    far = _t5_bucket_np(np.arange(tq + 1, max(seq, tq + 2)))
    assert (far == far[0]).all(), "bias must be constant beyond the sub-diagonal block"
    c = np.arange(2 * tq)
    dist_buckets = np.stack([np.where(c >= tq, _t5_bucket_np(np.maximum(c - tq, 0)), -1),
                             _t5_bucket_np(c)]).astype(np.int32)
    bias_t = bias_tiles(rel_bias.astype(F32), jnp.asarray(dist_buckets), tq, int(far[0]))

    x2d = x.reshape(t, d_model)
    q_scale = (LANES // 2) ** -0.5
    col_scale = np.ones((1, n_bf), np.float32)
    col_scale[:, :da_width] = q_scale
    proj_a, proj_b = in_proj(x2d, norm_mix_w[layer], w_in[layer].astype(BF16), n_bf, jnp.asarray(col_scale))

    da_o, w_up_bf, w_mkv_bf = diff_attention(
        proj_a, bias_t, lam_row, da_subln_w[layer], batch, seq, tq, 1.0 - lam_init,
        riders=(w_up, w_mkv), layer=layer)
    hg_o, w_down_bf, w_out_bf, w_mq_bf, w_mo_bf = hgrn2(
        proj_a, proj_b, lb, hg_norm_w[layer], batch, seq, ts, 3 * DA_HEADS,
        riders=(w_down, w_out, w_mq, w_mo), layer=layer)

    mkv = norm_matmul(mem.reshape(batch * n_mem, d_model), mem_kv_norm_w[layer], w_mkv_bf, BF16,
                      name="mem_kv_proj")
    x2 = mix_out_mem_attention(da_o, hg_o, w_out_bf, x2d, norm_mem_w[layer], w_mq_bf, mkv, w_mo_bf,
                               seq, n_mem)

    act = ffn_up(x2, norm_ffn_w[layer], w_up_bf, conv_w[layer], conv_b[layer], seq)
    out = ffn_down(act, w_down_bf, x2, final_norm_w)
    return out.reshape(batch, seq, d_model)
```

```python
import functools
import math

import numpy as np
import jax
import jax.numpy as jnp
from jax import lax
from jax.experimental import pallas as pl
from jax.experimental.pallas import tpu as pltpu

DA_HEADS = 8
HG_HEADS = 8
MEM_HEADS = 4
NUM_BUCKETS = 32
MAX_DISTANCE = 128
CONV_WIDTH = 3
EPS = 1e-6

LANES = 128
SUBLANES = 8
VMEM_LIMIT_BYTES = 56 * 1024 * 1024

HG_CHUNK = 128
HG_TILE = 512
HG_HEADS_PER_STEP = 4
HG_SMALL_LEVELS = 3
DA_BLOCK = 512
DA_Q_SUB = 512
MM_TM = 1024
MM_TN = 1024
FFN_TF = 512
FFN_COL_CHUNK = 256
RESIDENT_TM = 512
FFN_DOWN_TM = RESIDENT_TM
NEG_BIG = -1e30

F32 = jnp.float32
BF16 = jnp.bfloat16


def _params(*sem, flags=None):
    return pltpu.CompilerParams(dimension_semantics=sem, vmem_limit_bytes=VMEM_LIMIT_BYTES, flags=flags)


def _rms_rows(x, w):
    ms = jnp.mean(x * x, axis=-1, keepdims=True)
    return (x * lax.rsqrt(ms + EPS)) * w


def _pick(n, pref):
    t = min(pref, n)
    while n % t:
        t //= 2
    return t


def _norm_mm_kernel(x_ref, nw_ref, w_ref, o_ref, h_ref):
    @pl.when(pl.program_id(1) == 0)
    def _():
        h_ref[...] = _rms_rows(x_ref[...], nw_ref[...]).astype(h_ref.dtype)

    o_ref[...] = jnp.dot(h_ref[...], w_ref[...], preferred_element_type=F32).astype(o_ref.dtype)


def norm_matmul(x, nw, w, out_dtype, name="norm_matmul"):
    t, k = x.shape
    n = w.shape[1]
    tm, tn = _pick(t, MM_TM), _pick(n, MM_TN)
    return pl.pallas_call(
        _norm_mm_kernel,
        grid=(t // tm, n // tn),
        in_specs=[
            pl.BlockSpec((tm, k), lambda i, j: (i, 0)),
            pl.BlockSpec((1, k), lambda i, j: (0, 0)),
            pl.BlockSpec((k, tn), lambda i, j: (0, j)),
        ],
        out_specs=pl.BlockSpec((tm, tn), lambda i, j: (i, j)),
        out_shape=jax.ShapeDtypeStruct((t, n), out_dtype),
        scratch_shapes=[pltpu.VMEM((tm, k), BF16)],
        compiler_params=_params("parallel", "arbitrary"),
        name=name,
    )(x, nw.reshape(1, k), w)


BF16_ROWS = 16


def _with_cast_riders(body, n_in, n_out, n_riders):
    def kern(*refs):
        ins, rest = refs[:n_in], refs[n_in:]
        rid_in, rest = rest[:n_riders], rest[n_riders:]
        outs, rest = rest[:n_out], rest[n_out:]
        rid_out, scratch = rest[:n_riders], rest[n_riders:]
        for src, dst in zip(rid_in, rid_out):
            dst[...] = src[...].astype(dst.dtype)
        body(*ins, *outs, *scratch)
    return kern


def _rider_specs(riders, layer, grid):
    n_steps = math.prod(grid)

    def step(*g):
        idx = 0
        for gi, n in zip(g, grid):
            idx = idx * n + gi
        return idx

    in_specs, out_specs, shapes = [], [], []
    for a in riders:
        _, rows, cols = a.shape
        n_slabs = math.gcd(n_steps, rows // BF16_ROWS)
        assert rows % BF16_ROWS == 0 and n_steps % n_slabs == 0, (a.shape, grid)
        hold = n_steps // n_slabs
        in_specs.append(pl.BlockSpec((None, rows // n_slabs, cols),
                                     lambda *g, hold=hold: (layer, step(*g) // hold, 0)))
        out_specs.append(pl.BlockSpec((rows // n_slabs, cols),
                                      lambda *g, hold=hold: (step(*g) // hold, 0)))
        shapes.append(jax.ShapeDtypeStruct((rows, cols), BF16))
    return in_specs, out_specs, shapes


def _in_proj_kernel(x_ref, nw_ref, w_ref, cs_ref, oa_ref, ob_ref, h_ref, *, na_tiles):
    j = pl.program_id(1)

    @pl.when(j == 0)
    def _():
        h_ref[...] = _rms_rows(x_ref[...], nw_ref[...]).astype(h_ref.dtype)

    y = jnp.dot(h_ref[...], w_ref[...], preferred_element_type=F32)

    @pl.when(j < na_tiles)
    def _():
        oa_ref[...] = (y * cs_ref[...]).astype(oa_ref.dtype)

    @pl.when(j >= na_tiles)
    def _():
        ob_ref[...] = y


def in_proj(x, nw, w, n_a, col_scale):
    t, k = x.shape
    n = w.shape[1]
    tm, tn = _pick(t, MM_TM), math.gcd(_pick(n_a, MM_TN), _pick(n - n_a, MM_TN))
    na_tiles = n_a // tn
    return pl.pallas_call(
        functools.partial(_in_proj_kernel, na_tiles=na_tiles),
        grid=(t // tm, n // tn),
        in_specs=[
            pl.BlockSpec((tm, k), lambda i, j: (i, 0)),
            pl.BlockSpec((1, k), lambda i, j: (0, 0)),
            pl.BlockSpec((k, tn), lambda i, j: (0, j)),
            pl.BlockSpec((1, tn), lambda i, j: (0, jnp.minimum(j, na_tiles - 1))),
        ],
        out_specs=[
            pl.BlockSpec((tm, tn), lambda i, j: (i, jnp.minimum(j, na_tiles - 1))),
            pl.BlockSpec((tm, tn), lambda i, j: (i, jnp.maximum(j - na_tiles, 0))),
        ],
        out_shape=(jax.ShapeDtypeStruct((t, n_a), BF16), jax.ShapeDtypeStruct((t, n - n_a), F32)),
        scratch_shapes=[pltpu.VMEM((tm, k), BF16)],
        compiler_params=_params("parallel", "arbitrary"),
        name="in_proj",
    )(x, nw.reshape(1, k), w, col_scale)


def _mix_mem_kernel(a0_ref, a1_ref, w0_ref, w1_ref, x_ref, nw_ref, wq_ref, k_ref, v_ref, wo_ref,
                    o_ref, att_ref, *, scale):
    x1 = x_ref[...]
    x1 = x1 + jnp.dot(a0_ref[...], w0_ref[...], preferred_element_type=F32)
    x1 = x1 + jnp.dot(a1_ref[...], w1_ref[...], preferred_element_type=F32)
    hq = _rms_rows(x1, nw_ref[...]).astype(BF16)
    mq = jnp.dot(hq, wq_ref[...], preferred_element_type=F32).astype(BF16)
    dh = mq.shape[1] // MEM_HEADS
    for hd in range(MEM_HEADS):
        sl = slice(hd * dh, (hd + 1) * dh)
        s = lax.dot_general(mq[:, sl], k_ref[:, sl], (((1,), (1,)), ((), ())),
                            preferred_element_type=F32) * scale
        e = jnp.exp(s - jnp.max(s, axis=-1, keepdims=True))
        p = e / jnp.sum(e, axis=-1, keepdims=True)
        att_ref[:, sl] = jnp.dot(p.astype(BF16), v_ref[:, sl],
                                 preferred_element_type=F32).astype(att_ref.dtype)
    o_ref[...] = x1 + jnp.dot(att_ref[...], wo_ref[...], preferred_element_type=F32)


def mix_out_mem_attention(a0, a1, w_out, x, nw_mem, w_mq, mkv, w_mo, seq, n_mem):
    t, k0 = a0.shape
    k1 = a1.shape[1]
    d = x.shape[1]
    assert k0 == k1 and w_out.shape == (k0 + k1, d)
    tm = _pick(seq, RESIDENT_TM)
    per_batch = seq // tm
    scale = (d // MEM_HEADS) ** -0.5
    once = pl.Buffered(1)
    return pl.pallas_call(
        functools.partial(_mix_mem_kernel, scale=scale),
        grid=(t // tm,),
        in_specs=[
            pl.BlockSpec((tm, k0), lambda i: (i, 0)),
            pl.BlockSpec((tm, k1), lambda i: (i, 0)),
            pl.BlockSpec((k0, d), lambda i: (0, 0), pipeline_mode=once),
            pl.BlockSpec((k1, d), lambda i: (1, 0), pipeline_mode=once),
            pl.BlockSpec((tm, d), lambda i: (i, 0)),
            pl.BlockSpec((1, d), lambda i: (0, 0)),
            pl.BlockSpec((d, d), lambda i: (0, 0), pipeline_mode=once),
            pl.BlockSpec((n_mem, d), lambda i: (i // per_batch, 0), pipeline_mode=once),
            pl.BlockSpec((n_mem, d), lambda i: (i // per_batch, 1), pipeline_mode=once),
            pl.BlockSpec((d, d), lambda i: (0, 0), pipeline_mode=once),
        ],
        out_specs=pl.BlockSpec((tm, d), lambda i: (i, 0)),
        out_shape=jax.ShapeDtypeStruct((t, d), F32),
        scratch_shapes=[pltpu.VMEM((tm, d), BF16)],
        compiler_params=_params("parallel"),
        name="mix_out_mem_attention",
    )(a0, a1, w_out, w_out, x, nw_mem.reshape(1, d), w_mq, mkv, mkv, w_mo)


def _t5_bucket_np(n):
    max_exact = NUM_BUCKETS // 2
    nf = np.maximum(n, 1).astype(np.float32)
    large = max_exact + (np.log(nf / max_exact) / math.log(MAX_DISTANCE / max_exact)
                         * (NUM_BUCKETS - max_exact)).astype(np.int32)
    large = np.minimum(large, NUM_BUCKETS - 1)
    return np.where(n < max_exact, n, large).astype(np.int32)


def _tables_kernel(lamv_ref, lbraw_ref, lam_ref, lb_ref, *, lam_init, layer):
    v = lamv_ref[...]
    s1 = jnp.sum(v[0:1] * v[1:2], axis=-1, keepdims=True)
    s2 = jnp.sum(v[2:3] * v[3:4], axis=-1, keepdims=True)
    lam = jnp.exp(s1) - jnp.exp(s2) + lam_init
    lam_ref[...] = jnp.broadcast_to(lam, lam_ref.shape)
    raw = lbraw_ref[...]
    e = jnp.exp(raw - jnp.max(raw, axis=0, keepdims=True))
    sm = e / jnp.sum(e, axis=0, keepdims=True)
    lb_ref[...] = jnp.sum(sm[0:layer + 1], axis=0, keepdims=True)


def param_tables(lam_vecs, lb_raw, lam_init, layer):
    kw = lb_raw.shape[1]
    return pl.pallas_call(
        functools.partial(_tables_kernel, lam_init=lam_init, layer=layer),
        out_shape=(jax.ShapeDtypeStruct((1, LANES), F32), jax.ShapeDtypeStruct((1, kw), F32)),
        name="param_tables",
    )(lam_vecs, lb_raw)


def _bias_tiles_kernel(rb_ref, bk_ref, o_ref, *, tq, far_bucket):
    h = pl.program_id(0)
    bk = bk_ref[...]
    vec = jnp.full(bk.shape, NEG_BIG, F32)
    for n in range(NUM_BUCKETS):
        vec = jnp.where(bk == n, rb_ref[n, h], vec)
    for t in range(2):
        rows = jnp.broadcast_to(vec[t:t + 1], (tq, 2 * tq))
        o_ref[0, t] = pltpu.roll(rows, 0, 1, stride=1, stride_axis=0)[:, tq:]
    o_ref[0, FAR_TILE] = jnp.full((tq, tq), rb_ref[far_bucket, h], F32)


def bias_tiles(rel_bias, dist_buckets, tq, far_bucket):
    nb, h = rel_bias.shape
    return pl.pallas_call(
        functools.partial(_bias_tiles_kernel, tq=tq, far_bucket=far_bucket),
        grid=(h,),
        in_specs=[
            pl.BlockSpec(memory_space=pltpu.SMEM),
            pl.BlockSpec(dist_buckets.shape, lambda i: (0, 0)),
        ],
        out_specs=pl.BlockSpec((1, FAR_TILE + 1, tq, tq), lambda i: (i, 0, 0, 0)),
        out_shape=jax.ShapeDtypeStruct((h, FAR_TILE + 1, tq, tq), F32),
        compiler_params=_params("arbitrary"),
        name="bias_tiles",
    )(rel_bias, dist_buckets)


ONES_ROWS = 16
FAR_TILE = 2


def _diff_attn_kernel(q_ref, k_ref, v_ref, bias_ref, lam_ref, nw_ref, o_ref,
                      vt_ref, m_ref, acc_ref, st_ref, *, tq, out_scale):
    dv = v_ref.shape[1]
    dh = dv // 2
    nq = v_ref.shape[0] // tq

    for c in range(nq):
        vt_ref[c, 0:dv, :] = v_ref[c * tq:(c + 1) * tq, :].astype(F32).T.astype(BF16)
        vt_ref[c, dv:, :] = jnp.ones((ONES_ROWS, tq), BF16)

    lane = lax.broadcasted_iota(jnp.int32, (1, dv), 1)
    qw = min(DA_Q_SUB, tq)
    chains = [(mp, slice(qs * qw, (qs + 1) * qw)) for mp in range(2) for qs in range(tq // qw)]

    def query_block(i, carry):
        rows = pl.ds(pl.multiple_of(i * tq, tq), tq)
        q = q_ref[rows, :]
        qm = (jnp.where(lane < dh, q, jnp.zeros_like(q)), jnp.where(lane >= dh, q, jnp.zeros_like(q)))

        m_ref[...] = jnp.full(m_ref.shape, NEG_BIG, F32)
        acc_ref[...] = jnp.zeros(acc_ref.shape, F32)

        def scores(n, j):
            mp, cols = chains[n]
            kb = k_ref[pl.ds(pl.multiple_of(j * tq, tq), tq), :]
            st_ref[n] = lax.dot_general(kb, qm[mp][cols, :], (((1,), (1,)), ((), ())),
                                        preferred_element_type=F32)

        def softmax_pv(n, vt, tile):
            mp, cols = chains[n]
            m_old = m_ref[mp, :, cols]
            m_new = jnp.maximum(
                m_old, jnp.max(st_ref[n] + bias_ref[0, tile, :, cols], axis=0, keepdims=True))
            m_ref[mp, :, cols] = m_new
            p = jnp.exp(st_ref[n] + bias_ref[0, tile, :, cols] - m_new).astype(BF16)
            alpha = jnp.exp(m_old - m_new)
            acc_ref[mp, :, cols] = (alpha * acc_ref[mp, :, cols]
                                    + jnp.dot(vt, p, preferred_element_type=F32))

        def step(j, c2):
            vt = vt_ref[j]
            tile = jnp.minimum(i - j, FAR_TILE)
            for n in range(len(chains)):
                if n + 1 < len(chains):
                    scores(n + 1, j)
                else:
                    scores(0, jnp.minimum(j + 1, i))
                softmax_pv(n, vt, tile)
            return c2

        scores(0, 0)
        lax.fori_loop(0, i + 1, step, 0)

        o0 = acc_ref[0, 0:dv, :] / acc_ref[0, dv:dv + 1, :]
        o1 = acc_ref[1, 0:dv, :] / acc_ref[1, dv:dv + 1, :]
        o = o0 - lam_ref[0:1, 0:1] * o1
        ms = jnp.mean(o * o, axis=0, keepdims=True)
        y = (o * lax.rsqrt(ms + EPS)) * (nw_ref[...] * out_scale)
        o_ref[rows, :] = y.T.astype(o_ref.dtype)
        return carry

    lax.fori_loop(0, nq, query_block, 0)


def diff_attention(proj, bias_t, lam_row, nw, batch, seq, tq, out_scale, riders=(), layer=0):
    nq = seq // tq
    dv = LANES
    grid = (batch, DA_HEADS)
    rider_in, rider_out, rider_shapes = _rider_specs(riders, layer, grid)
    kern = _with_cast_riders(functools.partial(_diff_attn_kernel, tq=tq, out_scale=out_scale),
                             n_in=6, n_out=1, n_riders=len(riders))
    return pl.pallas_call(
        kern,
        grid=grid,
        in_specs=[
            pl.BlockSpec((seq, dv), lambda b, h: (b, h)),
            pl.BlockSpec((seq, dv), lambda b, h: (b, DA_HEADS + h)),
            pl.BlockSpec((seq, dv), lambda b, h: (b, 2 * DA_HEADS + h)),
            pl.BlockSpec((1, FAR_TILE + 1, tq, tq), lambda b, h: (h, 0, 0, 0)),
            pl.BlockSpec((1, LANES), lambda b, h: (0, 0)),
            pl.BlockSpec((dv, 1), lambda b, h: (0, 0)),
        ] + rider_in,
        out_specs=[pl.BlockSpec((seq, dv), lambda b, h: (b, h))] + rider_out,
        out_shape=[jax.ShapeDtypeStruct((batch * seq, DA_HEADS * dv), BF16)] + rider_shapes,
        scratch_shapes=[
            pltpu.VMEM((nq, dv + ONES_ROWS, tq), BF16),
            pltpu.VMEM((2, 1, tq), F32),
            pltpu.VMEM((2, dv + ONES_ROWS, tq), F32),
            pltpu.VMEM((2 * tq // min(DA_Q_SUB, tq), tq, min(DA_Q_SUB, tq)), F32),
        ],
        compiler_params=_params("parallel", "parallel"),
        name="diff_attention",
    )(proj, proj, proj, bias_t, lam_row, nw.reshape(dv, 1), *riders)


def _exact_bf16_dot(tri3, x):
    hi = x.astype(BF16)
    r1 = x - hi.astype(F32)
    mid = r1.astype(BF16)
    lo = (r1 - mid.astype(F32)).astype(BF16)
    return jnp.dot(tri3, jnp.concatenate([hi, mid, lo], axis=0), preferred_element_type=F32)


def _hgrn_tables(c):
    n_lv = int(math.log2(c))
    r = np.arange(c)[None, :]
    t = np.arange(c)[:, None]
    blocks_d, blocks_e = [], []
    for lv in range(2, HG_SMALL_LEVELS + 1):
        h = 1 << (lv - 1)
        upper = (t // h) % 2 == 1
        mid_t = (t // h) * h
        blocks_d.append(upper & (r >= mid_t) & (r <= t))
        mid_s = (t // h + 1) * h
        blocks_e.append((~upper) & (r > t) & (r < mid_s))
    stack = np.concatenate(blocks_d + blocks_e + [r <= t], axis=0).astype(np.float32)
    x = np.bitwise_xor(t, r)
    level = np.where(r == t, 0, np.floor(np.log2(np.maximum(x, 1))).astype(np.int32) + 1)
    level = np.where(r > t, -1, level).astype(np.int32)
    return stack, level, n_lv


def _hgrn_kernel(q_ref, f_ref, i_ref, g_ref, lb_ref, nw_ref, tst_ref, lvl_ref, o_ref, st_ref,
                 *, ts, c, n_lv):
    nt = (((1,), (1,)), ((), ()))
    d = LANES
    nh = q_ref.shape[1] // d
    nch = ts // c
    n_small = HG_SMALL_LEVELS

    @pl.when(pl.program_id(2) == 0)
    def _():
        st_ref[...] = jnp.zeros(st_ref.shape, F32)

    lb = lb_ref[...]
    f = lb + (1.0 - lb) * jax.nn.sigmoid(f_ref[...])
    logf = jnp.log(f)
    kk = 1.0 - f
    units = [(ch, hd) for ch in range(nch) for hd in range(nh)]
    lcat = jnp.concatenate([logf[ch * c:(ch + 1) * c, hd * d:(hd + 1) * d] for ch, hd in units], axis=1)
    seg = _exact_bf16_dot(tst_ref[...], lcat)
    lvl = lvl_ref[...]
    nw = nw_ref[...]

    st = [st_ref[hd] for hd in range(nh)]
    for u, (ch, hd) in enumerate(units):
        rows = slice(ch * c, (ch + 1) * c)
        cols = slice(hd * d, (hd + 1) * d)

        def blk(b, u=u):
            return seg[b * c:(b + 1) * c, u * d:(u + 1) * d]

        g_in = blk(2 * n_small - 2)
        g_last = g_in[c - 1:c]
        g_out = g_last - g_in
        q = q_ref[rows, cols]
        q32 = q.astype(F32)
        kc = kk[rows, cols]
        kb = kc.astype(BF16)
        vb = i_ref[rows, cols].astype(BF16)
        a = jnp.where(lvl == 0, lax.dot_general(q, kb, nt, preferred_element_type=F32), 0.0)
        qt = (q32 * f[rows, cols]).astype(BF16)
        a = jnp.where(lvl == 1, lax.dot_general(qt, kb, nt, preferred_element_type=F32), a)
        for lv in range(2, n_lv + 1):
            if lv <= n_small:
                d_up = blk(lv - 2)
                d_lo = blk(n_small + lv - 3)
            else:
                h = 1 << (lv - 1)
                ref = jnp.concatenate(
                    [jnp.broadcast_to(g_in[gi * 2 * h + h - 1:gi * 2 * h + h, :], (2 * h, d))
                     for gi in range(c // (2 * h))], axis=0)
                d_up = jnp.minimum(g_in - ref, 0.0)
                d_lo = jnp.minimum(ref - g_in, 0.0)
            qt = (q32 * jnp.exp(d_up)).astype(BF16)
            kt = (kc * jnp.exp(d_lo)).astype(BF16)
            a = jnp.where(lvl == lv, lax.dot_general(qt, kt, nt, preferred_element_type=F32), a)
        qg = (q32 * jnp.exp(g_in)).astype(BF16)
        o = (jnp.dot(a.astype(BF16), vb, preferred_element_type=F32)
             + lax.dot_general(qg, st[hd].astype(BF16), nt, preferred_element_type=F32))
        kd = (kc * jnp.exp(g_out)).astype(BF16)
        upd = lax.dot_general(vb, kd, (((0,), (0,)), ((), ())), preferred_element_type=F32)
        st[hd] = st[hd] * jnp.exp(g_last) + upd
        g = g_ref[rows, cols]
        o_ref[rows, cols] = (_rms_rows(o, nw) * (g * jax.nn.sigmoid(g))).astype(o_ref.dtype)
    for hd in range(nh):
        st_ref[hd] = st[hd]


def hgrn2(proj_a, proj_b, lb, nw, batch, seq, ts, q_col0, riders=(), layer=0):
    ns = seq // ts
    d = LANES
    nh = math.gcd(HG_HEADS_PER_STEP, HG_HEADS)
    assert q_col0 % nh == 0
    w = nh * d
    ng = HG_HEADS // nh
    c = min(HG_CHUNK, ts)
    stack, level, n_lv = _hgrn_tables(c)
    assert n_lv >= HG_SMALL_LEVELS
    grid = (batch, ng, ns)
    rider_in, rider_out, rider_shapes = _rider_specs(riders, layer, grid)
    kern = _with_cast_riders(functools.partial(_hgrn_kernel, ts=ts, c=c, n_lv=n_lv),
                             n_in=8, n_out=1, n_riders=len(riders))
    return pl.pallas_call(
        kern,
        grid=grid,
        in_specs=[
            pl.BlockSpec((ts, w), lambda b, h, i: (b * ns + i, q_col0 // nh + h)),
            pl.BlockSpec((ts, w), lambda b, h, i: (b * ns + i, h)),
            pl.BlockSpec((ts, w), lambda b, h, i: (b * ns + i, ng + h)),
            pl.BlockSpec((ts, w), lambda b, h, i: (b * ns + i, 2 * ng + h)),
            pl.BlockSpec((1, w), lambda b, h, i: (0, h)),
            pl.BlockSpec((1, d), lambda b, h, i: (0, 0)),
            pl.BlockSpec((stack.shape[0], 3 * stack.shape[1]), lambda b, h, i: (0, 0)),
            pl.BlockSpec(level.shape, lambda b, h, i: (0, 0)),
        ] + rider_in,
        out_specs=[pl.BlockSpec((ts, w), lambda b, h, i: (b * ns + i, h))] + rider_out,
        out_shape=[jax.ShapeDtypeStruct((batch * seq, HG_HEADS * d), BF16)] + rider_shapes,
        scratch_shapes=[pltpu.VMEM((nh, d, d), F32)],
        compiler_params=_params("parallel", "parallel", "arbitrary"),
        name="hgrn2",
    )(proj_a, proj_b, proj_b, proj_b, lb, nw.reshape(1, d),
      jnp.asarray(np.concatenate([stack] * 3, axis=1), BF16), jnp.asarray(level), *riders)


def _ffn_up_kernel(x_ref, halo_ref, nw_ref, wa_ref, wb_ref, cwa_ref, cwb_ref, cba_ref, cbb_ref,
                   o_ref, h_ref, u_ref, *, tm, per_batch):
    hl = SUBLANES

    @pl.when(pl.program_id(1) == 0)
    def _():
        nw = nw_ref[...]
        first = (pl.program_id(0) % per_batch) == 0
        halo = _rms_rows(halo_ref[...], nw)
        h_ref[0:hl, :] = jnp.where(first, 0.0, halo).astype(h_ref.dtype)
        h_ref[hl:, :] = _rms_rows(x_ref[...], nw).astype(h_ref.dtype)

    h = h_ref[...]
    tf = o_ref.shape[1]
    cw = min(FFN_COL_CHUNK, tf)

    def cols(cc):
        return slice(cc * cw, (cc + 1) * cw)

    def matmuls(cc):
        u_ref[cc, 0] = jnp.dot(h, wa_ref[:, cols(cc)], preferred_element_type=F32)
        u_ref[cc, 1] = jnp.dot(h, wb_ref[:, cols(cc)], preferred_element_type=F32)

    def conv(cc, half, cw_ref, cb_ref):
        acc = cb_ref[:, cols(cc)][None]
        for j in range(CONV_WIDTH):
            off = hl - (CONV_WIDTH - 1) + j
            rows = u_ref[cc, half, off:off + tm, :].reshape(tm // hl, hl, cw)
            acc = acc + cw_ref[j, :, cols(cc)][None] * rows
        return acc.reshape(tm, cw)

    def gate(cc):
        a = conv(cc, 0, cwa_ref, cba_ref)
        b = conv(cc, 1, cwb_ref, cbb_ref)
        o_ref[:, cols(cc)] = ((a * jax.nn.sigmoid(a)) * b).astype(o_ref.dtype)

    for cc in range(tf // cw):
        matmuls(cc)
        gate(cc)


def ffn_up(x, nw, w_up, conv_w, conv_b, seq):
    t, d = x.shape
    f = w_up.shape[1] // 2
    tm, tf = _pick(seq, MM_TM), _pick(f, FFN_TF)
    nf = f // tf
    per_batch = seq // tm
    hl = SUBLANES
    rows_per_tile = tm // hl
    cw = min(FFN_COL_CHUNK, tf)
    conv_w8 = jnp.broadcast_to(conv_w[:, None, :], (CONV_WIDTH, hl, 2 * f))
    conv_b8 = jnp.broadcast_to(conv_b.reshape(1, 2 * f), (hl, 2 * f))
    kern = functools.partial(_ffn_up_kernel, tm=tm, per_batch=per_batch)
    return pl.pallas_call(
        kern,
        grid=(t // tm, nf),
        in_specs=[
            pl.BlockSpec((tm, d), lambda i, j: (i, 0)),
            pl.BlockSpec((hl, d), lambda i, j: (jnp.maximum(i * rows_per_tile - 1, 0), 0)),
            pl.BlockSpec((1, d), lambda i, j: (0, 0)),
            pl.BlockSpec((d, tf), lambda i, j: (0, j)),
            pl.BlockSpec((d, tf), lambda i, j: (0, nf + j)),
            pl.BlockSpec((CONV_WIDTH, hl, tf), lambda i, j: (0, 0, j)),
            pl.BlockSpec((CONV_WIDTH, hl, tf), lambda i, j: (0, 0, nf + j)),
            pl.BlockSpec((hl, tf), lambda i, j: (0, j)),
            pl.BlockSpec((hl, tf), lambda i, j: (0, nf + j)),
        ],
        out_specs=pl.BlockSpec((tm, tf), lambda i, j: (i, j)),
        out_shape=jax.ShapeDtypeStruct((t, f), BF16),
        scratch_shapes=[
            pltpu.VMEM((tm + hl, d), BF16),
            pltpu.VMEM((tf // cw, 2, tm + hl, cw), F32),
        ],
        compiler_params=_params("parallel", "arbitrary"),
        name="ffn_up",
    )(x, x, nw.reshape(1, d), w_up, w_up, conv_w8, conv_w8, conv_b8, conv_b8)


def _ffn_down_kernel(a_ref, w_ref, res_ref, nw_ref, o_ref):
    y = res_ref[...] + jnp.dot(a_ref[...], w_ref[...], preferred_element_type=F32)
    o_ref[...] = _rms_rows(y, nw_ref[...])


def ffn_down(act, w_down, res, nw):
    t, f = act.shape
    d = w_down.shape[1]
    tm = _pick(t, FFN_DOWN_TM)
    return pl.pallas_call(
        _ffn_down_kernel,
        grid=(t // tm,),
        in_specs=[
            pl.BlockSpec((tm, f), lambda i: (i, 0)),
            pl.BlockSpec((f, d), lambda i: (0, 0), pipeline_mode=pl.Buffered(1)),
            pl.BlockSpec((tm, d), lambda i: (i, 0)),
            pl.BlockSpec((1, d), lambda i: (0, 0)),
        ],
        out_specs=pl.BlockSpec((tm, d), lambda i: (i, 0)),
        out_shape=jax.ShapeDtypeStruct((t, d), F32),
        compiler_params=_params("parallel"),
        name="ffn_down",
    )(act, w_down, res, nw.reshape(1, d))


def kernel(x, mem, w_in, w_out, norm_mix_w, lam_q1, lam_k1, lam_q2, lam_k2, da_subln_w,
           hg_lb_raw, hg_norm_w, rel_bias, norm_mem_w, mem_kv_norm_w, w_mq, w_mkv, w_mo,
           norm_ffn_w, w_up, conv_w, conv_b, w_down, final_norm_w):
    batch, seq, d_model = x.shape
    n_mem = mem.shape[1]
    depth = w_in.shape[0]
    assert depth == 1, "single-layer block"
    layer = 0
    t = batch * seq

    da_width = DA_HEADS * LANES
    hg_width = HG_HEADS * LANES
    assert w_in.shape[2] == 3 * da_width + 4 * hg_width
    assert da_subln_w.shape[1] == LANES and hg_norm_w.shape[1] == LANES
    assert lam_q1.shape[1] * 2 == LANES
    n_bf = 3 * da_width + hg_width

    tq = _pick(seq, DA_BLOCK)
    ts = _pick(seq, HG_TILE)

    lam_init = 0.8 - 0.6 * math.exp(-0.3 * layer)
    lam_vecs = jnp.stack([lam_q1[layer], lam_k1[layer], lam_q2[layer], lam_k2[layer]]).astype(F32)
    lam_row, lb = param_tables(lam_vecs, hg_lb_raw.astype(F32), lam_init, layer)

    far = _t5_bucket_np(np.arange(tq + 1, max(seq, tq + 2)))
    assert (far == far[0]).all(), "bias must be constant beyond the sub-diagonal block"
    c = np.arange(2 * tq)
    dist_buckets = np.stack([np.where(c >= tq, _t5_bucket_np(np.maximum(c - tq, 0)), -1),
                             _t5_bucket_np(c)]).astype(np.int32)
    bias_t = bias_tiles(rel_bias.astype(F32), jnp.asarray(dist_buckets), tq, int(far[0]))

    x2d = x.reshape(t, d_model)
    q_scale = (LANES // 2) ** -0.5
    col_scale = np.ones((1, n_bf), np.float32)
    col_scale[:, :da_width] = q_scale
    proj_a, proj_b = in_proj(x2d, norm_mix_w[layer], w_in[layer].astype(BF16), n_bf, jnp.asarray(col_scale))

    da_o, w_up_bf, w_mkv_bf = diff_attention(
        proj_a, bias_t, lam_row, da_subln_w[layer], batch, seq, tq, 1.0 - lam_init,
        riders=(w_up, w_mkv), layer=layer)
    hg_o, w_down_bf, w_out_bf, w_mq_bf, w_mo_bf = hgrn2(
        proj_a, proj_b, lb, hg_norm_w[layer], batch, seq, ts, 3 * DA_HEADS,
        riders=(w_down, w_out, w_mq, w_mo), layer=layer)

    mkv = norm_matmul(mem.reshape(batch * n_mem, d_model), mem_kv_norm_w[layer], w_mkv_bf, BF16,
                      name="mem_kv_proj")
    x2 = mix_out_mem_attention(da_o, hg_o, w_out_bf, x2d, norm_mem_w[layer], w_mq_bf, mkv, w_mo_bf,
                               seq, n_mem)

    act = ffn_up(x2, norm_ffn_w[layer], w_up_bf, conv_w[layer], conv_b[layer], seq)
    out = ffn_down(act, w_down_bf, x2, final_norm_w)
    return out.reshape(batch, seq, d_model)
```

```python
import functools
import math

import numpy as np
import jax
import jax.numpy as jnp
from jax import lax
from jax.experimental import pallas as pl
from jax.experimental.pallas import tpu as pltpu

DA_HEADS = 8
HG_HEADS = 8
MEM_HEADS = 4
NUM_BUCKETS = 32
MAX_DISTANCE = 128
CONV_WIDTH = 3
EPS = 1e-6

LANES = 128
SUBLANES = 8
VMEM_LIMIT_BYTES = 56 * 1024 * 1024

HG_CHUNK = 128
HG_TILE = 512
HG_HEADS_PER_STEP = 4
HG_SMALL_LEVELS = 3
DA_BLOCK = 512
DA_Q_SUB = 512
MM_TM = 1024
MM_TN = 1024
FFN_TF = 512
FFN_COL_CHUNK = 256
RESIDENT_TM = 512
FFN_DOWN_TM = RESIDENT_TM
NEG_BIG = -1e30

F32 = jnp.float32
BF16 = jnp.bfloat16


def _params(*sem, flags=None):
    return pltpu.CompilerParams(dimension_semantics=sem, vmem_limit_bytes=VMEM_LIMIT_BYTES, flags=flags)


def _rms_rows(x, w):
    ms = jnp.mean(x * x, axis=-1, keepdims=True)
    return (x * lax.rsqrt(ms + EPS)) * w


def _pick(n, pref):
    t = min(pref, n)
    while n % t:
        t //= 2
    return t


def _norm_mm_kernel(x_ref, nw_ref, w_ref, o_ref, h_ref):
    @pl.when(pl.program_id(1) == 0)
    def _():
        h_ref[...] = _rms_rows(x_ref[...], nw_ref[...]).astype(h_ref.dtype)

    o_ref[...] = jnp.dot(h_ref[...], w_ref[...], preferred_element_type=F32).astype(o_ref.dtype)


def norm_matmul(x, nw, w, out_dtype, name="norm_matmul"):
    t, k = x.shape
    n = w.shape[1]
    tm, tn = _pick(t, MM_TM), _pick(n, MM_TN)
    return pl.pallas_call(
        _norm_mm_kernel,
        grid=(t // tm, n // tn),
        in_specs=[
            pl.BlockSpec((tm, k), lambda i, j: (i, 0)),
            pl.BlockSpec((1, k), lambda i, j: (0, 0)),
            pl.BlockSpec((k, tn), lambda i, j: (0, j)),
        ],
        out_specs=pl.BlockSpec((tm, tn), lambda i, j: (i, j)),
        out_shape=jax.ShapeDtypeStruct((t, n), out_dtype),
        scratch_shapes=[pltpu.VMEM((tm, k), BF16)],
        compiler_params=_params("parallel", "arbitrary"),
        name=name,
    )(x, nw.reshape(1, k), w)


BF16_ROWS = 16


def _with_cast_riders(body, n_in, n_out, n_riders):
    def kern(*refs):
        ins, rest = refs[:n_in], refs[n_in:]
        rid_in, rest = rest[:n_riders], rest[n_riders:]
        outs, rest = rest[:n_out], rest[n_out:]
        rid_out, scratch = rest[:n_riders], rest[n_riders:]
        for src, dst in zip(rid_in, rid_out):
            dst[...] = src[...].astype(dst.dtype)
        body(*ins, *outs, *scratch)
    return kern


def _rider_specs(riders, layer, grid):
    n_steps = math.prod(grid)

    def step(*g):
        idx = 0
        for gi, n in zip(g, grid):
            idx = idx * n + gi
        return idx

    in_specs, out_specs, shapes = [], [], []
    for a in riders:
        _, rows, cols = a.shape
        n_slabs = math.gcd(n_steps, rows // BF16_ROWS)
        assert rows % BF16_ROWS == 0 and n_steps % n_slabs == 0, (a.shape, grid)
        hold = n_steps // n_slabs
        in_specs.append(pl.BlockSpec((None, rows // n_slabs, cols),
                                     lambda *g, hold=hold: (layer, step(*g) // hold, 0)))
        out_specs.append(pl.BlockSpec((rows // n_slabs, cols),
                                      lambda *g, hold=hold: (step(*g) // hold, 0)))
        shapes.append(jax.ShapeDtypeStruct((rows, cols), BF16))
    return in_specs, out_specs, shapes


def _in_proj_kernel(x_ref, nw_ref, w_ref, cs_ref, oa_ref, ob_ref, h_ref, *, na_tiles):
    j = pl.program_id(1)

    @pl.when(j == 0)
    def _():
        h_ref[...] = _rms_rows(x_ref[...], nw_ref[...]).astype(h_ref.dtype)

    y = jnp.dot(h_ref[...], w_ref[...], preferred_element_type=F32)

    @pl.when(j < na_tiles)
    def _():
        oa_ref[...] = (y * cs_ref[...]).astype(oa_ref.dtype)

    @pl.when(j >= na_tiles)
    def _():
        ob_ref[...] = y


def in_proj(x, nw, w, n_a, col_scale):
    t, k = x.shape
    n = w.shape[1]
    tm, tn = _pick(t, MM_TM), math.gcd(_pick(n_a, MM_TN), _pick(n - n_a, MM_TN))
    na_tiles = n_a // tn
    return pl.pallas_call(
        functools.partial(_in_proj_kernel, na_tiles=na_tiles),
        grid=(t // tm, n // tn),
        in_specs=[
            pl.BlockSpec((tm, k), lambda i, j: (i, 0)),
            pl.BlockSpec((1, k), lambda i, j: (0, 0)),
            pl.BlockSpec((k, tn), lambda i, j: (0, j)),
            pl.BlockSpec((1, tn), lambda i, j: (0, jnp.minimum(j, na_tiles - 1))),
        ],
        out_specs=[
            pl.BlockSpec((tm, tn), lambda i, j: (i, jnp.minimum(j, na_tiles - 1))),
            pl.BlockSpec((tm, tn), lambda i, j: (i, jnp.maximum(j - na_tiles, 0))),
        ],
        out_shape=(jax.ShapeDtypeStruct((t, n_a), BF16), jax.ShapeDtypeStruct((t, n - n_a), F32)),
        scratch_shapes=[pltpu.VMEM((tm, k), BF16)],
        compiler_params=_params("parallel", "arbitrary"),
        name="in_proj",
    )(x, nw.reshape(1, k), w, col_scale)


def _mix_mem_kernel(a0_ref, a1_ref, w0_ref, w1_ref, x_ref, nw_ref, wq_ref, k_ref, v_ref, wo_ref,
                    o_ref, att_ref, *, scale):
    x1 = x_ref[...]
    x1 = x1 + jnp.dot(a0_ref[...], w0_ref[...], preferred_element_type=F32)
    x1 = x1 + jnp.dot(a1_ref[...], w1_ref[...], preferred_element_type=F32)
    hq = _rms_rows(x1, nw_ref[...]).astype(BF16)
    mq = jnp.dot(hq, wq_ref[...], preferred_element_type=F32).astype(BF16)
    dh = mq.shape[1] // MEM_HEADS
    for hd in range(MEM_HEADS):
        sl = slice(hd * dh, (hd + 1) * dh)
        s = lax.dot_general(mq[:, sl], k_ref[:, sl], (((1,), (1,)), ((), ())),
                            preferred_element_type=F32) * scale
        e = jnp.exp(s - jnp.max(s, axis=-1, keepdims=True))
        p = e / jnp.sum(e, axis=-1, keepdims=True)
        att_ref[:, sl] = jnp.dot(p.astype(BF16), v_ref[:, sl],
                                 preferred_element_type=F32).astype(att_ref.dtype)
    o_ref[...] = x1 + jnp.dot(att_ref[...], wo_ref[...], preferred_element_type=F32)


def mix_out_mem_attention(a0, a1, w_out, x, nw_mem, w_mq, mkv, w_mo, seq, n_mem):
    t, k0 = a0.shape
    k1 = a1.shape[1]
    d = x.shape[1]
    assert k0 == k1 and w_out.shape == (k0 + k1, d)
    tm = _pick(seq, RESIDENT_TM)
    per_batch = seq // tm
    scale = (d // MEM_HEADS) ** -0.5
    once = pl.Buffered(1)
    return pl.pallas_call(
        functools.partial(_mix_mem_kernel, scale=scale),
        grid=(t // tm,),
        in_specs=[
            pl.BlockSpec((tm, k0), lambda i: (i, 0)),
            pl.BlockSpec((tm, k1), lambda i: (i, 0)),
            pl.BlockSpec((k0, d), lambda i: (0, 0), pipeline_mode=once),
            pl.BlockSpec((k1, d), lambda i: (1, 0), pipeline_mode=once),
            pl.BlockSpec((tm, d), lambda i: (i, 0)),
            pl.BlockSpec((1, d), lambda i: (0, 0)),
            pl.BlockSpec((d, d), lambda i: (0, 0), pipeline_mode=once),
            pl.BlockSpec((n_mem, d), lambda i: (i // per_batch, 0), pipeline_mode=once),
            pl.BlockSpec((n_mem, d), lambda i: (i // per_batch, 1), pipeline_mode=once),
            pl.BlockSpec((d, d), lambda i: (0, 0), pipeline_mode=once),
        ],
        out_specs=pl.BlockSpec((tm, d), lambda i: (i, 0)),
        out_shape=jax.ShapeDtypeStruct((t, d), F32),
        scratch_shapes=[pltpu.VMEM((tm, d), BF16)],
        compiler_params=_params("parallel"),
        name="mix_out_mem_attention",
    )(a0, a1, w_out, w_out, x, nw_mem.reshape(1, d), w_mq, mkv, mkv, w_mo)


def _t5_bucket_np(n):
    max_exact = NUM_BUCKETS // 2
    nf = np.maximum(n, 1).astype(np.float32)
    large = max_exact + (np.log(nf / max_exact) / math.log(MAX_DISTANCE / max_exact)
                         * (NUM_BUCKETS - max_exact)).astype(np.int32)
    large = np.minimum(large, NUM_BUCKETS - 1)
    return np.where(n < max_exact, n, large).astype(np.int32)


def _tables_kernel(lamv_ref, lbraw_ref, lam_ref, lb_ref, *, lam_init, layer):
    v = lamv_ref[...]
    s1 = jnp.sum(v[0:1] * v[1:2], axis=-1, keepdims=True)
    s2 = jnp.sum(v[2:3] * v[3:4], axis=-1, keepdims=True)
    lam = jnp.exp(s1) - jnp.exp(s2) + lam_init
    lam_ref[...] = jnp.broadcast_to(lam, lam_ref.shape)
    raw = lbraw_ref[...]
    e = jnp.exp(raw - jnp.max(raw, axis=0, keepdims=True))
    sm = e / jnp.sum(e, axis=0, keepdims=True)
    lb_ref[...] = jnp.sum(sm[0:layer + 1], axis=0, keepdims=True)


def param_tables(lam_vecs, lb_raw, lam_init, layer):
    kw = lb_raw.shape[1]
    return pl.pallas_call(
        functools.partial(_tables_kernel, lam_init=lam_init, layer=layer),
        out_shape=(jax.ShapeDtypeStruct((1, LANES), F32), jax.ShapeDtypeStruct((1, kw), F32)),
        name="param_tables",
    )(lam_vecs, lb_raw)


def _bias_tiles_kernel(rb_ref, bk_ref, o_ref, *, tq, far_bucket):
    h = pl.program_id(0)
    bk = bk_ref[...]
    vec = jnp.full(bk.shape, NEG_BIG, F32)
    for n in range(NUM_BUCKETS):
        vec = jnp.where(bk == n, rb_ref[n, h], vec)
    for t in range(2):
        rows = jnp.broadcast_to(vec[t:t + 1], (tq, 2 * tq))
        o_ref[0, t] = pltpu.roll(rows, 0, 1, stride=1, stride_axis=0)[:, tq:]
    o_ref[0, FAR_TILE] = jnp.full((tq, tq), rb_ref[far_bucket, h], F32)


def bias_tiles(rel_bias, dist_buckets, tq, far_bucket):
    nb, h = rel_bias.shape
    return pl.pallas_call(
        functools.partial(_bias_tiles_kernel, tq=tq, far_bucket=far_bucket),
        grid=(h,),
        in_specs=[
            pl.BlockSpec(memory_space=pltpu.SMEM),
            pl.BlockSpec(dist_buckets.shape, lambda i: (0, 0)),
        ],
        out_specs=pl.BlockSpec((1, FAR_TILE + 1, tq, tq), lambda i: (i, 0, 0, 0)),
        out_shape=jax.ShapeDtypeStruct((h, FAR_TILE + 1, tq, tq), F32),
        compiler_params=_params("arbitrary"),
        name="bias_tiles",
    )(rel_bias, dist_buckets)


ONES_ROWS = 16
FAR_TILE = 2


def _diff_attn_kernel(q_ref, k_ref, v_ref, bias_ref, lam_ref, nw_ref, o_ref,
                      vt_ref, m_ref, acc_ref, st_ref, *, tq, out_scale):
    dv = v_ref.shape[1]
    dh = dv // 2
    nq = v_ref.shape[0] // tq

    for c in range(nq):
        vt_ref[c, 0:dv, :] = v_ref[c * tq:(c + 1) * tq, :].astype(F32).T.astype(BF16)
        vt_ref[c, dv:, :] = jnp.ones((ONES_ROWS, tq), BF16)

    lane = lax.broadcasted_iota(jnp.int32, (1, dv), 1)
    qw = min(DA_Q_SUB, tq)
    chains = [(mp, slice(qs * qw, (qs + 1) * qw)) for mp in range(2) for qs in range(tq // qw)]

    def query_block(i, carry):
        rows = pl.ds(pl.multiple_of(i * tq, tq), tq)
        q = q_ref[rows, :]
        qm = (jnp.where(lane < dh, q, jnp.zeros_like(q)), jnp.where(lane >= dh, q, jnp.zeros_like(q)))

        m_ref[...] = jnp.full(m_ref.shape, NEG_BIG, F32)
        acc_ref[...] = jnp.zeros(acc_ref.shape, F32)

        def scores(n, j):
            mp, cols = chains[n]
            kb = k_ref[pl.ds(pl.multiple_of(j * tq, tq), tq), :]
            st_ref[n] = lax.dot_general(kb, qm[mp][cols, :], (((1,), (1,)), ((), ())),
                                        preferred_element_type=F32)

        far_bias = bias_ref[0, FAR_TILE, 0:1, 0:1]

        def softmax_pv(n, vt, tile):
            mp, cols = chains[n]
            st = st_ref[n]
            if tile == FAR_TILE:
                shift = far_bias
            else:
                st = st + bias_ref[0, tile, :, cols]
                shift = 0.0
            m_old = m_ref[mp, :, cols]
            m_new = jnp.maximum(m_old, jnp.max(st, axis=0, keepdims=True) + shift)
            p = jnp.exp(st - (m_new - shift)).astype(BF16)
            alpha = jnp.exp(m_old - m_new)
            acc_ref[mp, :, cols] = (alpha * acc_ref[mp, :, cols]
                                    + jnp.dot(vt, p, preferred_element_type=F32))
            m_ref[mp, :, cols] = m_new

        def step(j, tile, last=False):
            vt = vt_ref[j]
            for n in range(len(chains)):
                if n + 1 < len(chains):
                    scores(n + 1, j)
                elif not last:
                    scores(0, j + 1)
                softmax_pv(n, vt, tile)

        def far_step(j, c2):
            step(j, FAR_TILE)
            return c2

        scores(0, 0)
        lax.fori_loop(0, jnp.maximum(i - 1, 0), far_step, 0)

        @pl.when(i >= 1)
        def _():
            step(i - 1, 1)

        step(i, 0, last=True)

        o0 = acc_ref[0, 0:dv, :] / acc_ref[0, dv:dv + 1, :]
        o1 = acc_ref[1, 0:dv, :] / acc_ref[1, dv:dv + 1, :]
        o = o0 - lam_ref[0:1, 0:1] * o1
        ms = jnp.mean(o * o, axis=0, keepdims=True)
        y = (o * lax.rsqrt(ms + EPS)) * (nw_ref[...] * out_scale)
        o_ref[rows, :] = y.T.astype(o_ref.dtype)
        return carry

    lax.fori_loop(0, nq, query_block, 0)


def diff_attention(proj, bias_t, lam_row, nw, batch, seq, tq, out_scale, riders=(), layer=0):
    nq = seq // tq
    dv = LANES
    grid = (batch, DA_HEADS)
    rider_in, rider_out, rider_shapes = _rider_specs(riders, layer, grid)
    kern = _with_cast_riders(functools.partial(_diff_attn_kernel, tq=tq, out_scale=out_scale),
                             n_in=6, n_out=1, n_riders=len(riders))
    return pl.pallas_call(
        kern,
        grid=grid,
        in_specs=[
            pl.BlockSpec((seq, dv), lambda b, h: (b, h)),
            pl.BlockSpec((seq, dv), lambda b, h: (b, DA_HEADS + h)),
            pl.BlockSpec((seq, dv), lambda b, h: (b, 2 * DA_HEADS + h)),
            pl.BlockSpec((1, FAR_TILE + 1, tq, tq), lambda b, h: (h, 0, 0, 0)),
            pl.BlockSpec((1, LANES), lambda b, h: (0, 0)),
            pl.BlockSpec((dv, 1), lambda b, h: (0, 0)),
        ] + rider_in,
        out_specs=[pl.BlockSpec((seq, dv), lambda b, h: (b, h))] + rider_out,
        out_shape=[jax.ShapeDtypeStruct((batch * seq, DA_HEADS * dv), BF16)] + rider_shapes,
        scratch_shapes=[
            pltpu.VMEM((nq, dv + ONES_ROWS, tq), BF16),
            pltpu.VMEM((2, 1, tq), F32),
            pltpu.VMEM((2, dv + ONES_ROWS, tq), F32),
            pltpu.VMEM((2 * tq // min(DA_Q_SUB, tq), tq, min(DA_Q_SUB, tq)), F32),
        ],
        compiler_params=_params("parallel", "parallel"),
        name="diff_attention",
    )(proj, proj, proj, bias_t, lam_row, nw.reshape(dv, 1), *riders)


def _exact_bf16_dot(tri3, x):
    hi = x.astype(BF16)
    r1 = x - hi.astype(F32)
    mid = r1.astype(BF16)
    lo = (r1 - mid.astype(F32)).astype(BF16)
    return jnp.dot(tri3, jnp.concatenate([hi, mid, lo], axis=0), preferred_element_type=F32)


def _hgrn_tables(c):
    n_lv = int(math.log2(c))
    r = np.arange(c)[None, :]
    t = np.arange(c)[:, None]
    blocks_d, blocks_e = [], []
    for lv in range(2, HG_SMALL_LEVELS + 1):
        h = 1 << (lv - 1)
        upper = (t // h) % 2 == 1
        mid_t = (t // h) * h
        blocks_d.append(upper & (r >= mid_t) & (r <= t))
        mid_s = (t // h + 1) * h
        blocks_e.append((~upper) & (r > t) & (r < mid_s))
    stack = np.concatenate(blocks_d + blocks_e + [r <= t], axis=0).astype(np.float32)
    x = np.bitwise_xor(t, r)
    level = np.where(r == t, 0, np.floor(np.log2(np.maximum(x, 1))).astype(np.int32) + 1)
    level = np.where(r > t, -1, level).astype(np.int32)
    return stack, level, n_lv


def _hgrn_kernel(q_ref, f_ref, i_ref, g_ref, lb_ref, nw_ref, tst_ref, lvl_ref, o_ref, st_ref,
                 *, ts, c, n_lv):
    nt = (((1,), (1,)), ((), ()))
    d = LANES
    nh = q_ref.shape[1] // d
    nch = ts // c
    n_small = HG_SMALL_LEVELS

    @pl.when(pl.program_id(2) == 0)
    def _():
        st_ref[...] = jnp.zeros(st_ref.shape, F32)

    lb = lb_ref[...]
    f = lb + (1.0 - lb) * jax.nn.sigmoid(f_ref[...])
    logf = jnp.log(f)
    kk = 1.0 - f
    units = [(ch, hd) for ch in range(nch) for hd in range(nh)]
    lcat = jnp.concatenate([logf[ch * c:(ch + 1) * c, hd * d:(hd + 1) * d] for ch, hd in units], axis=1)
    seg = _exact_bf16_dot(tst_ref[...], lcat)
    lvl = lvl_ref[...]
    nw = nw_ref[...]

    st = [st_ref[hd] for hd in range(nh)]
    for u, (ch, hd) in enumerate(units):
        rows = slice(ch * c, (ch + 1) * c)
        cols = slice(hd * d, (hd + 1) * d)

        def blk(b, u=u):
            return seg[b * c:(b + 1) * c, u * d:(u + 1) * d]

        g_in = blk(2 * n_small - 2)
        g_last = g_in[c - 1:c]
        g_out = g_last - g_in
        q = q_ref[rows, cols]
        q32 = q.astype(F32)
        kc = kk[rows, cols]
        kb = kc.astype(BF16)
        vb = i_ref[rows, cols].astype(BF16)
        a = jnp.where(lvl == 0, lax.dot_general(q, kb, nt, preferred_element_type=F32), 0.0)
        qt = (q32 * f[rows, cols]).astype(BF16)
        a = jnp.where(lvl == 1, lax.dot_general(qt, kb, nt, preferred_element_type=F32), a)
        for lv in range(2, n_lv + 1):
            if lv <= n_small:
                d_up = blk(lv - 2)
                d_lo = blk(n_small + lv - 3)
            else:
                h = 1 << (lv - 1)
                ref = jnp.concatenate(
                    [jnp.broadcast_to(g_in[gi * 2 * h + h - 1:gi * 2 * h + h, :], (2 * h, d))
                     for gi in range(c // (2 * h))], axis=0)
                d_up = jnp.minimum(g_in - ref, 0.0)
                d_lo = jnp.minimum(ref - g_in, 0.0)
            qt = (q32 * jnp.exp(d_up)).astype(BF16)
            kt = (kc * jnp.exp(d_lo)).astype(BF16)
            a = jnp.where(lvl == lv, lax.dot_general(qt, kt, nt, preferred_element_type=F32), a)
        qg = (q32 * jnp.exp(g_in)).astype(BF16)
        o = (jnp.dot(a.astype(BF16), vb, preferred_element_type=F32)
             + lax.dot_general(qg, st[hd].astype(BF16), nt, preferred_element_type=F32))
        kd = (kc * jnp.exp(g_out)).astype(BF16)
        upd = lax.dot_general(vb, kd, (((0,), (0,)), ((), ())), preferred_element_type=F32)
        st[hd] = st[hd] * jnp.exp(g_last) + upd
        g = g_ref[rows, cols]
        o_ref[rows, cols] = (_rms_rows(o, nw) * (g * jax.nn.sigmoid(g))).astype(o_ref.dtype)
    for hd in range(nh):
        st_ref[hd] = st[hd]


def hgrn2(proj_a, proj_b, lb, nw, batch, seq, ts, q_col0, riders=(), layer=0):
    ns = seq // ts
    d = LANES
    nh = math.gcd(HG_HEADS_PER_STEP, HG_HEADS)
    assert q_col0 % nh == 0
    w = nh * d
    ng = HG_HEADS // nh
    c = min(HG_CHUNK, ts)
    stack, level, n_lv = _hgrn_tables(c)
    assert n_lv >= HG_SMALL_LEVELS
    grid = (batch, ng, ns)
    rider_in, rider_out, rider_shapes = _rider_specs(riders, layer, grid)
    kern = _with_cast_riders(functools.partial(_hgrn_kernel, ts=ts, c=c, n_lv=n_lv),
                             n_in=8, n_out=1, n_riders=len(riders))
    return pl.pallas_call(
        kern,
        grid=grid,
        in_specs=[
            pl.BlockSpec((ts, w), lambda b, h, i: (b * ns + i, q_col0 // nh + h)),
            pl.BlockSpec((ts, w), lambda b, h, i: (b * ns + i, h)),
            pl.BlockSpec((ts, w), lambda b, h, i: (b * ns + i, ng + h)),
            pl.BlockSpec((ts, w), lambda b, h, i: (b * ns + i, 2 * ng + h)),
            pl.BlockSpec((1, w), lambda b, h, i: (0, h)),
            pl.BlockSpec((1, d), lambda b, h, i: (0, 0)),
            pl.BlockSpec((stack.shape[0], 3 * stack.shape[1]), lambda b, h, i: (0, 0)),
            pl.BlockSpec(level.shape, lambda b, h, i: (0, 0)),
        ] + rider_in,
        out_specs=[pl.BlockSpec((ts, w), lambda b, h, i: (b * ns + i, h))] + rider_out,
        out_shape=[jax.ShapeDtypeStruct((batch * seq, HG_HEADS * d), BF16)] + rider_shapes,
        scratch_shapes=[pltpu.VMEM((nh, d, d), F32)],
        compiler_params=_params("parallel", "parallel", "arbitrary"),
        name="hgrn2",
    )(proj_a, proj_b, proj_b, proj_b, lb, nw.reshape(1, d),
      jnp.asarray(np.concatenate([stack] * 3, axis=1), BF16), jnp.asarray(level), *riders)


def _ffn_up_kernel(x_ref, halo_ref, nw_ref, wa_ref, wb_ref, cwa_ref, cwb_ref, cba_ref, cbb_ref,
                   o_ref, h_ref, u_ref, *, tm, per_batch):
    hl = SUBLANES

    @pl.when(pl.program_id(1) == 0)
    def _():
        nw = nw_ref[...]
        first = (pl.program_id(0) % per_batch) == 0
        halo = _rms_rows(halo_ref[...], nw)
        h_ref[0:hl, :] = jnp.where(first, 0.0, halo).astype(h_ref.dtype)
        h_ref[hl:, :] = _rms_rows(x_ref[...], nw).astype(h_ref.dtype)

    h = h_ref[...]
    tf = o_ref.shape[1]
    cw = min(FFN_COL_CHUNK, tf)

    def cols(cc):
        return slice(cc * cw, (cc + 1) * cw)

    def matmuls(cc):
        u_ref[cc, 0] = jnp.dot(h, wa_ref[:, cols(cc)], preferred_element_type=F32)
        u_ref[cc, 1] = jnp.dot(h, wb_ref[:, cols(cc)], preferred_element_type=F32)

    def conv(cc, half, cw_ref, cb_ref):
        acc = cb_ref[:, cols(cc)][None]
        for j in range(CONV_WIDTH):
            off = hl - (CONV_WIDTH - 1) + j
            rows = u_ref[cc, half, off:off + tm, :].reshape(tm // hl, hl, cw)
            acc = acc + cw_ref[j, :, cols(cc)][None] * rows
        return acc.reshape(tm, cw)

    def gate(cc):
        a = conv(cc, 0, cwa_ref, cba_ref)
        b = conv(cc, 1, cwb_ref, cbb_ref)
        o_ref[:, cols(cc)] = ((a * jax.nn.sigmoid(a)) * b).astype(o_ref.dtype)

    for cc in range(tf // cw):
        matmuls(cc)
        gate(cc)


def ffn_up(x, nw, w_up, conv_w, conv_b, seq):
    t, d = x.shape
    f = w_up.shape[1] // 2
    tm, tf = _pick(seq, MM_TM), _pick(f, FFN_TF)
    nf = f // tf
    per_batch = seq // tm
    hl = SUBLANES
    rows_per_tile = tm // hl
    cw = min(FFN_COL_CHUNK, tf)
    conv_w8 = jnp.broadcast_to(conv_w[:, None, :], (CONV_WIDTH, hl, 2 * f))
    conv_b8 = jnp.broadcast_to(conv_b.reshape(1, 2 * f), (hl, 2 * f))
    kern = functools.partial(_ffn_up_kernel, tm=tm, per_batch=per_batch)
    return pl.pallas_call(
        kern,
        grid=(t // tm, nf),
        in_specs=[
            pl.BlockSpec((tm, d), lambda i, j: (i, 0)),
            pl.BlockSpec((hl, d), lambda i, j: (jnp.maximum(i * rows_per_tile - 1, 0), 0)),
            pl.BlockSpec((1, d), lambda i, j: (0, 0)),
            pl.BlockSpec((d, tf), lambda i, j: (0, j)),
            pl.BlockSpec((d, tf), lambda i, j: (0, nf + j)),
            pl.BlockSpec((CONV_WIDTH, hl, tf), lambda i, j: (0, 0, j)),
            pl.BlockSpec((CONV_WIDTH, hl, tf), lambda i, j: (0, 0, nf + j)),
            pl.BlockSpec((hl, tf), lambda i, j: (0, j)),
            pl.BlockSpec((hl, tf), lambda i, j: (0, nf + j)),
        ],
        out_specs=pl.BlockSpec((tm, tf), lambda i, j: (i, j)),
        out_shape=jax.ShapeDtypeStruct((t, f), BF16),
        scratch_shapes=[
            pltpu.VMEM((tm + hl, d), BF16),
            pltpu.VMEM((tf // cw, 2, tm + hl, cw), F32),
        ],
        compiler_params=_params("parallel", "arbitrary"),
        name="ffn_up",
    )(x, x, nw.reshape(1, d), w_up, w_up, conv_w8, conv_w8, conv_b8, conv_b8)


def _ffn_down_kernel(a_ref, w_ref, res_ref, nw_ref, o_ref):
    y = res_ref[...] + jnp.dot(a_ref[...], w_ref[...], preferred_element_type=F32)
    o_ref[...] = _rms_rows(y, nw_ref[...])


def ffn_down(act, w_down, res, nw):
    t, f = act.shape
    d = w_down.shape[1]
    tm = _pick(t, FFN_DOWN_TM)
    return pl.pallas_call(
        _ffn_down_kernel,
        grid=(t // tm,),
        in_specs=[
            pl.BlockSpec((tm, f), lambda i: (i, 0)),
            pl.BlockSpec((f, d), lambda i: (0, 0), pipeline_mode=pl.Buffered(1)),
            pl.BlockSpec((tm, d), lambda i: (i, 0)),
            pl.BlockSpec((1, d), lambda i: (0, 0)),
        ],
        out_specs=pl.BlockSpec((tm, d), lambda i: (i, 0)),
        out_shape=jax.ShapeDtypeStruct((t, d), F32),
        compiler_params=_params("parallel"),
        name="ffn_down",
    )(act, w_down, res, nw.reshape(1, d))


def kernel(x, mem, w_in, w_out, norm_mix_w, lam_q1, lam_k1, lam_q2, lam_k2, da_subln_w,
           hg_lb_raw, hg_norm_w, rel_bias, norm_mem_w, mem_kv_norm_w, w_mq, w_mkv, w_mo,
           norm_ffn_w, w_up, conv_w, conv_b, w_down, final_norm_w):
    batch, seq, d_model = x.shape
    n_mem = mem.shape[1]
    depth = w_in.shape[0]
    assert depth == 1, "single-layer block"
    layer = 0
    t = batch * seq

    da_width = DA_HEADS * LANES
    hg_width = HG_HEADS * LANES
    assert w_in.shape[2] == 3 * da_width + 4 * hg_width
    assert da_subln_w.shape[1] == LANES and hg_norm_w.shape[1] == LANES
    assert lam_q1.shape[1] * 2 == LANES
    n_bf = 3 * da_width + hg_width

    tq = _pick(seq, DA_BLOCK)
    ts = _pick(seq, HG_TILE)

    lam_init = 0.8 - 0.6 * math.exp(-0.3 * layer)
    lam_vecs = jnp.stack([lam_q1[layer], lam_k1[layer], lam_q2[layer], lam_k2[layer]]).astype(F32)
    lam_row, lb = param_tables(lam_vecs, hg_lb_raw.astype(F32), lam_init, layer)

    far = _t5_bucket_np(np.arange(tq + 1, max(seq, tq + 2)))
    assert (far == far[0]).all(), "bias must be constant beyond the sub-diagonal block"
    c = np.arange(2 * tq)
    dist_buckets = np.stack([np.where(c >= tq, _t5_bucket_np(np.maximum(c - tq, 0)), -1),
                             _t5_bucket_np(c)]).astype(np.int32)
    bias_t = bias_tiles(rel_bias.astype(F32), jnp.asarray(dist_buckets), tq, int(far[0]))

    x2d = x.reshape(t, d_model)
    q_scale = (LANES // 2) ** -0.5
    col_scale = np.ones((1, n_bf), np.float32)
    col_scale[:, :da_width] = q_scale
    proj_a, proj_b = in_proj(x2d, norm_mix_w[layer], w_in[layer].astype(BF16), n_bf, jnp.asarray(col_scale))

    da_o, w_up_bf, w_mkv_bf = diff_attention(
        proj_a, bias_t, lam_row, da_subln_w[layer], batch, seq, tq, 1.0 - lam_init,
        riders=(w_up, w_mkv), layer=layer)
    hg_o, w_down_bf, w_out_bf, w_mq_bf, w_mo_bf = hgrn2(
        proj_a, proj_b, lb, hg_norm_w[layer], batch, seq, ts, 3 * DA_HEADS,
        riders=(w_down, w_out, w_mq, w_mo), layer=layer)

    mkv = norm_matmul(mem.reshape(batch * n_mem, d_model), mem_kv_norm_w[layer], w_mkv_bf, BF16,
                      name="mem_kv_proj")
    x2 = mix_out_mem_attention(da_o, hg_o, w_out_bf, x2d, norm_mem_w[layer], w_mq_bf, mkv, w_mo_bf,
                               seq, n_mem)

    act = ffn_up(x2, norm_ffn_w[layer], w_up_bf, conv_w[layer], conv_b[layer], seq)
    out = ffn_down(act, w_down_bf, x2, final_norm_w)
    return out.reshape(batch, seq, d_model)
```

```python
import functools
import math

import numpy as np
import jax
import jax.numpy as jnp
from jax import lax
from jax.experimental import pallas as pl
from jax.experimental.pallas import tpu as pltpu

DA_HEADS = 8
HG_HEADS = 8
MEM_HEADS = 4
NUM_BUCKETS = 32
MAX_DISTANCE = 128
CONV_WIDTH = 3
EPS = 1e-6

LANES = 128
SUBLANES = 8
VMEM_LIMIT_BYTES = 56 * 1024 * 1024

HG_CHUNK = 128
HG_TILE = 512
HG_HEADS_PER_STEP = 4
HG_SMALL_LEVELS = 3
DA_BLOCK = 512
DA_Q_SUB = 512
MM_TM = 1024
MM_TN = 1024
FFN_TF = 512
FFN_COL_CHUNK = 512
RESIDENT_TM = 512
FFN_DOWN_TM = RESIDENT_TM
NEG_BIG = -1e30

F32 = jnp.float32
BF16 = jnp.bfloat16


def _params(*sem):
    return pltpu.CompilerParams(dimension_semantics=sem, vmem_limit_bytes=VMEM_LIMIT_BYTES)


def _rms_rows(x, w):
    ms = jnp.mean(x * x, axis=-1, keepdims=True)
    return (x * lax.rsqrt(ms + EPS)) * w


def _pick(n, pref):
    t = min(pref, n)
    while n % t:
        t //= 2
    return t


def _norm_mm_kernel(x_ref, nw_ref, w_ref, o_ref, h_ref):
    @pl.when(pl.program_id(1) == 0)
    def _():
        h_ref[...] = _rms_rows(x_ref[...], nw_ref[...]).astype(h_ref.dtype)

    o_ref[...] = jnp.dot(h_ref[...], w_ref[...], preferred_element_type=F32).astype(o_ref.dtype)


def norm_matmul(x, nw, w, out_dtype, name="norm_matmul"):
    t, k = x.shape
    n = w.shape[1]
    tm, tn = _pick(t, MM_TM), _pick(n, MM_TN)
    return pl.pallas_call(
        _norm_mm_kernel,
        grid=(t // tm, n // tn),
        in_specs=[
            pl.BlockSpec((tm, k), lambda i, j: (i, 0)),
            pl.BlockSpec((1, k), lambda i, j: (0, 0)),
            pl.BlockSpec((k, tn), lambda i, j: (0, j)),
        ],
        out_specs=pl.BlockSpec((tm, tn), lambda i, j: (i, j)),
        out_shape=jax.ShapeDtypeStruct((t, n), out_dtype),
        scratch_shapes=[pltpu.VMEM((tm, k), BF16)],
        compiler_params=_params("parallel", "arbitrary"),
        name=name,
    )(x, nw.reshape(1, k), w)


BF16_ROWS = 16


def _with_cast_riders(body, n_in, n_out, n_riders):
    def kern(*refs):
        ins, rest = refs[:n_in], refs[n_in:]
        rid_in, rest = rest[:n_riders], rest[n_riders:]
        outs, rest = rest[:n_out], rest[n_out:]
        rid_out, scratch = rest[:n_riders], rest[n_riders:]
        for src, dst in zip(rid_in, rid_out):
            dst[...] = src[...].astype(dst.dtype)
        body(*ins, *outs, *scratch)
    return kern


def _rider_specs(riders, layer, grid):
    n_steps = math.prod(grid)

    def step(*g):
        idx = 0
        for gi, n in zip(g, grid):
            idx = idx * n + gi
        return idx

    in_specs, out_specs, shapes = [], [], []
    for a in riders:
        _, rows, cols = a.shape
        n_slabs = math.gcd(n_steps, rows // BF16_ROWS)
        assert rows % BF16_ROWS == 0 and n_steps % n_slabs == 0, (a.shape, grid)
        hold = n_steps // n_slabs
        in_specs.append(pl.BlockSpec((None, rows // n_slabs, cols),
                                     lambda *g, hold=hold: (layer, step(*g) // hold, 0)))
        out_specs.append(pl.BlockSpec((rows // n_slabs, cols),
                                      lambda *g, hold=hold: (step(*g) // hold, 0)))
        shapes.append(jax.ShapeDtypeStruct((rows, cols), BF16))
    return in_specs, out_specs, shapes


def _in_proj_kernel(x_ref, nw_ref, w_ref, cs_ref, oa_ref, ob_ref, h_ref, *, na_tiles):
    j = pl.program_id(1)

    @pl.when(j == 0)
    def _():
        h_ref[...] = _rms_rows(x_ref[...], nw_ref[...]).astype(h_ref.dtype)

    y = jnp.dot(h_ref[...], w_ref[...], preferred_element_type=F32)

    @pl.when(j < na_tiles)
    def _():
        oa_ref[...] = (y * cs_ref[...]).astype(oa_ref.dtype)

    @pl.when(j >= na_tiles)
    def _():
        ob_ref[...] = y


def in_proj(x, nw, w, n_a, col_scale):
    t, k = x.shape
    n = w.shape[1]
    tm, tn = _pick(t, MM_TM), math.gcd(_pick(n_a, MM_TN), _pick(n - n_a, MM_TN))
    na_tiles = n_a // tn
    return pl.pallas_call(
        functools.partial(_in_proj_kernel, na_tiles=na_tiles),
        grid=(t // tm, n // tn),
        in_specs=[
            pl.BlockSpec((tm, k), lambda i, j: (i, 0)),
            pl.BlockSpec((1, k), lambda i, j: (0, 0)),
            pl.BlockSpec((k, tn), lambda i, j: (0, j)),
            pl.BlockSpec((1, tn), lambda i, j: (0, jnp.minimum(j, na_tiles - 1))),
        ],
        out_specs=[
            pl.BlockSpec((tm, tn), lambda i, j: (i, jnp.minimum(j, na_tiles - 1))),
            pl.BlockSpec((tm, tn), lambda i, j: (i, jnp.maximum(j - na_tiles, 0))),
        ],
        out_shape=(jax.ShapeDtypeStruct((t, n_a), BF16), jax.ShapeDtypeStruct((t, n - n_a), F32)),
        scratch_shapes=[pltpu.VMEM((tm, k), BF16)],
        compiler_params=_params("parallel", "arbitrary"),
        name="in_proj",
    )(x, nw.reshape(1, k), w, col_scale)


def _mix_mem_kernel(a0_ref, a1_ref, w0_ref, w1_ref, x_ref, nw_ref, wq_ref, k_ref, v_ref, wo_ref,
                    o_ref, att_ref, *, scale):
    x1 = x_ref[...]
    x1 = x1 + jnp.dot(a0_ref[...], w0_ref[...], preferred_element_type=F32)
    x1 = x1 + jnp.dot(a1_ref[...], w1_ref[...], preferred_element_type=F32)
    hq = _rms_rows(x1, nw_ref[...]).astype(BF16)
    mq = jnp.dot(hq, wq_ref[...], preferred_element_type=F32).astype(BF16)
    dh = mq.shape[1] // MEM_HEADS
    for hd in range(MEM_HEADS):
        sl = slice(hd * dh, (hd + 1) * dh)
        s = lax.dot_general(mq[:, sl], k_ref[:, sl], (((1,), (1,)), ((), ())),
                            preferred_element_type=F32) * scale
        e = jnp.exp(s - jnp.max(s, axis=-1, keepdims=True))
        p = e / jnp.sum(e, axis=-1, keepdims=True)
        att_ref[:, sl] = jnp.dot(p.astype(BF16), v_ref[:, sl],
                                 preferred_element_type=F32).astype(att_ref.dtype)
    o_ref[...] = x1 + jnp.dot(att_ref[...], wo_ref[...], preferred_element_type=F32)


def mix_out_mem_attention(a0, a1, w_out, x, nw_mem, w_mq, mkv, w_mo, seq, n_mem):
    t, k0 = a0.shape
    k1 = a1.shape[1]
    d = x.shape[1]
    assert k0 == k1 and w_out.shape == (k0 + k1, d)
    tm = _pick(seq, RESIDENT_TM)
    per_batch = seq // tm
    scale = (d // MEM_HEADS) ** -0.5
    once = pl.Buffered(1)
    return pl.pallas_call(
        functools.partial(_mix_mem_kernel, scale=scale),
        grid=(t // tm,),
        in_specs=[
            pl.BlockSpec((tm, k0), lambda i: (i, 0)),
            pl.BlockSpec((tm, k1), lambda i: (i, 0)),
            pl.BlockSpec((k0, d), lambda i: (0, 0), pipeline_mode=once),
            pl.BlockSpec((k1, d), lambda i: (1, 0), pipeline_mode=once),
            pl.BlockSpec((tm, d), lambda i: (i, 0)),
            pl.BlockSpec((1, d), lambda i: (0, 0)),
            pl.BlockSpec((d, d), lambda i: (0, 0), pipeline_mode=once),
            pl.BlockSpec((n_mem, d), lambda i: (i // per_batch, 0), pipeline_mode=once),
            pl.BlockSpec((n_mem, d), lambda i: (i // per_batch, 1), pipeline_mode=once),
            pl.BlockSpec((d, d), lambda i: (0, 0), pipeline_mode=once),
        ],
        out_specs=pl.BlockSpec((tm, d), lambda i: (i, 0)),
        out_shape=jax.ShapeDtypeStruct((t, d), F32),
        scratch_shapes=[pltpu.VMEM((tm, d), BF16)],
        compiler_params=_params("parallel"),
        name="mix_out_mem_attention",
    )(a0, a1, w_out, w_out, x, nw_mem.reshape(1, d), w_mq, mkv, mkv, w_mo)


def _t5_bucket_np(n):
    max_exact = NUM_BUCKETS // 2
    nf = np.maximum(n, 1).astype(np.float32)
    large = max_exact + (np.log(nf / max_exact) / math.log(MAX_DISTANCE / max_exact)
                         * (NUM_BUCKETS - max_exact)).astype(np.int32)
    large = np.minimum(large, NUM_BUCKETS - 1)
    return np.where(n < max_exact, n, large).astype(np.int32)


def _tables_kernel(lamv_ref, lbraw_ref, lam_ref, lb_ref, *, lam_init, layer):
    v = lamv_ref[...]
    s1 = jnp.sum(v[0:1] * v[1:2], axis=-1, keepdims=True)
    s2 = jnp.sum(v[2:3] * v[3:4], axis=-1, keepdims=True)
    lam = jnp.exp(s1) - jnp.exp(s2) + lam_init
    lam_ref[...] = jnp.broadcast_to(lam, lam_ref.shape)
    raw = lbraw_ref[...]
    e = jnp.exp(raw - jnp.max(raw, axis=0, keepdims=True))
    sm = e / jnp.sum(e, axis=0, keepdims=True)
    lb_ref[...] = jnp.sum(sm[0:layer + 1], axis=0, keepdims=True)


def param_tables(lam_vecs, lb_raw, lam_init, layer):
    kw = lb_raw.shape[1]
    return pl.pallas_call(
        functools.partial(_tables_kernel, lam_init=lam_init, layer=layer),
        out_shape=(jax.ShapeDtypeStruct((1, LANES), F32), jax.ShapeDtypeStruct((1, kw), F32)),
        name="param_tables",
    )(lam_vecs, lb_raw)


def _bias_tiles_kernel(rb_ref, bk_ref, o_ref, *, tq, far_bucket):
    h = pl.program_id(0)
    bk = bk_ref[...]
    vec = jnp.full(bk.shape, NEG_BIG, F32)
    for n in range(NUM_BUCKETS):
        vec = jnp.where(bk == n, rb_ref[n, h], vec)
    for t in range(2):
        rows = jnp.broadcast_to(vec[t:t + 1], (tq, 2 * tq))
        o_ref[0, t] = pltpu.roll(rows, 0, 1, stride=1, stride_axis=0)[:, tq:]
    o_ref[0, FAR_TILE] = jnp.full((tq, tq), rb_ref[far_bucket, h], F32)


def bias_tiles(rel_bias, dist_buckets, tq, far_bucket):
    nb, h = rel_bias.shape
    return pl.pallas_call(
        functools.partial(_bias_tiles_kernel, tq=tq, far_bucket=far_bucket),
        grid=(h,),
        in_specs=[
            pl.BlockSpec(memory_space=pltpu.SMEM),
            pl.BlockSpec(dist_buckets.shape, lambda i: (0, 0)),
        ],
        out_specs=pl.BlockSpec((1, FAR_TILE + 1, tq, tq), lambda i: (i, 0, 0, 0)),
        out_shape=jax.ShapeDtypeStruct((h, FAR_TILE + 1, tq, tq), F32),
        compiler_params=_params("arbitrary"),
        name="bias_tiles",
    )(rel_bias, dist_buckets)


ONES_ROWS = BF16_ROWS
FAR_TILE = 2


def _diff_attn_kernel(q_ref, k_ref, v_ref, bias_ref, lam_ref, nw_ref, o_ref,
                      vt_ref, m_ref, acc_ref, st_ref, *, tq, out_scale):
    dv = v_ref.shape[1]
    dh = dv // 2
    nq = v_ref.shape[0] // tq

    for c in range(nq):
        vt_ref[c, 0:dv, :] = v_ref[c * tq:(c + 1) * tq, :].astype(F32).T.astype(BF16)
        vt_ref[c, dv:, :] = jnp.ones((ONES_ROWS, tq), BF16)

    lane = lax.broadcasted_iota(jnp.int32, (1, dv), 1)
    qw = min(DA_Q_SUB, tq)
    chains = [(mp, slice(qs * qw, (qs + 1) * qw)) for mp in range(2) for qs in range(tq // qw)]

    def query_block(i, carry):
        rows = pl.ds(pl.multiple_of(i * tq, tq), tq)
        q = q_ref[rows, :]
        qm = (jnp.where(lane < dh, q, jnp.zeros_like(q)), jnp.where(lane >= dh, q, jnp.zeros_like(q)))

        m_ref[...] = jnp.full(m_ref.shape, NEG_BIG, F32)
        acc_ref[...] = jnp.zeros(acc_ref.shape, F32)

        def scores(n, j):
            mp, cols = chains[n]
            kb = k_ref[pl.ds(pl.multiple_of(j * tq, tq), tq), :]
            st_ref[n] = lax.dot_general(kb, qm[mp][cols, :], (((1,), (1,)), ((), ())),
                                        preferred_element_type=F32)

        far_bias = bias_ref[0, FAR_TILE, 0:1, 0:1]

        def softmax_pv(n, vt, tile):
            mp, cols = chains[n]
            st = st_ref[n]
            if tile == FAR_TILE:
                shift = far_bias
            else:
                st = st + bias_ref[0, tile, :, cols]
                shift = 0.0
            m_old = m_ref[mp, :, cols]
            m_new = jnp.maximum(m_old, jnp.max(st, axis=0, keepdims=True) + shift)
            p = jnp.exp(st - (m_new - shift)).astype(BF16)
            alpha = jnp.exp(m_old - m_new)
            acc_ref[mp, :, cols] = (alpha * acc_ref[mp, :, cols]
                                    + jnp.dot(vt, p, preferred_element_type=F32))
            m_ref[mp, :, cols] = m_new

        def step(j, tile, last=False):
            vt = vt_ref[j]
            for n in range(len(chains)):
                if n + 1 < len(chains):
                    scores(n + 1, j)
                elif not last:
                    scores(0, j + 1)
                softmax_pv(n, vt, tile)

        def far_step(j, c2):
            step(j, FAR_TILE)
            return c2

        scores(0, 0)
        lax.fori_loop(0, jnp.maximum(i - 1, 0), far_step, 0)

        @pl.when(i >= 1)
        def _():
            step(i - 1, 1)

        step(i, 0, last=True)

        o0 = acc_ref[0, 0:dv, :] / acc_ref[0, dv:dv + 1, :]
        o1 = acc_ref[1, 0:dv, :] / acc_ref[1, dv:dv + 1, :]
        o = o0 - lam_ref[0:1, 0:1] * o1
        ms = jnp.mean(o * o, axis=0, keepdims=True)
        y = (o * lax.rsqrt(ms + EPS)) * (nw_ref[...] * out_scale)
        o_ref[rows, :] = y.T.astype(o_ref.dtype)
        return carry

    lax.fori_loop(0, nq, query_block, 0)


def diff_attention(proj, bias_t, lam_row, nw, batch, seq, tq, out_scale, riders=(), layer=0):
    nq = seq // tq
    dv = LANES
    grid = (batch, DA_HEADS)
    rider_in, rider_out, rider_shapes = _rider_specs(riders, layer, grid)
    kern = _with_cast_riders(functools.partial(_diff_attn_kernel, tq=tq, out_scale=out_scale),
                             n_in=6, n_out=1, n_riders=len(riders))
    return pl.pallas_call(
        kern,
        grid=grid,
        in_specs=[
            pl.BlockSpec((seq, dv), lambda b, h: (b, h)),
            pl.BlockSpec((seq, dv), lambda b, h: (b, DA_HEADS + h)),
            pl.BlockSpec((seq, dv), lambda b, h: (b, 2 * DA_HEADS + h)),
            pl.BlockSpec((1, FAR_TILE + 1, tq, tq), lambda b, h: (h, 0, 0, 0)),
            pl.BlockSpec((1, LANES), lambda b, h: (0, 0)),
            pl.BlockSpec((dv, 1), lambda b, h: (0, 0)),
        ] + rider_in,
        out_specs=[pl.BlockSpec((seq, dv), lambda b, h: (b, h))] + rider_out,
        out_shape=[jax.ShapeDtypeStruct((batch * seq, DA_HEADS * dv), BF16)] + rider_shapes,
        scratch_shapes=[
            pltpu.VMEM((nq, dv + ONES_ROWS, tq), BF16),
            pltpu.VMEM((2, 1, tq), F32),
            pltpu.VMEM((2, dv + ONES_ROWS, tq), F32),
            pltpu.VMEM((2 * tq // min(DA_Q_SUB, tq), tq, min(DA_Q_SUB, tq)), F32),
        ],
        compiler_params=_params("parallel", "parallel"),
        name="diff_attention",
    )(proj, proj, proj, bias_t, lam_row, nw.reshape(dv, 1), *riders)


def _exact_bf16_dot(tri3, x):
    hi = x.astype(BF16)
    r1 = x - hi.astype(F32)
    mid = r1.astype(BF16)
    lo = (r1 - mid.astype(F32)).astype(BF16)
    return jnp.dot(tri3, jnp.concatenate([hi, mid, lo], axis=0), preferred_element_type=F32)


def _hgrn_tables(c):
    n_lv = int(math.log2(c))
    r = np.arange(c)[None, :]
    t = np.arange(c)[:, None]
    blocks_d, blocks_e = [], []
    for lv in range(2, HG_SMALL_LEVELS + 1):
        h = 1 << (lv - 1)
        upper = (t // h) % 2 == 1
        mid_t = (t // h) * h
        blocks_d.append(upper & (r >= mid_t) & (r <= t))
        mid_s = (t // h + 1) * h
        blocks_e.append((~upper) & (r > t) & (r < mid_s))
    stack = np.concatenate(blocks_d + blocks_e + [r <= t], axis=0).astype(np.float32)
    x = np.bitwise_xor(t, r)
    level = np.where(r == t, 0, np.floor(np.log2(np.maximum(x, 1))).astype(np.int32) + 1)
    level = np.where(r > t, -1, level).astype(np.int32)
    return stack, level, n_lv


def _hgrn_kernel(q_ref, f_ref, i_ref, g_ref, lb_ref, nw_ref, tst_ref, lvl_ref, o_ref, st_ref,
                 *, ts, c, n_lv):
    nt = (((1,), (1,)), ((), ()))
    d = LANES
    nh = q_ref.shape[1] // d
    nch = ts // c
    n_small = HG_SMALL_LEVELS

    @pl.when(pl.program_id(2) == 0)
    def _():
        st_ref[...] = jnp.zeros(st_ref.shape, F32)

    lb = lb_ref[...]
    f = lb + (1.0 - lb) * jax.nn.sigmoid(f_ref[...])
    logf = jnp.log(f)
    kk = 1.0 - f
    units = [(ch, hd) for ch in range(nch) for hd in range(nh)]
    lcat = jnp.concatenate([logf[ch * c:(ch + 1) * c, hd * d:(hd + 1) * d] for ch, hd in units], axis=1)
    seg = _exact_bf16_dot(tst_ref[...], lcat)
    lvl = lvl_ref[...]
    nw = nw_ref[...]

    st = [st_ref[hd] for hd in range(nh)]
    for u, (ch, hd) in enumerate(units):
        rows = slice(ch * c, (ch + 1) * c)
        cols = slice(hd * d, (hd + 1) * d)

        def blk(b, u=u):
            return seg[b * c:(b + 1) * c, u * d:(u + 1) * d]

        g_in = blk(2 * n_small - 2)
        g_last = g_in[c - 1:c]
        g_out = g_last - g_in
        q = q_ref[rows, cols]
        q32 = q.astype(F32)
        kc = kk[rows, cols]
        kb = kc.astype(BF16)
        vb = i_ref[rows, cols].astype(BF16)
        a = jnp.where(lvl == 0, lax.dot_general(q, kb, nt, preferred_element_type=F32), 0.0)
        qt = (q32 * f[rows, cols]).astype(BF16)
        a = jnp.where(lvl == 1, lax.dot_general(qt, kb, nt, preferred_element_type=F32), a)
        for lv in range(2, n_lv + 1):
            if lv <= n_small:
                d_up = blk(lv - 2)
                d_lo = blk(n_small + lv - 3)
            else:
                h = 1 << (lv - 1)
                ref = jnp.concatenate(
                    [jnp.broadcast_to(g_in[gi * 2 * h + h - 1:gi * 2 * h + h, :], (2 * h, d))
                     for gi in range(c // (2 * h))], axis=0)
                d_up = jnp.minimum(g_in - ref, 0.0)
                d_lo = jnp.minimum(ref - g_in, 0.0)
            qt = (q32 * jnp.exp(d_up)).astype(BF16)
            kt = (kc * jnp.exp(d_lo)).astype(BF16)
            a = jnp.where(lvl == lv, lax.dot_general(qt, kt, nt, preferred_element_type=F32), a)
        qg = (q32 * jnp.exp(g_in)).astype(BF16)
        o = (jnp.dot(a.astype(BF16), vb, preferred_element_type=F32)
             + lax.dot_general(qg, st[hd].astype(BF16), nt, preferred_element_type=F32))
        kd = (kc * jnp.exp(g_out)).astype(BF16)
        upd = lax.dot_general(vb, kd, (((0,), (0,)), ((), ())), preferred_element_type=F32)
        st[hd] = st[hd] * jnp.exp(g_last) + upd
        g = g_ref[rows, cols]
        o_ref[rows, cols] = (_rms_rows(o, nw) * (g * jax.nn.sigmoid(g))).astype(o_ref.dtype)
    for hd in range(nh):
        st_ref[hd] = st[hd]


def hgrn2(proj_a, proj_b, lb, nw, batch, seq, ts, q_col0, riders=(), layer=0):
    ns = seq // ts
    d = LANES
    nh = math.gcd(HG_HEADS_PER_STEP, HG_HEADS)
    assert q_col0 % nh == 0
    w = nh * d
    ng = HG_HEADS // nh
    c = min(HG_CHUNK, ts)
    stack, level, n_lv = _hgrn_tables(c)
    assert n_lv >= HG_SMALL_LEVELS
    grid = (batch, ng, ns)
    rider_in, rider_out, rider_shapes = _rider_specs(riders, layer, grid)
    kern = _with_cast_riders(functools.partial(_hgrn_kernel, ts=ts, c=c, n_lv=n_lv),
                             n_in=8, n_out=1, n_riders=len(riders))
    return pl.pallas_call(
        kern,
        grid=grid,
        in_specs=[
            pl.BlockSpec((ts, w), lambda b, h, i: (b * ns + i, q_col0 // nh + h)),
            pl.BlockSpec((ts, w), lambda b, h, i: (b * ns + i, h)),
            pl.BlockSpec((ts, w), lambda b, h, i: (b * ns + i, ng + h)),
            pl.BlockSpec((ts, w), lambda b, h, i: (b * ns + i, 2 * ng + h)),
            pl.BlockSpec((1, w), lambda b, h, i: (0, h)),
            pl.BlockSpec((1, d), lambda b, h, i: (0, 0)),
            pl.BlockSpec((stack.shape[0], 3 * stack.shape[1]), lambda b, h, i: (0, 0)),
            pl.BlockSpec(level.shape, lambda b, h, i: (0, 0)),
        ] + rider_in,
        out_specs=[pl.BlockSpec((ts, w), lambda b, h, i: (b * ns + i, h))] + rider_out,
        out_shape=[jax.ShapeDtypeStruct((batch * seq, HG_HEADS * d), BF16)] + rider_shapes,
        scratch_shapes=[pltpu.VMEM((nh, d, d), F32)],
        compiler_params=_params("parallel", "parallel", "arbitrary"),
        name="hgrn2",
    )(proj_a, proj_b, proj_b, proj_b, lb, nw.reshape(1, d),
      jnp.asarray(np.concatenate([stack] * 3, axis=1), BF16), jnp.asarray(level), *riders)


def _ffn_up_kernel(x_ref, halo_ref, nw_ref, wa_ref, wb_ref, cwa_ref, cwb_ref, cba_ref, cbb_ref,
                   o_ref, h_ref, u_ref, *, tm, per_batch):
    hl = SUBLANES

    @pl.when(pl.program_id(1) == 0)
    def _():
        nw = nw_ref[...]
        first = (pl.program_id(0) % per_batch) == 0
        halo = _rms_rows(halo_ref[...], nw)
        h_ref[0:hl, :] = jnp.where(first, 0.0, halo).astype(h_ref.dtype)
        h_ref[hl:, :] = _rms_rows(x_ref[...], nw).astype(h_ref.dtype)

    h = h_ref[...]
    tf = o_ref.shape[1]
    cw = min(FFN_COL_CHUNK, tf)

    def cols(cc):
        return slice(cc * cw, (cc + 1) * cw)

    def matmuls(cc):
        u_ref[cc, 0] = jnp.dot(h, wa_ref[:, cols(cc)], preferred_element_type=F32)
        u_ref[cc, 1] = jnp.dot(h, wb_ref[:, cols(cc)], preferred_element_type=F32)

    def conv(cc, half, cw_ref, cb_ref):
        acc = cb_ref[:, cols(cc)][None]
        for j in range(CONV_WIDTH):
            off = hl - (CONV_WIDTH - 1) + j
            rows = u_ref[cc, half, off:off + tm, :].reshape(tm // hl, hl, cw)
            acc = acc + cw_ref[j, :, cols(cc)][None] * rows
        return acc.reshape(tm, cw)

    def gate(cc):
        a = conv(cc, 0, cwa_ref, cba_ref)
        b = conv(cc, 1, cwb_ref, cbb_ref)
        o_ref[:, cols(cc)] = ((a * jax.nn.sigmoid(a)) * b).astype(o_ref.dtype)

    for cc in range(tf // cw):
        matmuls(cc)
        gate(cc)


def ffn_up(x, nw, w_up, conv_w, conv_b, seq):
    t, d = x.shape
    f = w_up.shape[1] // 2
    tm, tf = _pick(seq, MM_TM), _pick(f, FFN_TF)
    nf = f // tf
    per_batch = seq // tm
    hl = SUBLANES
    rows_per_tile = tm // hl
    cw = min(FFN_COL_CHUNK, tf)
    conv_w8 = jnp.broadcast_to(conv_w[:, None, :], (CONV_WIDTH, hl, 2 * f))
    conv_b8 = jnp.broadcast_to(conv_b.reshape(1, 2 * f), (hl, 2 * f))
    kern = functools.partial(_ffn_up_kernel, tm=tm, per_batch=per_batch)
    return pl.pallas_call(
        kern,
        grid=(t // tm, nf),
        in_specs=[
            pl.BlockSpec((tm, d), lambda i, j: (i, 0)),
            pl.BlockSpec((hl, d), lambda i, j: (jnp.maximum(i * rows_per_tile - 1, 0), 0)),
            pl.BlockSpec((1, d), lambda i, j: (0, 0)),
            pl.BlockSpec((d, tf), lambda i, j: (0, j)),
            pl.BlockSpec((d, tf), lambda i, j: (0, nf + j)),
            pl.BlockSpec((CONV_WIDTH, hl, tf), lambda i, j: (0, 0, j)),
            pl.BlockSpec((CONV_WIDTH, hl, tf), lambda i, j: (0, 0, nf + j)),
            pl.BlockSpec((hl, tf), lambda i, j: (0, j)),
            pl.BlockSpec((hl, tf), lambda i, j: (0, nf + j)),
        ],
        out_specs=pl.BlockSpec((tm, tf), lambda i, j: (i, j)),
        out_shape=jax.ShapeDtypeStruct((t, f), BF16),
        scratch_shapes=[
            pltpu.VMEM((tm + hl, d), BF16),
            pltpu.VMEM((tf // cw, 2, tm + hl, cw), F32),
        ],
        compiler_params=_params("parallel", "arbitrary"),
        name="ffn_up",
    )(x, x, nw.reshape(1, d), w_up, w_up, conv_w8, conv_w8, conv_b8, conv_b8)


def _ffn_down_kernel(a_ref, w_ref, res_ref, nw_ref, o_ref):
    y = res_ref[...] + jnp.dot(a_ref[...], w_ref[...], preferred_element_type=F32)
    o_ref[...] = _rms_rows(y, nw_ref[...])


def ffn_down(act, w_down, res, nw):
    t, f = act.shape
    d = w_down.shape[1]
    tm = _pick(t, FFN_DOWN_TM)
    return pl.pallas_call(
        _ffn_down_kernel,
        grid=(t // tm,),
        in_specs=[
            pl.BlockSpec((tm, f), lambda i: (i, 0)),
            pl.BlockSpec((f, d), lambda i: (0, 0), pipeline_mode=pl.Buffered(1)),
            pl.BlockSpec((tm, d), lambda i: (i, 0)),
            pl.BlockSpec((1, d), lambda i: (0, 0)),
        ],
        out_specs=pl.BlockSpec((tm, d), lambda i: (i, 0)),
        out_shape=jax.ShapeDtypeStruct((t, d), F32),
        compiler_params=_params("parallel"),
        name="ffn_down",
    )(act, w_down, res, nw.reshape(1, d))


def kernel(x, mem, w_in, w_out, norm_mix_w, lam_q1, lam_k1, lam_q2, lam_k2, da_subln_w,
           hg_lb_raw, hg_norm_w, rel_bias, norm_mem_w, mem_kv_norm_w, w_mq, w_mkv, w_mo,
           norm_ffn_w, w_up, conv_w, conv_b, w_down, final_norm_w):
    batch, seq, d_model = x.shape
    n_mem = mem.shape[1]
    depth = w_in.shape[0]
    assert depth == 1, "single-layer block"
    layer = 0
    t = batch * seq

    da_width = DA_HEADS * LANES
    hg_width = HG_HEADS * LANES
    assert w_in.shape[2] == 3 * da_width + 4 * hg_width
    assert da_subln_w.shape[1] == LANES and hg_norm_w.shape[1] == LANES
    assert lam_q1.shape[1] * 2 == LANES
    n_bf = 3 * da_width + hg_width

    tq = _pick(seq, DA_BLOCK)
    ts = _pick(seq, HG_TILE)

    lam_init = 0.8 - 0.6 * math.exp(-0.3 * layer)
    lam_vecs = jnp.stack([lam_q1[layer], lam_k1[layer], lam_q2[layer], lam_k2[layer]]).astype(F32)
    lam_row, lb = param_tables(lam_vecs, hg_lb_raw.astype(F32), lam_init, layer)

    far = _t5_bucket_np(np.arange(tq + 1, max(seq, tq + 2)))
    assert (far == far[0]).all(), "bias must be constant beyond the sub-diagonal block"
    c = np.arange(2 * tq)
    dist_buckets = np.stack([np.where(c >= tq, _t5_bucket_np(np.maximum(c - tq, 0)), -1),
                             _t5_bucket_np(c)]).astype(np.int32)
    bias_t = bias_tiles(rel_bias.astype(F32), jnp.asarray(dist_buckets), tq, int(far[0]))

    x2d = x.reshape(t, d_model)
    q_scale = (LANES // 2) ** -0.5
    col_scale = np.ones((1, n_bf), np.float32)
    col_scale[:, :da_width] = q_scale
    proj_a, proj_b = in_proj(x2d, norm_mix_w[layer], w_in[layer].astype(BF16), n_bf, jnp.asarray(col_scale))

    da_o, w_up_bf, w_mkv_bf = diff_attention(
        proj_a, bias_t, lam_row, da_subln_w[layer], batch, seq, tq, 1.0 - lam_init,
        riders=(w_up, w_mkv), layer=layer)
    hg_o, w_down_bf, w_out_bf, w_mq_bf, w_mo_bf = hgrn2(
        proj_a, proj_b, lb, hg_norm_w[layer], batch, seq, ts, 3 * DA_HEADS,
        riders=(w_down, w_out, w_mq, w_mo), layer=layer)

    mkv = norm_matmul(mem.reshape(batch * n_mem, d_model), mem_kv_norm_w[layer], w_mkv_bf, BF16,
                      name="mem_kv_proj")
    x2 = mix_out_mem_attention(da_o, hg_o, w_out_bf, x2d, norm_mem_w[layer], w_mq_bf, mkv, w_mo_bf,
                               seq, n_mem)

    act = ffn_up(x2, norm_ffn_w[layer], w_up_bf, conv_w[layer], conv_b[layer], seq)
    out = ffn_down(act, w_down_bf, x2, final_norm_w)
    return out.reshape(batch, seq, d_model)
```

```python
import functools
import math

import numpy as np
import jax
import jax.numpy as jnp
from jax import lax
from jax.experimental import pallas as pl
from jax.experimental.pallas import tpu as pltpu

DA_HEADS = 8
HG_HEADS = 8
MEM_HEADS = 4
NUM_BUCKETS = 32
MAX_DISTANCE = 128
CONV_WIDTH = 3
EPS = 1e-6

LANES = 128
SUBLANES = 8
VMEM_LIMIT_BYTES = 56 * 1024 * 1024

HG_CHUNK = 128
HG_TILE = 512
HG_HEADS_PER_STEP = 4
HG_SMALL_LEVELS = 3
DA_BLOCK = 512
DA_Q_SUB = 512
MM_TM = 1024
MM_TN = 1024
FFN_TF = 512
FFN_COL_CHUNK = 512
RESIDENT_TM = 512
FFN_DOWN_TM = RESIDENT_TM
NEG_BIG = -1e30

F32 = jnp.float32
BF16 = jnp.bfloat16


def _params(*sem):
    return pltpu.CompilerParams(dimension_semantics=sem, vmem_limit_bytes=VMEM_LIMIT_BYTES)


def _rms_rows(x, w):
    ms = jnp.mean(x * x, axis=-1, keepdims=True)
    return (x * lax.rsqrt(ms + EPS)) * w


def _pick(n, pref):
    t = min(pref, n)
    while n % t:
        t //= 2
    return t


def _norm_mm_kernel(x_ref, nw_ref, w_ref, o_ref, h_ref):
    @pl.when(pl.program_id(1) == 0)
    def _():
        h_ref[...] = _rms_rows(x_ref[...], nw_ref[...]).astype(h_ref.dtype)

    o_ref[...] = jnp.dot(h_ref[...], w_ref[...], preferred_element_type=F32).astype(o_ref.dtype)


def norm_matmul(x, nw, w, out_dtype, name="norm_matmul"):
    t, k = x.shape
    n = w.shape[1]
    tm, tn = _pick(t, MM_TM), _pick(n, MM_TN)
    return pl.pallas_call(
        _norm_mm_kernel,
        grid=(t // tm, n // tn),
        in_specs=[
            pl.BlockSpec((tm, k), lambda i, j: (i, 0)),
            pl.BlockSpec((1, k), lambda i, j: (0, 0)),
            pl.BlockSpec((k, tn), lambda i, j: (0, j)),
        ],
        out_specs=pl.BlockSpec((tm, tn), lambda i, j: (i, j)),
        out_shape=jax.ShapeDtypeStruct((t, n), out_dtype),
        scratch_shapes=[pltpu.VMEM((tm, k), BF16)],
        compiler_params=_params("parallel", "arbitrary"),
        name=name,
    )(x, nw.reshape(1, k), w)


BF16_ROWS = 16


def _with_cast_riders(body, n_in, n_out, n_riders):
    def kern(*refs):
        ins, rest = refs[:n_in], refs[n_in:]
        rid_in, rest = rest[:n_riders], rest[n_riders:]
        outs, rest = rest[:n_out], rest[n_out:]
        rid_out, scratch = rest[:n_riders], rest[n_riders:]
        for src, dst in zip(rid_in, rid_out):
            dst[...] = src[...].astype(dst.dtype)
        body(*ins, *outs, *scratch)
    return kern


def _rider_specs(riders, layer, grid):
    n_steps = math.prod(grid)

    def step(*g):
        idx = 0
        for gi, n in zip(g, grid):
            idx = idx * n + gi
        return idx

    in_specs, out_specs, shapes = [], [], []
    for a in riders:
        _, rows, cols = a.shape
        n_slabs = math.gcd(n_steps, rows // BF16_ROWS)
        assert rows % BF16_ROWS == 0 and n_steps % n_slabs == 0, (a.shape, grid)
        hold = n_steps // n_slabs
        in_specs.append(pl.BlockSpec((None, rows // n_slabs, cols),
                                     lambda *g, hold=hold: (layer, step(*g) // hold, 0)))
        out_specs.append(pl.BlockSpec((rows // n_slabs, cols),
                                      lambda *g, hold=hold: (step(*g) // hold, 0)))
        shapes.append(jax.ShapeDtypeStruct((rows, cols), BF16))
    return in_specs, out_specs, shapes


def _in_proj_kernel(x_ref, nw_ref, w_ref, cs_ref, oa_ref, ob_ref, h_ref, *, na_tiles):
    j = pl.program_id(1)

    @pl.when(j == 0)
    def _():
        h_ref[...] = _rms_rows(x_ref[...], nw_ref[...]).astype(h_ref.dtype)

    y = jnp.dot(h_ref[...], w_ref[...], preferred_element_type=F32)

    @pl.when(j < na_tiles)
    def _():
        oa_ref[...] = (y * cs_ref[...]).astype(oa_ref.dtype)

    @pl.when(j >= na_tiles)
    def _():
        ob_ref[...] = y


def in_proj(x, nw, w, n_a, col_scale):
    t, k = x.shape
    n = w.shape[1]
    tm, tn = _pick(t, MM_TM), math.gcd(_pick(n_a, MM_TN), _pick(n - n_a, MM_TN))
    na_tiles = n_a // tn
    return pl.pallas_call(
        functools.partial(_in_proj_kernel, na_tiles=na_tiles),
        grid=(t // tm, n // tn),
        in_specs=[
            pl.BlockSpec((tm, k), lambda i, j: (i, 0)),
            pl.BlockSpec((1, k), lambda i, j: (0, 0)),
            pl.BlockSpec((k, tn), lambda i, j: (0, j)),
            pl.BlockSpec((1, tn), lambda i, j: (0, jnp.minimum(j, na_tiles - 1))),
        ],
        out_specs=[
            pl.BlockSpec((tm, tn), lambda i, j: (i, jnp.minimum(j, na_tiles - 1))),
            pl.BlockSpec((tm, tn), lambda i, j: (i, jnp.maximum(j - na_tiles, 0))),
        ],
        out_shape=(jax.ShapeDtypeStruct((t, n_a), BF16), jax.ShapeDtypeStruct((t, n - n_a), F32)),
        scratch_shapes=[pltpu.VMEM((tm, k), BF16)],
        compiler_params=_params("parallel", "arbitrary"),
        name="in_proj",
    )(x, nw.reshape(1, k), w, col_scale)


def _mix_mem_kernel(a0_ref, a1_ref, w0_ref, w1_ref, x_ref, nw_ref, wq_ref, k_ref, v_ref, wo_ref,
                    o_ref, att_ref, *, scale):
    x1 = x_ref[...]
    x1 = x1 + jnp.dot(a0_ref[...], w0_ref[...], preferred_element_type=F32)
    x1 = x1 + jnp.dot(a1_ref[...], w1_ref[...], preferred_element_type=F32)
    hq = _rms_rows(x1, nw_ref[...]).astype(BF16)
    mq = jnp.dot(hq, wq_ref[...], preferred_element_type=F32).astype(BF16)
    dh = mq.shape[1] // MEM_HEADS
    for hd in range(MEM_HEADS):
        sl = slice(hd * dh, (hd + 1) * dh)
        s = lax.dot_general(mq[:, sl], k_ref[:, sl], (((1,), (1,)), ((), ())),
                            preferred_element_type=F32) * scale
        e = jnp.exp(s - jnp.max(s, axis=-1, keepdims=True))
        p = e / jnp.sum(e, axis=-1, keepdims=True)
        att_ref[:, sl] = jnp.dot(p.astype(BF16), v_ref[:, sl],
                                 preferred_element_type=F32).astype(att_ref.dtype)
    o_ref[...] = x1 + jnp.dot(att_ref[...], wo_ref[...], preferred_element_type=F32)


def mix_out_mem_attention(a0, a1, w_out, x, nw_mem, w_mq, mkv, w_mo, seq, n_mem):
    t, k0 = a0.shape
    k1 = a1.shape[1]
    d = x.shape[1]
    assert k0 == k1 and w_out.shape == (k0 + k1, d)
    tm = _pick(seq, RESIDENT_TM)
    per_batch = seq // tm
    scale = (d // MEM_HEADS) ** -0.5
    once = pl.Buffered(1)
    return pl.pallas_call(
        functools.partial(_mix_mem_kernel, scale=scale),
        grid=(t // tm,),
        in_specs=[
            pl.BlockSpec((tm, k0), lambda i: (i, 0)),
            pl.BlockSpec((tm, k1), lambda i: (i, 0)),
            pl.BlockSpec((k0, d), lambda i: (0, 0), pipeline_mode=once),
            pl.BlockSpec((k1, d), lambda i: (1, 0), pipeline_mode=once),
            pl.BlockSpec((tm, d), lambda i: (i, 0)),
            pl.BlockSpec((1, d), lambda i: (0, 0)),
            pl.BlockSpec((d, d), lambda i: (0, 0), pipeline_mode=once),
            pl.BlockSpec((n_mem, d), lambda i: (i // per_batch, 0), pipeline_mode=once),
            pl.BlockSpec((n_mem, d), lambda i: (i // per_batch, 1), pipeline_mode=once),
            pl.BlockSpec((d, d), lambda i: (0, 0), pipeline_mode=once),
        ],
        out_specs=pl.BlockSpec((tm, d), lambda i: (i, 0)),
        out_shape=jax.ShapeDtypeStruct((t, d), F32),
        scratch_shapes=[pltpu.VMEM((tm, d), BF16)],
        compiler_params=_params("parallel"),
        name="mix_out_mem_attention",
    )(a0, a1, w_out, w_out, x, nw_mem.reshape(1, d), w_mq, mkv, mkv, w_mo)


def _t5_bucket_np(n):
    max_exact = NUM_BUCKETS // 2
    nf = np.maximum(n, 1).astype(np.float32)
    large = max_exact + (np.log(nf / max_exact) / math.log(MAX_DISTANCE / max_exact)
                         * (NUM_BUCKETS - max_exact)).astype(np.int32)
    large = np.minimum(large, NUM_BUCKETS - 1)
    return np.where(n < max_exact, n, large).astype(np.int32)


def _tables_kernel(lamv_ref, lbraw_ref, lam_ref, lb_ref, *, lam_init, layer):
    v = lamv_ref[...]
    s1 = jnp.sum(v[0:1] * v[1:2], axis=-1, keepdims=True)
    s2 = jnp.sum(v[2:3] * v[3:4], axis=-1, keepdims=True)
    lam = jnp.exp(s1) - jnp.exp(s2) + lam_init
    lam_ref[...] = jnp.broadcast_to(lam, lam_ref.shape)
    raw = lbraw_ref[...]
    e = jnp.exp(raw - jnp.max(raw, axis=0, keepdims=True))
    sm = e / jnp.sum(e, axis=0, keepdims=True)
    lb_ref[...] = jnp.sum(sm[0:layer + 1], axis=0, keepdims=True)


def param_tables(lam_vecs, lb_raw, lam_init, layer):
    kw = lb_raw.shape[1]
    return pl.pallas_call(
        functools.partial(_tables_kernel, lam_init=lam_init, layer=layer),
        out_shape=(jax.ShapeDtypeStruct((1, LANES), F32), jax.ShapeDtypeStruct((1, kw), F32)),
        name="param_tables",
    )(lam_vecs, lb_raw)


def _bias_tiles_kernel(rb_ref, bk_ref, o_ref, *, tq, far_bucket):
    h = pl.program_id(0)
    bk = bk_ref[...]
    vec = jnp.full(bk.shape, NEG_BIG, F32)
    for n in range(NUM_BUCKETS):
        vec = jnp.where(bk == n, rb_ref[n, h], vec)
    far = rb_ref[far_bucket, h]
    for t in range(2):
        rows = jnp.broadcast_to(vec[t:t + 1], (tq, 2 * tq))
        tile = pltpu.roll(rows, 0, 1, stride=1, stride_axis=0)[:, tq:]
        o_ref[0, t] = tile if t == 0 else tile - far
    o_ref[0, FAR_TILE] = jnp.full((tq, tq), far, F32)


def bias_tiles(rel_bias, dist_buckets, tq, far_bucket):
    nb, h = rel_bias.shape
    return pl.pallas_call(
        functools.partial(_bias_tiles_kernel, tq=tq, far_bucket=far_bucket),
        grid=(h,),
        in_specs=[
            pl.BlockSpec(memory_space=pltpu.SMEM),
            pl.BlockSpec(dist_buckets.shape, lambda i: (0, 0)),
        ],
        out_specs=pl.BlockSpec((1, FAR_TILE + 1, tq, tq), lambda i: (i, 0, 0, 0)),
        out_shape=jax.ShapeDtypeStruct((h, FAR_TILE + 1, tq, tq), F32),
        compiler_params=_params("arbitrary"),
        name="bias_tiles",
    )(rel_bias, dist_buckets)


ONES_ROWS = BF16_ROWS
FAR_TILE = 2


def _diff_attn_kernel(q_ref, k_ref, v_ref, bias_ref, lam_ref, nw_ref, o_ref,
                      vt_ref, m_ref, acc_ref, st_ref, *, tq, out_scale):
    dv = v_ref.shape[1]
    dh = dv // 2
    nq = v_ref.shape[0] // tq

    for c in range(nq):
        vt_ref[c, 0:dv, :] = v_ref[c * tq:(c + 1) * tq, :].astype(F32).T.astype(BF16)
        vt_ref[c, dv:, :] = jnp.ones((ONES_ROWS, tq), BF16)

    lane = lax.broadcasted_iota(jnp.int32, (1, dv), 1)
    qw = min(DA_Q_SUB, tq)
    chains = [(mp, slice(qs * qw, (qs + 1) * qw)) for mp in range(2) for qs in range(tq // qw)]
    corner, corner_q = slice(tq - LANES, tq), slice(0, LANES)

    def query_block(i, carry):
        rows = pl.ds(pl.multiple_of(i * tq, tq), tq)
        q = q_ref[rows, :]
        qm = (jnp.where(lane < dh, q, jnp.zeros_like(q)), jnp.where(lane >= dh, q, jnp.zeros_like(q)))

        m_ref[...] = jnp.full(m_ref.shape, NEG_BIG, F32)
        acc_ref[...] = jnp.zeros(acc_ref.shape, F32)

        def scores(n, j):
            mp, cols = chains[n]
            kb = k_ref[pl.ds(pl.multiple_of(j * tq, tq), tq), :]
            st_ref[n] = lax.dot_general(kb, qm[mp][cols, :], (((1,), (1,)), ((), ())),
                                        preferred_element_type=F32)

        far_bias = bias_ref[0, FAR_TILE, 0:1, 0:1]

        def softmax_pv(n, vt, tile):
            mp, cols = chains[n]
            if tile == 1 and cols.start == 0:
                st_ref[n, corner, corner_q] += bias_ref[0, 1, corner, corner_q]
            st = st_ref[n]
            if tile == 0:
                st = st + bias_ref[0, tile, :, cols]
                shift = 0.0
            else:
                shift = far_bias
            m_old = m_ref[mp, :, cols]
            m_new = jnp.maximum(m_old, jnp.max(st, axis=0, keepdims=True) + shift)
            p = jnp.exp(st - (m_new - shift)).astype(BF16)
            alpha = jnp.exp(m_old - m_new)
            acc_ref[mp, :, cols] = (alpha * acc_ref[mp, :, cols]
                                    + jnp.dot(vt, p, preferred_element_type=F32))
            m_ref[mp, :, cols] = m_new

        def step(j, tile, last=False):
            vt = vt_ref[j]
            for n in range(len(chains)):
                if n + 1 < len(chains):
                    scores(n + 1, j)
                elif not last:
                    scores(0, j + 1)
                softmax_pv(n, vt, tile)

        def far_step(j, c2):
            step(j, FAR_TILE)
            return c2

        scores(0, 0)
        lax.fori_loop(0, jnp.maximum(i - 1, 0), far_step, 0)

        @pl.when(i >= 1)
        def _():
            step(i - 1, 1)

        step(i, 0, last=True)

        o0 = acc_ref[0, 0:dv, :] / acc_ref[0, dv:dv + 1, :]
        o1 = acc_ref[1, 0:dv, :] / acc_ref[1, dv:dv + 1, :]
        o = o0 - lam_ref[0:1, 0:1] * o1
        ms = jnp.mean(o * o, axis=0, keepdims=True)
        y = (o * lax.rsqrt(ms + EPS)) * (nw_ref[...] * out_scale)
        o_ref[rows, :] = y.T.astype(o_ref.dtype)
        return carry

    lax.fori_loop(0, nq, query_block, 0)


def diff_attention(proj, bias_t, lam_row, nw, batch, seq, tq, out_scale, riders=(), layer=0):
    nq = seq // tq
    dv = LANES
    grid = (batch, DA_HEADS)
    rider_in, rider_out, rider_shapes = _rider_specs(riders, layer, grid)
    kern = _with_cast_riders(functools.partial(_diff_attn_kernel, tq=tq, out_scale=out_scale),
                             n_in=6, n_out=1, n_riders=len(riders))
    return pl.pallas_call(
        kern,
        grid=grid,
        in_specs=[
            pl.BlockSpec((seq, dv), lambda b, h: (b, h)),
            pl.BlockSpec((seq, dv), lambda b, h: (b, DA_HEADS + h)),
            pl.BlockSpec((seq, dv), lambda b, h: (b, 2 * DA_HEADS + h)),
            pl.BlockSpec((1, FAR_TILE + 1, tq, tq), lambda b, h: (h, 0, 0, 0)),
            pl.BlockSpec((1, LANES), lambda b, h: (0, 0)),
            pl.BlockSpec((dv, 1), lambda b, h: (0, 0)),
        ] + rider_in,
        out_specs=[pl.BlockSpec((seq, dv), lambda b, h: (b, h))] + rider_out,
        out_shape=[jax.ShapeDtypeStruct((batch * seq, DA_HEADS * dv), BF16)] + rider_shapes,
        scratch_shapes=[
            pltpu.VMEM((nq, dv + ONES_ROWS, tq), BF16),
            pltpu.VMEM((2, 1, tq), F32),
            pltpu.VMEM((2, dv + ONES_ROWS, tq), F32),
            pltpu.VMEM((2 * tq // min(DA_Q_SUB, tq), tq, min(DA_Q_SUB, tq)), F32),
        ],
        compiler_params=_params("parallel", "parallel"),
        name="diff_attention",
    )(proj, proj, proj, bias_t, lam_row, nw.reshape(dv, 1), *riders)


def _exact_bf16_dot(tri3, x):
    hi = x.astype(BF16)
    r1 = x - hi.astype(F32)
    mid = r1.astype(BF16)
    lo = (r1 - mid.astype(F32)).astype(BF16)
    return jnp.dot(tri3, jnp.concatenate([hi, mid, lo], axis=0), preferred_element_type=F32)


def _hgrn_tables(c):
    n_lv = int(math.log2(c))
    r = np.arange(c)[None, :]
    t = np.arange(c)[:, None]
    blocks_d, blocks_e = [], []
    for lv in range(2, HG_SMALL_LEVELS + 1):
        h = 1 << (lv - 1)
        upper = (t // h) % 2 == 1
        mid_t = (t // h) * h
        blocks_d.append(upper & (r >= mid_t) & (r <= t))
        mid_s = (t // h + 1) * h
        blocks_e.append((~upper) & (r > t) & (r < mid_s))
    stack = np.concatenate(blocks_d + blocks_e + [r <= t], axis=0).astype(np.float32)
    x = np.bitwise_xor(t, r)
    level = np.where(r == t, 0, np.floor(np.log2(np.maximum(x, 1))).astype(np.int32) + 1)
    level = np.where(r > t, -1, level).astype(np.int32)
    return stack, level, n_lv


def _hgrn_kernel(q_ref, f_ref, i_ref, g_ref, lb_ref, nw_ref, tst_ref, lvl_ref, o_ref, st_ref,
                 *, ts, c, n_lv):
    nt = (((1,), (1,)), ((), ()))
    d = LANES
    nh = q_ref.shape[1] // d
    nch = ts // c
    n_small = HG_SMALL_LEVELS

    @pl.when(pl.program_id(2) == 0)
    def _():
        st_ref[...] = jnp.zeros(st_ref.shape, F32)

    lb = lb_ref[...]
    f = lb + (1.0 - lb) * jax.nn.sigmoid(f_ref[...])
    logf = jnp.log(f)
    kk = 1.0 - f
    units = [(ch, hd) for ch in range(nch) for hd in range(nh)]
    lcat = jnp.concatenate([logf[ch * c:(ch + 1) * c, hd * d:(hd + 1) * d] for ch, hd in units], axis=1)
    seg = _exact_bf16_dot(tst_ref[...], lcat)
    lvl = lvl_ref[...]
    nw = nw_ref[...]

    st = [st_ref[hd] for hd in range(nh)]
    for u, (ch, hd) in enumerate(units):
        rows = slice(ch * c, (ch + 1) * c)
        cols = slice(hd * d, (hd + 1) * d)

        def blk(b, u=u):
            return seg[b * c:(b + 1) * c, u * d:(u + 1) * d]

        g_in = blk(2 * n_small - 2)
        g_last = g_in[c - 1:c]
        g_out = g_last - g_in
        q = q_ref[rows, cols]
        q32 = q.astype(F32)
        kc = kk[rows, cols]
        kb = kc.astype(BF16)
        vb = i_ref[rows, cols].astype(BF16)
        a = jnp.where(lvl == 0, lax.dot_general(q, kb, nt, preferred_element_type=F32), 0.0)
        qt = (q32 * f[rows, cols]).astype(BF16)
        a = jnp.where(lvl == 1, lax.dot_general(qt, kb, nt, preferred_element_type=F32), a)
        for lv in range(2, n_lv + 1):
            if lv <= n_small:
                d_up = blk(lv - 2)
                d_lo = blk(n_small + lv - 3)
            else:
                h = 1 << (lv - 1)
                ref = jnp.concatenate(
                    [jnp.broadcast_to(g_in[gi * 2 * h + h - 1:gi * 2 * h + h, :], (2 * h, d))
                     for gi in range(c // (2 * h))], axis=0)
                d_up = jnp.minimum(g_in - ref, 0.0)
                d_lo = jnp.minimum(ref - g_in, 0.0)
            qt = (q32 * jnp.exp(d_up)).astype(BF16)
            kt = (kc * jnp.exp(d_lo)).astype(BF16)
            a = jnp.where(lvl == lv, lax.dot_general(qt, kt, nt, preferred_element_type=F32), a)
        qg = (q32 * jnp.exp(g_in)).astype(BF16)
        o = (jnp.dot(a.astype(BF16), vb, preferred_element_type=F32)
             + lax.dot_general(qg, st[hd].astype(BF16), nt, preferred_element_type=F32))
        kd = (kc * jnp.exp(g_out)).astype(BF16)
        upd = lax.dot_general(vb, kd, (((0,), (0,)), ((), ())), preferred_element_type=F32)
        st[hd] = st[hd] * jnp.exp(g_last) + upd
        g = g_ref[rows, cols]
        o_ref[rows, cols] = (_rms_rows(o, nw) * (g * jax.nn.sigmoid(g))).astype(o_ref.dtype)
    for hd in range(nh):
        st_ref[hd] = st[hd]


def hgrn2(proj_a, proj_b, lb, nw, batch, seq, ts, q_col0, riders=(), layer=0):
    ns = seq // ts
    d = LANES
    nh = math.gcd(HG_HEADS_PER_STEP, HG_HEADS)
    assert q_col0 % nh == 0
    w = nh * d
    ng = HG_HEADS // nh
    c = min(HG_CHUNK, ts)
    stack, level, n_lv = _hgrn_tables(c)
    assert n_lv >= HG_SMALL_LEVELS
    grid = (batch, ng, ns)
    rider_in, rider_out, rider_shapes = _rider_specs(riders, layer, grid)
    kern = _with_cast_riders(functools.partial(_hgrn_kernel, ts=ts, c=c, n_lv=n_lv),
                             n_in=8, n_out=1, n_riders=len(riders))
    return pl.pallas_call(
        kern,
        grid=grid,
        in_specs=[
            pl.BlockSpec((ts, w), lambda b, h, i: (b * ns + i, q_col0 // nh + h)),
            pl.BlockSpec((ts, w), lambda b, h, i: (b * ns + i, h)),
            pl.BlockSpec((ts, w), lambda b, h, i: (b * ns + i, ng + h)),
            pl.BlockSpec((ts, w), lambda b, h, i: (b * ns + i, 2 * ng + h)),
            pl.BlockSpec((1, w), lambda b, h, i: (0, h)),
            pl.BlockSpec((1, d), lambda b, h, i: (0, 0)),
            pl.BlockSpec((stack.shape[0], 3 * stack.shape[1]), lambda b, h, i: (0, 0)),
            pl.BlockSpec(level.shape, lambda b, h, i: (0, 0)),
        ] + rider_in,
        out_specs=[pl.BlockSpec((ts, w), lambda b, h, i: (b * ns + i, h))] + rider_out,
        out_shape=[jax.ShapeDtypeStruct((batch * seq, HG_HEADS * d), BF16)] + rider_shapes,
        scratch_shapes=[pltpu.VMEM((nh, d, d), F32)],
        compiler_params=_params("parallel", "parallel", "arbitrary"),
        name="hgrn2",
    )(proj_a, proj_b, proj_b, proj_b, lb, nw.reshape(1, d),
      jnp.asarray(np.concatenate([stack] * 3, axis=1), BF16), jnp.asarray(level), *riders)


def _ffn_up_kernel(x_ref, halo_ref, nw_ref, wa_ref, wb_ref, cwa_ref, cwb_ref, cba_ref, cbb_ref,
                   o_ref, h_ref, u_ref, *, tm, per_batch):
    hl = SUBLANES

    @pl.when(pl.program_id(1) == 0)
    def _():
        nw = nw_ref[...]
        first = (pl.program_id(0) % per_batch) == 0
        halo = _rms_rows(halo_ref[...], nw)
        h_ref[0:hl, :] = jnp.where(first, 0.0, halo).astype(h_ref.dtype)
        h_ref[hl:, :] = _rms_rows(x_ref[...], nw).astype(h_ref.dtype)

    h = h_ref[...]
    tf = o_ref.shape[1]
    cw = min(FFN_COL_CHUNK, tf)

    def cols(cc):
        return slice(cc * cw, (cc + 1) * cw)

    def matmuls(cc):
        u_ref[cc, 0] = jnp.dot(h, wa_ref[:, cols(cc)], preferred_element_type=F32)
        u_ref[cc, 1] = jnp.dot(h, wb_ref[:, cols(cc)], preferred_element_type=F32)

    def conv(cc, half, cw_ref, cb_ref):
        acc = cb_ref[:, cols(cc)][None]
        for j in range(CONV_WIDTH):
            off = hl - (CONV_WIDTH - 1) + j
            rows = u_ref[cc, half, off:off + tm, :].reshape(tm // hl, hl, cw)
            acc = acc + cw_ref[j, :, cols(cc)][None] * rows
        return acc.reshape(tm, cw)

    def gate(cc):
        a = conv(cc, 0, cwa_ref, cba_ref)
        b = conv(cc, 1, cwb_ref, cbb_ref)
        o_ref[:, cols(cc)] = ((a * jax.nn.sigmoid(a)) * b).astype(o_ref.dtype)

    for cc in range(tf // cw):
        matmuls(cc)
        gate(cc)


def ffn_up(x, nw, w_up, conv_w, conv_b, seq):
    t, d = x.shape
    f = w_up.shape[1] // 2
    tm, tf = _pick(seq, MM_TM), _pick(f, FFN_TF)
    nf = f // tf
    per_batch = seq // tm
    hl = SUBLANES
    rows_per_tile = tm // hl
    cw = min(FFN_COL_CHUNK, tf)
    conv_w8 = jnp.broadcast_to(conv_w[:, None, :], (CONV_WIDTH, hl, 2 * f))
    conv_b8 = jnp.broadcast_to(conv_b.reshape(1, 2 * f), (hl, 2 * f))
    kern = functools.partial(_ffn_up_kernel, tm=tm, per_batch=per_batch)
    return pl.pallas_call(
        kern,
        grid=(t // tm, nf),
        in_specs=[
            pl.BlockSpec((tm, d), lambda i, j: (i, 0)),
            pl.BlockSpec((hl, d), lambda i, j: (jnp.maximum(i * rows_per_tile - 1, 0), 0)),
            pl.BlockSpec((1, d), lambda i, j: (0, 0)),
            pl.BlockSpec((d, tf), lambda i, j: (0, j)),
            pl.BlockSpec((d, tf), lambda i, j: (0, nf + j)),
            pl.BlockSpec((CONV_WIDTH, hl, tf), lambda i, j: (0, 0, j)),
            pl.BlockSpec((CONV_WIDTH, hl, tf), lambda i, j: (0, 0, nf + j)),
            pl.BlockSpec((hl, tf), lambda i, j: (0, j)),
            pl.BlockSpec((hl, tf), lambda i, j: (0, nf + j)),
        ],
        out_specs=pl.BlockSpec((tm, tf), lambda i, j: (i, j)),
        out_shape=jax.ShapeDtypeStruct((t, f), BF16),
        scratch_shapes=[
            pltpu.VMEM((tm + hl, d), BF16),
            pltpu.VMEM((tf // cw, 2, tm + hl, cw), F32),
        ],
        compiler_params=_params("parallel", "arbitrary"),
        name="ffn_up",
    )(x, x, nw.reshape(1, d), w_up, w_up, conv_w8, conv_w8, conv_b8, conv_b8)


def _ffn_down_kernel(a_ref, w_ref, res_ref, nw_ref, o_ref):
    y = res_ref[...] + jnp.dot(a_ref[...], w_ref[...], preferred_element_type=F32)
    o_ref[...] = _rms_rows(y, nw_ref[...])


def ffn_down(act, w_down, res, nw):
    t, f = act.shape
    d = w_down.shape[1]
    tm = _pick(t, FFN_DOWN_TM)
    return pl.pallas_call(
        _ffn_down_kernel,
        grid=(t // tm,),
        in_specs=[
            pl.BlockSpec((tm, f), lambda i: (i, 0)),
            pl.BlockSpec((f, d), lambda i: (0, 0), pipeline_mode=pl.Buffered(1)),
            pl.BlockSpec((tm, d), lambda i: (i, 0)),
            pl.BlockSpec((1, d), lambda i: (0, 0)),
        ],
        out_specs=pl.BlockSpec((tm, d), lambda i: (i, 0)),
        out_shape=jax.ShapeDtypeStruct((t, d), F32),
        compiler_params=_params("parallel"),
        name="ffn_down",
    )(act, w_down, res, nw.reshape(1, d))


def kernel(x, mem, w_in, w_out, norm_mix_w, lam_q1, lam_k1, lam_q2, lam_k2, da_subln_w,
           hg_lb_raw, hg_norm_w, rel_bias, norm_mem_w, mem_kv_norm_w, w_mq, w_mkv, w_mo,
           norm_ffn_w, w_up, conv_w, conv_b, w_down, final_norm_w):
    batch, seq, d_model = x.shape
    n_mem = mem.shape[1]
    depth = w_in.shape[0]
    assert depth == 1, "single-layer block"
    layer = 0
    t = batch * seq

    da_width = DA_HEADS * LANES
    hg_width = HG_HEADS * LANES
    assert w_in.shape[2] == 3 * da_width + 4 * hg_width
    assert da_subln_w.shape[1] == LANES and hg_norm_w.shape[1] == LANES
    assert lam_q1.shape[1] * 2 == LANES
    n_bf = 3 * da_width + hg_width

    tq = _pick(seq, DA_BLOCK)
    ts = _pick(seq, HG_TILE)

    lam_init = 0.8 - 0.6 * math.exp(-0.3 * layer)
    lam_vecs = jnp.stack([lam_q1[layer], lam_k1[layer], lam_q2[layer], lam_k2[layer]]).astype(F32)
    lam_row, lb = param_tables(lam_vecs, hg_lb_raw.astype(F32), lam_init, layer)

    far = _t5_bucket_np(np.arange(LANES + 1, max(seq, LANES + 2)))
    assert (far == far[0]).all(), "bias must be constant for distances beyond one lane tile"
    c = np.arange(2 * tq)
    dist_buckets = np.stack([np.where(c >= tq, _t5_bucket_np(np.maximum(c - tq, 0)), -1),
                             _t5_bucket_np(c)]).astype(np.int32)
    bias_t = bias_tiles(rel_bias.astype(F32), jnp.asarray(dist_buckets), tq, int(far[0]))

    x2d = x.reshape(t, d_model)
    q_scale = (LANES // 2) ** -0.5
    col_scale = np.ones((1, n_bf), np.float32)
    col_scale[:, :da_width] = q_scale
    proj_a, proj_b = in_proj(x2d, norm_mix_w[layer], w_in[layer].astype(BF16), n_bf, jnp.asarray(col_scale))

    da_o, w_up_bf, w_mkv_bf = diff_attention(
        proj_a, bias_t, lam_row, da_subln_w[layer], batch, seq, tq, 1.0 - lam_init,
        riders=(w_up, w_mkv), layer=layer)
    hg_o, w_down_bf, w_out_bf, w_mq_bf, w_mo_bf = hgrn2(
        proj_a, proj_b, lb, hg_norm_w[layer], batch, seq, ts, 3 * DA_HEADS,
        riders=(w_down, w_out, w_mq, w_mo), layer=layer)

    mkv = norm_matmul(mem.reshape(batch * n_mem, d_model), mem_kv_norm_w[layer], w_mkv_bf, BF16,
                      name="mem_kv_proj")
    x2 = mix_out_mem_attention(da_o, hg_o, w_out_bf, x2d, norm_mem_w[layer], w_mq_bf, mkv, w_mo_bf,
                               seq, n_mem)

    act = ffn_up(x2, norm_ffn_w[layer], w_up_bf, conv_w[layer], conv_b[layer], seq)
    out = ffn_down(act, w_down_bf, x2, final_norm_w)
    return out.reshape(batch, seq, d_model)
```

```python
import functools
import math

import numpy as np
import jax
import jax.numpy as jnp
from jax import lax
from jax.experimental import pallas as pl
from jax.experimental.pallas import tpu as pltpu

DA_HEADS = 8
HG_HEADS = 8
MEM_HEADS = 4
NUM_BUCKETS = 32
MAX_DISTANCE = 128
CONV_WIDTH = 3
EPS = 1e-6

LANES = 128
SUBLANES = 8
VMEM_LIMIT_BYTES = 56 * 1024 * 1024

HG_CHUNK = 128
HG_TILE = 1024
HG_HEADS_PER_STEP = 4
HG_SMALL_LEVELS = 3
DA_BLOCK = 512
DA_Q_SUB = 512
MM_TM = 1024
MM_TN = 1024
FFN_TF = 512
FFN_COL_CHUNK = 512
RESIDENT_TM = 512
FFN_DOWN_TM = RESIDENT_TM
NEG_BIG = -1e30

F32 = jnp.float32
BF16 = jnp.bfloat16


def _params(*sem):
    return pltpu.CompilerParams(dimension_semantics=sem, vmem_limit_bytes=VMEM_LIMIT_BYTES)


def _rms_rows(x, w):
    ms = jnp.mean(x * x, axis=-1, keepdims=True)
    return (x * lax.rsqrt(ms + EPS)) * w


def _pick(n, pref):
    t = min(pref, n)
    while n % t:
        t //= 2
    return t


def _norm_mm_kernel(x_ref, nw_ref, w_ref, o_ref, h_ref):
    @pl.when(pl.program_id(1) == 0)
    def _():
        h_ref[...] = _rms_rows(x_ref[...], nw_ref[...]).astype(h_ref.dtype)

    o_ref[...] = jnp.dot(h_ref[...], w_ref[...], preferred_element_type=F32).astype(o_ref.dtype)


def norm_matmul(x, nw, w, out_dtype, name="norm_matmul"):
    t, k = x.shape
    n = w.shape[1]
    tm, tn = _pick(t, MM_TM), _pick(n, MM_TN)
    return pl.pallas_call(
        _norm_mm_kernel,
        grid=(t // tm, n // tn),
        in_specs=[
            pl.BlockSpec((tm, k), lambda i, j: (i, 0)),
            pl.BlockSpec((1, k), lambda i, j: (0, 0)),
            pl.BlockSpec((k, tn), lambda i, j: (0, j)),
        ],
        out_specs=pl.BlockSpec((tm, tn), lambda i, j: (i, j)),
        out_shape=jax.ShapeDtypeStruct((t, n), out_dtype),
        scratch_shapes=[pltpu.VMEM((tm, k), BF16)],
        compiler_params=_params("parallel", "arbitrary"),
        name=name,
    )(x, nw.reshape(1, k), w)


BF16_ROWS = 16


def _with_cast_riders(body, n_in, n_out, n_riders):
    def kern(*refs):
        ins, rest = refs[:n_in], refs[n_in:]
        rid_in, rest = rest[:n_riders], rest[n_riders:]
        outs, rest = rest[:n_out], rest[n_out:]
        rid_out, scratch = rest[:n_riders], rest[n_riders:]
        for src, dst in zip(rid_in, rid_out):
            dst[...] = src[...].astype(dst.dtype)
        body(*ins, *outs, *scratch)
    return kern


def _rider_specs(riders, layer, grid):
    n_steps = math.prod(grid)

    def step(*g):
        idx = 0
        for gi, n in zip(g, grid):
            idx = idx * n + gi
        return idx

    in_specs, out_specs, shapes = [], [], []
    for a in riders:
        _, rows, cols = a.shape
        n_slabs = math.gcd(n_steps, rows // BF16_ROWS)
        assert rows % BF16_ROWS == 0 and n_steps % n_slabs == 0, (a.shape, grid)
        hold = n_steps // n_slabs
        in_specs.append(pl.BlockSpec((None, rows // n_slabs, cols),
                                     lambda *g, hold=hold: (layer, step(*g) // hold, 0)))
        out_specs.append(pl.BlockSpec((rows // n_slabs, cols),
                                      lambda *g, hold=hold: (step(*g) // hold, 0)))
        shapes.append(jax.ShapeDtypeStruct((rows, cols), BF16))
    return in_specs, out_specs, shapes


def _in_proj_kernel(x_ref, nw_ref, w_ref, cs_ref, oa_ref, ob_ref, h_ref, *, na_tiles):
    j = pl.program_id(1)

    @pl.when(j == 0)
    def _():
        h_ref[...] = _rms_rows(x_ref[...], nw_ref[...]).astype(h_ref.dtype)

    y = jnp.dot(h_ref[...], w_ref[...], preferred_element_type=F32)

    @pl.when(j < na_tiles)
    def _():
        oa_ref[...] = (y * cs_ref[...]).astype(oa_ref.dtype)

    @pl.when(j >= na_tiles)
    def _():
        ob_ref[...] = y


def in_proj(x, nw, w, n_a, col_scale):
    t, k = x.shape
    n = w.shape[1]
    tm, tn = _pick(t, MM_TM), math.gcd(_pick(n_a, MM_TN), _pick(n - n_a, MM_TN))
    na_tiles = n_a // tn
    return pl.pallas_call(
        functools.partial(_in_proj_kernel, na_tiles=na_tiles),
        grid=(t // tm, n // tn),
        in_specs=[
            pl.BlockSpec((tm, k), lambda i, j: (i, 0)),
            pl.BlockSpec((1, k), lambda i, j: (0, 0)),
            pl.BlockSpec((k, tn), lambda i, j: (0, j)),
            pl.BlockSpec((1, tn), lambda i, j: (0, jnp.minimum(j, na_tiles - 1))),
        ],
        out_specs=[
            pl.BlockSpec((tm, tn), lambda i, j: (i, jnp.minimum(j, na_tiles - 1))),
            pl.BlockSpec((tm, tn), lambda i, j: (i, jnp.maximum(j - na_tiles, 0))),
        ],
        out_shape=(jax.ShapeDtypeStruct((t, n_a), BF16), jax.ShapeDtypeStruct((t, n - n_a), F32)),
        scratch_shapes=[pltpu.VMEM((tm, k), BF16)],
        compiler_params=_params("parallel", "arbitrary"),
        name="in_proj",
    )(x, nw.reshape(1, k), w, col_scale)


def _mix_mem_kernel(a0_ref, a1_ref, w0_ref, w1_ref, x_ref, nw_ref, wq_ref, k_ref, v_ref, wo_ref,
                    o_ref, att_ref, *, scale):
    x1 = x_ref[...]
    x1 = x1 + jnp.dot(a0_ref[...], w0_ref[...], preferred_element_type=F32)
    x1 = x1 + jnp.dot(a1_ref[...], w1_ref[...], preferred_element_type=F32)
    hq = _rms_rows(x1, nw_ref[...]).astype(BF16)
    mq = jnp.dot(hq, wq_ref[...], preferred_element_type=F32).astype(BF16)
    dh = mq.shape[1] // MEM_HEADS
    for hd in range(MEM_HEADS):
        sl = slice(hd * dh, (hd + 1) * dh)
        s = lax.dot_general(mq[:, sl], k_ref[:, sl], (((1,), (1,)), ((), ())),
                            preferred_element_type=F32) * scale
        e = jnp.exp(s - jnp.max(s, axis=-1, keepdims=True))
        p = e / jnp.sum(e, axis=-1, keepdims=True)
        att_ref[:, sl] = jnp.dot(p.astype(BF16), v_ref[:, sl],
                                 preferred_element_type=F32).astype(att_ref.dtype)
    o_ref[...] = x1 + jnp.dot(att_ref[...], wo_ref[...], preferred_element_type=F32)


def mix_out_mem_attention(a0, a1, w_out, x, nw_mem, w_mq, mkv, w_mo, seq, n_mem):
    t, k0 = a0.shape
    k1 = a1.shape[1]
    d = x.shape[1]
    assert k0 == k1 and w_out.shape == (k0 + k1, d)
    tm = _pick(seq, RESIDENT_TM)
    per_batch = seq // tm
    scale = (d // MEM_HEADS) ** -0.5
    once = pl.Buffered(1)
    return pl.pallas_call(
        functools.partial(_mix_mem_kernel, scale=scale),
        grid=(t // tm,),
        in_specs=[
            pl.BlockSpec((tm, k0), lambda i: (i, 0)),
            pl.BlockSpec((tm, k1), lambda i: (i, 0)),
            pl.BlockSpec((k0, d), lambda i: (0, 0), pipeline_mode=once),
            pl.BlockSpec((k1, d), lambda i: (1, 0), pipeline_mode=once),
            pl.BlockSpec((tm, d), lambda i: (i, 0)),
            pl.BlockSpec((1, d), lambda i: (0, 0)),
            pl.BlockSpec((d, d), lambda i: (0, 0), pipeline_mode=once),
            pl.BlockSpec((n_mem, d), lambda i: (i // per_batch, 0), pipeline_mode=once),
            pl.BlockSpec((n_mem, d), lambda i: (i // per_batch, 1), pipeline_mode=once),
            pl.BlockSpec((d, d), lambda i: (0, 0), pipeline_mode=once),
        ],
        out_specs=pl.BlockSpec((tm, d), lambda i: (i, 0)),
        out_shape=jax.ShapeDtypeStruct((t, d), F32),
        scratch_shapes=[pltpu.VMEM((tm, d), BF16)],
        compiler_params=_params("parallel"),
        name="mix_out_mem_attention",
    )(a0, a1, w_out, w_out, x, nw_mem.reshape(1, d), w_mq, mkv, mkv, w_mo)


def _t5_bucket_np(n):
    max_exact = NUM_BUCKETS // 2
    nf = np.maximum(n, 1).astype(np.float32)
    large = max_exact + (np.log(nf / max_exact) / math.log(MAX_DISTANCE / max_exact)
                         * (NUM_BUCKETS - max_exact)).astype(np.int32)
    large = np.minimum(large, NUM_BUCKETS - 1)
    return np.where(n < max_exact, n, large).astype(np.int32)


def _tables_kernel(lamv_ref, lbraw_ref, lam_ref, lb_ref, *, lam_init, layer):
    v = lamv_ref[...]
    s1 = jnp.sum(v[0:1] * v[1:2], axis=-1, keepdims=True)
    s2 = jnp.sum(v[2:3] * v[3:4], axis=-1, keepdims=True)
    lam = jnp.exp(s1) - jnp.exp(s2) + lam_init
    lam_ref[...] = jnp.broadcast_to(lam, lam_ref.shape)
    raw = lbraw_ref[...]
    e = jnp.exp(raw - jnp.max(raw, axis=0, keepdims=True))
    sm = e / jnp.sum(e, axis=0, keepdims=True)
    lb_ref[...] = jnp.sum(sm[0:layer + 1], axis=0, keepdims=True)


def param_tables(lam_vecs, lb_raw, lam_init, layer):
    kw = lb_raw.shape[1]
    return pl.pallas_call(
        functools.partial(_tables_kernel, lam_init=lam_init, layer=layer),
        out_shape=(jax.ShapeDtypeStruct((1, LANES), F32), jax.ShapeDtypeStruct((1, kw), F32)),
        name="param_tables",
    )(lam_vecs, lb_raw)


def _bias_tiles_kernel(rb_ref, bk_ref, o_ref, *, tq, far_bucket):
    h = pl.program_id(0)
    bk = bk_ref[...]
    vec = jnp.full(bk.shape, NEG_BIG, F32)
    for n in range(NUM_BUCKETS):
        vec = jnp.where(bk == n, rb_ref[n, h], vec)
    far = rb_ref[far_bucket, h]
    for t in range(2):
        rows = jnp.broadcast_to(vec[t:t + 1], (tq, 2 * tq))
        tile = pltpu.roll(rows, 0, 1, stride=1, stride_axis=0)[:, tq:]
        o_ref[0, t] = tile if t == 0 else tile - far
    o_ref[0, FAR_TILE] = jnp.full((tq, tq), far, F32)


def bias_tiles(rel_bias, dist_buckets, tq, far_bucket):
    nb, h = rel_bias.shape
    return pl.pallas_call(
        functools.partial(_bias_tiles_kernel, tq=tq, far_bucket=far_bucket),
        grid=(h,),
        in_specs=[
            pl.BlockSpec(memory_space=pltpu.SMEM),
            pl.BlockSpec(dist_buckets.shape, lambda i: (0, 0)),
        ],
        out_specs=pl.BlockSpec((1, FAR_TILE + 1, tq, tq), lambda i: (i, 0, 0, 0)),
        out_shape=jax.ShapeDtypeStruct((h, FAR_TILE + 1, tq, tq), F32),
        compiler_params=_params("arbitrary"),
        name="bias_tiles",
    )(rel_bias, dist_buckets)


ONES_ROWS = BF16_ROWS
FAR_TILE = 2


def _diff_attn_kernel(q_ref, k_ref, v_ref, bias_ref, lam_ref, nw_ref, o_ref,
                      vt_ref, m_ref, acc_ref, st_ref, *, tq, out_scale):
    dv = v_ref.shape[1]
    dh = dv // 2
    nq = v_ref.shape[0] // tq

    for c in range(nq):
        vt_ref[c, 0:dv, :] = v_ref[c * tq:(c + 1) * tq, :].astype(F32).T.astype(BF16)
        vt_ref[c, dv:, :] = jnp.ones((ONES_ROWS, tq), BF16)

    lane = lax.broadcasted_iota(jnp.int32, (1, dv), 1)
    qw = min(DA_Q_SUB, tq)
    chains = [(mp, slice(qs * qw, (qs + 1) * qw)) for mp in range(2) for qs in range(tq // qw)]
    corner, corner_q = slice(tq - LANES, tq), slice(0, LANES)

    def query_block(i, carry):
        rows = pl.ds(pl.multiple_of(i * tq, tq), tq)
        q = q_ref[rows, :]
        qm = (jnp.where(lane < dh, q, jnp.zeros_like(q)), jnp.where(lane >= dh, q, jnp.zeros_like(q)))

        m_ref[...] = jnp.full(m_ref.shape, NEG_BIG, F32)
        acc_ref[...] = jnp.zeros(acc_ref.shape, F32)

        def scores(n, j):
            mp, cols = chains[n]
            kb = k_ref[pl.ds(pl.multiple_of(j * tq, tq), tq), :]
            st_ref[n] = lax.dot_general(kb, qm[mp][cols, :], (((1,), (1,)), ((), ())),
                                        preferred_element_type=F32)

        far_bias = bias_ref[0, FAR_TILE, 0:1, 0:1]

        def softmax_pv(n, vt, tile):
            mp, cols = chains[n]
            if tile == 1 and cols.start == 0:
                st_ref[n, corner, corner_q] += bias_ref[0, 1, corner, corner_q]
            st = st_ref[n]
            if tile == 0:
                st = st + bias_ref[0, tile, :, cols]
                shift = 0.0
            else:
                shift = far_bias
            m_old = m_ref[mp, :, cols]
            m_new = jnp.maximum(m_old, jnp.max(st, axis=0, keepdims=True) + shift)
            p = jnp.exp(st - (m_new - shift)).astype(BF16)
            alpha = jnp.exp(m_old - m_new)
            acc_ref[mp, :, cols] = (alpha * acc_ref[mp, :, cols]
                                    + jnp.dot(vt, p, preferred_element_type=F32))
            m_ref[mp, :, cols] = m_new

        def step(j, tile, last=False):
            vt = vt_ref[j]
            for n in range(len(chains)):
                if n + 1 < len(chains):
                    scores(n + 1, j)
                elif not last:
                    scores(0, j + 1)
                softmax_pv(n, vt, tile)

        def far_step(j, c2):
            step(j, FAR_TILE)
            return c2

        scores(0, 0)
        lax.fori_loop(0, jnp.maximum(i - 1, 0), far_step, 0)

        @pl.when(i >= 1)
        def _():
            step(i - 1, 1)

        step(i, 0, last=True)

        o0 = acc_ref[0, 0:dv, :] / acc_ref[0, dv:dv + 1, :]
        o1 = acc_ref[1, 0:dv, :] / acc_ref[1, dv:dv + 1, :]
        o = o0 - lam_ref[0:1, 0:1] * o1
        ms = jnp.mean(o * o, axis=0, keepdims=True)
        y = (o * lax.rsqrt(ms + EPS)) * (nw_ref[...] * out_scale)
        o_ref[rows, :] = y.T.astype(o_ref.dtype)
        return carry

    lax.fori_loop(0, nq, query_block, 0)


def diff_attention(proj, bias_t, lam_row, nw, batch, seq, tq, out_scale, riders=(), layer=0):
    nq = seq // tq
    dv = LANES
    grid = (batch, DA_HEADS)
    rider_in, rider_out, rider_shapes = _rider_specs(riders, layer, grid)
    kern = _with_cast_riders(functools.partial(_diff_attn_kernel, tq=tq, out_scale=out_scale),
                             n_in=6, n_out=1, n_riders=len(riders))
    return pl.pallas_call(
        kern,
        grid=grid,
        in_specs=[
            pl.BlockSpec((seq, dv), lambda b, h: (b, h)),
            pl.BlockSpec((seq, dv), lambda b, h: (b, DA_HEADS + h)),
            pl.BlockSpec((seq, dv), lambda b, h: (b, 2 * DA_HEADS + h)),
            pl.BlockSpec((1, FAR_TILE + 1, tq, tq), lambda b, h: (h, 0, 0, 0)),
            pl.BlockSpec((1, LANES), lambda b, h: (0, 0)),
            pl.BlockSpec((dv, 1), lambda b, h: (0, 0)),
        ] + rider_in,
        out_specs=[pl.BlockSpec((seq, dv), lambda b, h: (b, h))] + rider_out,
        out_shape=[jax.ShapeDtypeStruct((batch * seq, DA_HEADS * dv), BF16)] + rider_shapes,
        scratch_shapes=[
            pltpu.VMEM((nq, dv + ONES_ROWS, tq), BF16),
            pltpu.VMEM((2, 1, tq), F32),
            pltpu.VMEM((2, dv + ONES_ROWS, tq), F32),
            pltpu.VMEM((2 * tq // min(DA_Q_SUB, tq), tq, min(DA_Q_SUB, tq)), F32),
        ],
        compiler_params=_params("parallel", "parallel"),
        name="diff_attention",
    )(proj, proj, proj, bias_t, lam_row, nw.reshape(dv, 1), *riders)


def _exact_bf16_dot(tri3, x):
    hi = x.astype(BF16)
    r1 = x - hi.astype(F32)
    mid = r1.astype(BF16)
    lo = (r1 - mid.astype(F32)).astype(BF16)
    return jnp.dot(tri3, jnp.concatenate([hi, mid, lo], axis=0), preferred_element_type=F32)


def _hgrn_tables(c):
    n_lv = int(math.log2(c))
    r = np.arange(c)[None, :]
    t = np.arange(c)[:, None]
    blocks_d, blocks_e = [], []
    for lv in range(2, HG_SMALL_LEVELS + 1):
        h = 1 << (lv - 1)
        upper = (t // h) % 2 == 1
        mid_t = (t // h) * h
        blocks_d.append(upper & (r >= mid_t) & (r <= t))
        mid_s = (t // h + 1) * h
        blocks_e.append((~upper) & (r > t) & (r < mid_s))
    stack = np.concatenate(blocks_d + blocks_e + [r <= t], axis=0).astype(np.float32)
    x = np.bitwise_xor(t, r)
    level = np.where(r == t, 0, np.floor(np.log2(np.maximum(x, 1))).astype(np.int32) + 1)
    level = np.where(r > t, -1, level).astype(np.int32)
    return stack, level, n_lv


def _hgrn_kernel(q_ref, f_ref, i_ref, g_ref, lb_ref, nw_ref, tst_ref, lvl_ref, o_ref, st_ref,
                 *, ts, c, n_lv):
    nt = (((1,), (1,)), ((), ()))
    d = LANES
    nh = q_ref.shape[1] // d
    nch = ts // c
    n_small = HG_SMALL_LEVELS

    @pl.when(pl.program_id(2) == 0)
    def _():
        st_ref[...] = jnp.zeros(st_ref.shape, F32)

    lb = lb_ref[...]
    f = lb + (1.0 - lb) * jax.nn.sigmoid(f_ref[...])
    logf = jnp.log(f)
    kk = 1.0 - f
    units = [(ch, hd) for ch in range(nch) for hd in range(nh)]
    lcat = jnp.concatenate([logf[ch * c:(ch + 1) * c, hd * d:(hd + 1) * d] for ch, hd in units], axis=1)
    seg = _exact_bf16_dot(tst_ref[...], lcat)
    lvl = lvl_ref[...]
    nw = nw_ref[...]

    st = [st_ref[hd] for hd in range(nh)]
    for u, (ch, hd) in enumerate(units):
        rows = slice(ch * c, (ch + 1) * c)
        cols = slice(hd * d, (hd + 1) * d)

        def blk(b, u=u):
            return seg[b * c:(b + 1) * c, u * d:(u + 1) * d]

        g_in = blk(2 * n_small - 2)
        g_last = g_in[c - 1:c]
        g_out = g_last - g_in
        q = q_ref[rows, cols]
        q32 = q.astype(F32)
        kc = kk[rows, cols]
        kb = kc.astype(BF16)
        vb = i_ref[rows, cols].astype(BF16)
        a = jnp.where(lvl == 0, lax.dot_general(q, kb, nt, preferred_element_type=F32), 0.0)
        qt = (q32 * f[rows, cols]).astype(BF16)
        a = jnp.where(lvl == 1, lax.dot_general(qt, kb, nt, preferred_element_type=F32), a)
        for lv in range(2, n_lv + 1):
            if lv <= n_small:
                d_up = blk(lv - 2)
                d_lo = blk(n_small + lv - 3)
            else:
                h = 1 << (lv - 1)
                ref = jnp.concatenate(
                    [jnp.broadcast_to(g_in[gi * 2 * h + h - 1:gi * 2 * h + h, :], (2 * h, d))
                     for gi in range(c // (2 * h))], axis=0)
                d_up = jnp.minimum(g_in - ref, 0.0)
                d_lo = jnp.minimum(ref - g_in, 0.0)
            qt = (q32 * jnp.exp(d_up)).astype(BF16)
            kt = (kc * jnp.exp(d_lo)).astype(BF16)
            a = jnp.where(lvl == lv, lax.dot_general(qt, kt, nt, preferred_element_type=F32), a)
        qg = (q32 * jnp.exp(g_in)).astype(BF16)
        o = (jnp.dot(a.astype(BF16), vb, preferred_element_type=F32)
             + lax.dot_general(qg, st[hd].astype(BF16), nt, preferred_element_type=F32))
        kd = (kc * jnp.exp(g_out)).astype(BF16)
        upd = lax.dot_general(vb, kd, (((0,), (0,)), ((), ())), preferred_element_type=F32)
        st[hd] = st[hd] * jnp.exp(g_last) + upd
        g = g_ref[rows, cols]
        o_ref[rows, cols] = (_rms_rows(o, nw) * (g * jax.nn.sigmoid(g))).astype(o_ref.dtype)
    for hd in range(nh):
        st_ref[hd] = st[hd]


def hgrn2(proj_a, proj_b, lb, nw, batch, seq, ts, q_col0, riders=(), layer=0):
    ns = seq // ts
    d = LANES
    nh = math.gcd(HG_HEADS_PER_STEP, HG_HEADS)
    assert q_col0 % nh == 0
    w = nh * d
    ng = HG_HEADS // nh
    c = min(HG_CHUNK, ts)
    stack, level, n_lv = _hgrn_tables(c)
    assert n_lv >= HG_SMALL_LEVELS
    grid = (batch, ng, ns)
    rider_in, rider_out, rider_shapes = _rider_specs(riders, layer, grid)
    kern = _with_cast_riders(functools.partial(_hgrn_kernel, ts=ts, c=c, n_lv=n_lv),
                             n_in=8, n_out=1, n_riders=len(riders))
    return pl.pallas_call(
        kern,
        grid=grid,
        in_specs=[
            pl.BlockSpec((ts, w), lambda b, h, i: (b * ns + i, q_col0 // nh + h)),
            pl.BlockSpec((ts, w), lambda b, h, i: (b * ns + i, h)),
            pl.BlockSpec((ts, w), lambda b, h, i: (b * ns + i, ng + h)),
            pl.BlockSpec((ts, w), lambda b, h, i: (b * ns + i, 2 * ng + h)),
            pl.BlockSpec((1, w), lambda b, h, i: (0, h)),
            pl.BlockSpec((1, d), lambda b, h, i: (0, 0)),
            pl.BlockSpec((stack.shape[0], 3 * stack.shape[1]), lambda b, h, i: (0, 0)),
            pl.BlockSpec(level.shape, lambda b, h, i: (0, 0)),
        ] + rider_in,
        out_specs=[pl.BlockSpec((ts, w), lambda b, h, i: (b * ns + i, h))] + rider_out,
        out_shape=[jax.ShapeDtypeStruct((batch * seq, HG_HEADS * d), BF16)] + rider_shapes,
        scratch_shapes=[pltpu.VMEM((nh, d, d), F32)],
        compiler_params=_params("parallel", "parallel", "arbitrary"),
        name="hgrn2",
    )(proj_a, proj_b, proj_b, proj_b, lb, nw.reshape(1, d),
      jnp.asarray(np.concatenate([stack] * 3, axis=1), BF16), jnp.asarray(level), *riders)


def _ffn_up_kernel(x_ref, halo_ref, nw_ref, wa_ref, wb_ref, cwa_ref, cwb_ref, cba_ref, cbb_ref,
                   o_ref, h_ref, u_ref, *, tm, per_batch):
    hl = SUBLANES

    @pl.when(pl.program_id(1) == 0)
    def _():
        nw = nw_ref[...]
        first = (pl.program_id(0) % per_batch) == 0
        halo = _rms_rows(halo_ref[...], nw)
        h_ref[0:hl, :] = jnp.where(first, 0.0, halo).astype(h_ref.dtype)
        h_ref[hl:, :] = _rms_rows(x_ref[...], nw).astype(h_ref.dtype)

    h = h_ref[...]
    tf = o_ref.shape[1]
    cw = min(FFN_COL_CHUNK, tf)

    def cols(cc):
        return slice(cc * cw, (cc + 1) * cw)

    def matmuls(cc):
        u_ref[cc, 0] = jnp.dot(h, wa_ref[:, cols(cc)], preferred_element_type=F32)
        u_ref[cc, 1] = jnp.dot(h, wb_ref[:, cols(cc)], preferred_element_type=F32)

    def conv(cc, half, cw_ref, cb_ref):
        acc = cb_ref[:, cols(cc)][None]
        for j in range(CONV_WIDTH):
            off = hl - (CONV_WIDTH - 1) + j
            rows = u_ref[cc, half, off:off + tm, :].reshape(tm // hl, hl, cw)
            acc = acc + cw_ref[j, :, cols(cc)][None] * rows
        return acc.reshape(tm, cw)

    def gate(cc):
        a = conv(cc, 0, cwa_ref, cba_ref)
        b = conv(cc, 1, cwb_ref, cbb_ref)
        o_ref[:, cols(cc)] = ((a * jax.nn.sigmoid(a)) * b).astype(o_ref.dtype)

    for cc in range(tf // cw):
        matmuls(cc)
        gate(cc)


def ffn_up(x, nw, w_up, conv_w, conv_b, seq):
    t, d = x.shape
    f = w_up.shape[1] // 2
    tm, tf = _pick(seq, MM_TM), _pick(f, FFN_TF)
    nf = f // tf
    per_batch = seq // tm
    hl = SUBLANES
    rows_per_tile = tm // hl
    cw = min(FFN_COL_CHUNK, tf)
    conv_w8 = jnp.broadcast_to(conv_w[:, None, :], (CONV_WIDTH, hl, 2 * f))
    conv_b8 = jnp.broadcast_to(conv_b.reshape(1, 2 * f), (hl, 2 * f))
    kern = functools.partial(_ffn_up_kernel, tm=tm, per_batch=per_batch)
    return pl.pallas_call(
        kern,
        grid=(t // tm, nf),
        in_specs=[
            pl.BlockSpec((tm, d), lambda i, j: (i, 0)),
            pl.BlockSpec((hl, d), lambda i, j: (jnp.maximum(i * rows_per_tile - 1, 0), 0)),
            pl.BlockSpec((1, d), lambda i, j: (0, 0)),
            pl.BlockSpec((d, tf), lambda i, j: (0, j)),
            pl.BlockSpec((d, tf), lambda i, j: (0, nf + j)),
            pl.BlockSpec((CONV_WIDTH, hl, tf), lambda i, j: (0, 0, j)),
            pl.BlockSpec((CONV_WIDTH, hl, tf), lambda i, j: (0, 0, nf + j)),
            pl.BlockSpec((hl, tf), lambda i, j: (0, j)),
            pl.BlockSpec((hl, tf), lambda i, j: (0, nf + j)),
        ],
        out_specs=pl.BlockSpec((tm, tf), lambda i, j: (i, j)),
        out_shape=jax.ShapeDtypeStruct((t, f), BF16),
        scratch_shapes=[
            pltpu.VMEM((tm + hl, d), BF16),
            pltpu.VMEM((tf // cw, 2, tm + hl, cw), F32),
        ],
        compiler_params=_params("parallel", "arbitrary"),
        name="ffn_up",
    )(x, x, nw.reshape(1, d), w_up, w_up, conv_w8, conv_w8, conv_b8, conv_b8)


def _ffn_down_kernel(a_ref, w_ref, res_ref, nw_ref, o_ref):
    y = res_ref[...] + jnp.dot(a_ref[...], w_ref[...], preferred_element_type=F32)
    o_ref[...] = _rms_rows(y, nw_ref[...])


def ffn_down(act, w_down, res, nw):
    t, f = act.shape
    d = w_down.shape[1]
    tm = _pick(t, FFN_DOWN_TM)
    return pl.pallas_call(
        _ffn_down_kernel,
        grid=(t // tm,),
        in_specs=[
            pl.BlockSpec((tm, f), lambda i: (i, 0)),
            pl.BlockSpec((f, d), lambda i: (0, 0), pipeline_mode=pl.Buffered(1)),
            pl.BlockSpec((tm, d), lambda i: (i, 0)),
            pl.BlockSpec((1, d), lambda i: (0, 0)),
        ],
        out_specs=pl.BlockSpec((tm, d), lambda i: (i, 0)),
        out_shape=jax.ShapeDtypeStruct((t, d), F32),
        compiler_params=_params("parallel"),
        name="ffn_down",
    )(act, w_down, res, nw.reshape(1, d))


def kernel(x, mem, w_in, w_out, norm_mix_w, lam_q1, lam_k1, lam_q2, lam_k2, da_subln_w,
           hg_lb_raw, hg_norm_w, rel_bias, norm_mem_w, mem_kv_norm_w, w_mq, w_mkv, w_mo,
           norm_ffn_w, w_up, conv_w, conv_b, w_down, final_norm_w):
    batch, seq, d_model = x.shape
    n_mem = mem.shape[1]
    depth = w_in.shape[0]
    assert depth == 1, "single-layer block"
    layer = 0
    t = batch * seq

    da_width = DA_HEADS * LANES
    hg_width = HG_HEADS * LANES
    assert w_in.shape[2] == 3 * da_width + 4 * hg_width
    assert da_subln_w.shape[1] == LANES and hg_norm_w.shape[1] == LANES
    assert lam_q1.shape[1] * 2 == LANES
    n_bf = 3 * da_width + hg_width

    tq = _pick(seq, DA_BLOCK)
    ts = _pick(seq, HG_TILE)

    lam_init = 0.8 - 0.6 * math.exp(-0.3 * layer)
    lam_vecs = jnp.stack([lam_q1[layer], lam_k1[layer], lam_q2[layer], lam_k2[layer]]).astype(F32)
    lam_row, lb = param_tables(lam_vecs, hg_lb_raw.astype(F32), lam_init, layer)

    far = _t5_bucket_np(np.arange(LANES + 1, max(seq, LANES + 2)))
    assert (far == far[0]).all(), "bias must be constant for distances beyond one lane tile"
    c = np.arange(2 * tq)
    dist_buckets = np.stack([np.where(c >= tq, _t5_bucket_np(np.maximum(c - tq, 0)), -1),
                             _t5_bucket_np(c)]).astype(np.int32)
    bias_t = bias_tiles(rel_bias.astype(F32), jnp.asarray(dist_buckets), tq, int(far[0]))

    x2d = x.reshape(t, d_model)
    q_scale = (LANES // 2) ** -0.5
    col_scale = np.ones((1, n_bf), np.float32)
    col_scale[:, :da_width] = q_scale
    proj_a, proj_b = in_proj(x2d, norm_mix_w[layer], w_in[layer].astype(BF16), n_bf, jnp.asarray(col_scale))

    da_o, w_up_bf, w_mkv_bf = diff_attention(
        proj_a, bias_t, lam_row, da_subln_w[layer], batch, seq, tq, 1.0 - lam_init,
        riders=(w_up, w_mkv), layer=layer)
    hg_o, w_down_bf, w_out_bf, w_mq_bf, w_mo_bf = hgrn2(
        proj_a, proj_b, lb, hg_norm_w[layer], batch, seq, ts, 3 * DA_HEADS,
        riders=(w_down, w_out, w_mq, w_mo), layer=layer)

    mkv = norm_matmul(mem.reshape(batch * n_mem, d_model), mem_kv_norm_w[layer], w_mkv_bf, BF16,
                      name="mem_kv_proj")
    x2 = mix_out_mem_attention(da_o, hg_o, w_out_bf, x2d, norm_mem_w[layer], w_mq_bf, mkv, w_mo_bf,
                               seq, n_mem)

    act = ffn_up(x2, norm_ffn_w[layer], w_up_bf, conv_w[layer], conv_b[layer], seq)
    out = ffn_down(act, w_down_bf, x2, final_norm_w)
    return out.reshape(batch, seq, d_model)
```

```python
import functools
import math

import numpy as np
import jax
import jax.numpy as jnp
from jax import lax
from jax.experimental import pallas as pl
from jax.experimental.pallas import tpu as pltpu

DA_HEADS = 8
HG_HEADS = 8
MEM_HEADS = 4
NUM_BUCKETS = 32
MAX_DISTANCE = 128
CONV_WIDTH = 3
EPS = 1e-6

LANES = 128
SUBLANES = 8
VMEM_LIMIT_BYTES = 56 * 1024 * 1024

HG_CHUNK = 128
HG_TILE = 1024
HG_HEADS_PER_STEP = 4
HG_SMALL_LEVELS = 3
DA_BLOCK = 512
DA_Q_SUB = 512
MM_TM = 1024
MM_TN = 1024
FFN_TF = 512
FFN_COL_CHUNK = 512
RESIDENT_TM = 512
FFN_DOWN_TM = RESIDENT_TM
NEG_BIG = -1e30

F32 = jnp.float32
BF16 = jnp.bfloat16


def _params(*sem):
    return pltpu.CompilerParams(dimension_semantics=sem, vmem_limit_bytes=VMEM_LIMIT_BYTES)


def _rms_rows(x, w):
    ms = jnp.mean(x * x, axis=-1, keepdims=True)
    return (x * lax.rsqrt(ms + EPS)) * w


def _pick(n, pref):
    t = min(pref, n)
    while n % t:
        t //= 2
    return t


def _norm_mm_kernel(x_ref, nw_ref, w_ref, o_ref, h_ref):
    @pl.when(pl.program_id(1) == 0)
    def _():
        h_ref[...] = _rms_rows(x_ref[...], nw_ref[...]).astype(h_ref.dtype)

    o_ref[...] = jnp.dot(h_ref[...], w_ref[...], preferred_element_type=F32).astype(o_ref.dtype)


def norm_matmul(x, nw, w, out_dtype, name="norm_matmul"):
    t, k = x.shape
    n = w.shape[1]
    tm, tn = _pick(t, MM_TM), _pick(n, MM_TN)
    return pl.pallas_call(
        _norm_mm_kernel,
        grid=(t // tm, n // tn),
        in_specs=[
            pl.BlockSpec((tm, k), lambda i, j: (i, 0)),
            pl.BlockSpec((1, k), lambda i, j: (0, 0)),
            pl.BlockSpec((k, tn), lambda i, j: (0, j)),
        ],
        out_specs=pl.BlockSpec((tm, tn), lambda i, j: (i, j)),
        out_shape=jax.ShapeDtypeStruct((t, n), out_dtype),
        scratch_shapes=[pltpu.VMEM((tm, k), BF16)],
        compiler_params=_params("parallel", "arbitrary"),
        name=name,
    )(x, nw.reshape(1, k), w)


BF16_ROWS = 16


def _with_cast_riders(body, n_in, n_out, n_riders):
    def kern(*refs):
        ins, rest = refs[:n_in], refs[n_in:]
        rid_in, rest = rest[:n_riders], rest[n_riders:]
        outs, rest = rest[:n_out], rest[n_out:]
        rid_out, scratch = rest[:n_riders], rest[n_riders:]
        for src, dst in zip(rid_in, rid_out):
            dst[...] = src[...].astype(dst.dtype)
        body(*ins, *outs, *scratch)
    return kern


def _rider_specs(riders, layer, grid):
    n_steps = math.prod(grid)

    def step(*g):
        idx = 0
        for gi, n in zip(g, grid):
            idx = idx * n + gi
        return idx

    in_specs, out_specs, shapes = [], [], []
    for a in riders:
        _, rows, cols = a.shape
        n_slabs = math.gcd(n_steps, rows // BF16_ROWS)
        assert rows % BF16_ROWS == 0 and n_steps % n_slabs == 0, (a.shape, grid)
        hold = n_steps // n_slabs
        in_specs.append(pl.BlockSpec((None, rows // n_slabs, cols),
                                     lambda *g, hold=hold: (layer, step(*g) // hold, 0)))
        out_specs.append(pl.BlockSpec((rows // n_slabs, cols),
                                      lambda *g, hold=hold: (step(*g) // hold, 0)))
        shapes.append(jax.ShapeDtypeStruct((rows, cols), BF16))
    return in_specs, out_specs, shapes


def _in_proj_kernel(x_ref, nw_ref, w_ref, cs_ref, oa_ref, ob_ref, h_ref, *, na_tiles):
    j = pl.program_id(1)

    @pl.when(j == 0)
    def _():
        h_ref[...] = _rms_rows(x_ref[...], nw_ref[...]).astype(h_ref.dtype)

    y = jnp.dot(h_ref[...], w_ref[...], preferred_element_type=F32)

    @pl.when(j < na_tiles)
    def _():
        oa_ref[...] = (y * cs_ref[...]).astype(oa_ref.dtype)

    @pl.when(j >= na_tiles)
    def _():
        ob_ref[...] = y


def in_proj(x, nw, w, n_a, col_scale):
    t, k = x.shape
    n = w.shape[1]
    tm, tn = _pick(t, MM_TM), math.gcd(_pick(n_a, MM_TN), _pick(n - n_a, MM_TN))
    na_tiles = n_a // tn
    return pl.pallas_call(
        functools.partial(_in_proj_kernel, na_tiles=na_tiles),
        grid=(t // tm, n // tn),
        in_specs=[
            pl.BlockSpec((tm, k), lambda i, j: (i, 0)),
            pl.BlockSpec((1, k), lambda i, j: (0, 0)),
            pl.BlockSpec((k, tn), lambda i, j: (0, j)),
            pl.BlockSpec((1, tn), lambda i, j: (0, jnp.minimum(j, na_tiles - 1))),
        ],
        out_specs=[
            pl.BlockSpec((tm, tn), lambda i, j: (i, jnp.minimum(j, na_tiles - 1))),
            pl.BlockSpec((tm, tn), lambda i, j: (i, jnp.maximum(j - na_tiles, 0))),
        ],
        out_shape=(jax.ShapeDtypeStruct((t, n_a), BF16), jax.ShapeDtypeStruct((t, n - n_a), F32)),
        scratch_shapes=[pltpu.VMEM((tm, k), BF16)],
        compiler_params=_params("parallel", "arbitrary"),
        name="in_proj",
    )(x, nw.reshape(1, k), w, col_scale)


def _mix_mem_kernel(a0_ref, a1_ref, w0_ref, w1_ref, x_ref, nw_ref, wq_ref, k_ref, v_ref, wo_ref,
                    o_ref, att_ref, *, scale):
    x1 = x_ref[...]
    x1 = x1 + jnp.dot(a0_ref[...], w0_ref[...], preferred_element_type=F32)
    x1 = x1 + jnp.dot(a1_ref[...], w1_ref[...], preferred_element_type=F32)
    hq = _rms_rows(x1, nw_ref[...]).astype(BF16)
    mq = jnp.dot(hq, wq_ref[...], preferred_element_type=F32).astype(BF16)
    dh = mq.shape[1] // MEM_HEADS
    for hd in range(MEM_HEADS):
        sl = slice(hd * dh, (hd + 1) * dh)
        s = lax.dot_general(mq[:, sl], k_ref[:, sl], (((1,), (1,)), ((), ())),
                            preferred_element_type=F32) * scale
        e = jnp.exp(s - jnp.max(s, axis=-1, keepdims=True))
        p = e / jnp.sum(e, axis=-1, keepdims=True)
        att_ref[:, sl] = jnp.dot(p.astype(BF16), v_ref[:, sl],
                                 preferred_element_type=F32).astype(att_ref.dtype)
    o_ref[...] = x1 + jnp.dot(att_ref[...], wo_ref[...], preferred_element_type=F32)


def mix_out_mem_attention(a0, a1, w_out, x, nw_mem, w_mq, mkv, w_mo, seq, n_mem):
    t, k0 = a0.shape
    k1 = a1.shape[1]
    d = x.shape[1]
    assert k0 == k1 and w_out.shape == (k0 + k1, d)
    tm = _pick(seq, RESIDENT_TM)
    per_batch = seq // tm
    scale = (d // MEM_HEADS) ** -0.5
    once = pl.Buffered(1)
    return pl.pallas_call(
        functools.partial(_mix_mem_kernel, scale=scale),
        grid=(t // tm,),
        in_specs=[
            pl.BlockSpec((tm, k0), lambda i: (i, 0)),
            pl.BlockSpec((tm, k1), lambda i: (i, 0)),
            pl.BlockSpec((k0, d), lambda i: (0, 0), pipeline_mode=once),
            pl.BlockSpec((k1, d), lambda i: (1, 0), pipeline_mode=once),
            pl.BlockSpec((tm, d), lambda i: (i, 0)),
            pl.BlockSpec((1, d), lambda i: (0, 0)),
            pl.BlockSpec((d, d), lambda i: (0, 0), pipeline_mode=once),
            pl.BlockSpec((n_mem, d), lambda i: (i // per_batch, 0), pipeline_mode=once),
            pl.BlockSpec((n_mem, d), lambda i: (i // per_batch, 1), pipeline_mode=once),
            pl.BlockSpec((d, d), lambda i: (0, 0), pipeline_mode=once),
        ],
        out_specs=pl.BlockSpec((tm, d), lambda i: (i, 0)),
        out_shape=jax.ShapeDtypeStruct((t, d), F32),
        scratch_shapes=[pltpu.VMEM((tm, d), BF16)],
        compiler_params=_params("parallel"),
        name="mix_out_mem_attention",
    )(a0, a1, w_out, w_out, x, nw_mem.reshape(1, d), w_mq, mkv, mkv, w_mo)


def _t5_bucket_np(n):
    max_exact = NUM_BUCKETS // 2
    nf = np.maximum(n, 1).astype(np.float32)
    large = max_exact + (np.log(nf / max_exact) / math.log(MAX_DISTANCE / max_exact)
                         * (NUM_BUCKETS - max_exact)).astype(np.int32)
    large = np.minimum(large, NUM_BUCKETS - 1)
    return np.where(n < max_exact, n, large).astype(np.int32)


def _tables_kernel(lamv_ref, lbraw_ref, lam_ref, lb_ref, *, lam_init, layer):
    v = lamv_ref[...]
    s1 = jnp.sum(v[0:1] * v[1:2], axis=-1, keepdims=True)
    s2 = jnp.sum(v[2:3] * v[3:4], axis=-1, keepdims=True)
    lam = jnp.exp(s1) - jnp.exp(s2) + lam_init
    lam_ref[...] = jnp.broadcast_to(lam, lam_ref.shape)
    raw = lbraw_ref[...]
    e = jnp.exp(raw - jnp.max(raw, axis=0, keepdims=True))
    sm = e / jnp.sum(e, axis=0, keepdims=True)
    lb_ref[...] = jnp.sum(sm[0:layer + 1], axis=0, keepdims=True)


def param_tables(lam_vecs, lb_raw, lam_init, layer):
    kw = lb_raw.shape[1]
    return pl.pallas_call(
        functools.partial(_tables_kernel, lam_init=lam_init, layer=layer),
        out_shape=(jax.ShapeDtypeStruct((1, LANES), F32), jax.ShapeDtypeStruct((1, kw), F32)),
        name="param_tables",
    )(lam_vecs, lb_raw)


def _bias_tiles_kernel(rb_ref, bk_ref, o_ref, *, tq, far_bucket):
    h = pl.program_id(0)
    bk = bk_ref[...]
    vec = jnp.full(bk.shape, NEG_BIG, F32)
    for n in range(NUM_BUCKETS):
        vec = jnp.where(bk == n, rb_ref[n, h], vec)
    far = rb_ref[far_bucket, h]
    for t in range(2):
        rows = jnp.broadcast_to(vec[t:t + 1], (tq, 2 * tq))
        tile = pltpu.roll(rows, 0, 1, stride=1, stride_axis=0)[:, tq:]
        o_ref[0, t] = tile if t == 0 else tile - far
    o_ref[0, FAR_TILE] = jnp.full((tq, tq), far, F32)


def bias_tiles(rel_bias, dist_buckets, tq, far_bucket):
    nb, h = rel_bias.shape
    return pl.pallas_call(
        functools.partial(_bias_tiles_kernel, tq=tq, far_bucket=far_bucket),
        grid=(h,),
        in_specs=[
            pl.BlockSpec(memory_space=pltpu.SMEM),
            pl.BlockSpec(dist_buckets.shape, lambda i: (0, 0)),
        ],
        out_specs=pl.BlockSpec((1, FAR_TILE + 1, tq, tq), lambda i: (i, 0, 0, 0)),
        out_shape=jax.ShapeDtypeStruct((h, FAR_TILE + 1, tq, tq), F32),
        compiler_params=_params("arbitrary"),
        name="bias_tiles",
    )(rel_bias, dist_buckets)


ONES_ROWS = BF16_ROWS
FAR_TILE = 2


def _diff_attn_kernel(q_ref, k_ref, v_ref, bias_ref, lam_ref, nw_ref, o_ref,
                      vt_ref, m_ref, acc_ref, st_ref, *, tq, out_scale):
    dv = v_ref.shape[1]
    dh = dv // 2
    nq = v_ref.shape[0] // tq

    for c in range(nq):
        vt_ref[c, 0:dv, :] = v_ref[c * tq:(c + 1) * tq, :].astype(F32).T.astype(BF16)
        vt_ref[c, dv:, :] = jnp.ones((ONES_ROWS, tq), BF16)

    lane = lax.broadcasted_iota(jnp.int32, (1, dv), 1)
    qw = min(DA_Q_SUB, tq)
    chains = [(mp, slice(qs * qw, (qs + 1) * qw)) for mp in range(2) for qs in range(tq // qw)]
    corner, corner_q = slice(tq - LANES, tq), slice(0, LANES)

    def query_block(i, carry):
        rows = pl.ds(pl.multiple_of(i * tq, tq), tq)
        q = q_ref[rows, :]
        qm = (jnp.where(lane < dh, q, jnp.zeros_like(q)), jnp.where(lane >= dh, q, jnp.zeros_like(q)))

        m_ref[...] = jnp.full(m_ref.shape, NEG_BIG, F32)
        acc_ref[...] = jnp.zeros(acc_ref.shape, F32)

        def scores(n, j):
            mp, cols = chains[n]
            kb = k_ref[pl.ds(pl.multiple_of(j * tq, tq), tq), :]
            st_ref[n] = lax.dot_general(kb, qm[mp][cols, :], (((1,), (1,)), ((), ())),
                                        preferred_element_type=F32)

        far_bias = bias_ref[0, FAR_TILE, 0:1, 0:1]

        def softmax_pv(n, vt, tile):
            mp, cols = chains[n]
            if tile == 1 and cols.start == 0:
                st_ref[n, corner, corner_q] += bias_ref[0, 1, corner, corner_q]
            st = st_ref[n]
            if tile == 0:
                st = st + bias_ref[0, tile, :, cols]
                shift = 0.0
            else:
                shift = far_bias
            m_old = m_ref[mp, :, cols]
            m_new = jnp.maximum(m_old, jnp.max(st, axis=0, keepdims=True) + shift)
            p = jnp.exp(st - (m_new - shift)).astype(BF16)
            alpha = jnp.exp(m_old - m_new)
            acc_ref[mp, :, cols] = (alpha * acc_ref[mp, :, cols]
                                    + jnp.dot(vt, p, preferred_element_type=F32))
            m_ref[mp, :, cols] = m_new

        def step(j, tile, last=False):
            vt = vt_ref[j]
            for n in range(len(chains)):
                if n + 1 < len(chains):
                    scores(n + 1, j)
                elif not last:
                    scores(0, j + 1)
                softmax_pv(n, vt, tile)

        def far_step(j, c2):
            step(j, FAR_TILE)
            return c2

        scores(0, 0)
        lax.fori_loop(0, jnp.maximum(i - 1, 0), far_step, 0)

        @pl.when(i >= 1)
        def _():
            step(i - 1, 1)

        step(i, 0, last=True)

        o0 = acc_ref[0, 0:dv, :] / acc_ref[0, dv:dv + 1, :]
        o1 = acc_ref[1, 0:dv, :] / acc_ref[1, dv:dv + 1, :]
        o = o0 - lam_ref[0:1, 0:1] * o1
        ms = jnp.mean(o * o, axis=0, keepdims=True)
        y = (o * lax.rsqrt(ms + EPS)) * (nw_ref[...] * out_scale)
        o_ref[rows, :] = y.T.astype(o_ref.dtype)
        return carry

    lax.fori_loop(0, nq, query_block, 0)


def diff_attention(proj, bias_t, lam_row, nw, batch, seq, tq, out_scale, riders=(), layer=0):
    nq = seq // tq
    dv = LANES
    grid = (batch, DA_HEADS)
    rider_in, rider_out, rider_shapes = _rider_specs(riders, layer, grid)
    kern = _with_cast_riders(functools.partial(_diff_attn_kernel, tq=tq, out_scale=out_scale),
                             n_in=6, n_out=1, n_riders=len(riders))
    return pl.pallas_call(
        kern,
        grid=grid,
        in_specs=[
            pl.BlockSpec((seq, dv), lambda b, h: (b, h)),
            pl.BlockSpec((seq, dv), lambda b, h: (b, DA_HEADS + h)),
            pl.BlockSpec((seq, dv), lambda b, h: (b, 2 * DA_HEADS + h)),
            pl.BlockSpec((1, FAR_TILE + 1, tq, tq), lambda b, h: (h, 0, 0, 0)),
            pl.BlockSpec((1, LANES), lambda b, h: (0, 0)),
            pl.BlockSpec((dv, 1), lambda b, h: (0, 0)),
        ] + rider_in,
        out_specs=[pl.BlockSpec((seq, dv), lambda b, h: (b, h))] + rider_out,
        out_shape=[jax.ShapeDtypeStruct((batch * seq, DA_HEADS * dv), BF16)] + rider_shapes,
        scratch_shapes=[
            pltpu.VMEM((nq, dv + ONES_ROWS, tq), BF16),
            pltpu.VMEM((2, 1, tq), F32),
            pltpu.VMEM((2, dv + ONES_ROWS, tq), F32),
            pltpu.VMEM((2 * tq // min(DA_Q_SUB, tq), tq, min(DA_Q_SUB, tq)), F32),
        ],
        compiler_params=_params("parallel", "parallel"),
        name="diff_attention",
    )(proj, proj, proj, bias_t, lam_row, nw.reshape(dv, 1), *riders)


def _exact_bf16_dot(tri3, x):
    hi = x.astype(BF16)
    r1 = x - hi.astype(F32)
    mid = r1.astype(BF16)
    lo = (r1 - mid.astype(F32)).astype(BF16)
    return jnp.dot(tri3, jnp.concatenate([hi, mid, lo], axis=0), preferred_element_type=F32)


def _hgrn_tables(c):
    n_lv = int(math.log2(c))
    r = np.arange(c)[None, :]
    t = np.arange(c)[:, None]
    blocks = []
    for lv in range(2, HG_SMALL_LEVELS + 1):
        h = 1 << (lv - 1)
        upper = (t // h) % 2 == 1
        mid_t = (t // h) * h
        mid_s = (t // h + 1) * h
        blocks.append((upper & (r >= mid_t) & (r <= t)) | ((~upper) & (r > t) & (r < mid_s)))
    stack = np.concatenate(blocks + [r <= t], axis=0).astype(np.float32)
    x = np.bitwise_xor(t, r)
    level = np.where(r == t, 0, np.floor(np.log2(np.maximum(x, 1))).astype(np.int32) + 1)
    level = np.where(r > t, -1, level).astype(np.int32)
    return stack, level, n_lv


def _hgrn_kernel(q_ref, f_ref, i_ref, g_ref, lb_ref, nw_ref, tst_ref, lvl_ref, o_ref, st_ref,
                 *, ts, c, n_lv):
    nt = (((1,), (1,)), ((), ()))
    d = LANES
    nh = q_ref.shape[1] // d
    nch = ts // c
    n_small = HG_SMALL_LEVELS

    @pl.when(pl.program_id(2) == 0)
    def _():
        st_ref[...] = jnp.zeros(st_ref.shape, F32)

    lb = lb_ref[...]
    f = lb + (1.0 - lb) * jax.nn.sigmoid(f_ref[...])
    logf = jnp.log(f)
    kk = 1.0 - f
    units = [(ch, hd) for ch in range(nch) for hd in range(nh)]
    lcat = jnp.concatenate([logf[ch * c:(ch + 1) * c, hd * d:(hd + 1) * d] for ch, hd in units], axis=1)
    seg = _exact_bf16_dot(tst_ref[...], lcat)
    lvl = lvl_ref[...]
    nw = nw_ref[...]

    st = [st_ref[hd] for hd in range(nh)]
    for u, (ch, hd) in enumerate(units):
        rows = slice(ch * c, (ch + 1) * c)
        cols = slice(hd * d, (hd + 1) * d)

        def blk(b, u=u):
            return seg[b * c:(b + 1) * c, u * d:(u + 1) * d]

        g_in = blk(n_small - 1)
        g_last = g_in[c - 1:c]
        g_out = g_last - g_in
        q = q_ref[rows, cols]
        q32 = q.astype(F32)
        kc = kk[rows, cols]
        kb = kc.astype(BF16)
        vb = i_ref[rows, cols].astype(BF16)
        a = jnp.where(lvl == 0, lax.dot_general(q, kb, nt, preferred_element_type=F32), 0.0)
        qt = (q32 * f[rows, cols]).astype(BF16)
        a = jnp.where(lvl == 1, lax.dot_general(qt, kb, nt, preferred_element_type=F32), a)
        for lv in range(2, n_lv + 1):
            if lv <= n_small:
                seg_sum = blk(lv - 2)
            else:
                h = 1 << (lv - 1)
                ref = jnp.concatenate(
                    [jnp.broadcast_to(g_in[gi * 2 * h + h - 1:gi * 2 * h + h, :], (2 * h, d))
                     for gi in range(c // (2 * h))], axis=0)
                seg_sum = -jnp.abs(g_in - ref)
            dec = jnp.exp(seg_sum)
            qt = (q32 * dec).astype(BF16)
            kt = (kc * dec).astype(BF16)
            a = jnp.where(lvl == lv, lax.dot_general(qt, kt, nt, preferred_element_type=F32), a)
        qg = (q32 * jnp.exp(g_in)).astype(BF16)
        o = (jnp.dot(a.astype(BF16), vb, preferred_element_type=F32)
             + lax.dot_general(qg, st[hd].astype(BF16), nt, preferred_element_type=F32))
        kd = (kc * jnp.exp(g_out)).astype(BF16)
        upd = lax.dot_general(vb, kd, (((0,), (0,)), ((), ())), preferred_element_type=F32)
        st[hd] = st[hd] * jnp.exp(g_last) + upd
        g = g_ref[rows, cols]
        o_ref[rows, cols] = (_rms_rows(o, nw) * (g * jax.nn.sigmoid(g))).astype(o_ref.dtype)
    for hd in range(nh):
        st_ref[hd] = st[hd]


def hgrn2(proj_a, proj_b, lb, nw, batch, seq, ts, q_col0, riders=(), layer=0):
    ns = seq // ts
    d = LANES
    nh = math.gcd(HG_HEADS_PER_STEP, HG_HEADS)
    assert q_col0 % nh == 0
    w = nh * d
    ng = HG_HEADS // nh
    c = min(HG_CHUNK, ts)
    stack, level, n_lv = _hgrn_tables(c)
    assert n_lv >= HG_SMALL_LEVELS
    grid = (batch, ng, ns)
    rider_in, rider_out, rider_shapes = _rider_specs(riders, layer, grid)
    kern = _with_cast_riders(functools.partial(_hgrn_kernel, ts=ts, c=c, n_lv=n_lv),
                             n_in=8, n_out=1, n_riders=len(riders))
    return pl.pallas_call(
        kern,
        grid=grid,
        in_specs=[
            pl.BlockSpec((ts, w), lambda b, h, i: (b * ns + i, q_col0 // nh + h)),
            pl.BlockSpec((ts, w), lambda b, h, i: (b * ns + i, h)),
            pl.BlockSpec((ts, w), lambda b, h, i: (b * ns + i, ng + h)),
            pl.BlockSpec((ts, w), lambda b, h, i: (b * ns + i, 2 * ng + h)),
            pl.BlockSpec((1, w), lambda b, h, i: (0, h)),
            pl.BlockSpec((1, d), lambda b, h, i: (0, 0)),
            pl.BlockSpec((stack.shape[0], 3 * stack.shape[1]), lambda b, h, i: (0, 0)),
            pl.BlockSpec(level.shape, lambda b, h, i: (0, 0)),
        ] + rider_in,
        out_specs=[pl.BlockSpec((ts, w), lambda b, h, i: (b * ns + i, h))] + rider_out,
        out_shape=[jax.ShapeDtypeStruct((batch * seq, HG_HEADS * d), BF16)] + rider_shapes,
        scratch_shapes=[pltpu.VMEM((nh, d, d), F32)],
        compiler_params=_params("parallel", "parallel", "arbitrary"),
        name="hgrn2",
    )(proj_a, proj_b, proj_b, proj_b, lb, nw.reshape(1, d),
      jnp.asarray(np.concatenate([stack] * 3, axis=1), BF16), jnp.asarray(level), *riders)


def _ffn_up_kernel(x_ref, halo_ref, nw_ref, wa_ref, wb_ref, cwa_ref, cwb_ref, cba_ref, cbb_ref,
                   o_ref, h_ref, u_ref, *, tm, per_batch):
    hl = SUBLANES

    @pl.when(pl.program_id(1) == 0)
    def _():
        nw = nw_ref[...]
        first = (pl.program_id(0) % per_batch) == 0
        halo = _rms_rows(halo_ref[...], nw)
        h_ref[0:hl, :] = jnp.where(first, 0.0, halo).astype(h_ref.dtype)
        h_ref[hl:, :] = _rms_rows(x_ref[...], nw).astype(h_ref.dtype)

    h = h_ref[...]
    tf = o_ref.shape[1]
    cw = min(FFN_COL_CHUNK, tf)

    def cols(cc):
        return slice(cc * cw, (cc + 1) * cw)

    def matmuls(cc):
        u_ref[cc, 0] = jnp.dot(h, wa_ref[:, cols(cc)], preferred_element_type=F32)
        u_ref[cc, 1] = jnp.dot(h, wb_ref[:, cols(cc)], preferred_element_type=F32)

    def conv(cc, half, cw_ref, cb_ref):
        acc = cb_ref[:, cols(cc)][None]
        for j in range(CONV_WIDTH):
            off = hl - (CONV_WIDTH - 1) + j
            rows = u_ref[cc, half, off:off + tm, :].reshape(tm // hl, hl, cw)
            acc = acc + cw_ref[j, :, cols(cc)][None] * rows
        return acc.reshape(tm, cw)

    def gate(cc):
        a = conv(cc, 0, cwa_ref, cba_ref)
        b = conv(cc, 1, cwb_ref, cbb_ref)
        o_ref[:, cols(cc)] = ((a * jax.nn.sigmoid(a)) * b).astype(o_ref.dtype)

    for cc in range(tf // cw):
        matmuls(cc)
        gate(cc)


def ffn_up(x, nw, w_up, conv_w, conv_b, seq):
    t, d = x.shape
    f = w_up.shape[1] // 2
    tm, tf = _pick(seq, MM_TM), _pick(f, FFN_TF)
    nf = f // tf
    per_batch = seq // tm
    hl = SUBLANES
    rows_per_tile = tm // hl
    cw = min(FFN_COL_CHUNK, tf)
    conv_w8 = jnp.broadcast_to(conv_w[:, None, :], (CONV_WIDTH, hl, 2 * f))
    conv_b8 = jnp.broadcast_to(conv_b.reshape(1, 2 * f), (hl, 2 * f))
    kern = functools.partial(_ffn_up_kernel, tm=tm, per_batch=per_batch)
    return pl.pallas_call(
        kern,
        grid=(t // tm, nf),
        in_specs=[
            pl.BlockSpec((tm, d), lambda i, j: (i, 0)),
            pl.BlockSpec((hl, d), lambda i, j: (jnp.maximum(i * rows_per_tile - 1, 0), 0)),
            pl.BlockSpec((1, d), lambda i, j: (0, 0)),
            pl.BlockSpec((d, tf), lambda i, j: (0, j)),
            pl.BlockSpec((d, tf), lambda i, j: (0, nf + j)),
            pl.BlockSpec((CONV_WIDTH, hl, tf), lambda i, j: (0, 0, j)),
            pl.BlockSpec((CONV_WIDTH, hl, tf), lambda i, j: (0, 0, nf + j)),
            pl.BlockSpec((hl, tf), lambda i, j: (0, j)),
            pl.BlockSpec((hl, tf), lambda i, j: (0, nf + j)),
        ],
        out_specs=pl.BlockSpec((tm, tf), lambda i, j: (i, j)),
        out_shape=jax.ShapeDtypeStruct((t, f), BF16),
        scratch_shapes=[
            pltpu.VMEM((tm + hl, d), BF16),
            pltpu.VMEM((tf // cw, 2, tm + hl, cw), F32),
        ],
        compiler_params=_params("parallel", "arbitrary"),
        name="ffn_up",
    )(x, x, nw.reshape(1, d), w_up, w_up, conv_w8, conv_w8, conv_b8, conv_b8)


def _ffn_down_kernel(a_ref, w_ref, res_ref, nw_ref, o_ref):
    y = res_ref[...] + jnp.dot(a_ref[...], w_ref[...], preferred_element_type=F32)
    o_ref[...] = _rms_rows(y, nw_ref[...])


def ffn_down(act, w_down, res, nw):
    t, f = act.shape
    d = w_down.shape[1]
    tm = _pick(t, FFN_DOWN_TM)
    return pl.pallas_call(
        _ffn_down_kernel,
        grid=(t // tm,),
        in_specs=[
            pl.BlockSpec((tm, f), lambda i: (i, 0)),
            pl.BlockSpec((f, d), lambda i: (0, 0), pipeline_mode=pl.Buffered(1)),
            pl.BlockSpec((tm, d), lambda i: (i, 0)),
            pl.BlockSpec((1, d), lambda i: (0, 0)),
        ],
        out_specs=pl.BlockSpec((tm, d), lambda i: (i, 0)),
        out_shape=jax.ShapeDtypeStruct((t, d), F32),
        compiler_params=_params("parallel"),
        name="ffn_down",
    )(act, w_down, res, nw.reshape(1, d))


def kernel(x, mem, w_in, w_out, norm_mix_w, lam_q1, lam_k1, lam_q2, lam_k2, da_subln_w,
           hg_lb_raw, hg_norm_w, rel_bias, norm_mem_w, mem_kv_norm_w, w_mq, w_mkv, w_mo,
           norm_ffn_w, w_up, conv_w, conv_b, w_down, final_norm_w):
    batch, seq, d_model = x.shape
    n_mem = mem.shape[1]
    depth = w_in.shape[0]
    assert depth == 1, "single-layer block"
    layer = 0
    t = batch * seq

    da_width = DA_HEADS * LANES
    hg_width = HG_HEADS * LANES
    assert w_in.shape[2] == 3 * da_width + 4 * hg_width
    assert da_subln_w.shape[1] == LANES and hg_norm_w.shape[1] == LANES
    assert lam_q1.shape[1] * 2 == LANES
    n_bf = 3 * da_width + hg_width

    tq = _pick(seq, DA_BLOCK)
    ts = _pick(seq, HG_TILE)

    lam_init = 0.8 - 0.6 * math.exp(-0.3 * layer)
    lam_vecs = jnp.stack([lam_q1[layer], lam_k1[layer], lam_q2[layer], lam_k2[layer]]).astype(F32)
    lam_row, lb = param_tables(lam_vecs, hg_lb_raw.astype(F32), lam_init, layer)

    far = _t5_bucket_np(np.arange(LANES + 1, max(seq, LANES + 2)))
    assert (far == far[0]).all(), "bias must be constant for distances beyond one lane tile"
    c = np.arange(2 * tq)
    dist_buckets = np.stack([np.where(c >= tq, _t5_bucket_np(np.maximum(c - tq, 0)), -1),
                             _t5_bucket_np(c)]).astype(np.int32)
    bias_t = bias_tiles(rel_bias.astype(F32), jnp.asarray(dist_buckets), tq, int(far[0]))

    x2d = x.reshape(t, d_model)
    q_scale = (LANES // 2) ** -0.5
    col_scale = np.ones((1, n_bf), np.float32)
    col_scale[:, :da_width] = q_scale
    proj_a, proj_b = in_proj(x2d, norm_mix_w[layer], w_in[layer].astype(BF16), n_bf, jnp.asarray(col_scale))

    da_o, w_up_bf, w_mkv_bf = diff_attention(
        proj_a, bias_t, lam_row, da_subln_w[layer], batch, seq, tq, 1.0 - lam_init,
        riders=(w_up, w_mkv), layer=layer)
    hg_o, w_down_bf, w_out_bf, w_mq_bf, w_mo_bf = hgrn2(
        proj_a, proj_b, lb, hg_norm_w[layer], batch, seq, ts, 3 * DA_HEADS,
        riders=(w_down, w_out, w_mq, w_mo), layer=layer)

    mkv = norm_matmul(mem.reshape(batch * n_mem, d_model), mem_kv_norm_w[layer], w_mkv_bf, BF16,
                      name="mem_kv_proj")
    x2 = mix_out_mem_attention(da_o, hg_o, w_out_bf, x2d, norm_mem_w[layer], w_mq_bf, mkv, w_mo_bf,
                               seq, n_mem)

    act = ffn_up(x2, norm_ffn_w[layer], w_up_bf, conv_w[layer], conv_b[layer], seq)
    out = ffn_down(act, w_down_bf, x2, final_norm_w)
    return out.reshape(batch, seq, d_model)
```

```python
import functools
import math

import numpy as np
import jax
import jax.numpy as jnp
from jax import lax
from jax.experimental import pallas as pl
from jax.experimental.pallas import tpu as pltpu

DA_HEADS = 8
HG_HEADS = 8
MEM_HEADS = 4
NUM_BUCKETS = 32
MAX_DISTANCE = 128
CONV_WIDTH = 3
EPS = 1e-6

LANES = 128
SUBLANES = 8
VMEM_LIMIT_BYTES = 56 * 1024 * 1024

HG_CHUNK = 128
HG_TILE = 1024
HG_HEADS_PER_STEP = 4
DA_BLOCK = 512
DA_Q_SUB = 512
MM_TM = 1024
MM_TN = 1024
FFN_TF = 512
FFN_COL_CHUNK = 512
RESIDENT_TM = 512
FFN_DOWN_TM = RESIDENT_TM
NEG_BIG = -1e30

F32 = jnp.float32
BF16 = jnp.bfloat16


def _params(*sem):
    return pltpu.CompilerParams(dimension_semantics=sem, vmem_limit_bytes=VMEM_LIMIT_BYTES)


def _rms_rows(x, w):
    ms = jnp.mean(x * x, axis=-1, keepdims=True)
    return (x * lax.rsqrt(ms + EPS)) * w


def _pick(n, pref):
    t = min(pref, n)
    while n % t:
        t //= 2
    return t


def _norm_mm_kernel(x_ref, nw_ref, w_ref, o_ref, h_ref):
    @pl.when(pl.program_id(1) == 0)
    def _():
        h_ref[...] = _rms_rows(x_ref[...], nw_ref[...]).astype(h_ref.dtype)

    o_ref[...] = jnp.dot(h_ref[...], w_ref[...], preferred_element_type=F32).astype(o_ref.dtype)


def norm_matmul(x, nw, w, out_dtype, name="norm_matmul"):
    t, k = x.shape
    n = w.shape[1]
    tm, tn = _pick(t, MM_TM), _pick(n, MM_TN)
    return pl.pallas_call(
        _norm_mm_kernel,
        grid=(t // tm, n // tn),
        in_specs=[
            pl.BlockSpec((tm, k), lambda i, j: (i, 0)),
            pl.BlockSpec((1, k), lambda i, j: (0, 0)),
            pl.BlockSpec((k, tn), lambda i, j: (0, j)),
        ],
        out_specs=pl.BlockSpec((tm, tn), lambda i, j: (i, j)),
        out_shape=jax.ShapeDtypeStruct((t, n), out_dtype),
        scratch_shapes=[pltpu.VMEM((tm, k), BF16)],
        compiler_params=_params("parallel", "arbitrary"),
        name=name,
    )(x, nw.reshape(1, k), w)


BF16_ROWS = 16


def _with_cast_riders(body, n_in, n_out, n_riders):
    def kern(*refs):
        ins, rest = refs[:n_in], refs[n_in:]
        rid_in, rest = rest[:n_riders], rest[n_riders:]
        outs, rest = rest[:n_out], rest[n_out:]
        rid_out, scratch = rest[:n_riders], rest[n_riders:]
        for src, dst in zip(rid_in, rid_out):
            dst[...] = src[...].astype(dst.dtype)
        body(*ins, *outs, *scratch)
    return kern


def _rider_specs(riders, layer, grid):
    n_steps = math.prod(grid)

    def step(*g):
        idx = 0
        for gi, n in zip(g, grid):
            idx = idx * n + gi
        return idx

    in_specs, out_specs, shapes = [], [], []
    for a in riders:
        _, rows, cols = a.shape
        n_slabs = math.gcd(n_steps, rows // BF16_ROWS)
        assert rows % BF16_ROWS == 0 and n_steps % n_slabs == 0, (a.shape, grid)
        hold = n_steps // n_slabs
        in_specs.append(pl.BlockSpec((None, rows // n_slabs, cols),
                                     lambda *g, hold=hold: (layer, step(*g) // hold, 0)))
        out_specs.append(pl.BlockSpec((rows // n_slabs, cols),
                                      lambda *g, hold=hold: (step(*g) // hold, 0)))
        shapes.append(jax.ShapeDtypeStruct((rows, cols), BF16))
    return in_specs, out_specs, shapes


def _in_proj_kernel(x_ref, nw_ref, w_ref, cs_ref, oa_ref, ob_ref, h_ref, *, na_tiles):
    j = pl.program_id(1)

    @pl.when(j == 0)
    def _():
        h_ref[...] = _rms_rows(x_ref[...], nw_ref[...]).astype(h_ref.dtype)

    y = jnp.dot(h_ref[...], w_ref[...], preferred_element_type=F32)

    @pl.when(j < na_tiles)
    def _():
        oa_ref[...] = (y * cs_ref[...]).astype(oa_ref.dtype)

    @pl.when(j >= na_tiles)
    def _():
        ob_ref[...] = y


def in_proj(x, nw, w, n_a, col_scale):
    t, k = x.shape
    n = w.shape[1]
    tm, tn = _pick(t, MM_TM), math.gcd(_pick(n_a, MM_TN), _pick(n - n_a, MM_TN))
    na_tiles = n_a // tn
    return pl.pallas_call(
        functools.partial(_in_proj_kernel, na_tiles=na_tiles),
        grid=(t // tm, n // tn),
        in_specs=[
            pl.BlockSpec((tm, k), lambda i, j: (i, 0)),
            pl.BlockSpec((1, k), lambda i, j: (0, 0)),
            pl.BlockSpec((k, tn), lambda i, j: (0, j)),
            pl.BlockSpec((1, tn), lambda i, j: (0, jnp.minimum(j, na_tiles - 1))),
        ],
        out_specs=[
            pl.BlockSpec((tm, tn), lambda i, j: (i, jnp.minimum(j, na_tiles - 1))),
            pl.BlockSpec((tm, tn), lambda i, j: (i, jnp.maximum(j - na_tiles, 0))),
        ],
        out_shape=(jax.ShapeDtypeStruct((t, n_a), BF16), jax.ShapeDtypeStruct((t, n - n_a), F32)),
        scratch_shapes=[pltpu.VMEM((tm, k), BF16)],
        compiler_params=_params("parallel", "arbitrary"),
        name="in_proj",
    )(x, nw.reshape(1, k), w, col_scale)


def _mix_mem_kernel(a0_ref, a1_ref, w0_ref, w1_ref, x_ref, nw_ref, wq_ref, k_ref, v_ref, wo_ref,
                    o_ref, att_ref, *, scale):
    x1 = x_ref[...]
    x1 = x1 + jnp.dot(a0_ref[...], w0_ref[...], preferred_element_type=F32)
    x1 = x1 + jnp.dot(a1_ref[...], w1_ref[...], preferred_element_type=F32)
    hq = _rms_rows(x1, nw_ref[...]).astype(BF16)
    mq = jnp.dot(hq, wq_ref[...], preferred_element_type=F32).astype(BF16)
    dh = mq.shape[1] // MEM_HEADS
    for hd in range(MEM_HEADS):
        sl = slice(hd * dh, (hd + 1) * dh)
        s = lax.dot_general(mq[:, sl], k_ref[:, sl], (((1,), (1,)), ((), ())),
                            preferred_element_type=F32) * scale
        e = jnp.exp(s - jnp.max(s, axis=-1, keepdims=True))
        p = e / jnp.sum(e, axis=-1, keepdims=True)
        att_ref[:, sl] = jnp.dot(p.astype(BF16), v_ref[:, sl],
                                 preferred_element_type=F32).astype(att_ref.dtype)
    o_ref[...] = x1 + jnp.dot(att_ref[...], wo_ref[...], preferred_element_type=F32)


def mix_out_mem_attention(a0, a1, w_out, x, nw_mem, w_mq, mkv, w_mo, seq, n_mem):
    t, k0 = a0.shape
    k1 = a1.shape[1]
    d = x.shape[1]
    assert k0 == k1 and w_out.shape == (k0 + k1, d)
    tm = _pick(seq, RESIDENT_TM)
    per_batch = seq // tm
    scale = (d // MEM_HEADS) ** -0.5
    once = pl.Buffered(1)
    return pl.pallas_call(
        functools.partial(_mix_mem_kernel, scale=scale),
        grid=(t // tm,),
        in_specs=[
            pl.BlockSpec((tm, k0), lambda i: (i, 0)),
            pl.BlockSpec((tm, k1), lambda i: (i, 0)),
            pl.BlockSpec((k0, d), lambda i: (0, 0), pipeline_mode=once),
            pl.BlockSpec((k1, d), lambda i: (1, 0), pipeline_mode=once),
            pl.BlockSpec((tm, d), lambda i: (i, 0)),
            pl.BlockSpec((1, d), lambda i: (0, 0)),
            pl.BlockSpec((d, d), lambda i: (0, 0), pipeline_mode=once),
            pl.BlockSpec((n_mem, d), lambda i: (i // per_batch, 0), pipeline_mode=once),
            pl.BlockSpec((n_mem, d), lambda i: (i // per_batch, 1), pipeline_mode=once),
            pl.BlockSpec((d, d), lambda i: (0, 0), pipeline_mode=once),
        ],
        out_specs=pl.BlockSpec((tm, d), lambda i: (i, 0)),
        out_shape=jax.ShapeDtypeStruct((t, d), F32),
        scratch_shapes=[pltpu.VMEM((tm, d), BF16)],
        compiler_params=_params("parallel"),
        name="mix_out_mem_attention",
    )(a0, a1, w_out, w_out, x, nw_mem.reshape(1, d), w_mq, mkv, mkv, w_mo)


def _t5_bucket_np(n):
    max_exact = NUM_BUCKETS // 2
    nf = np.maximum(n, 1).astype(np.float32)
    large = max_exact + (np.log(nf / max_exact) / math.log(MAX_DISTANCE / max_exact)
                         * (NUM_BUCKETS - max_exact)).astype(np.int32)
    large = np.minimum(large, NUM_BUCKETS - 1)
    return np.where(n < max_exact, n, large).astype(np.int32)


def _tables_kernel(lamv_ref, lbraw_ref, lam_ref, lb_ref, *, lam_init, layer):
    v = lamv_ref[...]
    s1 = jnp.sum(v[0:1] * v[1:2], axis=-1, keepdims=True)
    s2 = jnp.sum(v[2:3] * v[3:4], axis=-1, keepdims=True)
    lam = jnp.exp(s1) - jnp.exp(s2) + lam_init
    lam_ref[...] = jnp.broadcast_to(lam, lam_ref.shape)
    raw = lbraw_ref[...]
    e = jnp.exp(raw - jnp.max(raw, axis=0, keepdims=True))
    sm = e / jnp.sum(e, axis=0, keepdims=True)
    lb_ref[...] = jnp.sum(sm[0:layer + 1], axis=0, keepdims=True)


def param_tables(lam_vecs, lb_raw, lam_init, layer):
    kw = lb_raw.shape[1]
    return pl.pallas_call(
        functools.partial(_tables_kernel, lam_init=lam_init, layer=layer),
        out_shape=(jax.ShapeDtypeStruct((1, LANES), F32), jax.ShapeDtypeStruct((1, kw), F32)),
        name="param_tables",
    )(lam_vecs, lb_raw)


def _bias_tiles_kernel(rb_ref, bk_ref, o_ref, *, tq, far_bucket):
    h = pl.program_id(0)
    bk = bk_ref[...]
    vec = jnp.full(bk.shape, NEG_BIG, F32)
    for n in range(NUM_BUCKETS):
        vec = jnp.where(bk == n, rb_ref[n, h], vec)
    far = rb_ref[far_bucket, h]
    for t in range(2):
        rows = jnp.broadcast_to(vec[t:t + 1], (tq, 2 * tq))
        tile = pltpu.roll(rows, 0, 1, stride=1, stride_axis=0)[:, tq:]
        o_ref[0, t] = tile if t == 0 else tile - far
    o_ref[0, FAR_TILE] = jnp.full((tq, tq), far, F32)


def bias_tiles(rel_bias, dist_buckets, tq, far_bucket):
    nb, h = rel_bias.shape
    return pl.pallas_call(
        functools.partial(_bias_tiles_kernel, tq=tq, far_bucket=far_bucket),
        grid=(h,),
        in_specs=[
            pl.BlockSpec(memory_space=pltpu.SMEM),
            pl.BlockSpec(dist_buckets.shape, lambda i: (0, 0)),
        ],
        out_specs=pl.BlockSpec((1, FAR_TILE + 1, tq, tq), lambda i: (i, 0, 0, 0)),
        out_shape=jax.ShapeDtypeStruct((h, FAR_TILE + 1, tq, tq), F32),
        compiler_params=_params("arbitrary"),
        name="bias_tiles",
    )(rel_bias, dist_buckets)


ONES_ROWS = BF16_ROWS
FAR_TILE = 2


def _diff_attn_kernel(q_ref, k_ref, v_ref, bias_ref, lam_ref, nw_ref, o_ref,
                      vt_ref, m_ref, acc_ref, st_ref, *, tq, out_scale):
    dv = v_ref.shape[1]
    dh = dv // 2
    nq = v_ref.shape[0] // tq

    for c in range(nq):
        vt_ref[c, 0:dv, :] = v_ref[c * tq:(c + 1) * tq, :].astype(F32).T.astype(BF16)
        vt_ref[c, dv:, :] = jnp.ones((ONES_ROWS, tq), BF16)

    lane = lax.broadcasted_iota(jnp.int32, (1, dv), 1)
    qw = min(DA_Q_SUB, tq)
    chains = [(mp, slice(qs * qw, (qs + 1) * qw)) for mp in range(2) for qs in range(tq // qw)]
    corner, corner_q = slice(tq - LANES, tq), slice(0, LANES)

    def query_block(i, carry):
        rows = pl.ds(pl.multiple_of(i * tq, tq), tq)
        q = q_ref[rows, :]
        qm = (jnp.where(lane < dh, q, jnp.zeros_like(q)), jnp.where(lane >= dh, q, jnp.zeros_like(q)))

        m_ref[...] = jnp.full(m_ref.shape, NEG_BIG, F32)
        acc_ref[...] = jnp.zeros(acc_ref.shape, F32)

        def scores(n, j):
            mp, cols = chains[n]
            kb = k_ref[pl.ds(pl.multiple_of(j * tq, tq), tq), :]
            st_ref[n] = lax.dot_general(kb, qm[mp][cols, :], (((1,), (1,)), ((), ())),
                                        preferred_element_type=F32)

        far_bias = bias_ref[0, FAR_TILE, 0:1, 0:1]

        def softmax_pv(n, vt, tile):
            mp, cols = chains[n]
            if tile == 1 and cols.start == 0:
                st_ref[n, corner, corner_q] += bias_ref[0, 1, corner, corner_q]
            st = st_ref[n]
            if tile == 0:
                st = st + bias_ref[0, tile, :, cols]
                shift = 0.0
            else:
                shift = far_bias
            m_old = m_ref[mp, :, cols]
            m_new = jnp.maximum(m_old, jnp.max(st, axis=0, keepdims=True) + shift)
            p = jnp.exp(st - (m_new - shift)).astype(BF16)
            alpha = jnp.exp(m_old - m_new)
            acc_ref[mp, :, cols] = (alpha * acc_ref[mp, :, cols]
                                    + jnp.dot(vt, p, preferred_element_type=F32))
            m_ref[mp, :, cols] = m_new

        def step(j, tile, last=False):
            vt = vt_ref[j]
            for n in range(len(chains)):
                if n + 1 < len(chains):
                    scores(n + 1, j)
                elif not last:
                    scores(0, j + 1)
                softmax_pv(n, vt, tile)

        def far_step(j, c2):
            step(j, FAR_TILE)
            return c2

        scores(0, 0)
        lax.fori_loop(0, jnp.maximum(i - 1, 0), far_step, 0)

        @pl.when(i >= 1)
        def _():
            step(i - 1, 1)

        step(i, 0, last=True)

        o0 = acc_ref[0, 0:dv, :] / acc_ref[0, dv:dv + 1, :]
        o1 = acc_ref[1, 0:dv, :] / acc_ref[1, dv:dv + 1, :]
        o = o0 - lam_ref[0:1, 0:1] * o1
        ms = jnp.mean(o * o, axis=0, keepdims=True)
        y = (o * lax.rsqrt(ms + EPS)) * (nw_ref[...] * out_scale)
        o_ref[rows, :] = y.T.astype(o_ref.dtype)
        return carry

    lax.fori_loop(0, nq, query_block, 0)


def diff_attention(proj, bias_t, lam_row, nw, batch, seq, tq, out_scale, riders=(), layer=0):
    nq = seq // tq
    dv = LANES
    grid = (batch, DA_HEADS)
    rider_in, rider_out, rider_shapes = _rider_specs(riders, layer, grid)
    kern = _with_cast_riders(functools.partial(_diff_attn_kernel, tq=tq, out_scale=out_scale),
                             n_in=6, n_out=1, n_riders=len(riders))
    return pl.pallas_call(
        kern,
        grid=grid,
        in_specs=[
            pl.BlockSpec((seq, dv), lambda b, h: (b, h)),
            pl.BlockSpec((seq, dv), lambda b, h: (b, DA_HEADS + h)),
            pl.BlockSpec((seq, dv), lambda b, h: (b, 2 * DA_HEADS + h)),
            pl.BlockSpec((1, FAR_TILE + 1, tq, tq), lambda b, h: (h, 0, 0, 0)),
            pl.BlockSpec((1, LANES), lambda b, h: (0, 0)),
            pl.BlockSpec((dv, 1), lambda b, h: (0, 0)),
        ] + rider_in,
        out_specs=[pl.BlockSpec((seq, dv), lambda b, h: (b, h))] + rider_out,
        out_shape=[jax.ShapeDtypeStruct((batch * seq, DA_HEADS * dv), BF16)] + rider_shapes,
        scratch_shapes=[
            pltpu.VMEM((nq, dv + ONES_ROWS, tq), BF16),
            pltpu.VMEM((2, 1, tq), F32),
            pltpu.VMEM((2, dv + ONES_ROWS, tq), F32),
            pltpu.VMEM((2 * tq // min(DA_Q_SUB, tq), tq, min(DA_Q_SUB, tq)), F32),
        ],
        compiler_params=_params("parallel", "parallel"),
        name="diff_attention",
    )(proj, proj, proj, bias_t, lam_row, nw.reshape(dv, 1), *riders)


def _exact_bf16_dot(tri3, x):
    hi = x.astype(BF16)
    r1 = x - hi.astype(F32)
    mid = r1.astype(BF16)
    lo = (r1 - mid.astype(F32)).astype(BF16)
    return jnp.dot(tri3, jnp.concatenate([hi, mid, lo], axis=0), preferred_element_type=F32)


def _hgrn_tables(c):
    n_lv = int(math.log2(c))
    r = np.arange(c)[None, :]
    t = np.arange(c)[:, None]
    stack = (r <= t).astype(np.float32)
    x = np.bitwise_xor(t, r)
    level = np.where(r == t, 0, np.floor(np.log2(np.maximum(x, 1))).astype(np.int32) + 1)
    level = np.where(r > t, -1, level).astype(np.int32)
    return stack, level, n_lv


def _hgrn_kernel(q_ref, f_ref, i_ref, g_ref, lb_ref, nw_ref, tst_ref, lvl_ref, o_ref, st_ref,
                 *, ts, c, n_lv):
    nt = (((1,), (1,)), ((), ()))
    d = LANES
    nh = q_ref.shape[1] // d
    nch = ts // c

    @pl.when(pl.program_id(2) == 0)
    def _():
        st_ref[...] = jnp.zeros(st_ref.shape, F32)

    lb = lb_ref[...]
    f = lb + (1.0 - lb) * jax.nn.sigmoid(f_ref[...])
    logf = jnp.log(f)
    kk = 1.0 - f
    units = [(ch, hd) for ch in range(nch) for hd in range(nh)]
    lcat = jnp.concatenate([logf[ch * c:(ch + 1) * c, hd * d:(hd + 1) * d] for ch, hd in units], axis=1)
    gsum = _exact_bf16_dot(tst_ref[...], lcat)
    lvl = lvl_ref[...]
    nw = nw_ref[...]

    st = [st_ref[hd] for hd in range(nh)]
    for u, (ch, hd) in enumerate(units):
        rows = slice(ch * c, (ch + 1) * c)
        cols = slice(hd * d, (hd + 1) * d)

        g_in = gsum[:, u * d:(u + 1) * d]
        g_last = g_in[c - 1:c]
        g_out = g_last - g_in
        q = q_ref[rows, cols]
        q32 = q.astype(F32)
        kc = kk[rows, cols]
        kb = kc.astype(BF16)
        vb = i_ref[rows, cols].astype(BF16)
        a = jnp.where(lvl == 0, lax.dot_general(q, kb, nt, preferred_element_type=F32), 0.0)
        qt = (q32 * f[rows, cols]).astype(BF16)
        a = jnp.where(lvl == 1, lax.dot_general(qt, kb, nt, preferred_element_type=F32), a)
        for lv in range(2, n_lv + 1):
            h = 1 << (lv - 1)
            ref = jnp.concatenate(
                [jnp.broadcast_to(g_in[gi * 2 * h + h - 1:gi * 2 * h + h, :], (2 * h, d))
                 for gi in range(c // (2 * h))], axis=0)
            dec = jnp.exp(-jnp.abs(g_in - ref))
            qt = (q32 * dec).astype(BF16)
            kt = (kc * dec).astype(BF16)
            a = jnp.where(lvl == lv, lax.dot_general(qt, kt, nt, preferred_element_type=F32), a)
        qg = (q32 * jnp.exp(g_in)).astype(BF16)
        o = (jnp.dot(a.astype(BF16), vb, preferred_element_type=F32)
             + lax.dot_general(qg, st[hd].astype(BF16), nt, preferred_element_type=F32))
        kd = (kc * jnp.exp(g_out)).astype(BF16)
        upd = lax.dot_general(vb, kd, (((0,), (0,)), ((), ())), preferred_element_type=F32)
        st[hd] = st[hd] * jnp.exp(g_last) + upd
        g = g_ref[rows, cols]
        o_ref[rows, cols] = (_rms_rows(o, nw) * (g * jax.nn.sigmoid(g))).astype(o_ref.dtype)
    for hd in range(nh):
        st_ref[hd] = st[hd]


def hgrn2(proj_a, proj_b, lb, nw, batch, seq, ts, q_col0, riders=(), layer=0):
    ns = seq // ts
    d = LANES
    nh = math.gcd(HG_HEADS_PER_STEP, HG_HEADS)
    assert q_col0 % nh == 0
    w = nh * d
    ng = HG_HEADS // nh
    c = min(HG_CHUNK, ts)
    stack, level, n_lv = _hgrn_tables(c)
    grid = (batch, ng, ns)
    rider_in, rider_out, rider_shapes = _rider_specs(riders, layer, grid)
    kern = _with_cast_riders(functools.partial(_hgrn_kernel, ts=ts, c=c, n_lv=n_lv),
                             n_in=8, n_out=1, n_riders=len(riders))
    return pl.pallas_call(
        kern,
        grid=grid,
        in_specs=[
            pl.BlockSpec((ts, w), lambda b, h, i: (b * ns + i, q_col0 // nh + h)),
            pl.BlockSpec((ts, w), lambda b, h, i: (b * ns + i, h)),
            pl.BlockSpec((ts, w), lambda b, h, i: (b * ns + i, ng + h)),
            pl.BlockSpec((ts, w), lambda b, h, i: (b * ns + i, 2 * ng + h)),
            pl.BlockSpec((1, w), lambda b, h, i: (0, h)),
            pl.BlockSpec((1, d), lambda b, h, i: (0, 0)),
            pl.BlockSpec((stack.shape[0], 3 * stack.shape[1]), lambda b, h, i: (0, 0)),
            pl.BlockSpec(level.shape, lambda b, h, i: (0, 0)),
        ] + rider_in,
        out_specs=[pl.BlockSpec((ts, w), lambda b, h, i: (b * ns + i, h))] + rider_out,
        out_shape=[jax.ShapeDtypeStruct((batch * seq, HG_HEADS * d), BF16)] + rider_shapes,
        scratch_shapes=[pltpu.VMEM((nh, d, d), F32)],
        compiler_params=_params("parallel", "parallel", "arbitrary"),
        name="hgrn2",
    )(proj_a, proj_b, proj_b, proj_b, lb, nw.reshape(1, d),
      jnp.asarray(np.concatenate([stack] * 3, axis=1), BF16), jnp.asarray(level), *riders)


def _ffn_up_kernel(x_ref, halo_ref, nw_ref, wa_ref, wb_ref, cwa_ref, cwb_ref, cba_ref, cbb_ref,
                   o_ref, h_ref, u_ref, *, tm, per_batch):
    hl = SUBLANES

    @pl.when(pl.program_id(1) == 0)
    def _():
        nw = nw_ref[...]
        first = (pl.program_id(0) % per_batch) == 0
        halo = _rms_rows(halo_ref[...], nw)
        h_ref[0:hl, :] = jnp.where(first, 0.0, halo).astype(h_ref.dtype)
        h_ref[hl:, :] = _rms_rows(x_ref[...], nw).astype(h_ref.dtype)

    h = h_ref[...]
    tf = o_ref.shape[1]
    cw = min(FFN_COL_CHUNK, tf)

    def cols(cc):
        return slice(cc * cw, (cc + 1) * cw)

    def matmuls(cc):
        u_ref[cc, 0] = jnp.dot(h, wa_ref[:, cols(cc)], preferred_element_type=F32)
        u_ref[cc, 1] = jnp.dot(h, wb_ref[:, cols(cc)], preferred_element_type=F32)

    def conv(cc, half, cw_ref, cb_ref):
        acc = cb_ref[:, cols(cc)][None]
        for j in range(CONV_WIDTH):
            off = hl - (CONV_WIDTH - 1) + j
            rows = u_ref[cc, half, off:off + tm, :].reshape(tm // hl, hl, cw)
            acc = acc + cw_ref[j, :, cols(cc)][None] * rows
        return acc.reshape(tm, cw)

    def gate(cc):
        a = conv(cc, 0, cwa_ref, cba_ref)
        b = conv(cc, 1, cwb_ref, cbb_ref)
        o_ref[:, cols(cc)] = ((a * jax.nn.sigmoid(a)) * b).astype(o_ref.dtype)

    for cc in range(tf // cw):
        matmuls(cc)
        gate(cc)


def ffn_up(x, nw, w_up, conv_w, conv_b, seq):
    t, d = x.shape
    f = w_up.shape[1] // 2
    tm, tf = _pick(seq, MM_TM), _pick(f, FFN_TF)
    nf = f // tf
    per_batch = seq // tm
    hl = SUBLANES
    rows_per_tile = tm // hl
    cw = min(FFN_COL_CHUNK, tf)
    conv_w8 = jnp.broadcast_to(conv_w[:, None, :], (CONV_WIDTH, hl, 2 * f))
    conv_b8 = jnp.broadcast_to(conv_b.reshape(1, 2 * f), (hl, 2 * f))
    kern = functools.partial(_ffn_up_kernel, tm=tm, per_batch=per_batch)
    return pl.pallas_call(
        kern,
        grid=(t // tm, nf),
        in_specs=[
            pl.BlockSpec((tm, d), lambda i, j: (i, 0)),
            pl.BlockSpec((hl, d), lambda i, j: (jnp.maximum(i * rows_per_tile - 1, 0), 0)),
            pl.BlockSpec((1, d), lambda i, j: (0, 0)),
            pl.BlockSpec((d, tf), lambda i, j: (0, j)),
            pl.BlockSpec((d, tf), lambda i, j: (0, nf + j)),
            pl.BlockSpec((CONV_WIDTH, hl, tf), lambda i, j: (0, 0, j)),
            pl.BlockSpec((CONV_WIDTH, hl, tf), lambda i, j: (0, 0, nf + j)),
            pl.BlockSpec((hl, tf), lambda i, j: (0, j)),
            pl.BlockSpec((hl, tf), lambda i, j: (0, nf + j)),
        ],
        out_specs=pl.BlockSpec((tm, tf), lambda i, j: (i, j)),
        out_shape=jax.ShapeDtypeStruct((t, f), BF16),
        scratch_shapes=[
            pltpu.VMEM((tm + hl, d), BF16),
            pltpu.VMEM((tf // cw, 2, tm + hl, cw), F32),
        ],
        compiler_params=_params("parallel", "arbitrary"),
        name="ffn_up",
    )(x, x, nw.reshape(1, d), w_up, w_up, conv_w8, conv_w8, conv_b8, conv_b8)


def _ffn_down_kernel(a_ref, w_ref, res_ref, nw_ref, o_ref):
    y = res_ref[...] + jnp.dot(a_ref[...], w_ref[...], preferred_element_type=F32)
    o_ref[...] = _rms_rows(y, nw_ref[...])


def ffn_down(act, w_down, res, nw):
    t, f = act.shape
    d = w_down.shape[1]
    tm = _pick(t, FFN_DOWN_TM)
    return pl.pallas_call(
        _ffn_down_kernel,
        grid=(t // tm,),
        in_specs=[
            pl.BlockSpec((tm, f), lambda i: (i, 0)),
            pl.BlockSpec((f, d), lambda i: (0, 0), pipeline_mode=pl.Buffered(1)),
            pl.BlockSpec((tm, d), lambda i: (i, 0)),
            pl.BlockSpec((1, d), lambda i: (0, 0)),
        ],
        out_specs=pl.BlockSpec((tm, d), lambda i: (i, 0)),
        out_shape=jax.ShapeDtypeStruct((t, d), F32),
        compiler_params=_params("parallel"),
        name="ffn_down",
    )(act, w_down, res, nw.reshape(1, d))


def kernel(x, mem, w_in, w_out, norm_mix_w, lam_q1, lam_k1, lam_q2, lam_k2, da_subln_w,
           hg_lb_raw, hg_norm_w, rel_bias, norm_mem_w, mem_kv_norm_w, w_mq, w_mkv, w_mo,
           norm_ffn_w, w_up, conv_w, conv_b, w_down, final_norm_w):
    batch, seq, d_model = x.shape
    n_mem = mem.shape[1]
    depth = w_in.shape[0]
    assert depth == 1, "single-layer block"
    layer = 0
    t = batch * seq

    da_width = DA_HEADS * LANES
    hg_width = HG_HEADS * LANES
    assert w_in.shape[2] == 3 * da_width + 4 * hg_width
    assert da_subln_w.shape[1] == LANES and hg_norm_w.shape[1] == LANES
    assert lam_q1.shape[1] * 2 == LANES
    n_bf = 3 * da_width + hg_width

    tq = _pick(seq, DA_BLOCK)
    ts = _pick(seq, HG_TILE)

    lam_init = 0.8 - 0.6 * math.exp(-0.3 * layer)
    lam_vecs = jnp.stack([lam_q1[layer], lam_k1[layer], lam_q2[layer], lam_k2[layer]]).astype(F32)
    lam_row, lb = param_tables(lam_vecs, hg_lb_raw.astype(F32), lam_init, layer)

    far = _t5_bucket_np(np.arange(LANES + 1, max(seq, LANES + 2)))
    assert (far == far[0]).all(), "bias must be constant for distances beyond one lane tile"
    c = np.arange(2 * tq)
    dist_buckets = np.stack([np.where(c >= tq, _t5_bucket_np(np.maximum(c - tq, 0)), -1),
                             _t5_bucket_np(c)]).astype(np.int32)
    bias_t = bias_tiles(rel_bias.astype(F32), jnp.asarray(dist_buckets), tq, int(far[0]))

    x2d = x.reshape(t, d_model)
    q_scale = (LANES // 2) ** -0.5
    col_scale = np.ones((1, n_bf), np.float32)
    col_scale[:, :da_width] = q_scale
    proj_a, proj_b = in_proj(x2d, norm_mix_w[layer], w_in[layer].astype(BF16), n_bf, jnp.asarray(col_scale))

    da_o, w_up_bf, w_mkv_bf = diff_attention(
        proj_a, bias_t, lam_row, da_subln_w[layer], batch, seq, tq, 1.0 - lam_init,
        riders=(w_up, w_mkv), layer=layer)
    hg_o, w_down_bf, w_out_bf, w_mq_bf, w_mo_bf = hgrn2(
        proj_a, proj_b, lb, hg_norm_w[layer], batch, seq, ts, 3 * DA_HEADS,
        riders=(w_down, w_out, w_mq, w_mo), layer=layer)

    mkv = norm_matmul(mem.reshape(batch * n_mem, d_model), mem_kv_norm_w[layer], w_mkv_bf, BF16,
                      name="mem_kv_proj")
    x2 = mix_out_mem_attention(da_o, hg_o, w_out_bf, x2d, norm_mem_w[layer], w_mq_bf, mkv, w_mo_bf,
                               seq, n_mem)

    act = ffn_up(x2, norm_ffn_w[layer], w_up_bf, conv_w[layer], conv_b[layer], seq)
    out = ffn_down(act, w_down_bf, x2, final_norm_w)
    return out.reshape(batch, seq, d_model)
```

```python
import functools
import math

import numpy as np
import jax
import jax.numpy as jnp
from jax import lax
from jax.experimental import pallas as pl
from jax.experimental.pallas import tpu as pltpu

DA_HEADS = 8
HG_HEADS = 8
MEM_HEADS = 4
NUM_BUCKETS = 32
MAX_DISTANCE = 128
CONV_WIDTH = 3
EPS = 1e-6

LANES = 128
SUBLANES = 8
VMEM_LIMIT_BYTES = 56 * 1024 * 1024

HG_CHUNK = 128
HG_TILE = 1024
HG_HEADS_PER_STEP = 4
DA_BLOCK = 512
DA_Q_SUB = 512
MM_TM = 1024
MM_TN = 1024
FFN_TF = 512
FFN_COL_CHUNK = 512
RESIDENT_TM = 512
FFN_DOWN_TM = RESIDENT_TM
NEG_BIG = -1e30

F32 = jnp.float32
BF16 = jnp.bfloat16


def _params(*sem):
    return pltpu.CompilerParams(dimension_semantics=sem, vmem_limit_bytes=VMEM_LIMIT_BYTES)


def _rms_rows(x, w):
    ms = jnp.mean(x * x, axis=-1, keepdims=True)
    return (x * lax.rsqrt(ms + EPS)) * w


def _pick(n, pref):
    t = min(pref, n)
    while n % t:
        t //= 2
    return t


def _norm_mm_kernel(x_ref, nw_ref, w_ref, o_ref, h_ref):
    @pl.when(pl.program_id(1) == 0)
    def _():
        h_ref[...] = _rms_rows(x_ref[...], nw_ref[...]).astype(h_ref.dtype)

    o_ref[...] = jnp.dot(h_ref[...], w_ref[...], preferred_element_type=F32).astype(o_ref.dtype)


def norm_matmul(x, nw, w, out_dtype, name="norm_matmul"):
    t, k = x.shape
    n = w.shape[1]
    tm, tn = _pick(t, MM_TM), _pick(n, MM_TN)
    return pl.pallas_call(
        _norm_mm_kernel,
        grid=(t // tm, n // tn),
        in_specs=[
            pl.BlockSpec((tm, k), lambda i, j: (i, 0)),
            pl.BlockSpec((1, k), lambda i, j: (0, 0)),
            pl.BlockSpec((k, tn), lambda i, j: (0, j)),
        ],
        out_specs=pl.BlockSpec((tm, tn), lambda i, j: (i, j)),
        out_shape=jax.ShapeDtypeStruct((t, n), out_dtype),
        scratch_shapes=[pltpu.VMEM((tm, k), BF16)],
        compiler_params=_params("parallel", "arbitrary"),
        name=name,
    )(x, nw.reshape(1, k), w)


BF16_ROWS = 16


def _with_cast_riders(body, n_in, n_out, n_riders):
    def kern(*refs):
        ins, rest = refs[:n_in], refs[n_in:]
        rid_in, rest = rest[:n_riders], rest[n_riders:]
        outs, rest = rest[:n_out], rest[n_out:]
        rid_out, scratch = rest[:n_riders], rest[n_riders:]
        for src, dst in zip(rid_in, rid_out):
            dst[...] = src[...].astype(dst.dtype)
        body(*ins, *outs, *scratch)
    return kern


def _rider_specs(riders, layer, grid):
    n_steps = math.prod(grid)

    def step(*g):
        idx = 0
        for gi, n in zip(g, grid):
            idx = idx * n + gi
        return idx

    in_specs, out_specs, shapes = [], [], []
    for a in riders:
        _, rows, cols = a.shape
        n_slabs = math.gcd(n_steps, rows // BF16_ROWS)
        assert rows % BF16_ROWS == 0 and n_steps % n_slabs == 0, (a.shape, grid)
        hold = n_steps // n_slabs
        in_specs.append(pl.BlockSpec((None, rows // n_slabs, cols),
                                     lambda *g, hold=hold: (layer, step(*g) // hold, 0)))
        out_specs.append(pl.BlockSpec((rows // n_slabs, cols),
                                      lambda *g, hold=hold: (step(*g) // hold, 0)))
        shapes.append(jax.ShapeDtypeStruct((rows, cols), BF16))
    return in_specs, out_specs, shapes


def _in_proj_kernel(x_ref, nw_ref, w_ref, cs_ref, oa_ref, ob_ref, h_ref, *, na_tiles):
    j = pl.program_id(1)

    @pl.when(j == 0)
    def _():
        h_ref[...] = _rms_rows(x_ref[...], nw_ref[...]).astype(h_ref.dtype)

    y = jnp.dot(h_ref[...], w_ref[...], preferred_element_type=F32)

    @pl.when(j < na_tiles)
    def _():
        oa_ref[...] = (y * cs_ref[...]).astype(oa_ref.dtype)

    @pl.when(j >= na_tiles)
    def _():
        ob_ref[...] = y


def in_proj(x, nw, w, n_a, col_scale):
    t, k = x.shape
    n = w.shape[1]
    tm, tn = _pick(t, MM_TM), math.gcd(_pick(n_a, MM_TN), _pick(n - n_a, MM_TN))
    na_tiles = n_a // tn
    return pl.pallas_call(
        functools.partial(_in_proj_kernel, na_tiles=na_tiles),
        grid=(t // tm, n // tn),
        in_specs=[
            pl.BlockSpec((tm, k), lambda i, j: (i, 0)),
            pl.BlockSpec((1, k), lambda i, j: (0, 0)),
            pl.BlockSpec((k, tn), lambda i, j: (0, j)),
            pl.BlockSpec((1, tn), lambda i, j: (0, jnp.minimum(j, na_tiles - 1))),
        ],
        out_specs=[
            pl.BlockSpec((tm, tn), lambda i, j: (i, jnp.minimum(j, na_tiles - 1))),
            pl.BlockSpec((tm, tn), lambda i, j: (i, jnp.maximum(j - na_tiles, 0))),
        ],
        out_shape=(jax.ShapeDtypeStruct((t, n_a), BF16), jax.ShapeDtypeStruct((t, n - n_a), F32)),
        scratch_shapes=[pltpu.VMEM((tm, k), BF16)],
        compiler_params=_params("parallel", "arbitrary"),
        name="in_proj",
    )(x, nw.reshape(1, k), w, col_scale)


def _mix_mem_kernel(a0_ref, a1_ref, w0_ref, w1_ref, x_ref, nw_ref, wq_ref, k_ref, v_ref, wo_ref,
                    o_ref, att_ref, *, scale):
    x1 = x_ref[...]
    x1 = x1 + jnp.dot(a0_ref[...], w0_ref[...], preferred_element_type=F32)
    x1 = x1 + jnp.dot(a1_ref[...], w1_ref[...], preferred_element_type=F32)
    hq = _rms_rows(x1, nw_ref[...]).astype(BF16)
    mq = jnp.dot(hq, wq_ref[...], preferred_element_type=F32).astype(BF16)
    dh = mq.shape[1] // MEM_HEADS
    for hd in range(MEM_HEADS):
        sl = slice(hd * dh, (hd + 1) * dh)
        s = lax.dot_general(mq[:, sl], k_ref[:, sl], (((1,), (1,)), ((), ())),
                            preferred_element_type=F32) * scale
        e = jnp.exp(s - jnp.max(s, axis=-1, keepdims=True))
        p = e / jnp.sum(e, axis=-1, keepdims=True)
        att_ref[:, sl] = jnp.dot(p.astype(BF16), v_ref[:, sl],
                                 preferred_element_type=F32).astype(att_ref.dtype)
    o_ref[...] = x1 + jnp.dot(att_ref[...], wo_ref[...], preferred_element_type=F32)


def mix_out_mem_attention(a0, a1, w_out, x, nw_mem, w_mq, mkv, w_mo, seq, n_mem):
    t, k0 = a0.shape
    k1 = a1.shape[1]
    d = x.shape[1]
    assert k0 == k1 and w_out.shape == (k0 + k1, d)
    tm = _pick(seq, RESIDENT_TM)
    per_batch = seq // tm
    scale = (d // MEM_HEADS) ** -0.5
    once = pl.Buffered(1)
    return pl.pallas_call(
        functools.partial(_mix_mem_kernel, scale=scale),
        grid=(t // tm,),
        in_specs=[
            pl.BlockSpec((tm, k0), lambda i: (i, 0)),
            pl.BlockSpec((tm, k1), lambda i: (i, 0)),
            pl.BlockSpec((k0, d), lambda i: (0, 0), pipeline_mode=once),
            pl.BlockSpec((k1, d), lambda i: (1, 0), pipeline_mode=once),
            pl.BlockSpec((tm, d), lambda i: (i, 0)),
            pl.BlockSpec((1, d), lambda i: (0, 0)),
            pl.BlockSpec((d, d), lambda i: (0, 0), pipeline_mode=once),
            pl.BlockSpec((n_mem, d), lambda i: (i // per_batch, 0), pipeline_mode=once),
            pl.BlockSpec((n_mem, d), lambda i: (i // per_batch, 1), pipeline_mode=once),
            pl.BlockSpec((d, d), lambda i: (0, 0), pipeline_mode=once),
        ],
        out_specs=pl.BlockSpec((tm, d), lambda i: (i, 0)),
        out_shape=jax.ShapeDtypeStruct((t, d), F32),
        scratch_shapes=[pltpu.VMEM((tm, d), BF16)],
        compiler_params=_params("parallel"),
        name="mix_out_mem_attention",
    )(a0, a1, w_out, w_out, x, nw_mem.reshape(1, d), w_mq, mkv, mkv, w_mo)


def _t5_bucket_np(n):
    max_exact = NUM_BUCKETS // 2
    nf = np.maximum(n, 1).astype(np.float32)
    large = max_exact + (np.log(nf / max_exact) / math.log(MAX_DISTANCE / max_exact)
                         * (NUM_BUCKETS - max_exact)).astype(np.int32)
    large = np.minimum(large, NUM_BUCKETS - 1)
    return np.where(n < max_exact, n, large).astype(np.int32)


def _tables_kernel(lamv_ref, lbraw_ref, lam_ref, lb_ref, *, lam_init, layer):
    v = lamv_ref[...]
    s1 = jnp.sum(v[0:1] * v[1:2], axis=-1, keepdims=True)
    s2 = jnp.sum(v[2:3] * v[3:4], axis=-1, keepdims=True)
    lam = jnp.exp(s1) - jnp.exp(s2) + lam_init
    lam_ref[...] = jnp.broadcast_to(lam, lam_ref.shape)
    raw = lbraw_ref[...]
    e = jnp.exp(raw - jnp.max(raw, axis=0, keepdims=True))
    sm = e / jnp.sum(e, axis=0, keepdims=True)
    lb_ref[...] = jnp.sum(sm[0:layer + 1], axis=0, keepdims=True)


def param_tables(lam_vecs, lb_raw, lam_init, layer):
    kw = lb_raw.shape[1]
    return pl.pallas_call(
        functools.partial(_tables_kernel, lam_init=lam_init, layer=layer),
        out_shape=(jax.ShapeDtypeStruct((1, LANES), F32), jax.ShapeDtypeStruct((1, kw), F32)),
        name="param_tables",
    )(lam_vecs, lb_raw)


def _bias_tiles_kernel(rb_ref, bk_ref, o_ref, *, tq, far_bucket):
    h = pl.program_id(0)
    bk = bk_ref[...]
    vec = jnp.full(bk.shape, NEG_BIG, F32)
    for n in range(NUM_BUCKETS):
        vec = jnp.where(bk == n, rb_ref[n, h], vec)
    far = rb_ref[far_bucket, h]
    for t in range(2):
        rows = jnp.broadcast_to(vec[t:t + 1], (tq, 2 * tq))
        tile = pltpu.roll(rows, 0, 1, stride=1, stride_axis=0)[:, tq:]
        o_ref[0, t] = tile if t == 0 else tile - far
    o_ref[0, FAR_TILE] = jnp.full((tq, tq), far, F32)


def bias_tiles(rel_bias, dist_buckets, tq, far_bucket):
    nb, h = rel_bias.shape
    return pl.pallas_call(
        functools.partial(_bias_tiles_kernel, tq=tq, far_bucket=far_bucket),
        grid=(h,),
        in_specs=[
            pl.BlockSpec(memory_space=pltpu.SMEM),
            pl.BlockSpec(dist_buckets.shape, lambda i: (0, 0)),
        ],
        out_specs=pl.BlockSpec((1, FAR_TILE + 1, tq, tq), lambda i: (i, 0, 0, 0)),
        out_shape=jax.ShapeDtypeStruct((h, FAR_TILE + 1, tq, tq), F32),
        compiler_params=_params("arbitrary"),
        name="bias_tiles",
    )(rel_bias, dist_buckets)


ONES_ROWS = BF16_ROWS
FAR_TILE = 2


def _diff_attn_kernel(q_ref, k_ref, v_ref, bias_ref, lam_ref, nw_ref, o_ref,
                      vt_ref, m_ref, acc_ref, st_ref, *, tq, out_scale):
    dv = v_ref.shape[1]
    dh = dv // 2
    nq = v_ref.shape[0] // tq

    for c in range(nq):
        vt_ref[c, 0:dv, :] = v_ref[c * tq:(c + 1) * tq, :].astype(F32).T.astype(BF16)
        vt_ref[c, dv:, :] = jnp.ones((ONES_ROWS, tq), BF16)

    lane = lax.broadcasted_iota(jnp.int32, (1, dv), 1)
    qw = min(DA_Q_SUB, tq)
    chains = [(mp, slice(qs * qw, (qs + 1) * qw)) for mp in range(2) for qs in range(tq // qw)]
    corner, corner_q = slice(tq - LANES, tq), slice(0, LANES)

    def map_queries(i):
        q = q_ref[pl.ds(pl.multiple_of(i * tq, tq), tq), :]
        return (jnp.where(lane < dh, q, jnp.zeros_like(q)), jnp.where(lane >= dh, q, jnp.zeros_like(q)))

    def score_block(n, j, qm):
        mp, cols = chains[n]
        kb = k_ref[pl.ds(pl.multiple_of(j * tq, tq), tq), :]
        st_ref[n] = lax.dot_general(kb, qm[mp][cols, :], (((1,), (1,)), ((), ())),
                                    preferred_element_type=F32)

    def query_block(i, carry):
        rows = pl.ds(pl.multiple_of(i * tq, tq), tq)
        qm = map_queries(i)

        m_ref[...] = jnp.full(m_ref.shape, NEG_BIG, F32)
        acc_ref[...] = jnp.zeros(acc_ref.shape, F32)

        def scores(n, j):
            score_block(n, j, qm)

        far_bias = bias_ref[0, FAR_TILE, 0:1, 0:1]

        def softmax_pv(n, vt, tile):
            mp, cols = chains[n]
            if tile == 1 and cols.start == 0:
                st_ref[n, corner, corner_q] += bias_ref[0, 1, corner, corner_q]
            st = st_ref[n]
            if tile == 0:
                st = st + bias_ref[0, tile, :, cols]
                shift = 0.0
            else:
                shift = far_bias
            m_old = m_ref[mp, :, cols]
            m_new = jnp.maximum(m_old, jnp.max(st, axis=0, keepdims=True) + shift)
            p = jnp.exp(st - (m_new - shift)).astype(BF16)
            alpha = jnp.exp(m_old - m_new)
            acc_ref[mp, :, cols] = (alpha * acc_ref[mp, :, cols]
                                    + jnp.dot(vt, p, preferred_element_type=F32))
            m_ref[mp, :, cols] = m_new

        def step(j, tile, last=False):
            vt = vt_ref[j]
            for n in range(len(chains)):
                if n + 1 < len(chains):
                    scores(n + 1, j)
                elif not last:
                    scores(0, j + 1)
                else:
                    score_block(0, 0, map_queries(jnp.minimum(i + 1, nq - 1)))
                softmax_pv(n, vt, tile)

        def far_step(j, c2):
            step(j, FAR_TILE)
            return c2

        lax.fori_loop(0, jnp.maximum(i - 1, 0), far_step, 0)

        @pl.when(i >= 1)
        def _():
            step(i - 1, 1)

        step(i, 0, last=True)

        o0 = acc_ref[0, 0:dv, :] / acc_ref[0, dv:dv + 1, :]
        o1 = acc_ref[1, 0:dv, :] / acc_ref[1, dv:dv + 1, :]
        o = o0 - lam_ref[0:1, 0:1] * o1
        ms = jnp.mean(o * o, axis=0, keepdims=True)
        y = (o * lax.rsqrt(ms + EPS)) * (nw_ref[...] * out_scale)
        o_ref[rows, :] = y.T.astype(o_ref.dtype)
        return carry

    score_block(0, 0, map_queries(0))
    lax.fori_loop(0, nq, query_block, 0)


def diff_attention(proj, bias_t, lam_row, nw, batch, seq, tq, out_scale, riders=(), layer=0):
    nq = seq // tq
    dv = LANES
    grid = (batch, DA_HEADS)
    rider_in, rider_out, rider_shapes = _rider_specs(riders, layer, grid)
    kern = _with_cast_riders(functools.partial(_diff_attn_kernel, tq=tq, out_scale=out_scale),
                             n_in=6, n_out=1, n_riders=len(riders))
    return pl.pallas_call(
        kern,
        grid=grid,
        in_specs=[
            pl.BlockSpec((seq, dv), lambda b, h: (b, h)),
            pl.BlockSpec((seq, dv), lambda b, h: (b, DA_HEADS + h)),
            pl.BlockSpec((seq, dv), lambda b, h: (b, 2 * DA_HEADS + h)),
            pl.BlockSpec((1, FAR_TILE + 1, tq, tq), lambda b, h: (h, 0, 0, 0)),
            pl.BlockSpec((1, LANES), lambda b, h: (0, 0)),
            pl.BlockSpec((dv, 1), lambda b, h: (0, 0)),
        ] + rider_in,
        out_specs=[pl.BlockSpec((seq, dv), lambda b, h: (b, h))] + rider_out,
        out_shape=[jax.ShapeDtypeStruct((batch * seq, DA_HEADS * dv), BF16)] + rider_shapes,
        scratch_shapes=[
            pltpu.VMEM((nq, dv + ONES_ROWS, tq), BF16),
            pltpu.VMEM((2, 1, tq), F32),
            pltpu.VMEM((2, dv + ONES_ROWS, tq), F32),
            pltpu.VMEM((2 * tq // min(DA_Q_SUB, tq), tq, min(DA_Q_SUB, tq)), F32),
        ],
        compiler_params=_params("parallel", "parallel"),
        name="diff_attention",
    )(proj, proj, proj, bias_t, lam_row, nw.reshape(dv, 1), *riders)


def _exact_bf16_dot(tri3, x):
    hi = x.astype(BF16)
    r1 = x - hi.astype(F32)
    mid = r1.astype(BF16)
    lo = (r1 - mid.astype(F32)).astype(BF16)
    return jnp.dot(tri3, jnp.concatenate([hi, mid, lo], axis=0), preferred_element_type=F32)


def _hgrn_tables(c):
    n_lv = int(math.log2(c))
    r = np.arange(c)[None, :]
    t = np.arange(c)[:, None]
    stack = (r <= t).astype(np.float32)
    x = np.bitwise_xor(t, r)
    level = np.where(r == t, 0, np.floor(np.log2(np.maximum(x, 1))).astype(np.int32) + 1)
    level = np.where(r > t, -1, level).astype(np.int32)
    return stack, level, n_lv


def _hgrn_kernel(q_ref, f_ref, i_ref, g_ref, lb_ref, nw_ref, tst_ref, lvl_ref, o_ref, st_ref,
                 *, ts, c, n_lv):
    nt = (((1,), (1,)), ((), ()))
    d = LANES
    nh = q_ref.shape[1] // d
    nch = ts // c

    @pl.when(pl.program_id(2) == 0)
    def _():
        st_ref[...] = jnp.zeros(st_ref.shape, F32)

    lb = lb_ref[...]
    f = lb + (1.0 - lb) * jax.nn.sigmoid(f_ref[...])
    logf = jnp.log(f)
    kk = 1.0 - f
    units = [(ch, hd) for ch in range(nch) for hd in range(nh)]
    lcat = jnp.concatenate([logf[ch * c:(ch + 1) * c, hd * d:(hd + 1) * d] for ch, hd in units], axis=1)
    gsum = _exact_bf16_dot(tst_ref[...], lcat)
    lvl = lvl_ref[...]
    nw = nw_ref[...]

    st = [st_ref[hd] for hd in range(nh)]
    for u, (ch, hd) in enumerate(units):
        rows = slice(ch * c, (ch + 1) * c)
        cols = slice(hd * d, (hd + 1) * d)

        g_in = gsum[:, u * d:(u + 1) * d]
        g_last = g_in[c - 1:c]
        g_out = g_last - g_in
        q = q_ref[rows, cols]
        q32 = q.astype(F32)
        kc = kk[rows, cols]
        kb = kc.astype(BF16)
        vb = i_ref[rows, cols].astype(BF16)
        a = jnp.where(lvl == 0, lax.dot_general(q, kb, nt, preferred_element_type=F32), 0.0)
        qt = (q32 * f[rows, cols]).astype(BF16)
        a = jnp.where(lvl == 1, lax.dot_general(qt, kb, nt, preferred_element_type=F32), a)
        for lv in range(2, n_lv + 1):
            h = 1 << (lv - 1)
            ref = jnp.concatenate(
                [jnp.broadcast_to(g_in[gi * 2 * h + h - 1:gi * 2 * h + h, :], (2 * h, d))
                 for gi in range(c // (2 * h))], axis=0)
            dec = jnp.exp(-jnp.abs(g_in - ref))
            qt = (q32 * dec).astype(BF16)
            kt = (kc * dec).astype(BF16)
            a = jnp.where(lvl == lv, lax.dot_general(qt, kt, nt, preferred_element_type=F32), a)
        qg = (q32 * jnp.exp(g_in)).astype(BF16)
        o = (jnp.dot(a.astype(BF16), vb, preferred_element_type=F32)
             + lax.dot_general(qg, st[hd].astype(BF16), nt, preferred_element_type=F32))
        kd = (kc * jnp.exp(g_out)).astype(BF16)
        upd = lax.dot_general(vb, kd, (((0,), (0,)), ((), ())), preferred_element_type=F32)
        st[hd] = st[hd] * jnp.exp(g_last) + upd
        g = g_ref[rows, cols]
        o_ref[rows, cols] = (_rms_rows(o, nw) * (g * jax.nn.sigmoid(g))).astype(o_ref.dtype)
    for hd in range(nh):
        st_ref[hd] = st[hd]


def hgrn2(proj_a, proj_b, lb, nw, batch, seq, ts, q_col0, riders=(), layer=0):
    ns = seq // ts
    d = LANES
    nh = math.gcd(HG_HEADS_PER_STEP, HG_HEADS)
    assert q_col0 % nh == 0
    w = nh * d
    ng = HG_HEADS // nh
    c = min(HG_CHUNK, ts)
    stack, level, n_lv = _hgrn_tables(c)
    grid = (batch, ng, ns)
    rider_in, rider_out, rider_shapes = _rider_specs(riders, layer, grid)
    kern = _with_cast_riders(functools.partial(_hgrn_kernel, ts=ts, c=c, n_lv=n_lv),
                             n_in=8, n_out=1, n_riders=len(riders))
    return pl.pallas_call(
        kern,
        grid=grid,
        in_specs=[
            pl.BlockSpec((ts, w), lambda b, h, i: (b * ns + i, q_col0 // nh + h)),
            pl.BlockSpec((ts, w), lambda b, h, i: (b * ns + i, h)),
            pl.BlockSpec((ts, w), lambda b, h, i: (b * ns + i, ng + h)),
            pl.BlockSpec((ts, w), lambda b, h, i: (b * ns + i, 2 * ng + h)),
            pl.BlockSpec((1, w), lambda b, h, i: (0, h)),
            pl.BlockSpec((1, d), lambda b, h, i: (0, 0)),
            pl.BlockSpec((stack.shape[0], 3 * stack.shape[1]), lambda b, h, i: (0, 0)),
            pl.BlockSpec(level.shape, lambda b, h, i: (0, 0)),
        ] + rider_in,
        out_specs=[pl.BlockSpec((ts, w), lambda b, h, i: (b * ns + i, h))] + rider_out,
        out_shape=[jax.ShapeDtypeStruct((batch * seq, HG_HEADS * d), BF16)] + rider_shapes,
        scratch_shapes=[pltpu.VMEM((nh, d, d), F32)],
        compiler_params=_params("parallel", "parallel", "arbitrary"),
        name="hgrn2",
    )(proj_a, proj_b, proj_b, proj_b, lb, nw.reshape(1, d),
      jnp.asarray(np.concatenate([stack] * 3, axis=1), BF16), jnp.asarray(level), *riders)


def _ffn_up_kernel(x_ref, halo_ref, nw_ref, wa_ref, wb_ref, cwa_ref, cwb_ref, cba_ref, cbb_ref,
                   o_ref, h_ref, u_ref, *, tm, per_batch):
    hl = SUBLANES

    @pl.when(pl.program_id(1) == 0)
    def _():
        nw = nw_ref[...]
        first = (pl.program_id(0) % per_batch) == 0
        halo = _rms_rows(halo_ref[...], nw)
        h_ref[0:hl, :] = jnp.where(first, 0.0, halo).astype(h_ref.dtype)
        h_ref[hl:, :] = _rms_rows(x_ref[...], nw).astype(h_ref.dtype)

    h = h_ref[...]
    tf = o_ref.shape[1]
    cw = min(FFN_COL_CHUNK, tf)

    def cols(cc):
        return slice(cc * cw, (cc + 1) * cw)

    def matmuls(cc):
        u_ref[cc, 0] = jnp.dot(h, wa_ref[:, cols(cc)], preferred_element_type=F32)
        u_ref[cc, 1] = jnp.dot(h, wb_ref[:, cols(cc)], preferred_element_type=F32)

    def conv(cc, half, cw_ref, cb_ref):
        acc = cb_ref[:, cols(cc)][None]
        for j in range(CONV_WIDTH):
            off = hl - (CONV_WIDTH - 1) + j
            rows = u_ref[cc, half, off:off + tm, :].reshape(tm // hl, hl, cw)
            acc = acc + cw_ref[j, :, cols(cc)][None] * rows
        return acc.reshape(tm, cw)

    def gate(cc):
        a = conv(cc, 0, cwa_ref, cba_ref)
        b = conv(cc, 1, cwb_ref, cbb_ref)
        o_ref[:, cols(cc)] = ((a * jax.nn.sigmoid(a)) * b).astype(o_ref.dtype)

    for cc in range(tf // cw):
        matmuls(cc)
        gate(cc)


def ffn_up(x, nw, w_up, conv_w, conv_b, seq):
    t, d = x.shape
    f = w_up.shape[1] // 2
    tm, tf = _pick(seq, MM_TM), _pick(f, FFN_TF)
    nf = f // tf
    per_batch = seq // tm
    hl = SUBLANES
    rows_per_tile = tm // hl
    cw = min(FFN_COL_CHUNK, tf)
    conv_w8 = jnp.broadcast_to(conv_w[:, None, :], (CONV_WIDTH, hl, 2 * f))
    conv_b8 = jnp.broadcast_to(conv_b.reshape(1, 2 * f), (hl, 2 * f))
    kern = functools.partial(_ffn_up_kernel, tm=tm, per_batch=per_batch)
    return pl.pallas_call(
        kern,
        grid=(t // tm, nf),
        in_specs=[
            pl.BlockSpec((tm, d), lambda i, j: (i, 0)),
            pl.BlockSpec((hl, d), lambda i, j: (jnp.maximum(i * rows_per_tile - 1, 0), 0)),
            pl.BlockSpec((1, d), lambda i, j: (0, 0)),
            pl.BlockSpec((d, tf), lambda i, j: (0, j)),
            pl.BlockSpec((d, tf), lambda i, j: (0, nf + j)),
            pl.BlockSpec((CONV_WIDTH, hl, tf), lambda i, j: (0, 0, j)),
            pl.BlockSpec((CONV_WIDTH, hl, tf), lambda i, j: (0, 0, nf + j)),
            pl.BlockSpec((hl, tf), lambda i, j: (0, j)),
            pl.BlockSpec((hl, tf), lambda i, j: (0, nf + j)),
        ],
        out_specs=pl.BlockSpec((tm, tf), lambda i, j: (i, j)),
        out_shape=jax.ShapeDtypeStruct((t, f), BF16),
        scratch_shapes=[
            pltpu.VMEM((tm + hl, d), BF16),
            pltpu.VMEM((tf // cw, 2, tm + hl, cw), F32),
        ],
        compiler_params=_params("parallel", "arbitrary"),
        name="ffn_up",
    )(x, x, nw.reshape(1, d), w_up, w_up, conv_w8, conv_w8, conv_b8, conv_b8)


def _ffn_down_kernel(a_ref, w_ref, res_ref, nw_ref, o_ref):
    y = res_ref[...] + jnp.dot(a_ref[...], w_ref[...], preferred_element_type=F32)
    o_ref[...] = _rms_rows(y, nw_ref[...])


def ffn_down(act, w_down, res, nw):
    t, f = act.shape
    d = w_down.shape[1]
    tm = _pick(t, FFN_DOWN_TM)
    return pl.pallas_call(
        _ffn_down_kernel,
        grid=(t // tm,),
        in_specs=[
            pl.BlockSpec((tm, f), lambda i: (i, 0)),
            pl.BlockSpec((f, d), lambda i: (0, 0), pipeline_mode=pl.Buffered(1)),
            pl.BlockSpec((tm, d), lambda i: (i, 0)),
            pl.BlockSpec((1, d), lambda i: (0, 0)),
        ],
        out_specs=pl.BlockSpec((tm, d), lambda i: (i, 0)),
        out_shape=jax.ShapeDtypeStruct((t, d), F32),
        compiler_params=_params("parallel"),
        name="ffn_down",
    )(act, w_down, res, nw.reshape(1, d))


def kernel(x, mem, w_in, w_out, norm_mix_w, lam_q1, lam_k1, lam_q2, lam_k2, da_subln_w,
           hg_lb_raw, hg_norm_w, rel_bias, norm_mem_w, mem_kv_norm_w, w_mq, w_mkv, w_mo,
           norm_ffn_w, w_up, conv_w, conv_b, w_down, final_norm_w):
    batch, seq, d_model = x.shape
    n_mem = mem.shape[1]
    depth = w_in.shape[0]
    assert depth == 1, "single-layer block"
    layer = 0
    t = batch * seq

    da_width = DA_HEADS * LANES
    hg_width = HG_HEADS * LANES
    assert w_in.shape[2] == 3 * da_width + 4 * hg_width
    assert da_subln_w.shape[1] == LANES and hg_norm_w.shape[1] == LANES
    assert lam_q1.shape[1] * 2 == LANES
    n_bf = 3 * da_width + hg_width

    tq = _pick(seq, DA_BLOCK)
    ts = _pick(seq, HG_TILE)

    lam_init = 0.8 - 0.6 * math.exp(-0.3 * layer)
    lam_vecs = jnp.stack([lam_q1[layer], lam_k1[layer], lam_q2[layer], lam_k2[layer]]).astype(F32)
    lam_row, lb = param_tables(lam_vecs, hg_lb_raw.astype(F32), lam_init, layer)

    far = _t5_bucket_np(np.arange(LANES + 1, max(seq, LANES + 2)))
    assert (far == far[0]).all(), "bias must be constant for distances beyond one lane tile"
    c = np.arange(2 * tq)
    dist_buckets = np.stack([np.where(c >= tq, _t5_bucket_np(np.maximum(c - tq, 0)), -1),
                             _t5_bucket_np(c)]).astype(np.int32)
    bias_t = bias_tiles(rel_bias.astype(F32), jnp.asarray(dist_buckets), tq, int(far[0]))

    x2d = x.reshape(t, d_model)
    q_scale = (LANES // 2) ** -0.5
    col_scale = np.ones((1, n_bf), np.float32)
    col_scale[:, :da_width] = q_scale
    proj_a, proj_b = in_proj(x2d, norm_mix_w[layer], w_in[layer].astype(BF16), n_bf, jnp.asarray(col_scale))

    da_o, w_up_bf, w_mkv_bf = diff_attention(
        proj_a, bias_t, lam_row, da_subln_w[layer], batch, seq, tq, 1.0 - lam_init,
        riders=(w_up, w_mkv), layer=layer)
    hg_o, w_down_bf, w_out_bf, w_mq_bf, w_mo_bf = hgrn2(
        proj_a, proj_b, lb, hg_norm_w[layer], batch, seq, ts, 3 * DA_HEADS,
        riders=(w_down, w_out, w_mq, w_mo), layer=layer)

    mkv = norm_matmul(mem.reshape(batch * n_mem, d_model), mem_kv_norm_w[layer], w_mkv_bf, BF16,
                      name="mem_kv_proj")
    x2 = mix_out_mem_attention(da_o, hg_o, w_out_bf, x2d, norm_mem_w[layer], w_mq_bf, mkv, w_mo_bf,
                               seq, n_mem)

    act = ffn_up(x2, norm_ffn_w[layer], w_up_bf, conv_w[layer], conv_b[layer], seq)
    out = ffn_down(act, w_down_bf, x2, final_norm_w)
    return out.reshape(batch, seq, d_model)
```

```python
import functools
import math

import numpy as np
import jax
import jax.numpy as jnp
from jax import lax
from jax.experimental import pallas as pl
from jax.experimental.pallas import tpu as pltpu

DA_HEADS = 8
HG_HEADS = 8
MEM_HEADS = 4
NUM_BUCKETS = 32
MAX_DISTANCE = 128
CONV_WIDTH = 3
EPS = 1e-6

LANES = 128
SUBLANES = 8
VMEM_LIMIT_BYTES = 56 * 1024 * 1024

HG_CHUNK = 128
HG_TILE = 1024
HG_HEADS_PER_STEP = 4
DA_BLOCK = 512
DA_Q_SUB = 512
MM_TM = 1024
MM_TN = 1024
FFN_TF = 512
FFN_COL_CHUNK = 512
RESIDENT_TM = 512
FFN_DOWN_TM = RESIDENT_TM
NEG_BIG = -1e30

F32 = jnp.float32
BF16 = jnp.bfloat16


def _params(*sem):
    return pltpu.CompilerParams(dimension_semantics=sem, vmem_limit_bytes=VMEM_LIMIT_BYTES)


def _rms_rows(x, w):
    ms = jnp.mean(x * x, axis=-1, keepdims=True)
    return (x * lax.rsqrt(ms + EPS)) * w


NORM_ROWS = 128


def _rms_rows_into(dst_ref, row0, x_ref, w):
    rows = x_ref.shape[0]
    step = NORM_ROWS if rows % NORM_ROWS == 0 else rows
    for r in range(0, rows, step):
        dst_ref[row0 + r:row0 + r + step, :] = _rms_rows(x_ref[r:r + step, :], w).astype(dst_ref.dtype)


def _pick(n, pref):
    t = min(pref, n)
    while n % t:
        t //= 2
    return t


def _norm_mm_kernel(x_ref, nw_ref, w_ref, o_ref, h_ref):
    @pl.when(pl.program_id(1) == 0)
    def _():
        _rms_rows_into(h_ref, 0, x_ref, nw_ref[...])

    o_ref[...] = jnp.dot(h_ref[...], w_ref[...], preferred_element_type=F32).astype(o_ref.dtype)


def norm_matmul(x, nw, w, out_dtype, name="norm_matmul", whole_n=False):
    t, k = x.shape
    n = w.shape[1]
    tm, tn = _pick(t, MM_TM), (n if whole_n else _pick(n, MM_TN))
    return pl.pallas_call(
        _norm_mm_kernel,
        grid=(t // tm, n // tn),
        in_specs=[
            pl.BlockSpec((tm, k), lambda i, j: (i, 0)),
            pl.BlockSpec((1, k), lambda i, j: (0, 0)),
            pl.BlockSpec((k, tn), lambda i, j: (0, j)),
        ],
        out_specs=pl.BlockSpec((tm, tn), lambda i, j: (i, j)),
        out_shape=jax.ShapeDtypeStruct((t, n), out_dtype),
        scratch_shapes=[pltpu.VMEM((tm, k), BF16)],
        compiler_params=_params("parallel", "arbitrary"),
        name=name,
    )(x, nw.reshape(1, k), w)


BF16_ROWS = 16


def _with_cast_riders(body, n_in, n_out, n_riders):
    def kern(*refs):
        ins, rest = refs[:n_in], refs[n_in:]
        rid_in, rest = rest[:n_riders], rest[n_riders:]
        outs, rest = rest[:n_out], rest[n_out:]
        rid_out, scratch = rest[:n_riders], rest[n_riders:]
        for src, dst in zip(rid_in, rid_out):
            dst[...] = src[...].astype(dst.dtype)
        body(*ins, *outs, *scratch)
    return kern


def _rider_specs(riders, layer, grid):
    n_steps = math.prod(grid)

    def step(*g):
        idx = 0
        for gi, n in zip(g, grid):
            idx = idx * n + gi
        return idx

    in_specs, out_specs, shapes = [], [], []
    for a in riders:
        _, rows, cols = a.shape
        n_slabs = math.gcd(n_steps, rows // BF16_ROWS)
        assert rows % BF16_ROWS == 0 and n_steps % n_slabs == 0, (a.shape, grid)
        hold = n_steps // n_slabs
        in_specs.append(pl.BlockSpec((None, rows // n_slabs, cols),
                                     lambda *g, hold=hold: (layer, step(*g) // hold, 0)))
        out_specs.append(pl.BlockSpec((rows // n_slabs, cols),
                                      lambda *g, hold=hold: (step(*g) // hold, 0)))
        shapes.append(jax.ShapeDtypeStruct((rows, cols), BF16))
    return in_specs, out_specs, shapes


def _in_proj_kernel(x_ref, nw_ref, w_ref, cs_ref, oa_ref, ob_ref, h_ref, *, na_tiles):
    j = pl.program_id(1)

    @pl.when(j == 0)
    def _():
        _rms_rows_into(h_ref, 0, x_ref, nw_ref[...])

    y = jnp.dot(h_ref[...], w_ref[...], preferred_element_type=F32)

    @pl.when(j < na_tiles)
    def _():
        oa_ref[...] = (y * cs_ref[...]).astype(oa_ref.dtype)

    @pl.when(j >= na_tiles)
    def _():
        ob_ref[...] = y


def in_proj(x, nw, w, n_a, col_scale):
    t, k = x.shape
    n = w.shape[1]
    tm, tn = _pick(t, MM_TM), math.gcd(_pick(n_a, MM_TN), _pick(n - n_a, MM_TN))
    na_tiles = n_a // tn
    return pl.pallas_call(
        functools.partial(_in_proj_kernel, na_tiles=na_tiles),
        grid=(t // tm, n // tn),
        in_specs=[
            pl.BlockSpec((tm, k), lambda i, j: (i, 0)),
            pl.BlockSpec((1, k), lambda i, j: (0, 0)),
            pl.BlockSpec((k, tn), lambda i, j: (0, j)),
            pl.BlockSpec((1, tn), lambda i, j: (0, jnp.minimum(j, na_tiles - 1))),
        ],
        out_specs=[
            pl.BlockSpec((tm, tn), lambda i, j: (i, jnp.minimum(j, na_tiles - 1))),
            pl.BlockSpec((tm, tn), lambda i, j: (i, jnp.maximum(j - na_tiles, 0))),
        ],
        out_shape=(jax.ShapeDtypeStruct((t, n_a), BF16), jax.ShapeDtypeStruct((t, n - n_a), F32)),
        scratch_shapes=[pltpu.VMEM((tm, k), BF16)],
        compiler_params=_params("parallel", "arbitrary"),
        name="in_proj",
    )(x, nw.reshape(1, k), w, col_scale)


def _mix_mem_kernel(a0_ref, a1_ref, w0_ref, w1_ref, x_ref, nw_ref, wq_ref, k_ref, v_ref, wo_ref,
                    o_ref, att_ref, *, scale):
    x1 = x_ref[...]
    x1 = x1 + jnp.dot(a0_ref[...], w0_ref[...], preferred_element_type=F32)
    x1 = x1 + jnp.dot(a1_ref[...], w1_ref[...], preferred_element_type=F32)
    hq = _rms_rows(x1, nw_ref[...]).astype(BF16)
    mq = jnp.dot(hq, wq_ref[...], preferred_element_type=F32).astype(BF16)
    dh = mq.shape[1] // MEM_HEADS
    for hd in range(MEM_HEADS):
        sl = slice(hd * dh, (hd + 1) * dh)
        s = lax.dot_general(mq[:, sl], k_ref[:, sl], (((1,), (1,)), ((), ())),
                            preferred_element_type=F32) * scale
        e = jnp.exp(s - jnp.max(s, axis=-1, keepdims=True))
        p = e / jnp.sum(e, axis=-1, keepdims=True)
        att_ref[:, sl] = jnp.dot(p.astype(BF16), v_ref[:, sl],
                                 preferred_element_type=F32).astype(att_ref.dtype)
    o_ref[...] = x1 + jnp.dot(att_ref[...], wo_ref[...], preferred_element_type=F32)


def mix_out_mem_attention(a0, a1, w_out, x, nw_mem, w_mq, mkv, w_mo, seq, n_mem):
    t, k0 = a0.shape
    k1 = a1.shape[1]
    d = x.shape[1]
    assert k0 == k1 and w_out.shape == (k0 + k1, d)
    tm = _pick(seq, RESIDENT_TM)
    per_batch = seq // tm
    scale = (d // MEM_HEADS) ** -0.5
    once = pl.Buffered(1)
    return pl.pallas_call(
        functools.partial(_mix_mem_kernel, scale=scale),
        grid=(t // tm,),
        in_specs=[
            pl.BlockSpec((tm, k0), lambda i: (i, 0)),
            pl.BlockSpec((tm, k1), lambda i: (i, 0)),
            pl.BlockSpec((k0, d), lambda i: (0, 0), pipeline_mode=once),
            pl.BlockSpec((k1, d), lambda i: (1, 0), pipeline_mode=once),
            pl.BlockSpec((tm, d), lambda i: (i, 0)),
            pl.BlockSpec((1, d), lambda i: (0, 0)),
            pl.BlockSpec((d, d), lambda i: (0, 0), pipeline_mode=once),
            pl.BlockSpec((n_mem, d), lambda i: (i // per_batch, 0), pipeline_mode=once),
            pl.BlockSpec((n_mem, d), lambda i: (i // per_batch, 1), pipeline_mode=once),
            pl.BlockSpec((d, d), lambda i: (0, 0), pipeline_mode=once),
        ],
        out_specs=pl.BlockSpec((tm, d), lambda i: (i, 0)),
        out_shape=jax.ShapeDtypeStruct((t, d), F32),
        scratch_shapes=[pltpu.VMEM((tm, d), BF16)],
        compiler_params=_params("parallel"),
        name="mix_out_mem_attention",
    )(a0, a1, w_out, w_out, x, nw_mem.reshape(1, d), w_mq, mkv, mkv, w_mo)


def _t5_bucket_np(n):
    max_exact = NUM_BUCKETS // 2
    nf = np.maximum(n, 1).astype(np.float32)
    large = max_exact + (np.log(nf / max_exact) / math.log(MAX_DISTANCE / max_exact)
                         * (NUM_BUCKETS - max_exact)).astype(np.int32)
    large = np.minimum(large, NUM_BUCKETS - 1)
    return np.where(n < max_exact, n, large).astype(np.int32)


def _tables_kernel(lamv_ref, lbraw_ref, lam_ref, lb_ref, *, lam_init, layer):
    v = lamv_ref[...]
    s1 = jnp.sum(v[0:1] * v[1:2], axis=-1, keepdims=True)
    s2 = jnp.sum(v[2:3] * v[3:4], axis=-1, keepdims=True)
    lam = jnp.exp(s1) - jnp.exp(s2) + lam_init
    lam_ref[...] = jnp.broadcast_to(lam, lam_ref.shape)
    raw = lbraw_ref[...]
    e = jnp.exp(raw - jnp.max(raw, axis=0, keepdims=True))
    sm = e / jnp.sum(e, axis=0, keepdims=True)
    lb_ref[...] = jnp.sum(sm[0:layer + 1], axis=0, keepdims=True)


def param_tables(lam_vecs, lb_raw, lam_init, layer):
    kw = lb_raw.shape[1]
    return pl.pallas_call(
        functools.partial(_tables_kernel, lam_init=lam_init, layer=layer),
        out_shape=(jax.ShapeDtypeStruct((1, LANES), F32), jax.ShapeDtypeStruct((1, kw), F32)),
        name="param_tables",
    )(lam_vecs, lb_raw)


def _bias_tiles_kernel(rb_ref, bk_ref, o_ref, *, tq, far_bucket):
    h = pl.program_id(0)
    bk = bk_ref[...]
    vec = jnp.full(bk.shape, NEG_BIG, F32)
    for n in range(NUM_BUCKETS):
        vec = jnp.where(bk == n, rb_ref[n, h], vec)
    far = rb_ref[far_bucket, h]
    for t in range(2):
        rows = jnp.broadcast_to(vec[t:t + 1], (tq, 2 * tq))
        tile = pltpu.roll(rows, 0, 1, stride=1, stride_axis=0)[:, tq:]
        o_ref[0, t] = tile if t == 0 else tile - far
    o_ref[0, FAR_TILE] = jnp.full((tq, tq), far, F32)


def bias_tiles(rel_bias, dist_buckets, tq, far_bucket):
    nb, h = rel_bias.shape
    return pl.pallas_call(
        functools.partial(_bias_tiles_kernel, tq=tq, far_bucket=far_bucket),
        grid=(h,),
        in_specs=[
            pl.BlockSpec(memory_space=pltpu.SMEM),
            pl.BlockSpec(dist_buckets.shape, lambda i: (0, 0)),
        ],
        out_specs=pl.BlockSpec((1, FAR_TILE + 1, tq, tq), lambda i: (i, 0, 0, 0)),
        out_shape=jax.ShapeDtypeStruct((h, FAR_TILE + 1, tq, tq), F32),
        compiler_params=_params("arbitrary"),
        name="bias_tiles",
    )(rel_bias, dist_buckets)


ONES_ROWS = BF16_ROWS
FAR_TILE = 2


def _diff_attn_kernel(q_ref, k_ref, v_ref, bias_ref, lam_ref, nw_ref, o_ref,
                      vt_ref, m_ref, acc_ref, st_ref, *, tq, out_scale):
    dv = v_ref.shape[1]
    dh = dv // 2
    nq = v_ref.shape[0] // tq

    for c in range(nq):
        vt_ref[c, 0:dv, :] = v_ref[c * tq:(c + 1) * tq, :].astype(F32).T.astype(BF16)
        vt_ref[c, dv:, :] = jnp.ones((ONES_ROWS, tq), BF16)

    lane = lax.broadcasted_iota(jnp.int32, (1, dv), 1)
    qw = min(DA_Q_SUB, tq)
    chains = [(mp, slice(qs * qw, (qs + 1) * qw)) for mp in range(2) for qs in range(tq // qw)]
    corner, corner_q = slice(tq - LANES, tq), slice(0, LANES)

    def map_queries(i):
        q = q_ref[pl.ds(pl.multiple_of(i * tq, tq), tq), :]
        return (jnp.where(lane < dh, q, jnp.zeros_like(q)), jnp.where(lane >= dh, q, jnp.zeros_like(q)))

    def score_block(n, j, qm):
        mp, cols = chains[n]
        kb = k_ref[pl.ds(pl.multiple_of(j * tq, tq), tq), :]
        st_ref[n] = lax.dot_general(kb, qm[mp][cols, :], (((1,), (1,)), ((), ())),
                                    preferred_element_type=F32)

    def query_block(i, carry):
        rows = pl.ds(pl.multiple_of(i * tq, tq), tq)
        qm = map_queries(i)

        m_ref[...] = jnp.full(m_ref.shape, NEG_BIG, F32)
        acc_ref[...] = jnp.zeros(acc_ref.shape, F32)

        def scores(n, j):
            score_block(n, j, qm)

        far_bias = bias_ref[0, FAR_TILE, 0:1, 0:1]

        def softmax_pv(n, vt, tile):
            mp, cols = chains[n]
            if tile == 1 and cols.start == 0:
                st_ref[n, corner, corner_q] += bias_ref[0, 1, corner, corner_q]
            st = st_ref[n]
            if tile == 0:
                st = st + bias_ref[0, tile, :, cols]
                shift = 0.0
            else:
                shift = far_bias
            m_old = m_ref[mp, :, cols]
            m_new = jnp.maximum(m_old, jnp.max(st, axis=0, keepdims=True) + shift)
            p = jnp.exp(st - (m_new - shift)).astype(BF16)
            alpha = jnp.exp(m_old - m_new)
            acc_ref[mp, :, cols] = (alpha * acc_ref[mp, :, cols]
                                    + jnp.dot(vt, p, preferred_element_type=F32))
            m_ref[mp, :, cols] = m_new

        def step(j, tile, last=False):
            vt = vt_ref[j]
            for n in range(len(chains)):
                if n + 1 < len(chains):
                    scores(n + 1, j)
                elif not last:
                    scores(0, j + 1)
                else:
                    score_block(0, 0, map_queries(jnp.minimum(i + 1, nq - 1)))
                softmax_pv(n, vt, tile)

        def far_step(j, c2):
            step(j, FAR_TILE)
            return c2

        lax.fori_loop(0, jnp.maximum(i - 1, 0), far_step, 0)

        @pl.when(i >= 1)
        def _():
            step(i - 1, 1)

        step(i, 0, last=True)

        o0 = acc_ref[0, 0:dv, :] / acc_ref[0, dv:dv + 1, :]
        o1 = acc_ref[1, 0:dv, :] / acc_ref[1, dv:dv + 1, :]
        o = o0 - lam_ref[0:1, 0:1] * o1
        ms = jnp.mean(o * o, axis=0, keepdims=True)
        y = (o * lax.rsqrt(ms + EPS)) * (nw_ref[...] * out_scale)
        o_ref[rows, :] = y.T.astype(o_ref.dtype)
        return carry

    score_block(0, 0, map_queries(0))
    lax.fori_loop(0, nq, query_block, 0)


def diff_attention(proj, bias_t, lam_row, nw, batch, seq, tq, out_scale, riders=(), layer=0):
    nq = seq // tq
    dv = LANES
    grid = (batch, DA_HEADS)
    rider_in, rider_out, rider_shapes = _rider_specs(riders, layer, grid)
    kern = _with_cast_riders(functools.partial(_diff_attn_kernel, tq=tq, out_scale=out_scale),
                             n_in=6, n_out=1, n_riders=len(riders))
    return pl.pallas_call(
        kern,
        grid=grid,
        in_specs=[
            pl.BlockSpec((seq, dv), lambda b, h: (b, h)),
            pl.BlockSpec((seq, dv), lambda b, h: (b, DA_HEADS + h)),
            pl.BlockSpec((seq, dv), lambda b, h: (b, 2 * DA_HEADS + h)),
            pl.BlockSpec((1, FAR_TILE + 1, tq, tq), lambda b, h: (h, 0, 0, 0)),
            pl.BlockSpec((1, LANES), lambda b, h: (0, 0)),
            pl.BlockSpec((dv, 1), lambda b, h: (0, 0)),
        ] + rider_in,
        out_specs=[pl.BlockSpec((seq, dv), lambda b, h: (b, h))] + rider_out,
        out_shape=[jax.ShapeDtypeStruct((batch * seq, DA_HEADS * dv), BF16)] + rider_shapes,
        scratch_shapes=[
            pltpu.VMEM((nq, dv + ONES_ROWS, tq), BF16),
            pltpu.VMEM((2, 1, tq), F32),
            pltpu.VMEM((2, dv + ONES_ROWS, tq), F32),
            pltpu.VMEM((2 * tq // min(DA_Q_SUB, tq), tq, min(DA_Q_SUB, tq)), F32),
        ],
        compiler_params=_params("parallel", "parallel"),
        name="diff_attention",
    )(proj, proj, proj, bias_t, lam_row, nw.reshape(dv, 1), *riders)


def _exact_bf16_dot(tri3, x):
    hi = x.astype(BF16)
    r1 = x - hi.astype(F32)
    mid = r1.astype(BF16)
    lo = (r1 - mid.astype(F32)).astype(BF16)
    return jnp.dot(tri3, jnp.concatenate([hi, mid, lo], axis=0), preferred_element_type=F32)


def _hgrn_tables(c):
    n_lv = int(math.log2(c))
    r = np.arange(c)[None, :]
    t = np.arange(c)[:, None]
    stack = (r <= t).astype(np.float32)
    x = np.bitwise_xor(t, r)
    level = np.where(r == t, 0, np.floor(np.log2(np.maximum(x, 1))).astype(np.int32) + 1)
    level = np.where(r > t, -1, level).astype(np.int32)
    return stack, level, n_lv


def _hgrn_kernel(q_ref, f_ref, i_ref, g_ref, lb_ref, nw_ref, tst_ref, lvl_ref, o_ref, st_ref,
                 *, ts, c, n_lv):
    nt = (((1,), (1,)), ((), ()))
    d = LANES
    nh = q_ref.shape[1] // d
    nch = ts // c

    @pl.when(pl.program_id(2) == 0)
    def _():
        st_ref[...] = jnp.zeros(st_ref.shape, F32)

    lb = lb_ref[...]
    f = lb + (1.0 - lb) * jax.nn.sigmoid(f_ref[...])
    logf = jnp.log(f)
    kk = 1.0 - f
    units = [(ch, hd) for ch in range(nch) for hd in range(nh)]
    lcat = jnp.concatenate([logf[ch * c:(ch + 1) * c, hd * d:(hd + 1) * d] for ch, hd in units], axis=1)
    gsum = _exact_bf16_dot(tst_ref[...], lcat)
    lvl = lvl_ref[...]
    nw = nw_ref[...]

    st = [st_ref[hd] for hd in range(nh)]
    for u, (ch, hd) in enumerate(units):
        rows = slice(ch * c, (ch + 1) * c)
        cols = slice(hd * d, (hd + 1) * d)

        g_in = gsum[:, u * d:(u + 1) * d]
        g_last = g_in[c - 1:c]
        g_out = g_last - g_in
        q = q_ref[rows, cols]
        q32 = q.astype(F32)
        kc = kk[rows, cols]
        kb = kc.astype(BF16)
        vb = i_ref[rows, cols].astype(BF16)
        a = jnp.where(lvl == 0, lax.dot_general(q, kb, nt, preferred_element_type=F32), 0.0)
        qt = (q32 * f[rows, cols]).astype(BF16)
        a = jnp.where(lvl == 1, lax.dot_general(qt, kb, nt, preferred_element_type=F32), a)
        for lv in range(2, n_lv + 1):
            h = 1 << (lv - 1)
            ref = jnp.concatenate(
                [jnp.broadcast_to(g_in[gi * 2 * h + h - 1:gi * 2 * h + h, :], (2 * h, d))
                 for gi in range(c // (2 * h))], axis=0)
            dec = jnp.exp(-jnp.abs(g_in - ref))
            qt = (q32 * dec).astype(BF16)
            kt = (kc * dec).astype(BF16)
            a = jnp.where(lvl == lv, lax.dot_general(qt, kt, nt, preferred_element_type=F32), a)
        qg = (q32 * jnp.exp(g_in)).astype(BF16)
        o = (jnp.dot(a.astype(BF16), vb, preferred_element_type=F32)
             + lax.dot_general(qg, st[hd].astype(BF16), nt, preferred_element_type=F32))
        kd = (kc * jnp.exp(g_out)).astype(BF16)
        upd = lax.dot_general(vb, kd, (((0,), (0,)), ((), ())), preferred_element_type=F32)
        st[hd] = st[hd] * jnp.exp(g_last) + upd
        g = g_ref[rows, cols]
        o_ref[rows, cols] = (_rms_rows(o, nw) * (g * jax.nn.sigmoid(g))).astype(o_ref.dtype)
    for hd in range(nh):
        st_ref[hd] = st[hd]


def hgrn2(proj_a, proj_b, lb, nw, batch, seq, ts, q_col0, riders=(), layer=0):
    ns = seq // ts
    d = LANES
    nh = math.gcd(HG_HEADS_PER_STEP, HG_HEADS)
    assert q_col0 % nh == 0
    w = nh * d
    ng = HG_HEADS // nh
    c = min(HG_CHUNK, ts)
    stack, level, n_lv = _hgrn_tables(c)
    grid = (batch, ng, ns)
    rider_in, rider_out, rider_shapes = _rider_specs(riders, layer, grid)
    kern = _with_cast_riders(functools.partial(_hgrn_kernel, ts=ts, c=c, n_lv=n_lv),
                             n_in=8, n_out=1, n_riders=len(riders))
    return pl.pallas_call(
        kern,
        grid=grid,
        in_specs=[
            pl.BlockSpec((ts, w), lambda b, h, i: (b * ns + i, q_col0 // nh + h)),
            pl.BlockSpec((ts, w), lambda b, h, i: (b * ns + i, h)),
            pl.BlockSpec((ts, w), lambda b, h, i: (b * ns + i, ng + h)),
            pl.BlockSpec((ts, w), lambda b, h, i: (b * ns + i, 2 * ng + h)),
            pl.BlockSpec((1, w), lambda b, h, i: (0, h)),
            pl.BlockSpec((1, d), lambda b, h, i: (0, 0)),
            pl.BlockSpec((stack.shape[0], 3 * stack.shape[1]), lambda b, h, i: (0, 0)),
            pl.BlockSpec(level.shape, lambda b, h, i: (0, 0)),
        ] + rider_in,
        out_specs=[pl.BlockSpec((ts, w), lambda b, h, i: (b * ns + i, h))] + rider_out,
        out_shape=[jax.ShapeDtypeStruct((batch * seq, HG_HEADS * d), BF16)] + rider_shapes,
        scratch_shapes=[pltpu.VMEM((nh, d, d), F32)],
        compiler_params=_params("parallel", "parallel", "arbitrary"),
        name="hgrn2",
    )(proj_a, proj_b, proj_b, proj_b, lb, nw.reshape(1, d),
      jnp.asarray(np.concatenate([stack] * 3, axis=1), BF16), jnp.asarray(level), *riders)


def _ffn_up_kernel(x_ref, halo_ref, nw_ref, wa_ref, wb_ref, cwa_ref, cwb_ref, cba_ref, cbb_ref,
                   o_ref, h_ref, u_ref, *, tm, per_batch):
    hl = SUBLANES

    @pl.when(pl.program_id(1) == 0)
    def _():
        nw = nw_ref[...]
        first = (pl.program_id(0) % per_batch) == 0
        halo = _rms_rows(halo_ref[...], nw)
        h_ref[0:hl, :] = jnp.where(first, 0.0, halo).astype(h_ref.dtype)
        _rms_rows_into(h_ref, hl, x_ref, nw)

    h = h_ref[...]
    tf = o_ref.shape[1]
    cw = min(FFN_COL_CHUNK, tf)

    def cols(cc):
        return slice(cc * cw, (cc + 1) * cw)

    def matmuls(cc):
        u_ref[cc, 0] = jnp.dot(h, wa_ref[:, cols(cc)], preferred_element_type=F32)
        u_ref[cc, 1] = jnp.dot(h, wb_ref[:, cols(cc)], preferred_element_type=F32)

    def conv(cc, half, cw_ref, cb_ref):
        acc = cb_ref[:, cols(cc)][None]
        for j in range(CONV_WIDTH):
            off = hl - (CONV_WIDTH - 1) + j
            rows = u_ref[cc, half, off:off + tm, :].reshape(tm // hl, hl, cw)
            acc = acc + cw_ref[j, :, cols(cc)][None] * rows
        return acc.reshape(tm, cw)

    def gate(cc):
        a = conv(cc, 0, cwa_ref, cba_ref)
        b = conv(cc, 1, cwb_ref, cbb_ref)
        o_ref[:, cols(cc)] = ((a * jax.nn.sigmoid(a)) * b).astype(o_ref.dtype)

    for cc in range(tf // cw):
        matmuls(cc)
        gate(cc)


def ffn_up(x, nw, w_up, conv_w, conv_b, seq):
    t, d = x.shape
    f = w_up.shape[1] // 2
    tm, tf = _pick(seq, MM_TM), _pick(f, FFN_TF)
    nf = f // tf
    per_batch = seq // tm
    hl = SUBLANES
    rows_per_tile = tm // hl
    cw = min(FFN_COL_CHUNK, tf)
    conv_w8 = jnp.broadcast_to(conv_w[:, None, :], (CONV_WIDTH, hl, 2 * f))
    conv_b8 = jnp.broadcast_to(conv_b.reshape(1, 2 * f), (hl, 2 * f))
    kern = functools.partial(_ffn_up_kernel, tm=tm, per_batch=per_batch)
    return pl.pallas_call(
        kern,
        grid=(t // tm, nf),
        in_specs=[
            pl.BlockSpec((tm, d), lambda i, j: (i, 0)),
            pl.BlockSpec((hl, d), lambda i, j: (jnp.maximum(i * rows_per_tile - 1, 0), 0)),
            pl.BlockSpec((1, d), lambda i, j: (0, 0)),
            pl.BlockSpec((d, tf), lambda i, j: (0, j)),
            pl.BlockSpec((d, tf), lambda i, j: (0, nf + j)),
            pl.BlockSpec((CONV_WIDTH, hl, tf), lambda i, j: (0, 0, j)),
            pl.BlockSpec((CONV_WIDTH, hl, tf), lambda i, j: (0, 0, nf + j)),
            pl.BlockSpec((hl, tf), lambda i, j: (0, j)),
            pl.BlockSpec((hl, tf), lambda i, j: (0, nf + j)),
        ],
        out_specs=pl.BlockSpec((tm, tf), lambda i, j: (i, j)),
        out_shape=jax.ShapeDtypeStruct((t, f), BF16),
        scratch_shapes=[
            pltpu.VMEM((tm + hl, d), BF16),
            pltpu.VMEM((tf // cw, 2, tm + hl, cw), F32),
        ],
        compiler_params=_params("parallel", "arbitrary"),
        name="ffn_up",
    )(x, x, nw.reshape(1, d), w_up, w_up, conv_w8, conv_w8, conv_b8, conv_b8)


def _ffn_down_kernel(a_ref, w_ref, res_ref, nw_ref, o_ref):
    y = res_ref[...] + jnp.dot(a_ref[...], w_ref[...], preferred_element_type=F32)
    o_ref[...] = _rms_rows(y, nw_ref[...])


def ffn_down(act, w_down, res, nw):
    t, f = act.shape
    d = w_down.shape[1]
    tm = _pick(t, FFN_DOWN_TM)
    return pl.pallas_call(
        _ffn_down_kernel,
        grid=(t // tm,),
        in_specs=[
            pl.BlockSpec((tm, f), lambda i: (i, 0)),
            pl.BlockSpec((f, d), lambda i: (0, 0), pipeline_mode=pl.Buffered(1)),
            pl.BlockSpec((tm, d), lambda i: (i, 0)),
            pl.BlockSpec((1, d), lambda i: (0, 0)),
        ],
        out_specs=pl.BlockSpec((tm, d), lambda i: (i, 0)),
        out_shape=jax.ShapeDtypeStruct((t, d), F32),
        compiler_params=_params("parallel"),
        name="ffn_down",
    )(act, w_down, res, nw.reshape(1, d))


def kernel(x, mem, w_in, w_out, norm_mix_w, lam_q1, lam_k1, lam_q2, lam_k2, da_subln_w,
           hg_lb_raw, hg_norm_w, rel_bias, norm_mem_w, mem_kv_norm_w, w_mq, w_mkv, w_mo,
           norm_ffn_w, w_up, conv_w, conv_b, w_down, final_norm_w):
    batch, seq, d_model = x.shape
    n_mem = mem.shape[1]
    depth = w_in.shape[0]
    assert depth == 1, "single-layer block"
    layer = 0
    t = batch * seq

    da_width = DA_HEADS * LANES
    hg_width = HG_HEADS * LANES
    assert w_in.shape[2] == 3 * da_width + 4 * hg_width
    assert da_subln_w.shape[1] == LANES and hg_norm_w.shape[1] == LANES
    assert lam_q1.shape[1] * 2 == LANES
    n_bf = 3 * da_width + hg_width

    tq = _pick(seq, DA_BLOCK)
    ts = _pick(seq, HG_TILE)

    lam_init = 0.8 - 0.6 * math.exp(-0.3 * layer)
    lam_vecs = jnp.stack([lam_q1[layer], lam_k1[layer], lam_q2[layer], lam_k2[layer]]).astype(F32)
    lam_row, lb = param_tables(lam_vecs, hg_lb_raw.astype(F32), lam_init, layer)

    far = _t5_bucket_np(np.arange(LANES + 1, max(seq, LANES + 2)))
    assert (far == far[0]).all(), "bias must be constant for distances beyond one lane tile"
    c = np.arange(2 * tq)
    dist_buckets = np.stack([np.where(c >= tq, _t5_bucket_np(np.maximum(c - tq, 0)), -1),
                             _t5_bucket_np(c)]).astype(np.int32)
    bias_t = bias_tiles(rel_bias.astype(F32), jnp.asarray(dist_buckets), tq, int(far[0]))

    x2d = x.reshape(t, d_model)
    q_scale = (LANES // 2) ** -0.5
    col_scale = np.ones((1, n_bf), np.float32)
    col_scale[:, :da_width] = q_scale
    proj_a, proj_b = in_proj(x2d, norm_mix_w[layer], w_in[layer].astype(BF16), n_bf, jnp.asarray(col_scale))

    da_o, w_up_bf, w_mkv_bf = diff_attention(
        proj_a, bias_t, lam_row, da_subln_w[layer], batch, seq, tq, 1.0 - lam_init,
        riders=(w_up, w_mkv), layer=layer)
    hg_o, w_down_bf, w_out_bf, w_mq_bf, w_mo_bf = hgrn2(
        proj_a, proj_b, lb, hg_norm_w[layer], batch, seq, ts, 3 * DA_HEADS,
        riders=(w_down, w_out, w_mq, w_mo), layer=layer)

    mkv = norm_matmul(mem.reshape(batch * n_mem, d_model), mem_kv_norm_w[layer], w_mkv_bf, BF16,
                      name="mem_kv_proj", whole_n=True)
    x2 = mix_out_mem_attention(da_o, hg_o, w_out_bf, x2d, norm_mem_w[layer], w_mq_bf, mkv, w_mo_bf,
                               seq, n_mem)

    act = ffn_up(x2, norm_ffn_w[layer], w_up_bf, conv_w[layer], conv_b[layer], seq)
    out = ffn_down(act, w_down_bf, x2, final_norm_w)
    return out.reshape(batch, seq, d_model)
```
